```python
import math
import jax, jax.numpy as jnp
from jax import lax
import numpy as np

D_MODEL = 1024
BATCH = 16
SEQ = 4096
DEPTH = 2

A_HEADS = 8
A_KV_HEADS = 2
A_HEAD_DIM = 64
A_REP = A_HEADS // A_KV_HEADS
A_WIDTH = A_HEADS * A_HEAD_DIM
WINDOW = 128
BLOCK = 128
REL_BUCKETS = 32
REL_MAX_DIST = 128
B_HEADS = 8
Q_LORA = 256
KV_LORA = 128
NOPE_DIM = 64
ROPE_DIM = 32
B_QK_DIM = NOPE_DIM + ROPE_DIM
B_V_DIM = 64
B_WIDTH = B_HEADS * B_V_DIM
ROPE_THETA = 10000.0
IN_SIZES = (A_HEADS * A_HEAD_DIM, A_KV_HEADS * A_HEAD_DIM, A_KV_HEADS * A_HEAD_DIM,
            Q_LORA, KV_LORA, ROPE_DIM, 2 * D_MODEL)
IN_COLS = sum(IN_SIZES)
N_GROUPS = 4
EXPERTS_PER_GROUP = 8
N_EXPERTS = N_GROUPS * EXPERTS_PER_GROUP
TOP_K = 2
D_EXPERT = 256
MOE_BLOCK = 128
EPS = 1e-6

kernel_name = "hybrid_swa_mla_hier_moe"


def rmsnorm(x, g):
    xf = x.astype(jnp.float32)
    y = xf * lax.rsqrt(jnp.mean(xf * xf, axis=-1, keepdims=True) + EPS)
    return (y * g.astype(jnp.float32)).astype(x.dtype)


def t5_bucket(dist):
    max_exact = REL_BUCKETS // 2
    d = jnp.maximum(dist, 0)
    ratio = jnp.maximum(d, 1).astype(jnp.float32) / max_exact
    large = max_exact + (jnp.log(ratio) / math.log(REL_MAX_DIST / max_exact)
                         * (REL_BUCKETS - max_exact)).astype(jnp.int32)
    large = jnp.minimum(large, REL_BUCKETS - 1)
    return jnp.where(d < max_exact, d, large)


def apply_rope(x, positions):
    half = ROPE_DIM // 2
    inv_freq = ROPE_THETA ** (-jnp.arange(half, dtype=jnp.float32) / half)
    ang = positions.astype(jnp.float32)[:, None] * inv_freq[None, :]
    cos = jnp.cos(ang)[None, :, None, :]
    sin = jnp.sin(ang)[None, :, None, :]
    xf = x.astype(jnp.float32)
    x1, x2 = xf[..., :half], xf[..., half:]
    out = jnp.concatenate([x1 * cos - x2 * sin, x1 * sin + x2 * cos], axis=-1)
    return out.astype(x.dtype)


def sliding_window_attention(q, k, v, sinks, rel_table):
    Bsz, S = q.shape[0], q.shape[1]
    nb = S // BLOCK
    scale = A_HEAD_DIM ** -0.5
    qb = q.reshape(Bsz, nb, BLOCK, A_KV_HEADS, A_REP, A_HEAD_DIM)

    def band(t):
        tb = t.reshape(Bsz, nb, BLOCK, A_KV_HEADS, A_HEAD_DIM)
        prev = jnp.concatenate([jnp.zeros_like(tb[:, :1]), tb[:, :-1]], axis=1)
        return jnp.concatenate([prev, tb], axis=2)

    kb, vb = band(k), band(v)
    s = jnp.einsum('bnqgrd,bnkgd->bngrqk', qb, kb).astype(jnp.float32) * scale
    qi = jnp.arange(BLOCK)[:, None]
    kj = jnp.arange(2 * BLOCK)[None, :]
    dist = qi + BLOCK - kj
    bias = rel_table[t5_bucket(dist)]
    bias = bias.astype(jnp.float32).transpose(2, 0, 1).reshape(A_KV_HEADS, A_REP, BLOCK, 2 * BLOCK)
    kpos = jnp.arange(nb)[:, None, None] * BLOCK - BLOCK + kj[None]
    mask = (dist >= 0) & (dist < WINDOW) & (kpos >= 0)
    s = jnp.where(mask[None, :, None, None], s + bias, -jnp.inf)
    sink = sinks.astype(jnp.float32).reshape(A_KV_HEADS, A_REP)[None, None, :, :, None, None]
    m = jnp.maximum(jnp.max(s, axis=-1, keepdims=True), sink)
    p = jnp.exp(s - m)
    denom = jnp.sum(p, axis=-1, keepdims=True) + jnp.exp(sink - m)
    p = (p / denom).astype(v.dtype)
    o = jnp.einsum('bngrqk,bnkgd->bnqgrd', p, vb)
    return o.reshape(Bsz, S, A_WIDTH)


def causal_block_attention(q, k, v):
    Bsz, S, H, dq = q.shape
    nb = S // BLOCK
    scale = dq ** -0.5
    qblocks = q.reshape(Bsz, nb, BLOCK, H, dq).transpose(1, 0, 2, 3, 4)
    kpos = jnp.arange(S)

    def one_block(args):
        qb, n = args
        s = jnp.einsum('bqhd,bkhd->bhqk', qb, k).astype(jnp.float32) * scale
        qpos = n * BLOCK + jnp.arange(BLOCK)
        s = jnp.where(kpos[None, :] <= qpos[:, None], s, -jnp.inf)
        p = jax.nn.softmax(s, axis=-1).astype(v.dtype)
        return jnp.einsum('bhqk,bkhd->bqhd', p, v)

    o = lax.map(one_block, (qblocks, jnp.arange(nb)))
    return o.transpose(1, 0, 2, 3, 4).reshape(Bsz, S, H * v.shape[-1])


def mixer_block(x, attn_norm, w_in, b_gate, a_q_norm, a_k_norm, a_sinks, rel_table, w_oa,
                q_a_norm, w_qb, kv_a_norm, w_kvb, b_q_norm, b_k_norm, w_ob, w_o, positions):
    Bsz, S, _ = x.shape
    h = rmsnorm(x, attn_norm)
    proj = h @ w_in
    cuts = [int(c) for c in np.cumsum(IN_SIZES)[:-1]]
    qa, ka, va, cq, ckv, kpe, glog = jnp.split(proj, cuts, axis=-1)

    qa = rmsnorm(qa.reshape(Bsz, S, A_HEADS, A_HEAD_DIM), a_q_norm)
    ka = rmsnorm(ka.reshape(Bsz, S, A_KV_HEADS, A_HEAD_DIM), a_k_norm)
    va = va.reshape(Bsz, S, A_KV_HEADS, A_HEAD_DIM)
    ya = sliding_window_attention(qa, ka, va, a_sinks, rel_table) @ w_oa

    qb = (rmsnorm(cq, q_a_norm) @ w_qb).reshape(Bsz, S, B_HEADS, B_QK_DIM)
    kv = (rmsnorm(ckv, kv_a_norm) @ w_kvb).reshape(Bsz, S, B_HEADS, NOPE_DIM + B_V_DIM)
    k_nope, vb = kv[..., :NOPE_DIM], kv[..., NOPE_DIM:]
    k_pe = jnp.broadcast_to(kpe[:, :, None, :], (Bsz, S, B_HEADS, ROPE_DIM))
    kb = jnp.concatenate([k_nope, k_pe], axis=-1)
    qb = rmsnorm(qb, b_q_norm)
    kb = rmsnorm(kb, b_k_norm)
    qb = jnp.concatenate([qb[..., :NOPE_DIM], apply_rope(qb[..., NOPE_DIM:], positions)], axis=-1)
    kb = jnp.concatenate([kb[..., :NOPE_DIM], apply_rope(kb[..., NOPE_DIM:], positions)], axis=-1)
    yb = causal_block_attention(qb, kb, vb) @ w_ob

    gates = jax.nn.sigmoid((glog + b_gate).astype(jnp.float32)).astype(x.dtype)
    ga, gb = gates[..., :D_MODEL], gates[..., D_MODEL:]
    return (ga * ya + gb * yb) @ w_o


def hier_moe(h, w_rg, b_rg, w_re, b_re, w_gate, w_up, w_down):
    Bsz, S, D = h.shape
    T = Bsz * S
    A = T * TOP_K
    xf = h.reshape(T, D)
    g_prob = jax.nn.softmax((xf @ w_rg + b_rg).astype(jnp.float32), axis=-1)
    g_p, g_idx = lax.top_k(g_prob, 1)
    e_logits = (xf @ w_re + b_re).astype(jnp.float32).reshape(T, N_GROUPS, EXPERTS_PER_GROUP)
    e_in_group = e_logits[jnp.arange(T), g_idx[:, 0]]
    e_top, e_loc = lax.top_k(e_in_group, TOP_K)
    weights = g_p * jax.nn.softmax(e_top, axis=-1)
    expert = g_idx * EXPERTS_PER_GROUP + e_loc

    flat_e = expert.reshape(A)
    flat_w = weights.reshape(A)
    flat_t = jnp.repeat(jnp.arange(T, dtype=jnp.int32), TOP_K)
    order = jnp.argsort(flat_e)
    se, st, sw = flat_e[order], flat_t[order], flat_w[order]
    counts = jnp.zeros((N_EXPERTS,), jnp.int32).at[flat_e].add(1)
    starts = jnp.cumsum(counts) - counts
    padded = ((counts + MOE_BLOCK - 1) // MOE_BLOCK) * MOE_BLOCK
    pends = jnp.cumsum(padded)
    pstarts = pends - padded
    dest = pstarts[se] + (jnp.arange(A, dtype=jnp.int32) - starts[se])
    n_rows = A + N_EXPERTS * MOE_BLOCK
    n_blocks = n_rows // MOE_BLOCK
    row_tok = jnp.zeros((n_rows,), jnp.int32).at[dest].set(st)
    row_w = jnp.zeros((n_rows,), jnp.float32).at[dest].set(sw)
    blk_exp = jnp.minimum(jnp.searchsorted(pends, jnp.arange(n_blocks) * MOE_BLOCK, side='right'),
                          N_EXPERTS - 1)
    xr = xf[row_tok].reshape(n_blocks, MOE_BLOCK, D)

    def expert_block(args):
        xb, e = args
        return (jax.nn.silu(xb @ w_gate[e]) * (xb @ w_up[e])) @ w_down[e]

    yr = lax.map(expert_block, (xr, blk_exp)).reshape(n_rows, D)
    y = jax.ops.segment_sum(yr * row_w[:, None].astype(yr.dtype), row_tok, num_segments=T)
    return y.reshape(Bsz, S, D)


def setup_inputs(seed: int = 0) -> dict:
    key = jax.random.key(seed)
    ks = jax.random.split(key, 32)

    def nrm(k, shape, scale):
        return jax.random.normal(k, shape, jnp.float32) * scale

    def gain(k, shape):
        return 1.0 + 0.02 * jax.random.normal(k, shape, jnp.float32)

    L = DEPTH
    return {
        "x": nrm(ks[0], (BATCH, SEQ, D_MODEL), 1.0),
        "attn_norm": gain(ks[1], (L, D_MODEL)),
        "w_in": nrm(ks[2], (L, D_MODEL, IN_COLS), D_MODEL ** -0.5),
        "b_gate": nrm(ks[3], (L, 2 * D_MODEL), 0.02),
        "a_q_norm": gain(ks[4], (L, A_HEAD_DIM)),
        "a_k_norm": gain(ks[5], (L, A_HEAD_DIM)),
        "a_sinks": nrm(ks[6], (L, A_HEADS), 0.5),
        "rel_bias": nrm(ks[7], (REL_BUCKETS, A_HEADS), 0.5),
        "w_oa": nrm(ks[8], (L, A_WIDTH, D_MODEL), A_WIDTH ** -0.5),
        "q_a_norm": gain(ks[9], (L, Q_LORA)),
        "w_qb": nrm(ks[10], (L, Q_LORA, B_HEADS * B_QK_DIM), Q_LORA ** -0.5),
        "kv_a_norm": gain(ks[11], (L, KV_LORA)),
        "w_kvb": nrm(ks[12], (L, KV_LORA, B_HEADS * (NOPE_DIM + B_V_DIM)), KV_LORA ** -0.5),
        "b_q_norm": gain(ks[13], (L, B_QK_DIM)),
        "b_k_norm": gain(ks[14], (L, B_QK_DIM)),
        "w_ob": nrm(ks[15], (L, B_WIDTH, D_MODEL), B_WIDTH ** -0.5),
        "w_o": nrm(ks[16], (L, D_MODEL, D_MODEL), D_MODEL ** -0.5),
        "ffn_norm": gain(ks[17], (L, D_MODEL)),
        "w_router_group": nrm(ks[18], (L, D_MODEL, N_GROUPS), D_MODEL ** -0.5),
        "b_router_group": nrm(ks[19], (L, N_GROUPS), 0.01),
        "w_router_expert": nrm(ks[20], (L, D_MODEL, N_EXPERTS), D_MODEL ** -0.5),
        "b_router_expert": nrm(ks[21], (L, N_EXPERTS), 0.01),
        "w_exp_gate": nrm(ks[22], (L, N_EXPERTS, D_MODEL, D_EXPERT), D_MODEL ** -0.5),
        "w_exp_up": nrm(ks[23], (L, N_EXPERTS, D_MODEL, D_EXPERT), D_MODEL ** -0.5),
        "w_exp_down": nrm(ks[24], (L, N_EXPERTS, D_EXPERT, D_MODEL), D_EXPERT ** -0.5),
    }


def reference(x, attn_norm, w_in, b_gate, a_q_norm, a_k_norm, a_sinks, rel_bias, w_oa,
              q_a_norm, w_qb, kv_a_norm, w_kvb, b_q_norm, b_k_norm, w_ob, w_o,
              ffn_norm, w_router_group, b_router_group, w_router_expert, b_router_expert,
              w_exp_gate, w_exp_up, w_exp_down):
    positions = jnp.arange(x.shape[1], dtype=jnp.int32)
    for l in range(DEPTH):
        x = x + mixer_block(x, attn_norm[l], w_in[l], b_gate[l], a_q_norm[l], a_k_norm[l],
                            a_sinks[l], rel_bias, w_oa[l], q_a_norm[l], w_qb[l], kv_a_norm[l],
                            w_kvb[l], b_q_norm[l], b_k_norm[l], w_ob[l], w_o[l], positions)
        x = x + hier_moe(rmsnorm(x, ffn_norm[l]), w_router_group[l], b_router_group[l],
                         w_router_expert[l], b_router_expert[l],
                         w_exp_gate[l], w_exp_up[l], w_exp_down[l])
    return x
```

```python
import functools
import math

import jax
import jax.numpy as jnp
from jax import lax
from jax.experimental import pallas as pl
from jax.experimental.pallas import tpu as pltpu

F32 = jnp.float32
BF16 = jnp.bfloat16

D_MODEL = 1024
A_HEADS = 8
A_KV_HEADS = 2
A_HEAD_DIM = 64
A_WIDTH = A_HEADS * A_HEAD_DIM
WINDOW = 128
REL_BUCKETS = 32
REL_MAX_DIST = 128
B_HEADS = 8
Q_LORA = 256
KV_LORA = 128
NOPE_DIM = 64
ROPE_DIM = 32
B_QK_DIM = NOPE_DIM + ROPE_DIM
B_V_DIM = 64
B_WIDTH = B_HEADS * B_V_DIM
ROPE_THETA = 10000.0
N_GROUPS = 4
EXPERTS_PER_GROUP = 8
N_EXPERTS = N_GROUPS * EXPERTS_PER_GROUP
TOP_K = 2
D_EXPERT = 256
EPS = 1e-6

LANES = 128
NEG = -1e30
VMEM_LIMIT = 56 * 1024 * 1024

C_QA = 0
C_KA = C_QA + A_WIDTH
C_VA = C_KA + LANES
C_CQ = C_VA + LANES
C_CKV = C_CQ + Q_LORA
C_KPE = C_CKV + KV_LORA
C_KPR = C_KPE + LANES
C_GATE = C_KPR + LANES
C_END = C_GATE + 2 * D_MODEL

PROJ_TM = 512
SWA_TQ = 128
MLA_TQ = 512
MERGE_TM = 512
MOE_ROWS = 128
COMB_TM = 256


def _cparams(sem):
    return pltpu.CompilerParams(dimension_semantics=sem, vmem_limit_bytes=VMEM_LIMIT)


def _proj_kernel(x_ref, gn_ref, w1_ref, wq2_ref, wk2_ref, wv2_ref, gcq_ref, gckv_ref, bg_ref,
                 gqa_ref, gka_ref, aq_ref, bq_ref, ak_ref, bk_ref,
                 qa_ref, ka_ref, va_ref, qb_ref, kb_ref, vb_ref, gate_ref):
    tm = x_ref.shape[0]
    x = x_ref[...]
    h = (x * lax.rsqrt(jnp.mean(x * x, axis=-1, keepdims=True) + EPS) * gn_ref[...]).astype(BF16)

    def proj(a, b):
        return jnp.dot(h, w1_ref[:, a:b], preferred_element_type=F32)

    lo = lax.broadcasted_iota(jnp.int32, (tm, LANES), 1) < A_HEAD_DIM

    def pair_norm(t, gain):
        t2 = t * t
        s_lo = jnp.sum(jnp.where(lo, t2, 0.0), axis=-1, keepdims=True)
        s_hi = jnp.sum(jnp.where(lo, 0.0, t2), axis=-1, keepdims=True)
        r = jnp.where(lo, lax.rsqrt(s_lo / A_HEAD_DIM + EPS), lax.rsqrt(s_hi / A_HEAD_DIM + EPS))
        return (t * r * gain).astype(BF16)

    qa = proj(C_QA, C_KA)
    for c in range(A_WIDTH // LANES):
        sl = slice(c * LANES, (c + 1) * LANES)
        qa_ref[:, sl] = pair_norm(qa[:, sl], gqa_ref[...])
    kvp = proj(C_KA, C_CQ)
    ka_ref[...] = pair_norm(kvp[:, :LANES], gka_ref[...])
    va_ref[...] = kvp[:, LANES:].astype(BF16)

    lat = proj(C_CQ, C_GATE)
    cq = lat[:, :Q_LORA]
    cqn = (cq * lax.rsqrt(jnp.mean(cq * cq, axis=-1, keepdims=True) + EPS) * gcq_ref[...]).astype(BF16)
    ckv = lat[:, Q_LORA:Q_LORA + KV_LORA]
    ckvn = (ckv * lax.rsqrt(jnp.mean(ckv * ckv, axis=-1, keepdims=True) + EPS) * gckv_ref[...]).astype(BF16)
    kpe = lat[:, C_KPE - C_CQ:C_KPR - C_CQ]
    kpr = lat[:, C_KPR - C_CQ:C_GATE - C_CQ]

    aq, bq, ak, bk = aq_ref[...], bq_ref[...], ak_ref[...], bk_ref[...]
    q2 = jnp.dot(cqn, wq2_ref[...], preferred_element_type=F32)
    k2 = jnp.dot(ckvn, wk2_ref[...], preferred_element_type=F32)
    vb_ref[...] = jnp.dot(ckvn, wv2_ref[...], preferred_element_type=F32).astype(BF16)
    kr = kpr * bk
    for hd in range(B_HEADS):
        sl = slice(hd * LANES, (hd + 1) * LANES)
        qp = q2[:, sl]
        qr = q2[:, B_HEADS * LANES + hd * LANES:B_HEADS * LANES + (hd + 1) * LANES]
        rq = lax.rsqrt(jnp.sum(qp * qp, axis=-1, keepdims=True) / B_QK_DIM + EPS)
        qb_ref[:, sl] = (rq * (qp * aq + qr * bq)).astype(BF16)
        kp = k2[:, sl] + kpe
        rk = lax.rsqrt(jnp.sum(kp * kp, axis=-1, keepdims=True) / B_QK_DIM + EPS)
        kb_ref[:, sl] = (rk * (kp * ak + kr)).astype(BF16)

    z = proj(C_GATE, C_END) + bg_ref[...]
    gate_ref[...] = (1.0 / (1.0 + jnp.exp(-z))).astype(BF16)


def _proj_call(x2d, seq, p):
    t = x2d.shape[0]
    tm = PROJ_TM
    nt = t // tm
    npos = seq // tm
    row = lambda w: pl.BlockSpec((tm, w), lambda i: (i, 0))
    full = lambda a: pl.BlockSpec(a.shape, lambda i: (0,) * a.ndim)
    pos = pl.BlockSpec((tm, LANES), lambda i: (i % npos, 0))
    consts = [p["gn"], p["w1"], p["wq2"], p["wk2"], p["wv2"], p["gcq"], p["gckv"], p["bg"], p["gqa"], p["gka"]]
    tabs = [p["aq"], p["bq"], p["ak"], p["bk"]]
    widths = [A_WIDTH, LANES, LANES, B_HEADS * LANES, B_HEADS * LANES, B_WIDTH, 2 * D_MODEL]
    return pl.pallas_call(
        _proj_kernel,
        grid=(nt,),
        in_specs=[row(D_MODEL)] + [full(a) for a in consts] + [pos] * 4,
        out_specs=[row(w) for w in widths],
        out_shape=[jax.ShapeDtypeStruct((t, w), BF16) for w in widths],
        compiler_params=_cparams(("parallel",)),
        name="proj",
    )(x2d, *consts, *tabs)


def _swa_kernel(sink_ref, q_ref, kc_ref, kp_ref, vc_ref, vp_ref, bias_ref, o_ref):
    tq = q_ref.shape[0]
    first = pl.program_id(1) == 0
    k = jnp.concatenate([kp_ref[...], kc_ref[...]], axis=0)
    v = jnp.concatenate([vp_ref[...], vc_ref[...]], axis=0)
    lo = lax.broadcasted_iota(jnp.int32, (tq, LANES), 1) < A_HEAD_DIM
    no_prev = jnp.logical_and(first, lax.broadcasted_iota(jnp.int32, (tq, 2 * tq), 1) < tq)
    for c in range(A_WIDTH // LANES):
        qc = q_ref[:, c * LANES:(c + 1) * LANES]
        outs = []
        for half in range(2):
            hd = c + (A_HEADS // 2) * half
            qm = jnp.where(lo if half == 0 else jnp.logical_not(lo), qc, jnp.zeros_like(qc))
            s = lax.dot_general(qm, k, (((1,), (1,)), ((), ())), preferred_element_type=F32)
            s = jnp.where(no_prev, NEG, s + bias_ref[hd])
            sink = sink_ref[hd]
            m = jnp.maximum(jnp.max(s, axis=-1, keepdims=True), sink)
            e = jnp.exp(s - m)
            denom = jnp.sum(e, axis=-1, keepdims=True) + jnp.exp(sink - m)
            pv = jnp.dot(e.astype(BF16), v, preferred_element_type=F32)
            outs.append(pv * (1.0 / denom))
        o_ref[:, c * LANES:(c + 1) * LANES] = jnp.where(lo, outs[0], outs[1]).astype(BF16)


def _swa_call(qa, ka, va, sinks, bias, bsz, seq):
    t = qa.shape[0]
    tq = SWA_TQ
    nq = seq // tq
    cur = lambda b, i, s: (b * nq + i, 0)
    prev = lambda b, i, s: (b * nq + jnp.maximum(i - 1, 0), 0)
    return pl.pallas_call(
        _swa_kernel,
        grid_spec=pltpu.PrefetchScalarGridSpec(
            num_scalar_prefetch=1,
            grid=(bsz, nq),
            in_specs=[pl.BlockSpec((tq, A_WIDTH), cur),
                      pl.BlockSpec((tq, LANES), cur), pl.BlockSpec((tq, LANES), prev),
                      pl.BlockSpec((tq, LANES), cur), pl.BlockSpec((tq, LANES), prev),
                      pl.BlockSpec(bias.shape, lambda b, i, s: (0, 0, 0))],
            out_specs=pl.BlockSpec((tq, A_WIDTH), cur)),
        out_shape=jax.ShapeDtypeStruct((t, A_WIDTH), BF16),
        compiler_params=_cparams(("parallel", "parallel")),
        name="swa",
    )(sinks, qa, ka, ka, va, va, bias)


def _mla_kernel(q_ref, k_ref, v_ref, o_ref, m_sc, l_sc, acc_sc):
    tq = q_ref.shape[0]
    tk = tq
    qi = pl.program_id(2)
    causal = (lax.broadcasted_iota(jnp.int32, (tq, tk), 1) <= lax.broadcasted_iota(jnp.int32, (tq, tk), 0))
    lo = lax.broadcasted_iota(jnp.int32, (tq, LANES), 1) < B_V_DIM
    for half in range(2):
        hs = slice(half * LANES, (half + 1) * LANES)
        q = q_ref[:, hs]
        m_sc[...] = jnp.full(m_sc.shape, NEG, F32)
        l_sc[...] = jnp.zeros(l_sc.shape, F32)
        acc_sc[...] = jnp.zeros(acc_sc.shape, F32)

        def step(j, masked):
            rows = pl.ds(pl.multiple_of(j * tk, tk), tk)
            s = lax.dot_general(q, k_ref[rows, hs], (((1,), (1,)), ((), ())), preferred_element_type=F32)
            if masked:
                s = jnp.where(causal, s, NEG)
            m_prev = m_sc[...]
            m_new = jnp.maximum(m_prev, jnp.max(s, axis=-1, keepdims=True))
            alpha = jnp.exp(m_prev - m_new)
            e = jnp.exp(s - m_new)
            l_sc[...] = alpha * l_sc[...] + jnp.sum(e, axis=-1, keepdims=True)
            acc_sc[...] = alpha * acc_sc[...] + jnp.dot(e.astype(BF16), v_ref[rows, :], preferred_element_type=F32)
            m_sc[...] = m_new

        def body(j, carry):
            step(j, False)
            return carry

        lax.fori_loop(0, qi, body, 0)
        step(qi, True)
        res = (acc_sc[...] * (1.0 / l_sc[...])).astype(BF16)
        if half == 0:
            o_ref[...] = res
        else:
            o_ref[...] = jnp.where(lo, o_ref[...], res)


def _mla_call(qb, kb, vb, bsz, seq):
    t = qb.shape[0]
    tq = MLA_TQ
    nq = seq // tq
    pairs = B_HEADS // 2
    return pl.pallas_call(
        _mla_kernel,
        grid=(bsz, pairs, nq),
        in_specs=[pl.BlockSpec((tq, 2 * LANES), lambda b, p, i: (b * nq + i, p)),
                  pl.BlockSpec((seq, 2 * LANES), lambda b, p, i: (b, p)),
                  pl.BlockSpec((seq, LANES), lambda b, p, i: (b, p))],
        out_specs=pl.BlockSpec((tq, LANES), lambda b, p, i: (b * nq + i, p)),
        out_shape=jax.ShapeDtypeStruct((t, B_WIDTH), BF16),
        scratch_shapes=[pltpu.VMEM((tq, 1), F32), pltpu.VMEM((tq, 1), F32), pltpu.VMEM((tq, LANES), F32)],
        compiler_params=_cparams(("parallel", "parallel", "arbitrary")),
        name="mla",
    )(qb, kb, vb)


def _merge_kernel(x_ref, oa_ref, ob_ref, gate_ref, woa_ref, wob_ref, wo_ref, gf_ref, wrh_ref, wrl_ref, br_ref,
                  x1_ref, w0_ref, w1_ref, eid_ref):
    tm = x_ref.shape[0]
    ya = jnp.dot(oa_ref[...], woa_ref[...], preferred_element_type=F32)
    yb = jnp.dot(ob_ref[...], wob_ref[...], preferred_element_type=F32)
    g = gate_ref[...].astype(F32)
    mix = (g[:, :D_MODEL] * ya + g[:, D_MODEL:] * yb).astype(BF16)
    x1 = x_ref[...] + jnp.dot(mix, wo_ref[...], preferred_element_type=F32)
    x1_ref[...] = x1

    hn = x1 * lax.rsqrt(jnp.mean(x1 * x1, axis=-1, keepdims=True) + EPS) * gf_ref[...]
    hh = hn.astype(BF16)
    hl = (hn - hh.astype(F32)).astype(BF16)
    wrh = wrh_ref[...]
    logits = (jnp.dot(hh, wrh, preferred_element_type=F32) + jnp.dot(hl, wrh, preferred_element_type=F32)
              + jnp.dot(hh, wrl_ref[...], preferred_element_type=F32) + br_ref[...])

    lane = lax.broadcasted_iota(jnp.int32, (tm, LANES), 1).astype(F32)
    big = float(LANES)
    gmask = lane < N_GROUPS
    gl = jnp.where(gmask, logits, NEG)
    gmax = jnp.max(gl, axis=-1, keepdims=True)
    gsum = jnp.sum(jnp.where(gmask, jnp.exp(gl - gmax), 0.0), axis=-1, keepdims=True)
    g_p = 1.0 / gsum
    g_idx = jnp.min(jnp.where(gl == gmax, lane, big), axis=-1, keepdims=True)
    e_lo = N_GROUPS + EXPERTS_PER_GROUP * g_idx
    emask = jnp.logical_and(lane >= e_lo, lane < e_lo + EXPERTS_PER_GROUP)
    el = jnp.where(emask, logits, NEG)
    t1 = jnp.max(el, axis=-1, keepdims=True)
    i1 = jnp.min(jnp.where(el == t1, lane, big), axis=-1, keepdims=True)
    el2 = jnp.where(lane == i1, NEG, el)
    t2 = jnp.max(el2, axis=-1, keepdims=True)
    i2 = jnp.min(jnp.where(el2 == t2, lane, big), axis=-1, keepdims=True)
    e2 = jnp.exp(t2 - t1)
    w_first = g_p / (1.0 + e2)
    w0_ref[...] = jnp.broadcast_to(w_first, (tm, LANES))
    w1_ref[...] = jnp.broadcast_to(w_first * e2, (tm, LANES))
    ids = jnp.where(lane == 0.0, i1 - N_GROUPS, jnp.where(lane == 1.0, i2 - N_GROUPS, 0.0))
    eid_ref[...] = ids.astype(jnp.int32)


def _merge_call(x2d, oa, ob, gates, p):
    t = x2d.shape[0]
    tm = MERGE_TM
    row = lambda w: pl.BlockSpec((tm, w), lambda i: (i, 0))
    full = lambda a: pl.BlockSpec(a.shape, lambda i: (0,) * a.ndim)
    consts = [p["woa"], p["wob"], p["wo"], p["gf"], p["wrh"], p["wrl"], p["br"]]
    return pl.pallas_call(
        _merge_kernel,
        grid=(t // tm,),
        in_specs=[row(D_MODEL), row(A_WIDTH), row(B_WIDTH), row(2 * D_MODEL)] + [full(a) for a in consts],
        out_specs=[row(D_MODEL), row(LANES), row(LANES), row(LANES)],
        out_shape=[jax.ShapeDtypeStruct((t, D_MODEL), F32), jax.ShapeDtypeStruct((t, LANES), F32),
                   jax.ShapeDtypeStruct((t, LANES), F32), jax.ShapeDtypeStruct((t, LANES), jnp.int32)],
        compiler_params=_cparams(("parallel",)),
        name="merge",
    )(x2d, oa, ob, gates, *consts)


def _rows_copy(src_hbm, row, dst, j, sem):
    return pltpu.make_async_copy(src_hbm.at[pl.ds(row, 1), :], dst.at[pl.ds(j, 1), :], sem)


def _start_rows(src_hbm, idx, base, dst, sem, n):
    def body(j, carry):
        _rows_copy(src_hbm, idx[base + j], dst, j, sem).start()
        return carry
    lax.fori_loop(0, n, body, 0, unroll=8)


WAIT_ROWS = 128


def _wait_rows(src_hbm, dst, sem, n):
    for c in range(n // WAIT_ROWS):
        rows = pl.ds(c * WAIT_ROWS, WAIT_ROWS)
        pltpu.make_async_copy(src_hbm.at[rows, :], dst.at[rows, :], sem).wait()


def _gather_step(i, n, src_hbm, idx_hbm, buf, idx, sem_rows, sem_idx):
    rows = buf.shape[1]
    slot = i % 2
    nslot = 1 - slot

    def idx_copy(b, s):
        return pltpu.make_async_copy(idx_hbm.at[pl.ds(b * rows, rows)], idx.at[pl.ds(s * rows, rows)], sem_idx.at[s])

    @pl.when(jnp.logical_and(i == 0, n > 0))
    def _():
        idx_copy(0, 0).start()
        idx_copy(0, 0).wait()
        _start_rows(src_hbm, idx, 0, buf.at[0], sem_rows.at[0], rows)

        @pl.when(n > 1)
        def _():
            idx_copy(1, 1).start()

    @pl.when(i + 1 < n)
    def _():
        idx_copy(i + 1, nslot).wait()
        _start_rows(src_hbm, idx, nslot * rows, buf.at[nslot], sem_rows.at[nslot], rows)

    @pl.when(i + 2 < n)
    def _():
        idx_copy(i + 2, slot).start()

    @pl.when(i < n)
    def _():
        _wait_rows(src_hbm, buf.at[slot], sem_rows.at[slot], rows)
    return slot


def _expert_kernel(blk_exp_ref, nblk_ref, x_hbm, tok_hbm, gf_ref, wgu_ref, wd_ref, y_ref,
                   buf, idx, sem_rows, sem_idx):
    del blk_exp_ref
    i = pl.program_id(0)
    n = nblk_ref[0]
    slot = _gather_step(i, n, x_hbm, tok_hbm, buf, idx, sem_rows, sem_idx)

    @pl.when(i < n)
    def _():
        x = buf[slot]
        hn = (x * lax.rsqrt(jnp.mean(x * x, axis=-1, keepdims=True) + EPS) * gf_ref[...]).astype(BF16)
        gu = jnp.dot(hn, wgu_ref[0], preferred_element_type=F32)
        gt, up = gu[:, :D_EXPERT], gu[:, D_EXPERT:]
        act = (gt * (1.0 / (1.0 + jnp.exp(-gt))) * up).astype(BF16)
        y_ref[...] = jnp.dot(act, wd_ref[0], preferred_element_type=F32)

    @pl.when(i >= n)
    def _():
        y_ref[...] = jnp.zeros(y_ref.shape, F32)


def _expert_call(x1, row_tok, rows, blk_exp, nblk, gf, wgu, wd):
    nb = row_tok.shape[0] // rows
    return pl.pallas_call(
        _expert_kernel,
        grid_spec=pltpu.PrefetchScalarGridSpec(
            num_scalar_prefetch=2,
            grid=(nb,),
            in_specs=[pl.BlockSpec(memory_space=pl.ANY), pl.BlockSpec(memory_space=pl.ANY),
                      pl.BlockSpec(gf.shape, lambda i, be, n: (0, 0)),
                      pl.BlockSpec((1, D_MODEL, 2 * D_EXPERT), lambda i, be, n: (be[i], 0, 0)),
                      pl.BlockSpec((1, D_EXPERT, D_MODEL), lambda i, be, n: (be[i], 0, 0))],
            out_specs=pl.BlockSpec((rows, D_MODEL), lambda i, be, n: (i, 0)),
            scratch_shapes=[pltpu.VMEM((2, rows, D_MODEL), F32), pltpu.SMEM((2 * rows,), jnp.int32),
                            pltpu.SemaphoreType.DMA((2,)), pltpu.SemaphoreType.DMA((2,))]),
        out_shape=jax.ShapeDtypeStruct((nb * rows, D_MODEL), F32),
        compiler_params=_cparams(("arbitrary",)),
        name="experts",
    )(blk_exp, nblk, x1, row_tok, gf, wgu, wd)


def _combine_kernel(x1_ref, w0_ref, w1_ref, y_hbm, pos_hbm, o_ref, buf, idx, sem_rows, sem_idx):
    rows = buf.shape[1]
    tm = rows // 2
    slot = _gather_step(pl.program_id(0), pl.num_programs(0), y_hbm, pos_hbm, buf, idx, sem_rows, sem_idx)
    w0 = w0_ref[...]
    w1 = w1_ref[...]
    for c in range(D_MODEL // LANES):
        sl = slice(c * LANES, (c + 1) * LANES)
        o_ref[:, sl] = x1_ref[:, sl] + w0 * buf[slot, 0:tm, sl] + w1 * buf[slot, tm:rows, sl]


def _combine_call(x1, w0, w1, yr, pos_tiles, tm):
    t = x1.shape[0]
    rows = TOP_K * tm
    row = lambda w: pl.BlockSpec((tm, w), lambda i: (i, 0))
    return pl.pallas_call(
        _combine_kernel,
        grid=(t // tm,),
        in_specs=[row(D_MODEL), row(LANES), row(LANES),
                  pl.BlockSpec(memory_space=pl.ANY), pl.BlockSpec(memory_space=pl.ANY)],
        out_specs=row(D_MODEL),
        out_shape=jax.ShapeDtypeStruct((t, D_MODEL), F32),
        scratch_shapes=[pltpu.VMEM((2, rows, D_MODEL), F32), pltpu.SMEM((2 * rows,), jnp.int32),
                        pltpu.SemaphoreType.DMA((2,)), pltpu.SemaphoreType.DMA((2,))],
        compiler_params=_cparams(("arbitrary",)),
        name="combine",
    )(x1, w0, w1, yr, pos_tiles)


def _rot_cols(w):
    half = ROPE_DIM // 2
    return jnp.concatenate([-w[..., half:], w[..., :half]], axis=-1)


def _head_pad(nope, rope):
    z = jnp.zeros(rope.shape[:-1] + (LANES - B_QK_DIM,), rope.dtype)
    out = jnp.concatenate([nope, rope, z], axis=-1)
    return out.reshape(out.shape[:-2] + (out.shape[-2] * LANES,))


def _t5_bucket(dist):
    max_exact = REL_BUCKETS // 2
    d = jnp.maximum(dist, 0)
    ratio = jnp.maximum(d, 1).astype(F32) / max_exact
    large = max_exact + (jnp.log(ratio) / math.log(REL_MAX_DIST / max_exact)
                         * (REL_BUCKETS - max_exact)).astype(jnp.int32)
    large = jnp.minimum(large, REL_BUCKETS - 1)
    return jnp.where(d < max_exact, d, large)


def _swa_bias(rel_bias, tq):
    qi = jnp.arange(tq)[:, None]
    kj = jnp.arange(2 * tq)[None, :]
    dist = qi + tq - kj
    bias = rel_bias[_t5_bucket(dist)].astype(F32).transpose(2, 0, 1)
    mask = (dist >= 0) & (dist < WINDOW)
    return jnp.where(mask[None], bias, NEG)


def _rope_tables(seq):
    half = ROPE_DIM // 2
    inv_freq = ROPE_THETA ** (-jnp.arange(half, dtype=F32) / half)
    ang = jnp.arange(seq, dtype=F32)[:, None] * inv_freq[None, :]
    cos = jnp.concatenate([jnp.cos(ang), jnp.cos(ang)], axis=-1)
    sin = jnp.concatenate([jnp.sin(ang), jnp.sin(ang)], axis=-1)
    return cos, sin


def _lane_table(nope_part, rope_part):
    seq = rope_part.shape[0]
    nope_part = jnp.broadcast_to(nope_part, (seq, NOPE_DIM))
    return jnp.concatenate([nope_part, rope_part, jnp.zeros((seq, LANES - B_QK_DIM), F32)], axis=-1)


def _prep_layer(l, seq, cos, sin, attn_norm, w_in, b_gate, a_q_norm, a_k_norm, w_oa, q_a_norm, w_qb, kv_a_norm,
                w_kvb, b_q_norm, b_k_norm, w_ob, w_o, ffn_norm, w_rg, b_rg, w_re, b_re, w_eg, w_eu, w_ed):
    w = w_in[l]
    o = 0
    parts = []
    for sz in (A_WIDTH, A_KV_HEADS * A_HEAD_DIM, A_KV_HEADS * A_HEAD_DIM, Q_LORA, KV_LORA, ROPE_DIM, 2 * D_MODEL):
        parts.append(w[:, o:o + sz])
        o += sz
    wqa, wka, wva, wcq, wckv, wkpe, wg = parts
    perm = jnp.array([c + (A_HEADS // 2) * h for c in range(A_HEADS // 2) for h in range(2)])
    wqa = wqa.reshape(D_MODEL, A_HEADS, A_HEAD_DIM)[:, perm].reshape(D_MODEL, A_WIDTH)
    zl = jnp.zeros((D_MODEL, NOPE_DIM), F32)
    zr = jnp.zeros((D_MODEL, LANES - B_QK_DIM), F32)
    w1 = jnp.concatenate([wqa, wka, wva, wcq, wckv, zl, wkpe, zr, zl, _rot_cols(wkpe), zr, wg], axis=1).astype(BF16)

    wq = w_qb[l].reshape(Q_LORA, B_HEADS, B_QK_DIM)
    zq = jnp.zeros((Q_LORA, B_HEADS, NOPE_DIM), F32)
    wq2 = jnp.concatenate([_head_pad(wq[..., :NOPE_DIM], wq[..., NOPE_DIM:]),
                           _head_pad(zq, _rot_cols(wq[..., NOPE_DIM:]))], axis=1).astype(BF16)
    wkv = w_kvb[l].reshape(KV_LORA, B_HEADS, NOPE_DIM + B_V_DIM)
    wk2 = _head_pad(wkv[..., :NOPE_DIM], jnp.zeros((KV_LORA, B_HEADS, ROPE_DIM), F32)).astype(BF16)
    wv2 = wkv[..., NOPE_DIM:].reshape(KV_LORA, B_WIDTH).astype(BF16)

    def rope_tabs(gain, scale):
        gn, gr = gain[:NOPE_DIM], gain[NOPE_DIM:]
        gr_rot = jnp.concatenate([gr[ROPE_DIM // 2:], gr[:ROPE_DIM // 2]])
        a = _lane_table(gn[None, :] * scale, gr[None, :] * cos * scale)
        b = _lane_table(jnp.zeros((1, NOPE_DIM), F32), gr_rot[None, :] * sin * scale)
        return a, b

    aq, bq = rope_tabs(b_q_norm[l], B_QK_DIM ** -0.5)
    ak, bk = rope_tabs(b_k_norm[l], 1.0)

    gqa = (jnp.tile(a_q_norm[l], 2) * A_HEAD_DIM ** -0.5)[None, :]
    gka = jnp.tile(a_k_norm[l], 2)[None, :]
    woa = w_oa[l].reshape(A_HEADS, A_HEAD_DIM, D_MODEL)[perm].reshape(A_WIDTH, D_MODEL).astype(BF16)

    wr = jnp.concatenate([w_rg[l], w_re[l], jnp.zeros((D_MODEL, LANES - N_GROUPS - N_EXPERTS), F32)], axis=1)
    wrh = wr.astype(BF16)
    wrl = (wr - wrh.astype(F32)).astype(BF16)
    br = jnp.concatenate([b_rg[l], b_re[l], jnp.zeros((LANES - N_GROUPS - N_EXPERTS,), F32)])[None, :]

    return dict(
        gn=attn_norm[l][None, :], w1=w1, wq2=wq2, wk2=wk2, wv2=wv2, gcq=q_a_norm[l][None, :],
        gckv=kv_a_norm[l][None, :], bg=b_gate[l][None, :], gqa=gqa, gka=gka, aq=aq, bq=bq, ak=ak, bk=bk,
        woa=woa, wob=w_ob[l].astype(BF16), wo=w_o[l].astype(BF16), gf=ffn_norm[l][None, :], wrh=wrh, wrl=wrl, br=br,
        wgu=jnp.concatenate([w_eg[l], w_eu[l]], axis=-1).astype(BF16), wd=w_ed[l].astype(BF16))


def _dispatch(eid, rows):
    t = eid.shape[0]
    a = t * TOP_K
    flat_e = eid.reshape(a)
    onehot = (flat_e[:, None] == jnp.arange(N_EXPERTS, dtype=jnp.int32)[None, :]).astype(jnp.int32)
    csum = jnp.cumsum(onehot, axis=0)
    counts = csum[-1]
    rank = jnp.sum(onehot * csum, axis=1) - 1
    padded = ((counts + rows - 1) // rows) * rows
    pends = jnp.cumsum(padded)
    pstarts = pends - padded
    dest = jnp.sum(onehot * pstarts[None, :], axis=1) + rank
    n_rows = a + N_EXPERTS * rows
    n_blocks = n_rows // rows
    flat_t = jnp.arange(a, dtype=jnp.int32) // TOP_K
    row_tok = jnp.zeros((n_rows,), jnp.int32).at[dest].set(flat_t, unique_indices=True)
    blk_exp = jnp.minimum(jnp.searchsorted(pends, jnp.arange(n_blocks, dtype=jnp.int32) * rows, side="right"),
                          N_EXPERTS - 1).astype(jnp.int32)
    nblk = (pends[-1] // rows).astype(jnp.int32).reshape(1)
    return row_tok, blk_exp, nblk, dest.reshape(t, TOP_K)


def _layer(x2d, bsz, seq, p, sinks, bias):
    t = x2d.shape[0]
    qa, ka, va, qb, kb, vb, gates = _proj_call(x2d, seq, p)
    oa = _swa_call(qa, ka, va, sinks, bias, bsz, seq)
    ob = _mla_call(qb, kb, vb, bsz, seq)
    x1, w0, w1, eid = _merge_call(x2d, oa, ob, gates, p)
    row_tok, blk_exp, nblk, dest = _dispatch(eid[:, :TOP_K], MOE_ROWS)
    yr = _expert_call(x1, row_tok, MOE_ROWS, blk_exp, nblk, p["gf"], p["wgu"], p["wd"])
    pos_tiles = dest.reshape(t // COMB_TM, COMB_TM, TOP_K).transpose(0, 2, 1).reshape(t * TOP_K)
    return _combine_call(x1, w0, w1, yr, pos_tiles, COMB_TM)


def kernel(x, attn_norm, w_in, b_gate, a_q_norm, a_k_norm, a_sinks, rel_bias, w_oa, q_a_norm, w_qb, kv_a_norm, w_kvb, b_q_norm, b_k_norm, w_ob, w_o, ffn_norm, w_router_group, b_router_group, w_router_expert, b_router_expert, w_exp_gate, w_exp_up, w_exp_down):
    bsz, seq, d = x.shape
    assert d == D_MODEL and seq % MLA_TQ == 0 and seq % PROJ_TM == 0 and (bsz * seq) % COMB_TM == 0
    depth = w_in.shape[0]
    cos, sin = _rope_tables(seq)
    bias = _swa_bias(rel_bias, SWA_TQ)
    x2d = x.reshape(bsz * seq, d)
    for l in range(depth):
        p = _prep_layer(l, seq, cos, sin, attn_norm, w_in, b_gate, a_q_norm, a_k_norm, w_oa, q_a_norm, w_qb,
                        kv_a_norm, w_kvb, b_q_norm, b_k_norm, w_ob, w_o, ffn_norm, w_router_group,
                        b_router_group, w_router_expert, b_router_expert, w_exp_gate, w_exp_up, w_exp_down)
        x2d = _layer(x2d, bsz, seq, p, a_sinks[l], bias)
    return x2d.reshape(bsz, seq, d)
```

```python
import functools
import math

import jax
import jax.numpy as jnp
from jax import lax
from jax.experimental import pallas as pl
from jax.experimental.pallas import tpu as pltpu

F32 = jnp.float32
BF16 = jnp.bfloat16

D_MODEL = 1024
A_HEADS = 8
A_KV_HEADS = 2
A_HEAD_DIM = 64
A_WIDTH = A_HEADS * A_HEAD_DIM
WINDOW = 128
REL_BUCKETS = 32
REL_MAX_DIST = 128
B_HEADS = 8
Q_LORA = 256
KV_LORA = 128
NOPE_DIM = 64
ROPE_DIM = 32
B_QK_DIM = NOPE_DIM + ROPE_DIM
B_V_DIM = 64
B_WIDTH = B_HEADS * B_V_DIM
ROPE_THETA = 10000.0
N_GROUPS = 4
EXPERTS_PER_GROUP = 8
N_EXPERTS = N_GROUPS * EXPERTS_PER_GROUP
TOP_K = 2
D_EXPERT = 256
EPS = 1e-6

LANES = 128
NEG = -1e30
LOG2E = math.log2(math.e)
V_ONE_EVEN = B_V_DIM
V_ONE_ODD = 0
EXP2_SAFE_BOUND = 100.0
VMEM_LIMIT = 56 * 1024 * 1024

C_QA = 0
C_KA = C_QA + A_WIDTH
C_VA = C_KA + LANES
C_CQ = C_VA + LANES
C_CKV = C_CQ + Q_LORA
C_KPE = C_CKV + KV_LORA
C_KPR = C_KPE + LANES
C_GATE = C_KPR + LANES
C_END = C_GATE + 2 * D_MODEL

PROJ_TM = 512
SWA_TQ = 128
MLA_TQ = 512
MERGE_TM = 512
MOE_ROWS = 128
COMB_TM = 256


def _cparams(sem):
    return pltpu.CompilerParams(dimension_semantics=sem, vmem_limit_bytes=VMEM_LIMIT)


def _proj_kernel(x_ref, gn_ref, w1_ref, wq2_ref, wk2_ref, wv2_ref, vone_ref, gcq_ref, gckv_ref, bg_ref,
                 gqa_ref, gka_ref, aq_ref, bq_ref, ak_ref, bk_ref,
                 qa_ref, ka_ref, va_ref, qb_ref, kb_ref, vb_ref, gate_ref):
    tm = x_ref.shape[0]
    x = x_ref[...]
    h = (x * lax.rsqrt(jnp.mean(x * x, axis=-1, keepdims=True) + EPS) * gn_ref[...]).astype(BF16)

    def proj(a, b):
        return jnp.dot(h, w1_ref[:, a:b], preferred_element_type=F32)

    lo = lax.broadcasted_iota(jnp.int32, (tm, LANES), 1) < A_HEAD_DIM

    def pair_norm(t, gain):
        t2 = t * t
        s_lo = jnp.sum(jnp.where(lo, t2, 0.0), axis=-1, keepdims=True)
        s_hi = jnp.sum(jnp.where(lo, 0.0, t2), axis=-1, keepdims=True)
        r = jnp.where(lo, lax.rsqrt(s_lo / A_HEAD_DIM + EPS), lax.rsqrt(s_hi / A_HEAD_DIM + EPS))
        return (t * r * gain).astype(BF16)

    qa = proj(C_QA, C_KA)
    for c in range(A_WIDTH // LANES):
        sl = slice(c * LANES, (c + 1) * LANES)
        qa_ref[:, sl] = pair_norm(qa[:, sl], gqa_ref[...])
    kvp = proj(C_KA, C_CQ)
    ka_ref[...] = pair_norm(kvp[:, :LANES], gka_ref[...])
    va_ref[...] = kvp[:, LANES:].astype(BF16)

    lat = proj(C_CQ, C_GATE)
    cq = lat[:, :Q_LORA]
    cqn = (cq * lax.rsqrt(jnp.mean(cq * cq, axis=-1, keepdims=True) + EPS) * gcq_ref[...]).astype(BF16)
    ckv = lat[:, Q_LORA:Q_LORA + KV_LORA]
    ckvn = (ckv * lax.rsqrt(jnp.mean(ckv * ckv, axis=-1, keepdims=True) + EPS) * gckv_ref[...]).astype(BF16)
    kpe = lat[:, C_KPE - C_CQ:C_KPR - C_CQ]
    kpr = lat[:, C_KPR - C_CQ:C_GATE - C_CQ]

    aq, bq, ak, bk = aq_ref[...], bq_ref[...], ak_ref[...], bk_ref[...]
    q2 = jnp.dot(cqn, wq2_ref[...], preferred_element_type=F32)
    k2 = jnp.dot(ckvn, wk2_ref[...], preferred_element_type=F32)
    vb_ref[...] = (jnp.dot(ckvn, wv2_ref[...], preferred_element_type=F32) + vone_ref[...]).astype(BF16)
    kr = kpr * bk
    for hd in range(B_HEADS):
        sl = slice(hd * LANES, (hd + 1) * LANES)
        qp = q2[:, sl]
        qr = q2[:, B_HEADS * LANES + hd * LANES:B_HEADS * LANES + (hd + 1) * LANES]
        rq = lax.rsqrt(jnp.sum(qp * qp, axis=-1, keepdims=True) / B_QK_DIM + EPS)
        qb_ref[:, sl] = (rq * (qp * aq + qr * bq)).astype(BF16)
        kp = k2[:, sl] + kpe
        rk = lax.rsqrt(jnp.sum(kp * kp, axis=-1, keepdims=True) / B_QK_DIM + EPS)
        kb_ref[:, sl] = (rk * (kp * ak + kr)).astype(BF16)

    z = proj(C_GATE, C_END) + bg_ref[...]
    gate_ref[...] = (1.0 / (1.0 + jnp.exp(-z))).astype(BF16)


def _proj_call(x2d, seq, p):
    t = x2d.shape[0]
    tm = PROJ_TM
    nt = t // tm
    npos = seq // tm
    row = lambda w: pl.BlockSpec((tm, w), lambda i: (i, 0))
    full = lambda a: pl.BlockSpec(a.shape, lambda i: (0,) * a.ndim)
    pos = pl.BlockSpec((tm, LANES), lambda i: (i % npos, 0))
    consts = [p["gn"], p["w1"], p["wq2"], p["wk2"], p["wv2"], p["vone"], p["gcq"], p["gckv"], p["bg"], p["gqa"],
              p["gka"]]
    tabs = [p["aq"], p["bq"], p["ak"], p["bk"]]
    widths = [A_WIDTH, LANES, LANES, B_HEADS * LANES, B_HEADS * LANES, B_HEADS * LANES, 2 * D_MODEL]
    return pl.pallas_call(
        _proj_kernel,
        grid=(nt,),
        in_specs=[row(D_MODEL)] + [full(a) for a in consts] + [pos] * 4,
        out_specs=[row(w) for w in widths],
        out_shape=[jax.ShapeDtypeStruct((t, w), BF16) for w in widths],
        compiler_params=_cparams(("parallel",)),
        name="proj",
    )(x2d, *consts, *tabs)


def _swa_kernel(sink_ref, q_ref, kc_ref, kp_ref, vc_ref, vp_ref, bias_ref, o_ref):
    tq = q_ref.shape[0]
    first = pl.program_id(1) == 0
    k = jnp.concatenate([kp_ref[...], kc_ref[...]], axis=0)
    v = jnp.concatenate([vp_ref[...], vc_ref[...]], axis=0)
    lo = lax.broadcasted_iota(jnp.int32, (tq, LANES), 1) < A_HEAD_DIM
    no_prev = jnp.logical_and(first, lax.broadcasted_iota(jnp.int32, (tq, 2 * tq), 1) < tq)
    for c in range(A_WIDTH // LANES):
        qc = q_ref[:, c * LANES:(c + 1) * LANES]
        outs = []
        for half in range(2):
            hd = c + (A_HEADS // 2) * half
            qm = jnp.where(lo if half == 0 else jnp.logical_not(lo), qc, jnp.zeros_like(qc))
            s = lax.dot_general(qm, k, (((1,), (1,)), ((), ())), preferred_element_type=F32)
            s = jnp.where(no_prev, NEG, s + bias_ref[hd])
            sink = sink_ref[hd]
            m = jnp.maximum(jnp.max(s, axis=-1, keepdims=True), sink)
            e = jnp.exp(s - m)
            denom = jnp.sum(e, axis=-1, keepdims=True) + jnp.exp(sink - m)
            pv = jnp.dot(e.astype(BF16), v, preferred_element_type=F32)
            outs.append(pv * (1.0 / denom))
        o_ref[:, c * LANES:(c + 1) * LANES] = jnp.where(lo, outs[0], outs[1]).astype(BF16)


def _swa_call(qa, ka, va, sinks, bias, bsz, seq):
    t = qa.shape[0]
    tq = SWA_TQ
    nq = seq // tq
    cur = lambda b, i, s: (b * nq + i, 0)
    prev = lambda b, i, s: (b * nq + jnp.maximum(i - 1, 0), 0)
    return pl.pallas_call(
        _swa_kernel,
        grid_spec=pltpu.PrefetchScalarGridSpec(
            num_scalar_prefetch=1,
            grid=(bsz, nq),
            in_specs=[pl.BlockSpec((tq, A_WIDTH), cur),
                      pl.BlockSpec((tq, LANES), cur), pl.BlockSpec((tq, LANES), prev),
                      pl.BlockSpec((tq, LANES), cur), pl.BlockSpec((tq, LANES), prev),
                      pl.BlockSpec(bias.shape, lambda b, i, s: (0, 0, 0))],
            out_specs=pl.BlockSpec((tq, A_WIDTH), cur)),
        out_shape=jax.ShapeDtypeStruct((t, A_WIDTH), BF16),
        compiler_params=_cparams(("parallel", "parallel")),
        name="swa",
    )(sinks, qa, ka, ka, va, va, bias)


def _mla_kernel(q_ref, k_ref, v_ref, o_ref, acc_sc, m_sc, *, shift):
    tq = q_ref.shape[0]
    tk = tq
    qi = pl.program_id(2)
    causal = (lax.broadcasted_iota(jnp.int32, (tq, tk), 1) <= lax.broadcasted_iota(jnp.int32, (tq, tk), 0))
    acc_sc[...] = jnp.zeros(acc_sc.shape, F32)
    if shift:
        m_sc[...] = jnp.full(m_sc.shape, NEG, F32)

    def step(j, masked):
        rows = pl.ds(pl.multiple_of(j * tk, tk), tk)
        for half in range(2):
            hs = slice(half * LANES, (half + 1) * LANES)
            s = lax.dot_general(q_ref[:, hs], k_ref[rows, hs], (((1,), (1,)), ((), ())), preferred_element_type=F32)
            if shift:
                if masked:
                    s = jnp.where(causal, s, NEG)
                m_prev = m_sc[half]
                m_new = jnp.maximum(m_prev, jnp.max(s, axis=-1, keepdims=True))
                m_sc[half] = m_new
                e = jnp.exp2(s - m_new)
                acc_sc[half] = jnp.exp2(m_prev - m_new) * acc_sc[half]
            else:
                e = jnp.exp2(s)
                if masked:
                    e = jnp.where(causal, e, 0.0)
            acc_sc[half] += jnp.dot(e.astype(BF16), v_ref[rows, hs], preferred_element_type=F32)

    def body(j, carry):
        step(j, False)
        return carry

    lax.fori_loop(0, qi, body, 0)
    step(qi, True)
    a0 = acc_sc[0]
    a1 = acc_sc[1]
    lo = lax.broadcasted_iota(jnp.int32, (tq, LANES), 1) < B_V_DIM
    o_ref[...] = jnp.where(lo, a0 * (1.0 / a0[:, V_ONE_EVEN:V_ONE_EVEN + 1]),
                           a1 * (1.0 / a1[:, V_ONE_ODD:V_ONE_ODD + 1])).astype(BF16)


def _mla_call(qb, kb, vb, bsz, seq, score_bound):
    t = qb.shape[0]
    tq = MLA_TQ
    nq = seq // tq
    pairs = B_HEADS // 2

    def call(shift):
        return pl.pallas_call(
            functools.partial(_mla_kernel, shift=shift),
            grid=(bsz, pairs, nq),
            in_specs=[pl.BlockSpec((tq, 2 * LANES), lambda b, p, i: (b * nq + i, p)),
                      pl.BlockSpec((seq, 2 * LANES), lambda b, p, i: (b, p)),
                      pl.BlockSpec((seq, 2 * LANES), lambda b, p, i: (b, p))],
            out_specs=pl.BlockSpec((tq, LANES), lambda b, p, i: (b * nq + i, p)),
            out_shape=jax.ShapeDtypeStruct((t, B_WIDTH), BF16),
            scratch_shapes=[pltpu.VMEM((2, tq, LANES), F32), pltpu.VMEM((2, tq, 1), F32)],
            compiler_params=_cparams(("parallel", "parallel", "arbitrary")),
            name="mla_shifted" if shift else "mla",
        )(qb, kb, vb)

    return lax.cond(score_bound <= EXP2_SAFE_BOUND, lambda: call(False), lambda: call(True))


def _merge_kernel(x_ref, oa_ref, ob_ref, gate_ref, woa_ref, wob_ref, wo_ref, gf_ref, wrh_ref, wrl_ref, br_ref,
                  x1_ref, w0_ref, w1_ref, eid_ref):
    tm = x_ref.shape[0]
    ya = jnp.dot(oa_ref[...], woa_ref[...], preferred_element_type=F32)
    yb = jnp.dot(ob_ref[...], wob_ref[...], preferred_element_type=F32)
    g = gate_ref[...].astype(F32)
    mix = (g[:, :D_MODEL] * ya + g[:, D_MODEL:] * yb).astype(BF16)
    x1 = x_ref[...] + jnp.dot(mix, wo_ref[...], preferred_element_type=F32)
    x1_ref[...] = x1

    hn = x1 * lax.rsqrt(jnp.mean(x1 * x1, axis=-1, keepdims=True) + EPS) * gf_ref[...]
    hh = hn.astype(BF16)
    hl = (hn - hh.astype(F32)).astype(BF16)
    wrh = wrh_ref[...]
    logits = (jnp.dot(hh, wrh, preferred_element_type=F32) + jnp.dot(hl, wrh, preferred_element_type=F32)
              + jnp.dot(hh, wrl_ref[...], preferred_element_type=F32) + br_ref[...])

    lane = lax.broadcasted_iota(jnp.int32, (tm, LANES), 1).astype(F32)
    big = float(LANES)
    gmask = lane < N_GROUPS
    gl = jnp.where(gmask, logits, NEG)
    gmax = jnp.max(gl, axis=-1, keepdims=True)
    gsum = jnp.sum(jnp.where(gmask, jnp.exp(gl - gmax), 0.0), axis=-1, keepdims=True)
    g_p = 1.0 / gsum
    g_idx = jnp.min(jnp.where(gl == gmax, lane, big), axis=-1, keepdims=True)
    e_lo = N_GROUPS + EXPERTS_PER_GROUP * g_idx
    emask = jnp.logical_and(lane >= e_lo, lane < e_lo + EXPERTS_PER_GROUP)
    el = jnp.where(emask, logits, NEG)
    t1 = jnp.max(el, axis=-1, keepdims=True)
    i1 = jnp.min(jnp.where(el == t1, lane, big), axis=-1, keepdims=True)
    el2 = jnp.where(lane == i1, NEG, el)
    t2 = jnp.max(el2, axis=-1, keepdims=True)
    i2 = jnp.min(jnp.where(el2 == t2, lane, big), axis=-1, keepdims=True)
    e2 = jnp.exp(t2 - t1)
    w_first = g_p / (1.0 + e2)
    w0_ref[...] = jnp.broadcast_to(w_first, (tm, LANES))
    w1_ref[...] = jnp.broadcast_to(w_first * e2, (tm, LANES))
    ids = jnp.where(lane == 0.0, i1 - N_GROUPS, jnp.where(lane == 1.0, i2 - N_GROUPS, 0.0))
    eid_ref[...] = ids.astype(jnp.int32)


def _merge_call(x2d, oa, ob, gates, p):
    t = x2d.shape[0]
    tm = MERGE_TM
    row = lambda w: pl.BlockSpec((tm, w), lambda i: (i, 0))
    full = lambda a: pl.BlockSpec(a.shape, lambda i: (0,) * a.ndim)
    consts = [p["woa"], p["wob"], p["wo"], p["gf"], p["wrh"], p["wrl"], p["br"]]
    return pl.pallas_call(
        _merge_kernel,
        grid=(t // tm,),
        in_specs=[row(D_MODEL), row(A_WIDTH), row(B_WIDTH), row(2 * D_MODEL)] + [full(a) for a in consts],
        out_specs=[row(D_MODEL), row(LANES), row(LANES), row(LANES)],
        out_shape=[jax.ShapeDtypeStruct((t, D_MODEL), F32), jax.ShapeDtypeStruct((t, LANES), F32),
                   jax.ShapeDtypeStruct((t, LANES), F32), jax.ShapeDtypeStruct((t, LANES), jnp.int32)],
        compiler_params=_cparams(("parallel",)),
        name="merge",
    )(x2d, oa, ob, gates, *consts)


def _rows_copy(src_hbm, row, dst, j, sem):
    return pltpu.make_async_copy(src_hbm.at[pl.ds(row, 1), :], dst.at[pl.ds(j, 1), :], sem)


def _start_rows(src_hbm, idx, base, dst, sem, n):
    def body(j, carry):
        _rows_copy(src_hbm, idx[base + j], dst, j, sem).start()
        return carry
    lax.fori_loop(0, n, body, 0, unroll=8)


WAIT_ROWS = 128


def _wait_rows(src_hbm, dst, sem, n):
    for c in range(n // WAIT_ROWS):
        rows = pl.ds(c * WAIT_ROWS, WAIT_ROWS)
        pltpu.make_async_copy(src_hbm.at[rows, :], dst.at[rows, :], sem).wait()


def _gather_step(i, n, src_hbm, idx_hbm, buf, idx, sem_rows, sem_idx):
    rows = buf.shape[1]
    slot = i % 2
    nslot = 1 - slot

    def idx_copy(b, s):
        return pltpu.make_async_copy(idx_hbm.at[pl.ds(b * rows, rows)], idx.at[pl.ds(s * rows, rows)], sem_idx.at[s])

    @pl.when(jnp.logical_and(i == 0, n > 0))
    def _():
        idx_copy(0, 0).start()
        idx_copy(0, 0).wait()
        _start_rows(src_hbm, idx, 0, buf.at[0], sem_rows.at[0], rows)

        @pl.when(n > 1)
        def _():
            idx_copy(1, 1).start()

    @pl.when(i + 1 < n)
    def _():
        idx_copy(i + 1, nslot).wait()
        _start_rows(src_hbm, idx, nslot * rows, buf.at[nslot], sem_rows.at[nslot], rows)

    @pl.when(i + 2 < n)
    def _():
        idx_copy(i + 2, slot).start()

    @pl.when(i < n)
    def _():
        _wait_rows(src_hbm, buf.at[slot], sem_rows.at[slot], rows)
    return slot


def _expert_kernel(blk_exp_ref, nblk_ref, x_hbm, tok_hbm, gf_ref, wgu_ref, wd_ref, y_ref,
                   buf, idx, sem_rows, sem_idx):
    del blk_exp_ref
    i = pl.program_id(0)
    n = nblk_ref[0]
    slot = _gather_step(i, n, x_hbm, tok_hbm, buf, idx, sem_rows, sem_idx)

    @pl.when(i < n)
    def _():
        x = buf[slot]
        hn = (x * lax.rsqrt(jnp.mean(x * x, axis=-1, keepdims=True) + EPS) * gf_ref[...]).astype(BF16)
        gu = jnp.dot(hn, wgu_ref[0], preferred_element_type=F32)
        gt, up = gu[:, :D_EXPERT], gu[:, D_EXPERT:]
        act = (gt * (1.0 / (1.0 + jnp.exp(-gt))) * up).astype(BF16)
        y_ref[...] = jnp.dot(act, wd_ref[0], preferred_element_type=F32)

    @pl.when(i >= n)
    def _():
        y_ref[...] = jnp.zeros(y_ref.shape, F32)


def _expert_call(x1, row_tok, rows, blk_exp, nblk, gf, wgu, wd):
    nb = row_tok.shape[0] // rows
    return pl.pallas_call(
        _expert_kernel,
        grid_spec=pltpu.PrefetchScalarGridSpec(
            num_scalar_prefetch=2,
            grid=(nb,),
            in_specs=[pl.BlockSpec(memory_space=pl.ANY), pl.BlockSpec(memory_space=pl.ANY),
                      pl.BlockSpec(gf.shape, lambda i, be, n: (0, 0)),
                      pl.BlockSpec((1, D_MODEL, 2 * D_EXPERT), lambda i, be, n: (be[i], 0, 0)),
                      pl.BlockSpec((1, D_EXPERT, D_MODEL), lambda i, be, n: (be[i], 0, 0))],
            out_specs=pl.BlockSpec((rows, D_MODEL), lambda i, be, n: (i, 0)),
            scratch_shapes=[pltpu.VMEM((2, rows, D_MODEL), F32), pltpu.SMEM((2 * rows,), jnp.int32),
                            pltpu.SemaphoreType.DMA((2,)), pltpu.SemaphoreType.DMA((2,))]),
        out_shape=jax.ShapeDtypeStruct((nb * rows, D_MODEL), F32),
        compiler_params=_cparams(("arbitrary",)),
        name="experts",
    )(blk_exp, nblk, x1, row_tok, gf, wgu, wd)


def _combine_kernel(x1_ref, w0_ref, w1_ref, y_hbm, pos_hbm, o_ref, buf, idx, sem_rows, sem_idx):
    rows = buf.shape[1]
    tm = rows // 2
    slot = _gather_step(pl.program_id(0), pl.num_programs(0), y_hbm, pos_hbm, buf, idx, sem_rows, sem_idx)
    w0 = w0_ref[...]
    w1 = w1_ref[...]
    for c in range(D_MODEL // LANES):
        sl = slice(c * LANES, (c + 1) * LANES)
        o_ref[:, sl] = x1_ref[:, sl] + w0 * buf[slot, 0:tm, sl] + w1 * buf[slot, tm:rows, sl]


def _combine_call(x1, w0, w1, yr, pos_tiles, tm):
    t = x1.shape[0]
    rows = TOP_K * tm
    row = lambda w: pl.BlockSpec((tm, w), lambda i: (i, 0))
    return pl.pallas_call(
        _combine_kernel,
        grid=(t // tm,),
        in_specs=[row(D_MODEL), row(LANES), row(LANES),
                  pl.BlockSpec(memory_space=pl.ANY), pl.BlockSpec(memory_space=pl.ANY)],
        out_specs=row(D_MODEL),
        out_shape=jax.ShapeDtypeStruct((t, D_MODEL), F32),
        scratch_shapes=[pltpu.VMEM((2, rows, D_MODEL), F32), pltpu.SMEM((2 * rows,), jnp.int32),
                        pltpu.SemaphoreType.DMA((2,)), pltpu.SemaphoreType.DMA((2,))],
        compiler_params=_cparams(("arbitrary",)),
        name="combine",
    )(x1, w0, w1, yr, pos_tiles)


def _rot_cols(w):
    half = ROPE_DIM // 2
    return jnp.concatenate([-w[..., half:], w[..., :half]], axis=-1)


def _head_pad(nope, rope):
    z = jnp.zeros(rope.shape[:-1] + (LANES - B_QK_DIM,), rope.dtype)
    out = jnp.concatenate([nope, rope, z], axis=-1)
    return out.reshape(out.shape[:-2] + (out.shape[-2] * LANES,))


def _t5_bucket(dist):
    max_exact = REL_BUCKETS // 2
    d = jnp.maximum(dist, 0)
    ratio = jnp.maximum(d, 1).astype(F32) / max_exact
    large = max_exact + (jnp.log(ratio) / math.log(REL_MAX_DIST / max_exact)
                         * (REL_BUCKETS - max_exact)).astype(jnp.int32)
    large = jnp.minimum(large, REL_BUCKETS - 1)
    return jnp.where(d < max_exact, d, large)


def _swa_bias(rel_bias, tq):
    qi = jnp.arange(tq)[:, None]
    kj = jnp.arange(2 * tq)[None, :]
    dist = qi + tq - kj
    bias = rel_bias[_t5_bucket(dist)].astype(F32).transpose(2, 0, 1)
    mask = (dist >= 0) & (dist < WINDOW)
    return jnp.where(mask[None], bias, NEG)


def _rope_tables(seq):
    half = ROPE_DIM // 2
    inv_freq = ROPE_THETA ** (-jnp.arange(half, dtype=F32) / half)
    ang = jnp.arange(seq, dtype=F32)[:, None] * inv_freq[None, :]
    cos = jnp.concatenate([jnp.cos(ang), jnp.cos(ang)], axis=-1)
    sin = jnp.concatenate([jnp.sin(ang), jnp.sin(ang)], axis=-1)
    return cos, sin


def _lane_table(nope_part, rope_part):
    seq = rope_part.shape[0]
    nope_part = jnp.broadcast_to(nope_part, (seq, NOPE_DIM))
    return jnp.concatenate([nope_part, rope_part, jnp.zeros((seq, LANES - B_QK_DIM), F32)], axis=-1)


def _prep_layer(l, seq, cos, sin, attn_norm, w_in, b_gate, a_q_norm, a_k_norm, w_oa, q_a_norm, w_qb, kv_a_norm,
                w_kvb, b_q_norm, b_k_norm, w_ob, w_o, ffn_norm, w_rg, b_rg, w_re, b_re, w_eg, w_eu, w_ed):
    w = w_in[l]
    o = 0
    parts = []
    for sz in (A_WIDTH, A_KV_HEADS * A_HEAD_DIM, A_KV_HEADS * A_HEAD_DIM, Q_LORA, KV_LORA, ROPE_DIM, 2 * D_MODEL):
        parts.append(w[:, o:o + sz])
        o += sz
    wqa, wka, wva, wcq, wckv, wkpe, wg = parts
    perm = jnp.array([c + (A_HEADS // 2) * h for c in range(A_HEADS // 2) for h in range(2)])
    wqa = wqa.reshape(D_MODEL, A_HEADS, A_HEAD_DIM)[:, perm].reshape(D_MODEL, A_WIDTH)
    zl = jnp.zeros((D_MODEL, NOPE_DIM), F32)
    zr = jnp.zeros((D_MODEL, LANES - B_QK_DIM), F32)
    w1 = jnp.concatenate([wqa, wka, wva, wcq, wckv, zl, wkpe, zr, zl, _rot_cols(wkpe), zr, wg], axis=1).astype(BF16)

    wq = w_qb[l].reshape(Q_LORA, B_HEADS, B_QK_DIM)
    zq = jnp.zeros((Q_LORA, B_HEADS, NOPE_DIM), F32)
    wq2 = jnp.concatenate([_head_pad(wq[..., :NOPE_DIM], wq[..., NOPE_DIM:]),
                           _head_pad(zq, _rot_cols(wq[..., NOPE_DIM:]))], axis=1).astype(BF16)
    wkv = w_kvb[l].reshape(KV_LORA, B_HEADS, NOPE_DIM + B_V_DIM)
    wk2 = _head_pad(wkv[..., :NOPE_DIM], jnp.zeros((KV_LORA, B_HEADS, ROPE_DIM), F32)).astype(BF16)
    wv = wkv[..., NOPE_DIM:]
    zv = jnp.zeros_like(wv)
    wv2 = jnp.where((jnp.arange(B_HEADS) % 2 == 0)[None, :, None], jnp.concatenate([wv, zv], -1),
                    jnp.concatenate([zv, wv], -1)).reshape(KV_LORA, B_HEADS * LANES).astype(BF16)
    one_lane = jnp.where(jnp.arange(B_HEADS) % 2 == 0, V_ONE_EVEN, V_ONE_ODD)
    vone = (jnp.arange(LANES)[None, :] == one_lane[:, None]).astype(F32).reshape(1, B_HEADS * LANES)

    def rope_tabs(gain, scale):
        gn, gr = gain[:NOPE_DIM], gain[NOPE_DIM:]
        gr_rot = jnp.concatenate([gr[ROPE_DIM // 2:], gr[:ROPE_DIM // 2]])
        a = _lane_table(gn[None, :] * scale, gr[None, :] * cos * scale)
        b = _lane_table(jnp.zeros((1, NOPE_DIM), F32), gr_rot[None, :] * sin * scale)
        return a, b

    aq, bq = rope_tabs(b_q_norm[l], B_QK_DIM ** -0.5 * LOG2E)
    score_bound = math.sqrt(B_QK_DIM) * LOG2E * jnp.max(jnp.abs(b_q_norm[l])) * jnp.max(jnp.abs(b_k_norm[l]))
    ak, bk = rope_tabs(b_k_norm[l], 1.0)

    gqa = (jnp.tile(a_q_norm[l], 2) * A_HEAD_DIM ** -0.5)[None, :]
    gka = jnp.tile(a_k_norm[l], 2)[None, :]
    woa = w_oa[l].reshape(A_HEADS, A_HEAD_DIM, D_MODEL)[perm].reshape(A_WIDTH, D_MODEL).astype(BF16)

    wr = jnp.concatenate([w_rg[l], w_re[l], jnp.zeros((D_MODEL, LANES - N_GROUPS - N_EXPERTS), F32)], axis=1)
    wrh = wr.astype(BF16)
    wrl = (wr - wrh.astype(F32)).astype(BF16)
    br = jnp.concatenate([b_rg[l], b_re[l], jnp.zeros((LANES - N_GROUPS - N_EXPERTS,), F32)])[None, :]

    return dict(
        gn=attn_norm[l][None, :], w1=w1, wq2=wq2, wk2=wk2, wv2=wv2, vone=vone, gcq=q_a_norm[l][None, :],
        gckv=kv_a_norm[l][None, :], score_bound=score_bound, bg=b_gate[l][None, :], gqa=gqa, gka=gka,
        aq=aq, bq=bq, ak=ak, bk=bk,
        woa=woa, wob=w_ob[l].astype(BF16), wo=w_o[l].astype(BF16), gf=ffn_norm[l][None, :], wrh=wrh, wrl=wrl, br=br,
        wgu=jnp.concatenate([w_eg[l], w_eu[l]], axis=-1).astype(BF16), wd=w_ed[l].astype(BF16))


def _dispatch(eid, rows):
    t = eid.shape[0]
    a = t * TOP_K
    flat_e = eid.reshape(a)
    onehot = (flat_e[:, None] == jnp.arange(N_EXPERTS, dtype=jnp.int32)[None, :]).astype(jnp.int32)
    csum = jnp.cumsum(onehot, axis=0)
    counts = csum[-1]
    rank = jnp.sum(onehot * csum, axis=1) - 1
    padded = ((counts + rows - 1) // rows) * rows
    pends = jnp.cumsum(padded)
    pstarts = pends - padded
    dest = jnp.sum(onehot * pstarts[None, :], axis=1) + rank
    n_rows = a + N_EXPERTS * rows
    n_blocks = n_rows // rows
    flat_t = jnp.arange(a, dtype=jnp.int32) // TOP_K
    row_tok = jnp.zeros((n_rows,), jnp.int32).at[dest].set(flat_t, unique_indices=True)
    blk_exp = jnp.minimum(jnp.searchsorted(pends, jnp.arange(n_blocks, dtype=jnp.int32) * rows, side="right"),
                          N_EXPERTS - 1).astype(jnp.int32)
    nblk = (pends[-1] // rows).astype(jnp.int32).reshape(1)
    return row_tok, blk_exp, nblk, dest.reshape(t, TOP_K)


def _layer(x2d, bsz, seq, p, sinks, bias):
    t = x2d.shape[0]
    qa, ka, va, qb, kb, vb, gates = _proj_call(x2d, seq, p)
    oa = _swa_call(qa, ka, va, sinks, bias, bsz, seq)
    ob = _mla_call(qb, kb, vb, bsz, seq, p["score_bound"])
    x1, w0, w1, eid = _merge_call(x2d, oa, ob, gates, p)
    row_tok, blk_exp, nblk, dest = _dispatch(eid[:, :TOP_K], MOE_ROWS)
    yr = _expert_call(x1, row_tok, MOE_ROWS, blk_exp, nblk, p["gf"], p["wgu"], p["wd"])
    pos_tiles = dest.reshape(t // COMB_TM, COMB_TM, TOP_K).transpose(0, 2, 1).reshape(t * TOP_K)
    return _combine_call(x1, w0, w1, yr, pos_tiles, COMB_TM)


def kernel(x, attn_norm, w_in, b_gate, a_q_norm, a_k_norm, a_sinks, rel_bias, w_oa, q_a_norm, w_qb, kv_a_norm, w_kvb, b_q_norm, b_k_norm, w_ob, w_o, ffn_norm, w_router_group, b_router_group, w_router_expert, b_router_expert, w_exp_gate, w_exp_up, w_exp_down):
    bsz, seq, d = x.shape
    assert d == D_MODEL and seq % MLA_TQ == 0 and seq % PROJ_TM == 0 and (bsz * seq) % COMB_TM == 0
    depth = w_in.shape[0]
    cos, sin = _rope_tables(seq)
    bias = _swa_bias(rel_bias, SWA_TQ)
    x2d = x.reshape(bsz * seq, d)
    for l in range(depth):
        p = _prep_layer(l, seq, cos, sin, attn_norm, w_in, b_gate, a_q_norm, a_k_norm, w_oa, q_a_norm, w_qb,
                        kv_a_norm, w_kvb, b_q_norm, b_k_norm, w_ob, w_o, ffn_norm, w_router_group,
                        b_router_group, w_router_expert, b_router_expert, w_exp_gate, w_exp_up, w_exp_down)
        x2d = _layer(x2d, bsz, seq, p, a_sinks[l], bias)
    return x2d.reshape(bsz, seq, d)
```

```python
import functools
import math

import jax
import jax.numpy as jnp
from jax import lax
from jax.experimental import pallas as pl
from jax.experimental.pallas import tpu as pltpu

F32 = jnp.float32
BF16 = jnp.bfloat16

D_MODEL = 1024
A_HEADS = 8
A_KV_HEADS = 2
A_HEAD_DIM = 64
A_WIDTH = A_HEADS * A_HEAD_DIM
WINDOW = 128
REL_BUCKETS = 32
REL_MAX_DIST = 128
B_HEADS = 8
Q_LORA = 256
KV_LORA = 128
NOPE_DIM = 64
ROPE_DIM = 32
B_QK_DIM = NOPE_DIM + ROPE_DIM
B_V_DIM = 64
B_WIDTH = B_HEADS * B_V_DIM
ROPE_THETA = 10000.0
N_GROUPS = 4
EXPERTS_PER_GROUP = 8
N_EXPERTS = N_GROUPS * EXPERTS_PER_GROUP
TOP_K = 2
D_EXPERT = 256
EPS = 1e-6

LANES = 128
NEG = -1e30
LOG2E = math.log2(math.e)
V_ONE_EVEN = B_V_DIM
V_ONE_ODD = 0
EXP2_SAFE_BOUND = 100.0
VMEM_LIMIT = 56 * 1024 * 1024

C_QA = 0
C_KA = C_QA + A_WIDTH
C_VA = C_KA + LANES
C_CQ = C_VA + LANES
C_CKV = C_CQ + Q_LORA
C_KPE = C_CKV + KV_LORA
C_KPR = C_KPE + LANES
C_GATE = C_KPR + LANES
C_END = C_GATE + 2 * D_MODEL

PROJ_TM = 512
SWA_TQ = 128
MLA_TQ = 512
MERGE_TM = 512
MOE_ROWS = 128
COMB_TM = 256


def _cparams(sem):
    return pltpu.CompilerParams(dimension_semantics=sem, vmem_limit_bytes=VMEM_LIMIT)


def _proj_kernel(x_ref, gn_ref, w1_ref, wq2_ref, wk2_ref, wv2_ref, vone_ref, gcq_ref, gckv_ref, bg_ref,
                 gqa_ref, gka_ref, aq_ref, bq_ref, ak_ref, bk_ref,
                 qa_ref, ka_ref, va_ref, qb_ref, kb_ref, vb_ref, gate_ref):
    tm = x_ref.shape[0]
    x = x_ref[...]
    h = (x * lax.rsqrt(jnp.mean(x * x, axis=-1, keepdims=True) + EPS) * gn_ref[...]).astype(BF16)

    def proj(a, b):
        return jnp.dot(h, w1_ref[:, a:b], preferred_element_type=F32)

    lo = lax.broadcasted_iota(jnp.int32, (tm, LANES), 1) < A_HEAD_DIM

    def pair_norm(t, gain):
        t2 = t * t
        s_lo = jnp.sum(jnp.where(lo, t2, 0.0), axis=-1, keepdims=True)
        s_hi = jnp.sum(jnp.where(lo, 0.0, t2), axis=-1, keepdims=True)
        r = jnp.where(lo, lax.rsqrt(s_lo / A_HEAD_DIM + EPS), lax.rsqrt(s_hi / A_HEAD_DIM + EPS))
        return (t * r * gain).astype(BF16)

    qa = proj(C_QA, C_KA)
    for c in range(A_WIDTH // LANES):
        sl = slice(c * LANES, (c + 1) * LANES)
        qa_ref[:, sl] = pair_norm(qa[:, sl], gqa_ref[...])
    kvp = proj(C_KA, C_CQ)
    ka_ref[...] = pair_norm(kvp[:, :LANES], gka_ref[...])
    va_ref[...] = kvp[:, LANES:].astype(BF16)

    lat = proj(C_CQ, C_GATE)
    cq = lat[:, :Q_LORA]
    cqn = (cq * lax.rsqrt(jnp.mean(cq * cq, axis=-1, keepdims=True) + EPS) * gcq_ref[...]).astype(BF16)
    ckv = lat[:, Q_LORA:Q_LORA + KV_LORA]
    ckvn = (ckv * lax.rsqrt(jnp.mean(ckv * ckv, axis=-1, keepdims=True) + EPS) * gckv_ref[...]).astype(BF16)
    kpe = lat[:, C_KPE - C_CQ:C_KPR - C_CQ]
    kpr = lat[:, C_KPR - C_CQ:C_GATE - C_CQ]

    aq, bq, ak, bk = aq_ref[...], bq_ref[...], ak_ref[...], bk_ref[...]
    q2 = jnp.dot(cqn, wq2_ref[...], preferred_element_type=F32)
    k2 = jnp.dot(ckvn, wk2_ref[...], preferred_element_type=F32)
    vb_ref[...] = (jnp.dot(ckvn, wv2_ref[...], preferred_element_type=F32) + vone_ref[...]).astype(BF16)
    kr = kpr * bk
    for hd in range(B_HEADS):
        sl = slice(hd * LANES, (hd + 1) * LANES)
        qp = q2[:, sl]
        qr = q2[:, B_HEADS * LANES + hd * LANES:B_HEADS * LANES + (hd + 1) * LANES]
        rq = lax.rsqrt(jnp.sum(qp * qp, axis=-1, keepdims=True) / B_QK_DIM + EPS)
        qb_ref[:, sl] = (rq * (qp * aq + qr * bq)).astype(BF16)
        kp = k2[:, sl] + kpe
        rk = lax.rsqrt(jnp.sum(kp * kp, axis=-1, keepdims=True) / B_QK_DIM + EPS)
        kb_ref[:, sl] = (rk * (kp * ak + kr)).astype(BF16)

    z = proj(C_GATE, C_END) + bg_ref[...]
    gate_ref[...] = (1.0 / (1.0 + jnp.exp(-z))).astype(BF16)


def _proj_call(x2d, seq, p):
    t = x2d.shape[0]
    tm = PROJ_TM
    nt = t // tm
    npos = seq // tm
    row = lambda w: pl.BlockSpec((tm, w), lambda i: (i, 0))
    full = lambda a: pl.BlockSpec(a.shape, lambda i: (0,) * a.ndim)
    pos = pl.BlockSpec((tm, LANES), lambda i: (i % npos, 0))
    consts = [p["gn"], p["w1"], p["wq2"], p["wk2"], p["wv2"], p["vone"], p["gcq"], p["gckv"], p["bg"], p["gqa"],
              p["gka"]]
    tabs = [p["aq"], p["bq"], p["ak"], p["bk"]]
    widths = [A_WIDTH, LANES, LANES, B_HEADS * LANES, B_HEADS * LANES, B_HEADS * LANES, 2 * D_MODEL]
    return pl.pallas_call(
        _proj_kernel,
        grid=(nt,),
        in_specs=[row(D_MODEL)] + [full(a) for a in consts] + [pos] * 4,
        out_specs=[row(w) for w in widths],
        out_shape=[jax.ShapeDtypeStruct((t, w), BF16) for w in widths],
        compiler_params=_cparams(("parallel",)),
        name="proj",
    )(x2d, *consts, *tabs)


def _swa_kernel(sink_ref, q_ref, kc_ref, kp_ref, vc_ref, vp_ref, bias_ref, o_ref):
    tq = q_ref.shape[0]
    first = pl.program_id(1) == 0
    k = jnp.concatenate([kp_ref[...], kc_ref[...]], axis=0)
    v = jnp.concatenate([vp_ref[...], vc_ref[...]], axis=0)
    lo = lax.broadcasted_iota(jnp.int32, (tq, LANES), 1) < A_HEAD_DIM
    no_prev = jnp.logical_and(first, lax.broadcasted_iota(jnp.int32, (tq, 2 * tq), 1) < tq)
    for c in range(A_WIDTH // LANES):
        qc = q_ref[:, c * LANES:(c + 1) * LANES]
        outs = []
        for half in range(2):
            hd = c + (A_HEADS // 2) * half
            qm = jnp.where(lo if half == 0 else jnp.logical_not(lo), qc, jnp.zeros_like(qc))
            s = lax.dot_general(qm, k, (((1,), (1,)), ((), ())), preferred_element_type=F32)
            s = jnp.where(no_prev, NEG, s + bias_ref[hd])
            sink = sink_ref[hd]
            m = jnp.maximum(jnp.max(s, axis=-1, keepdims=True), sink)
            e = jnp.exp(s - m)
            denom = jnp.sum(e, axis=-1, keepdims=True) + jnp.exp(sink - m)
            pv = jnp.dot(e.astype(BF16), v, preferred_element_type=F32)
            outs.append(pv * (1.0 / denom))
        o_ref[:, c * LANES:(c + 1) * LANES] = jnp.where(lo, outs[0], outs[1]).astype(BF16)


def _swa_call(qa, ka, va, sinks, bias, bsz, seq):
    t = qa.shape[0]
    tq = SWA_TQ
    nq = seq // tq
    cur = lambda b, i, s: (b * nq + i, 0)
    prev = lambda b, i, s: (b * nq + jnp.maximum(i - 1, 0), 0)
    return pl.pallas_call(
        _swa_kernel,
        grid_spec=pltpu.PrefetchScalarGridSpec(
            num_scalar_prefetch=1,
            grid=(bsz, nq),
            in_specs=[pl.BlockSpec((tq, A_WIDTH), cur),
                      pl.BlockSpec((tq, LANES), cur), pl.BlockSpec((tq, LANES), prev),
                      pl.BlockSpec((tq, LANES), cur), pl.BlockSpec((tq, LANES), prev),
                      pl.BlockSpec(bias.shape, lambda b, i, s: (0, 0, 0))],
            out_specs=pl.BlockSpec((tq, A_WIDTH), cur)),
        out_shape=jax.ShapeDtypeStruct((t, A_WIDTH), BF16),
        compiler_params=_cparams(("parallel", "parallel")),
        name="swa",
    )(sinks, qa, ka, ka, va, va, bias)


def _mla_kernel(q_ref, k_ref, v_ref, o_ref, acc_sc, m_sc, *, shift):
    tq = q_ref.shape[0]
    tk = tq
    qi = pl.program_id(2)
    causal = (lax.broadcasted_iota(jnp.int32, (tq, tk), 1) <= lax.broadcasted_iota(jnp.int32, (tq, tk), 0))
    acc_sc[...] = jnp.zeros(acc_sc.shape, F32)
    if shift:
        m_sc[...] = jnp.full(m_sc.shape, NEG, F32)

    def step(j, masked):
        rows = pl.ds(pl.multiple_of(j * tk, tk), tk)
        for half in range(2):
            hs = slice(half * LANES, (half + 1) * LANES)
            s = lax.dot_general(q_ref[:, hs], k_ref[rows, hs], (((1,), (1,)), ((), ())), preferred_element_type=F32)
            if shift:
                if masked:
                    s = jnp.where(causal, s, NEG)
                m_prev = m_sc[half]
                m_new = jnp.maximum(m_prev, jnp.max(s, axis=-1, keepdims=True))
                m_sc[half] = m_new
                e = jnp.exp2(s - m_new)
                acc_sc[half] = jnp.exp2(m_prev - m_new) * acc_sc[half]
            else:
                e = jnp.exp2(s)
                if masked:
                    e = jnp.where(causal, e, 0.0)
            acc_sc[half] += jnp.dot(e.astype(BF16), v_ref[rows, hs], preferred_element_type=F32)

    def body(j, carry):
        step(j, False)
        return carry

    lax.fori_loop(0, qi, body, 0)
    step(qi, True)
    a0 = acc_sc[0]
    a1 = acc_sc[1]
    lo = lax.broadcasted_iota(jnp.int32, (tq, LANES), 1) < B_V_DIM
    o_ref[...] = jnp.where(lo, a0 * (1.0 / a0[:, V_ONE_EVEN:V_ONE_EVEN + 1]),
                           a1 * (1.0 / a1[:, V_ONE_ODD:V_ONE_ODD + 1])).astype(BF16)


def _mla_call(qb, kb, vb, bsz, seq, score_bound):
    t = qb.shape[0]
    tq = MLA_TQ
    nq = seq // tq
    pairs = B_HEADS // 2

    def call(shift):
        return pl.pallas_call(
            functools.partial(_mla_kernel, shift=shift),
            grid=(bsz, pairs, nq),
            in_specs=[pl.BlockSpec((tq, 2 * LANES), lambda b, p, i: (b * nq + i, p)),
                      pl.BlockSpec((seq, 2 * LANES), lambda b, p, i: (b, p)),
                      pl.BlockSpec((seq, 2 * LANES), lambda b, p, i: (b, p))],
            out_specs=pl.BlockSpec((tq, LANES), lambda b, p, i: (b * nq + i, p)),
            out_shape=jax.ShapeDtypeStruct((t, B_WIDTH), BF16),
            scratch_shapes=[pltpu.VMEM((2, tq, LANES), F32), pltpu.VMEM((2, tq, 1), F32)],
            compiler_params=_cparams(("parallel", "parallel", "arbitrary")),
            name="mla_shifted" if shift else "mla",
        )(qb, kb, vb)

    return lax.cond(score_bound <= EXP2_SAFE_BOUND, lambda: call(False), lambda: call(True))


def _merge_kernel(x_ref, oa_ref, ob_ref, gate_ref, woa_ref, wob_ref, wo_ref, gf_ref, wrh_ref, wrl_ref, br_ref,
                  tri_ref, x1_ref, w0_ref, w1_ref, eid_ref, cnt_ref):
    tm = x_ref.shape[0]
    ya = jnp.dot(oa_ref[...], woa_ref[...], preferred_element_type=F32)
    yb = jnp.dot(ob_ref[...], wob_ref[...], preferred_element_type=F32)
    g = gate_ref[...].astype(F32)
    mix = (g[:, :D_MODEL] * ya + g[:, D_MODEL:] * yb).astype(BF16)
    x1 = x_ref[...] + jnp.dot(mix, wo_ref[...], preferred_element_type=F32)
    x1_ref[...] = x1

    hn = x1 * lax.rsqrt(jnp.mean(x1 * x1, axis=-1, keepdims=True) + EPS) * gf_ref[...]
    hh = hn.astype(BF16)
    hl = (hn - hh.astype(F32)).astype(BF16)
    wrh = wrh_ref[...]
    logits = (jnp.dot(hh, wrh, preferred_element_type=F32) + jnp.dot(hl, wrh, preferred_element_type=F32)
              + jnp.dot(hh, wrl_ref[...], preferred_element_type=F32) + br_ref[...])

    lane = lax.broadcasted_iota(jnp.int32, (tm, LANES), 1).astype(F32)
    big = float(LANES)
    gmask = lane < N_GROUPS
    gl = jnp.where(gmask, logits, NEG)
    gmax = jnp.max(gl, axis=-1, keepdims=True)
    gsum = jnp.sum(jnp.where(gmask, jnp.exp(gl - gmax), 0.0), axis=-1, keepdims=True)
    g_p = 1.0 / gsum
    g_idx = jnp.min(jnp.where(gl == gmax, lane, big), axis=-1, keepdims=True)
    e_lo = N_GROUPS + EXPERTS_PER_GROUP * g_idx
    emask = jnp.logical_and(lane >= e_lo, lane < e_lo + EXPERTS_PER_GROUP)
    el = jnp.where(emask, logits, NEG)
    t1 = jnp.max(el, axis=-1, keepdims=True)
    i1 = jnp.min(jnp.where(el == t1, lane, big), axis=-1, keepdims=True)
    el2 = jnp.where(lane == i1, NEG, el)
    t2 = jnp.max(el2, axis=-1, keepdims=True)
    i2 = jnp.min(jnp.where(el2 == t2, lane, big), axis=-1, keepdims=True)
    e2 = jnp.exp(t2 - t1)
    w_first = g_p / (1.0 + e2)
    w0_ref[...] = jnp.broadcast_to(w_first, (tm, LANES))
    w1_ref[...] = jnp.broadcast_to(w_first * e2, (tm, LANES))
    ex1 = i1 - N_GROUPS
    ex2 = i2 - N_GROUPS

    oh1 = lane == ex1
    oh2 = lane == ex2
    cnt = (oh1.astype(F32) + oh2.astype(F32)).astype(BF16)
    before = jnp.dot(tri_ref[...], cnt, preferred_element_type=F32)
    r1 = jnp.sum(jnp.where(oh1, before, 0.0), axis=-1, keepdims=True)
    r2 = jnp.sum(jnp.where(oh2, before, 0.0), axis=-1, keepdims=True)
    ids = jnp.where(lane == 0.0, ex1, jnp.where(lane == 1.0, ex2, jnp.where(lane == 2.0, r1,
                                                                             jnp.where(lane == 3.0, r2, 0.0))))
    eid_ref[...] = ids.astype(jnp.int32)
    tile_cnt = jnp.sum(cnt.astype(F32), axis=0, keepdims=True)
    cnt_ref[...] = jnp.broadcast_to(tile_cnt, cnt_ref.shape[1:]).astype(jnp.int32)[None]


def _merge_call(x2d, oa, ob, gates, p):
    t = x2d.shape[0]
    tm = MERGE_TM
    nt = t // tm
    row = lambda w: pl.BlockSpec((tm, w), lambda i: (i, 0))
    full = lambda a: pl.BlockSpec(a.shape, lambda i: (0,) * a.ndim)
    tri = (jnp.arange(tm)[:, None] > jnp.arange(tm)[None, :]).astype(BF16)
    consts = [p["woa"], p["wob"], p["wo"], p["gf"], p["wrh"], p["wrl"], p["br"], tri]
    return pl.pallas_call(
        _merge_kernel,
        grid=(nt,),
        in_specs=[row(D_MODEL), row(A_WIDTH), row(B_WIDTH), row(2 * D_MODEL)] + [full(a) for a in consts],
        out_specs=[row(D_MODEL), row(LANES), row(LANES), row(LANES),
                   pl.BlockSpec((1, 8, LANES), lambda i: (i, 0, 0))],
        out_shape=[jax.ShapeDtypeStruct((t, D_MODEL), F32), jax.ShapeDtypeStruct((t, LANES), F32),
                   jax.ShapeDtypeStruct((t, LANES), F32), jax.ShapeDtypeStruct((t, LANES), jnp.int32),
                   jax.ShapeDtypeStruct((nt, 8, LANES), jnp.int32)],
        compiler_params=_cparams(("parallel",)),
        name="merge",
    )(x2d, oa, ob, gates, *consts)


def _rows_copy(src_hbm, row, dst, j, sem):
    return pltpu.make_async_copy(src_hbm.at[pl.ds(row, 1), :], dst.at[pl.ds(j, 1), :], sem)


def _start_rows(src_hbm, idx, base, dst, sem, n):
    def body(j, carry):
        _rows_copy(src_hbm, idx[base + j], dst, j, sem).start()
        return carry
    lax.fori_loop(0, n, body, 0, unroll=8)


WAIT_ROWS = 128


def _wait_rows(src_hbm, dst, sem, n):
    for c in range(n // WAIT_ROWS):
        rows = pl.ds(c * WAIT_ROWS, WAIT_ROWS)
        pltpu.make_async_copy(src_hbm.at[rows, :], dst.at[rows, :], sem).wait()


def _gather_step(i, n, src_hbm, idx_hbm, buf, idx, sem_rows, sem_idx):
    rows = buf.shape[1]
    slot = i % 2
    nslot = 1 - slot

    def idx_copy(b, s):
        return pltpu.make_async_copy(idx_hbm.at[pl.ds(b * rows, rows)], idx.at[pl.ds(s * rows, rows)], sem_idx.at[s])

    @pl.when(jnp.logical_and(i == 0, n > 0))
    def _():
        idx_copy(0, 0).start()
        idx_copy(0, 0).wait()
        _start_rows(src_hbm, idx, 0, buf.at[0], sem_rows.at[0], rows)

        @pl.when(n > 1)
        def _():
            idx_copy(1, 1).start()

    @pl.when(i + 1 < n)
    def _():
        idx_copy(i + 1, nslot).wait()
        _start_rows(src_hbm, idx, nslot * rows, buf.at[nslot], sem_rows.at[nslot], rows)

    @pl.when(i + 2 < n)
    def _():
        idx_copy(i + 2, slot).start()

    @pl.when(i < n)
    def _():
        _wait_rows(src_hbm, buf.at[slot], sem_rows.at[slot], rows)
    return slot


def _scatter_kernel(x_hbm, dest_hbm, init_hbm, xr_hbm, idx, sem_rows, sem_idx):
    del init_hbm
    i = pl.program_id(0)
    n = pl.num_programs(0)
    rows = idx.shape[0] // 2
    tm = rows // TOP_K
    slot = i % 2
    nslot = 1 - slot

    def idx_copy(b, s):
        return pltpu.make_async_copy(dest_hbm.at[pl.ds(b * rows, rows)], idx.at[pl.ds(s * rows, rows)], sem_idx.at[s])

    def wait_rows(s):
        for c in range(rows // WAIT_ROWS):
            blk = pl.ds(c * WAIT_ROWS, WAIT_ROWS)
            pltpu.make_async_copy(x_hbm.at[blk, :], xr_hbm.at[blk, :], sem_rows.at[s]).wait()

    @pl.when(i == 0)
    def _():
        idx_copy(0, 0).start()

    idx_copy(i, slot).wait()

    @pl.when(i + 1 < n)
    def _():
        idx_copy(i + 1, nslot).start()

    def body(j, carry):
        src = x_hbm.at[pl.ds(i * tm + j, 1), :]
        for k in range(TOP_K):
            dst = xr_hbm.at[pl.ds(idx[slot * rows + k * tm + j], 1), :]
            pltpu.make_async_copy(src, dst, sem_rows.at[slot]).start()
        return carry
    lax.fori_loop(0, tm, body, 0, unroll=4)

    @pl.when(i > 0)
    def _():
        wait_rows(nslot)

    @pl.when(i == n - 1)
    def _():
        wait_rows(slot)


def _scatter_call(x1, pos_tiles, n_rows, tm):
    t = x1.shape[0]
    rows = TOP_K * tm
    any_spec = pl.BlockSpec(memory_space=pl.ANY)
    return pl.pallas_call(
        _scatter_kernel,
        grid=(t // tm,),
        in_specs=[any_spec, any_spec, any_spec],
        out_specs=any_spec,
        out_shape=jax.ShapeDtypeStruct((n_rows, D_MODEL), F32),
        input_output_aliases={2: 0},
        scratch_shapes=[pltpu.SMEM((2 * rows,), jnp.int32),
                        pltpu.SemaphoreType.DMA((2,)), pltpu.SemaphoreType.DMA((2,))],
        compiler_params=_cparams(("arbitrary",)),
        name="scatter",
    )(x1, pos_tiles, jnp.zeros((n_rows, D_MODEL), F32))


def _expert_kernel(blk_exp_ref, nblk_ref, x_ref, gf_ref, wgu_ref, wd_ref, y_ref):
    del blk_exp_ref
    i = pl.program_id(0)

    @pl.when(i < nblk_ref[0])
    def _():
        x = x_ref[...]
        hn = (x * lax.rsqrt(jnp.mean(x * x, axis=-1, keepdims=True) + EPS) * gf_ref[...]).astype(BF16)
        gu = jnp.dot(hn, wgu_ref[0], preferred_element_type=F32)
        gt, up = gu[:, :D_EXPERT], gu[:, D_EXPERT:]
        act = (gt * (1.0 / (1.0 + jnp.exp(-gt))) * up).astype(BF16)
        y_ref[...] = jnp.dot(act, wd_ref[0], preferred_element_type=F32)

    @pl.when(i >= nblk_ref[0])
    def _():
        y_ref[...] = jnp.zeros(y_ref.shape, F32)


def _expert_call(xr, rows, blk_exp, nblk, gf, wgu, wd):
    nb = xr.shape[0] // rows
    return pl.pallas_call(
        _expert_kernel,
        grid_spec=pltpu.PrefetchScalarGridSpec(
            num_scalar_prefetch=2,
            grid=(nb,),
            in_specs=[pl.BlockSpec((rows, D_MODEL), lambda i, be, n: (i, 0)),
                      pl.BlockSpec(gf.shape, lambda i, be, n: (0, 0)),
                      pl.BlockSpec((1, D_MODEL, 2 * D_EXPERT), lambda i, be, n: (be[i], 0, 0)),
                      pl.BlockSpec((1, D_EXPERT, D_MODEL), lambda i, be, n: (be[i], 0, 0))],
            out_specs=pl.BlockSpec((rows, D_MODEL), lambda i, be, n: (i, 0))),
        out_shape=jax.ShapeDtypeStruct((nb * rows, D_MODEL), F32),
        compiler_params=_cparams(("arbitrary",)),
        name="experts",
    )(blk_exp, nblk, xr, gf, wgu, wd)


def _combine_kernel(x1_ref, w0_ref, w1_ref, y_hbm, pos_hbm, o_ref, buf, idx, sem_rows, sem_idx):
    rows = buf.shape[1]
    tm = rows // 2
    slot = _gather_step(pl.program_id(0), pl.num_programs(0), y_hbm, pos_hbm, buf, idx, sem_rows, sem_idx)
    w0 = w0_ref[...]
    w1 = w1_ref[...]
    for c in range(D_MODEL // LANES):
        sl = slice(c * LANES, (c + 1) * LANES)
        o_ref[:, sl] = x1_ref[:, sl] + w0 * buf[slot, 0:tm, sl] + w1 * buf[slot, tm:rows, sl]


def _combine_call(x1, w0, w1, yr, pos_tiles, tm):
    t = x1.shape[0]
    rows = TOP_K * tm
    row = lambda w: pl.BlockSpec((tm, w), lambda i: (i, 0))
    return pl.pallas_call(
        _combine_kernel,
        grid=(t // tm,),
        in_specs=[row(D_MODEL), row(LANES), row(LANES),
                  pl.BlockSpec(memory_space=pl.ANY), pl.BlockSpec(memory_space=pl.ANY)],
        out_specs=row(D_MODEL),
        out_shape=jax.ShapeDtypeStruct((t, D_MODEL), F32),
        scratch_shapes=[pltpu.VMEM((2, rows, D_MODEL), F32), pltpu.SMEM((2 * rows,), jnp.int32),
                        pltpu.SemaphoreType.DMA((2,)), pltpu.SemaphoreType.DMA((2,))],
        compiler_params=_cparams(("arbitrary",)),
        name="combine",
    )(x1, w0, w1, yr, pos_tiles)


def _rot_cols(w):
    half = ROPE_DIM // 2
    return jnp.concatenate([-w[..., half:], w[..., :half]], axis=-1)


def _head_pad(nope, rope):
    z = jnp.zeros(rope.shape[:-1] + (LANES - B_QK_DIM,), rope.dtype)
    out = jnp.concatenate([nope, rope, z], axis=-1)
    return out.reshape(out.shape[:-2] + (out.shape[-2] * LANES,))


def _t5_bucket(dist):
    max_exact = REL_BUCKETS // 2
    d = jnp.maximum(dist, 0)
    ratio = jnp.maximum(d, 1).astype(F32) / max_exact
    large = max_exact + (jnp.log(ratio) / math.log(REL_MAX_DIST / max_exact)
                         * (REL_BUCKETS - max_exact)).astype(jnp.int32)
    large = jnp.minimum(large, REL_BUCKETS - 1)
    return jnp.where(d < max_exact, d, large)


def _swa_bias(rel_bias, tq):
    qi = jnp.arange(tq)[:, None]
    kj = jnp.arange(2 * tq)[None, :]
    dist = qi + tq - kj
    bias = rel_bias[_t5_bucket(dist)].astype(F32).transpose(2, 0, 1)
    mask = (dist >= 0) & (dist < WINDOW)
    return jnp.where(mask[None], bias, NEG)


def _rope_tables(seq):
    half = ROPE_DIM // 2
    inv_freq = ROPE_THETA ** (-jnp.arange(half, dtype=F32) / half)
    ang = jnp.arange(seq, dtype=F32)[:, None] * inv_freq[None, :]
    cos = jnp.concatenate([jnp.cos(ang), jnp.cos(ang)], axis=-1)
    sin = jnp.concatenate([jnp.sin(ang), jnp.sin(ang)], axis=-1)
    return cos, sin


def _lane_table(nope_part, rope_part):
    seq = rope_part.shape[0]
    nope_part = jnp.broadcast_to(nope_part, (seq, NOPE_DIM))
    return jnp.concatenate([nope_part, rope_part, jnp.zeros((seq, LANES - B_QK_DIM), F32)], axis=-1)


def _prep_layer(l, seq, cos, sin, attn_norm, w_in, b_gate, a_q_norm, a_k_norm, w_oa, q_a_norm, w_qb, kv_a_norm,
                w_kvb, b_q_norm, b_k_norm, w_ob, w_o, ffn_norm, w_rg, b_rg, w_re, b_re, w_eg, w_eu, w_ed):
    w = w_in[l]
    o = 0
    parts = []
    for sz in (A_WIDTH, A_KV_HEADS * A_HEAD_DIM, A_KV_HEADS * A_HEAD_DIM, Q_LORA, KV_LORA, ROPE_DIM, 2 * D_MODEL):
        parts.append(w[:, o:o + sz])
        o += sz
    wqa, wka, wva, wcq, wckv, wkpe, wg = parts
    perm = jnp.array([c + (A_HEADS // 2) * h for c in range(A_HEADS // 2) for h in range(2)])
    wqa = wqa.reshape(D_MODEL, A_HEADS, A_HEAD_DIM)[:, perm].reshape(D_MODEL, A_WIDTH)
    zl = jnp.zeros((D_MODEL, NOPE_DIM), F32)
    zr = jnp.zeros((D_MODEL, LANES - B_QK_DIM), F32)
    w1 = jnp.concatenate([wqa, wka, wva, wcq, wckv, zl, wkpe, zr, zl, _rot_cols(wkpe), zr, wg], axis=1).astype(BF16)

    wq = w_qb[l].reshape(Q_LORA, B_HEADS, B_QK_DIM)
    zq = jnp.zeros((Q_LORA, B_HEADS, NOPE_DIM), F32)
    wq2 = jnp.concatenate([_head_pad(wq[..., :NOPE_DIM], wq[..., NOPE_DIM:]),
                           _head_pad(zq, _rot_cols(wq[..., NOPE_DIM:]))], axis=1).astype(BF16)
    wkv = w_kvb[l].reshape(KV_LORA, B_HEADS, NOPE_DIM + B_V_DIM)
    wk2 = _head_pad(wkv[..., :NOPE_DIM], jnp.zeros((KV_LORA, B_HEADS, ROPE_DIM), F32)).astype(BF16)
    wv = wkv[..., NOPE_DIM:]
    zv = jnp.zeros_like(wv)
    wv2 = jnp.where((jnp.arange(B_HEADS) % 2 == 0)[None, :, None], jnp.concatenate([wv, zv], -1),
                    jnp.concatenate([zv, wv], -1)).reshape(KV_LORA, B_HEADS * LANES).astype(BF16)
    one_lane = jnp.where(jnp.arange(B_HEADS) % 2 == 0, V_ONE_EVEN, V_ONE_ODD)
    vone = (jnp.arange(LANES)[None, :] == one_lane[:, None]).astype(F32).reshape(1, B_HEADS * LANES)

    def rope_tabs(gain, scale):
        gn, gr = gain[:NOPE_DIM], gain[NOPE_DIM:]
        gr_rot = jnp.concatenate([gr[ROPE_DIM // 2:], gr[:ROPE_DIM // 2]])
        a = _lane_table(gn[None, :] * scale, gr[None, :] * cos * scale)
        b = _lane_table(jnp.zeros((1, NOPE_DIM), F32), gr_rot[None, :] * sin * scale)
        return a, b

    aq, bq = rope_tabs(b_q_norm[l], B_QK_DIM ** -0.5 * LOG2E)
    score_bound = math.sqrt(B_QK_DIM) * LOG2E * jnp.max(jnp.abs(b_q_norm[l])) * jnp.max(jnp.abs(b_k_norm[l]))
    ak, bk = rope_tabs(b_k_norm[l], 1.0)

    gqa = (jnp.tile(a_q_norm[l], 2) * A_HEAD_DIM ** -0.5)[None, :]
    gka = jnp.tile(a_k_norm[l], 2)[None, :]
    woa = w_oa[l].reshape(A_HEADS, A_HEAD_DIM, D_MODEL)[perm].reshape(A_WIDTH, D_MODEL).astype(BF16)

    wr = jnp.concatenate([w_rg[l], w_re[l], jnp.zeros((D_MODEL, LANES - N_GROUPS - N_EXPERTS), F32)], axis=1)
    wrh = wr.astype(BF16)
    wrl = (wr - wrh.astype(F32)).astype(BF16)
    br = jnp.concatenate([b_rg[l], b_re[l], jnp.zeros((LANES - N_GROUPS - N_EXPERTS,), F32)])[None, :]

    return dict(
        gn=attn_norm[l][None, :], w1=w1, wq2=wq2, wk2=wk2, wv2=wv2, vone=vone, gcq=q_a_norm[l][None, :],
        gckv=kv_a_norm[l][None, :], score_bound=score_bound, bg=b_gate[l][None, :], gqa=gqa, gka=gka,
        aq=aq, bq=bq, ak=ak, bk=bk,
        woa=woa, wob=w_ob[l].astype(BF16), wo=w_o[l].astype(BF16), gf=ffn_norm[l][None, :], wrh=wrh, wrl=wrl, br=br,
        wgu=jnp.concatenate([w_eg[l], w_eu[l]], axis=-1).astype(BF16), wd=w_ed[l].astype(BF16))


def _dispatch(route, tile_cnt, rows):
    t = route.shape[0]
    nt = tile_cnt.shape[0]
    tile_cnt = tile_cnt[:, 0, :N_EXPERTS]
    counts = jnp.sum(tile_cnt, axis=0)
    padded = ((counts + rows - 1) // rows) * rows
    pends = jnp.cumsum(padded)
    pstarts = pends - padded
    tile_base = pstarts[None, :] + jnp.cumsum(tile_cnt, axis=0) - tile_cnt
    base = jnp.broadcast_to(tile_base[:, None, None, :], (nt, t // nt, 1, N_EXPERTS)).reshape(t, 1, N_EXPERTS)
    onehot = route[:, :TOP_K, None] == jnp.arange(N_EXPERTS, dtype=jnp.int32)[None, None, :]
    dest = jnp.sum(jnp.where(onehot, base, 0), axis=-1) + route[:, TOP_K:2 * TOP_K]
    n_rows = t * TOP_K + N_EXPERTS * rows
    n_blocks = n_rows // rows
    blk_exp = jnp.minimum(jnp.searchsorted(pends, jnp.arange(n_blocks, dtype=jnp.int32) * rows, side="right"),
                          N_EXPERTS - 1).astype(jnp.int32)
    nblk = (pends[-1] // rows).astype(jnp.int32).reshape(1)
    return dest, blk_exp, nblk, n_rows


def _layer(x2d, bsz, seq, p, sinks, bias):
    t = x2d.shape[0]
    qa, ka, va, qb, kb, vb, gates = _proj_call(x2d, seq, p)
    oa = _swa_call(qa, ka, va, sinks, bias, bsz, seq)
    ob = _mla_call(qb, kb, vb, bsz, seq, p["score_bound"])
    x1, w0, w1, route, tile_cnt = _merge_call(x2d, oa, ob, gates, p)
    dest, blk_exp, nblk, n_rows = _dispatch(route, tile_cnt, MOE_ROWS)
    pos_tiles = dest.reshape(t // COMB_TM, COMB_TM, TOP_K).transpose(0, 2, 1).reshape(t * TOP_K)
    xr = _scatter_call(x1, pos_tiles, n_rows, COMB_TM)
    yr = _expert_call(xr, MOE_ROWS, blk_exp, nblk, p["gf"], p["wgu"], p["wd"])
    return _combine_call(x1, w0, w1, yr, pos_tiles, COMB_TM)


def kernel(x, attn_norm, w_in, b_gate, a_q_norm, a_k_norm, a_sinks, rel_bias, w_oa, q_a_norm, w_qb, kv_a_norm, w_kvb, b_q_norm, b_k_norm, w_ob, w_o, ffn_norm, w_router_group, b_router_group, w_router_expert, b_router_expert, w_exp_gate, w_exp_up, w_exp_down):
    bsz, seq, d = x.shape
    assert d == D_MODEL and seq % MLA_TQ == 0 and seq % PROJ_TM == 0 and (bsz * seq) % COMB_TM == 0
    depth = w_in.shape[0]
    cos, sin = _rope_tables(seq)
    bias = _swa_bias(rel_bias, SWA_TQ)
    x2d = x.reshape(bsz * seq, d)
    for l in range(depth):
        p = _prep_layer(l, seq, cos, sin, attn_norm, w_in, b_gate, a_q_norm, a_k_norm, w_oa, q_a_norm, w_qb,
                        kv_a_norm, w_kvb, b_q_norm, b_k_norm, w_ob, w_o, ffn_norm, w_router_group,
                        b_router_group, w_router_expert, b_router_expert, w_exp_gate, w_exp_up, w_exp_down)
        x2d = _layer(x2d, bsz, seq, p, a_sinks[l], bias)
    return x2d.reshape(bsz, seq, d)
```

```python
import functools
import math

import jax
import jax.numpy as jnp
from jax import lax
from jax.experimental import pallas as pl
from jax.experimental.pallas import tpu as pltpu

F32 = jnp.float32
BF16 = jnp.bfloat16

D_MODEL = 1024
A_HEADS = 8
A_KV_HEADS = 2
A_HEAD_DIM = 64
A_WIDTH = A_HEADS * A_HEAD_DIM
WINDOW = 128
REL_BUCKETS = 32
REL_MAX_DIST = 128
B_HEADS = 8
Q_LORA = 256
KV_LORA = 128
NOPE_DIM = 64
ROPE_DIM = 32
B_QK_DIM = NOPE_DIM + ROPE_DIM
B_V_DIM = 64
B_WIDTH = B_HEADS * B_V_DIM
ROPE_THETA = 10000.0
N_GROUPS = 4
EXPERTS_PER_GROUP = 8
N_EXPERTS = N_GROUPS * EXPERTS_PER_GROUP
TOP_K = 2
D_EXPERT = 256
EPS = 1e-6

LANES = 128
NEG = -1e30
LOG2E = math.log2(math.e)
V_ONE_EVEN = B_V_DIM
V_ONE_ODD = 0
EXP2_SAFE_BOUND = 100.0
VMEM_LIMIT = 56 * 1024 * 1024

C_QA = 0
C_KA = C_QA + A_WIDTH
C_VA = C_KA + LANES
C_CQ = C_VA + LANES
C_CKV = C_CQ + Q_LORA
C_KPE = C_CKV + KV_LORA
C_KPR = C_KPE + LANES
C_GATE = C_KPR + LANES
C_END = C_GATE + 2 * D_MODEL

PROJ_TM = 512
SWA_TQ = 128
MLA_TQ = 512
MERGE_TM = 512
MOE_ROWS = 256
COMB_TM = 256


def _cparams(sem):
    return pltpu.CompilerParams(dimension_semantics=sem, vmem_limit_bytes=VMEM_LIMIT)


def _proj_kernel(x_ref, gn_ref, w1_ref, wq2_ref, wk2_ref, wv2_ref, vone_ref, gcq_ref, gckv_ref, bg_ref,
                 gqa_ref, gka_ref, aq_ref, bq_ref, ak_ref, bk_ref,
                 qa_ref, ka_ref, va_ref, qb_ref, kb_ref, vb_ref, gate_ref):
    tm = x_ref.shape[0]
    x = x_ref[...]
    h = (x * lax.rsqrt(jnp.mean(x * x, axis=-1, keepdims=True) + EPS) * gn_ref[...]).astype(BF16)

    def proj(a, b):
        return jnp.dot(h, w1_ref[:, a:b], preferred_element_type=F32)

    lo = lax.broadcasted_iota(jnp.int32, (tm, LANES), 1) < A_HEAD_DIM

    def pair_norm(t, gain):
        t2 = t * t
        s_lo = jnp.sum(jnp.where(lo, t2, 0.0), axis=-1, keepdims=True)
        s_hi = jnp.sum(jnp.where(lo, 0.0, t2), axis=-1, keepdims=True)
        r = jnp.where(lo, lax.rsqrt(s_lo / A_HEAD_DIM + EPS), lax.rsqrt(s_hi / A_HEAD_DIM + EPS))
        return (t * r * gain).astype(BF16)

    qa = proj(C_QA, C_KA)
    for c in range(A_WIDTH // LANES):
        sl = slice(c * LANES, (c + 1) * LANES)
        qa_ref[:, sl] = pair_norm(qa[:, sl], gqa_ref[...])
    kvp = proj(C_KA, C_CQ)
    ka_ref[...] = pair_norm(kvp[:, :LANES], gka_ref[...])
    va_ref[...] = kvp[:, LANES:].astype(BF16)

    lat = proj(C_CQ, C_GATE)
    cq = lat[:, :Q_LORA]
    cqn = (cq * lax.rsqrt(jnp.mean(cq * cq, axis=-1, keepdims=True) + EPS) * gcq_ref[...]).astype(BF16)
    ckv = lat[:, Q_LORA:Q_LORA + KV_LORA]
    ckvn = (ckv * lax.rsqrt(jnp.mean(ckv * ckv, axis=-1, keepdims=True) + EPS) * gckv_ref[...]).astype(BF16)
    kpe = lat[:, C_KPE - C_CQ:C_KPR - C_CQ]
    kpr = lat[:, C_KPR - C_CQ:C_GATE - C_CQ]

    aq, bq, ak, bk = aq_ref[...], bq_ref[...], ak_ref[...], bk_ref[...]
    q2 = jnp.dot(cqn, wq2_ref[...], preferred_element_type=F32)
    k2 = jnp.dot(ckvn, wk2_ref[...], preferred_element_type=F32)
    vb_ref[...] = (jnp.dot(ckvn, wv2_ref[...], preferred_element_type=F32) + vone_ref[...]).astype(BF16)
    kr = kpr * bk
    for hd in range(B_HEADS):
        sl = slice(hd * LANES, (hd + 1) * LANES)
        qp = q2[:, sl]
        qr = q2[:, B_HEADS * LANES + hd * LANES:B_HEADS * LANES + (hd + 1) * LANES]
        rq = lax.rsqrt(jnp.sum(qp * qp, axis=-1, keepdims=True) / B_QK_DIM + EPS)
        qb_ref[:, sl] = (rq * (qp * aq + qr * bq)).astype(BF16)
        kp = k2[:, sl] + kpe
        rk = lax.rsqrt(jnp.sum(kp * kp, axis=-1, keepdims=True) / B_QK_DIM + EPS)
        kb_ref[:, sl] = (rk * (kp * ak + kr)).astype(BF16)

    z = proj(C_GATE, C_END) + bg_ref[...]
    gate_ref[...] = (1.0 / (1.0 + jnp.exp(-z))).astype(BF16)


def _proj_call(x2d, seq, p):
    t = x2d.shape[0]
    tm = PROJ_TM
    nt = t // tm
    npos = seq // tm
    row = lambda w: pl.BlockSpec((tm, w), lambda i: (i, 0))
    full = lambda a: pl.BlockSpec(a.shape, lambda i: (0,) * a.ndim)
    pos = pl.BlockSpec((tm, LANES), lambda i: (i % npos, 0))
    consts = [p["gn"], p["w1"], p["wq2"], p["wk2"], p["wv2"], p["vone"], p["gcq"], p["gckv"], p["bg"], p["gqa"],
              p["gka"]]
    tabs = [p["aq"], p["bq"], p["ak"], p["bk"]]
    widths = [A_WIDTH, LANES, LANES, B_HEADS * LANES, B_HEADS * LANES, B_HEADS * LANES, 2 * D_MODEL]
    return pl.pallas_call(
        _proj_kernel,
        grid=(nt,),
        in_specs=[row(D_MODEL)] + [full(a) for a in consts] + [pos] * 4,
        out_specs=[row(w) for w in widths],
        out_shape=[jax.ShapeDtypeStruct((t, w), BF16) for w in widths],
        compiler_params=_cparams(("parallel",)),
        name="proj",
    )(x2d, *consts, *tabs)


def _swa_kernel(sink_ref, q_ref, kc_ref, kp_ref, vc_ref, vp_ref, bias_ref, o_ref):
    tq = q_ref.shape[0]
    first = pl.program_id(1) == 0
    k = jnp.concatenate([kp_ref[...], kc_ref[...]], axis=0)
    v = jnp.concatenate([vp_ref[...], vc_ref[...]], axis=0)
    lo = lax.broadcasted_iota(jnp.int32, (tq, LANES), 1) < A_HEAD_DIM
    no_prev = jnp.logical_and(first, lax.broadcasted_iota(jnp.int32, (tq, 2 * tq), 1) < tq)
    for c in range(A_WIDTH // LANES):
        qc = q_ref[:, c * LANES:(c + 1) * LANES]
        outs = []
        for half in range(2):
            hd = c + (A_HEADS // 2) * half
            qm = jnp.where(lo if half == 0 else jnp.logical_not(lo), qc, jnp.zeros_like(qc))
            s = lax.dot_general(qm, k, (((1,), (1,)), ((), ())), preferred_element_type=F32)
            s = jnp.where(no_prev, NEG, s + bias_ref[hd])
            sink = sink_ref[hd]
            m = jnp.maximum(jnp.max(s, axis=-1, keepdims=True), sink)
            e = jnp.exp(s - m)
            denom = jnp.sum(e, axis=-1, keepdims=True) + jnp.exp(sink - m)
            pv = jnp.dot(e.astype(BF16), v, preferred_element_type=F32)
            outs.append(pv * (1.0 / denom))
        o_ref[:, c * LANES:(c + 1) * LANES] = jnp.where(lo, outs[0], outs[1]).astype(BF16)


def _swa_call(qa, ka, va, sinks, bias, bsz, seq):
    t = qa.shape[0]
    tq = SWA_TQ
    nq = seq // tq
    cur = lambda b, i, s: (b * nq + i, 0)
    prev = lambda b, i, s: (b * nq + jnp.maximum(i - 1, 0), 0)
    return pl.pallas_call(
        _swa_kernel,
        grid_spec=pltpu.PrefetchScalarGridSpec(
            num_scalar_prefetch=1,
            grid=(bsz, nq),
            in_specs=[pl.BlockSpec((tq, A_WIDTH), cur),
                      pl.BlockSpec((tq, LANES), cur), pl.BlockSpec((tq, LANES), prev),
                      pl.BlockSpec((tq, LANES), cur), pl.BlockSpec((tq, LANES), prev),
                      pl.BlockSpec(bias.shape, lambda b, i, s: (0, 0, 0))],
            out_specs=pl.BlockSpec((tq, A_WIDTH), cur)),
        out_shape=jax.ShapeDtypeStruct((t, A_WIDTH), BF16),
        compiler_params=_cparams(("parallel", "parallel")),
        name="swa",
    )(sinks, qa, ka, ka, va, va, bias)


def _mla_kernel(q_ref, k_ref, v_ref, o_ref, acc_sc, m_sc, *, shift):
    tq = q_ref.shape[0]
    tk = tq
    qi = pl.program_id(2)
    causal = (lax.broadcasted_iota(jnp.int32, (tq, tk), 1) <= lax.broadcasted_iota(jnp.int32, (tq, tk), 0))
    acc_sc[...] = jnp.zeros(acc_sc.shape, F32)
    if shift:
        m_sc[...] = jnp.full(m_sc.shape, NEG, F32)

    def step(j, masked):
        rows = pl.ds(pl.multiple_of(j * tk, tk), tk)
        for half in range(2):
            hs = slice(half * LANES, (half + 1) * LANES)
            s = lax.dot_general(q_ref[:, hs], k_ref[rows, hs], (((1,), (1,)), ((), ())), preferred_element_type=F32)
            if shift:
                if masked:
                    s = jnp.where(causal, s, NEG)
                m_prev = m_sc[half]
                m_new = jnp.maximum(m_prev, jnp.max(s, axis=-1, keepdims=True))
                m_sc[half] = m_new
                e = jnp.exp2(s - m_new)
                acc_sc[half] = jnp.exp2(m_prev - m_new) * acc_sc[half]
            else:
                e = jnp.exp2(s)
                if masked:
                    e = jnp.where(causal, e, 0.0)
            acc_sc[half] += jnp.dot(e.astype(BF16), v_ref[rows, hs], preferred_element_type=F32)

    def body(j, carry):
        step(j, False)
        return carry

    lax.fori_loop(0, qi, body, 0)
    step(qi, True)
    a0 = acc_sc[0]
    a1 = acc_sc[1]
    lo = lax.broadcasted_iota(jnp.int32, (tq, LANES), 1) < B_V_DIM
    o_ref[...] = jnp.where(lo, a0 * (1.0 / a0[:, V_ONE_EVEN:V_ONE_EVEN + 1]),
                           a1 * (1.0 / a1[:, V_ONE_ODD:V_ONE_ODD + 1])).astype(BF16)


def _mla_call(qb, kb, vb, bsz, seq, score_bound):
    t = qb.shape[0]
    tq = MLA_TQ
    nq = seq // tq
    pairs = B_HEADS // 2

    def call(shift):
        return pl.pallas_call(
            functools.partial(_mla_kernel, shift=shift),
            grid=(bsz, pairs, nq),
            in_specs=[pl.BlockSpec((tq, 2 * LANES), lambda b, p, i: (b * nq + i, p)),
                      pl.BlockSpec((seq, 2 * LANES), lambda b, p, i: (b, p)),
                      pl.BlockSpec((seq, 2 * LANES), lambda b, p, i: (b, p))],
            out_specs=pl.BlockSpec((tq, LANES), lambda b, p, i: (b * nq + i, p)),
            out_shape=jax.ShapeDtypeStruct((t, B_WIDTH), BF16),
            scratch_shapes=[pltpu.VMEM((2, tq, LANES), F32), pltpu.VMEM((2, tq, 1), F32)],
            compiler_params=_cparams(("parallel", "parallel", "arbitrary")),
            name="mla_shifted" if shift else "mla",
        )(qb, kb, vb)

    return lax.cond(score_bound <= EXP2_SAFE_BOUND, lambda: call(False), lambda: call(True))


def _merge_kernel(x_ref, oa_ref, ob_ref, gate_ref, woa_ref, wob_ref, wo_ref, gf_ref, wrh_ref, wrl_ref, br_ref,
                  tri_ref, x1_ref, w0_ref, w1_ref, eid_ref, cnt_ref):
    tm = x_ref.shape[0]
    ya = jnp.dot(oa_ref[...], woa_ref[...], preferred_element_type=F32)
    yb = jnp.dot(ob_ref[...], wob_ref[...], preferred_element_type=F32)
    g = gate_ref[...].astype(F32)
    mix = (g[:, :D_MODEL] * ya + g[:, D_MODEL:] * yb).astype(BF16)
    x1 = x_ref[...] + jnp.dot(mix, wo_ref[...], preferred_element_type=F32)
    x1_ref[...] = x1

    hn = x1 * lax.rsqrt(jnp.mean(x1 * x1, axis=-1, keepdims=True) + EPS) * gf_ref[...]
    hh = hn.astype(BF16)
    hl = (hn - hh.astype(F32)).astype(BF16)
    wrh = wrh_ref[...]
    logits = (jnp.dot(hh, wrh, preferred_element_type=F32) + jnp.dot(hl, wrh, preferred_element_type=F32)
              + jnp.dot(hh, wrl_ref[...], preferred_element_type=F32) + br_ref[...])

    lane = lax.broadcasted_iota(jnp.int32, (tm, LANES), 1).astype(F32)
    big = float(LANES)
    gmask = lane < N_GROUPS
    gl = jnp.where(gmask, logits, NEG)
    gmax = jnp.max(gl, axis=-1, keepdims=True)
    gsum = jnp.sum(jnp.where(gmask, jnp.exp(gl - gmax), 0.0), axis=-1, keepdims=True)
    g_p = 1.0 / gsum
    g_idx = jnp.min(jnp.where(gl == gmax, lane, big), axis=-1, keepdims=True)
    e_lo = N_GROUPS + EXPERTS_PER_GROUP * g_idx
    emask = jnp.logical_and(lane >= e_lo, lane < e_lo + EXPERTS_PER_GROUP)
    el = jnp.where(emask, logits, NEG)
    t1 = jnp.max(el, axis=-1, keepdims=True)
    i1 = jnp.min(jnp.where(el == t1, lane, big), axis=-1, keepdims=True)
    el2 = jnp.where(lane == i1, NEG, el)
    t2 = jnp.max(el2, axis=-1, keepdims=True)
    i2 = jnp.min(jnp.where(el2 == t2, lane, big), axis=-1, keepdims=True)
    e2 = jnp.exp(t2 - t1)
    w_first = g_p / (1.0 + e2)
    w0_ref[...] = jnp.broadcast_to(w_first, (tm, LANES))
    w1_ref[...] = jnp.broadcast_to(w_first * e2, (tm, LANES))
    ex1 = i1 - N_GROUPS
    ex2 = i2 - N_GROUPS

    oh1 = lane == ex1
    oh2 = lane == ex2
    cnt = (oh1.astype(F32) + oh2.astype(F32)).astype(BF16)
    before = jnp.dot(tri_ref[...], cnt, preferred_element_type=F32)
    r1 = jnp.sum(jnp.where(oh1, before, 0.0), axis=-1, keepdims=True)
    r2 = jnp.sum(jnp.where(oh2, before, 0.0), axis=-1, keepdims=True)
    ids = jnp.where(lane == 0.0, ex1, jnp.where(lane == 1.0, ex2, jnp.where(lane == 2.0, r1,
                                                                             jnp.where(lane == 3.0, r2, 0.0))))
    eid_ref[...] = ids.astype(jnp.int32)
    tile_cnt = jnp.sum(cnt.astype(F32), axis=0, keepdims=True)
    cnt_ref[...] = jnp.broadcast_to(tile_cnt, cnt_ref.shape[1:]).astype(jnp.int32)[None]


def _merge_call(x2d, oa, ob, gates, p):
    t = x2d.shape[0]
    tm = MERGE_TM
    nt = t // tm
    row = lambda w: pl.BlockSpec((tm, w), lambda i: (i, 0))
    full = lambda a: pl.BlockSpec(a.shape, lambda i: (0,) * a.ndim)
    tri = (jnp.arange(tm)[:, None] > jnp.arange(tm)[None, :]).astype(BF16)
    consts = [p["woa"], p["wob"], p["wo"], p["gf"], p["wrh"], p["wrl"], p["br"], tri]
    return pl.pallas_call(
        _merge_kernel,
        grid=(nt,),
        in_specs=[row(D_MODEL), row(A_WIDTH), row(B_WIDTH), row(2 * D_MODEL)] + [full(a) for a in consts],
        out_specs=[row(D_MODEL), row(LANES), row(LANES), row(LANES),
                   pl.BlockSpec((1, 8, LANES), lambda i: (i, 0, 0))],
        out_shape=[jax.ShapeDtypeStruct((t, D_MODEL), F32), jax.ShapeDtypeStruct((t, LANES), F32),
                   jax.ShapeDtypeStruct((t, LANES), F32), jax.ShapeDtypeStruct((t, LANES), jnp.int32),
                   jax.ShapeDtypeStruct((nt, 8, LANES), jnp.int32)],
        compiler_params=_cparams(("parallel",)),
        name="merge",
    )(x2d, oa, ob, gates, *consts)


def _rows_copy(src_hbm, row, dst, j, sem):
    return pltpu.make_async_copy(src_hbm.at[pl.ds(row, 1), :], dst.at[pl.ds(j, 1), :], sem)


def _start_rows(src_hbm, idx, base, dst, sem, n):
    def body(j, carry):
        _rows_copy(src_hbm, idx[base + j], dst, j, sem).start()
        return carry
    lax.fori_loop(0, n, body, 0, unroll=8)


WAIT_ROWS = 128


def _wait_rows(src_hbm, dst, sem, n):
    for c in range(n // WAIT_ROWS):
        rows = pl.ds(c * WAIT_ROWS, WAIT_ROWS)
        pltpu.make_async_copy(src_hbm.at[rows, :], dst.at[rows, :], sem).wait()


def _gather_step(i, n, src_hbm, idx_hbm, buf, idx, sem_rows, sem_idx):
    rows = buf.shape[1]
    slot = i % 2
    nslot = 1 - slot

    def idx_copy(b, s):
        return pltpu.make_async_copy(idx_hbm.at[pl.ds(b * rows, rows)], idx.at[pl.ds(s * rows, rows)], sem_idx.at[s])

    @pl.when(jnp.logical_and(i == 0, n > 0))
    def _():
        idx_copy(0, 0).start()
        idx_copy(0, 0).wait()
        _start_rows(src_hbm, idx, 0, buf.at[0], sem_rows.at[0], rows)

        @pl.when(n > 1)
        def _():
            idx_copy(1, 1).start()

    @pl.when(i + 1 < n)
    def _():
        idx_copy(i + 1, nslot).wait()
        _start_rows(src_hbm, idx, nslot * rows, buf.at[nslot], sem_rows.at[nslot], rows)

    @pl.when(i + 2 < n)
    def _():
        idx_copy(i + 2, slot).start()

    @pl.when(i < n)
    def _():
        _wait_rows(src_hbm, buf.at[slot], sem_rows.at[slot], rows)
    return slot


SCATTER_BUFS = 3


def _scatter_kernel(x_hbm, dest_hbm, init_hbm, xr_hbm, buf, idx, sem_x, sem_rows, sem_idx):
    del init_hbm
    i = pl.program_id(0)
    n = pl.num_programs(0)
    tm = buf.shape[1]
    rows = TOP_K * tm
    islot = i % 2
    xslot = i % SCATTER_BUFS

    def idx_copy(b, s):
        return pltpu.make_async_copy(dest_hbm.at[pl.ds(b * rows, rows)], idx.at[pl.ds(s * rows, rows)], sem_idx.at[s])

    def x_copy(b, s):
        return pltpu.make_async_copy(x_hbm.at[pl.ds(b * tm, tm), :], buf.at[s], sem_x.at[s])

    def wait_rows(s):
        for k in range(TOP_K):
            pltpu.make_async_copy(buf.at[s], xr_hbm.at[pl.ds(0, tm), :], sem_rows.at[s]).wait()

    @pl.when(i == 0)
    def _():
        idx_copy(0, 0).start()
        x_copy(0, 0).start()

    @pl.when(i >= 2)
    def _():
        wait_rows((i - 2) % SCATTER_BUFS)

    @pl.when(i + 1 < n)
    def _():
        idx_copy(i + 1, 1 - islot).start()
        x_copy(i + 1, (i + 1) % SCATTER_BUFS).start()

    idx_copy(i, islot).wait()
    x_copy(i, xslot).wait()

    def body(j, carry):
        src = buf.at[xslot, pl.ds(j, 1), :]
        for k in range(TOP_K):
            dst = xr_hbm.at[pl.ds(idx[islot * rows + k * tm + j], 1), :]
            pltpu.make_async_copy(src, dst, sem_rows.at[xslot]).start()
        return carry
    lax.fori_loop(0, tm, body, 0, unroll=8)

    @pl.when(i == n - 1)
    def _():
        @pl.when(n > 1)
        def _():
            wait_rows((i - 1) % SCATTER_BUFS)
        wait_rows(xslot)


def _scatter_call(x1, pos_tiles, n_rows, tm):
    t = x1.shape[0]
    rows = TOP_K * tm
    any_spec = pl.BlockSpec(memory_space=pl.ANY)
    return pl.pallas_call(
        _scatter_kernel,
        grid=(t // tm,),
        in_specs=[any_spec, any_spec, any_spec],
        out_specs=any_spec,
        out_shape=jax.ShapeDtypeStruct((n_rows, D_MODEL), F32),
        input_output_aliases={2: 0},
        scratch_shapes=[pltpu.VMEM((SCATTER_BUFS, tm, D_MODEL), F32), pltpu.SMEM((2 * rows,), jnp.int32),
                        pltpu.SemaphoreType.DMA((SCATTER_BUFS,)), pltpu.SemaphoreType.DMA((SCATTER_BUFS,)),
                        pltpu.SemaphoreType.DMA((2,))],
        compiler_params=_cparams(("arbitrary",)),
        name="scatter",
    )(x1, pos_tiles, jnp.zeros((n_rows, D_MODEL), F32))


def _expert_kernel(blk_exp_ref, nblk_ref, x_ref, gf_ref, wgu_ref, wd_ref, y_ref):
    del blk_exp_ref
    i = pl.program_id(0)

    @pl.when(i < nblk_ref[0])
    def _():
        x = x_ref[...]
        hn = (x * lax.rsqrt(jnp.mean(x * x, axis=-1, keepdims=True) + EPS) * gf_ref[...]).astype(BF16)
        gu = jnp.dot(hn, wgu_ref[0], preferred_element_type=F32)
        gt, up = gu[:, :D_EXPERT], gu[:, D_EXPERT:]
        act = (gt * (1.0 / (1.0 + jnp.exp(-gt))) * up).astype(BF16)
        y_ref[...] = jnp.dot(act, wd_ref[0], preferred_element_type=F32)

    @pl.when(i >= nblk_ref[0])
    def _():
        y_ref[...] = jnp.zeros(y_ref.shape, F32)


def _expert_call(xr, rows, blk_exp, nblk, gf, wgu, wd):
    nb = xr.shape[0] // rows
    return pl.pallas_call(
        _expert_kernel,
        grid_spec=pltpu.PrefetchScalarGridSpec(
            num_scalar_prefetch=2,
            grid=(nb,),
            in_specs=[pl.BlockSpec((rows, D_MODEL), lambda i, be, n: (i, 0)),
                      pl.BlockSpec(gf.shape, lambda i, be, n: (0, 0)),
                      pl.BlockSpec((1, D_MODEL, 2 * D_EXPERT), lambda i, be, n: (be[i], 0, 0)),
                      pl.BlockSpec((1, D_EXPERT, D_MODEL), lambda i, be, n: (be[i], 0, 0))],
            out_specs=pl.BlockSpec((rows, D_MODEL), lambda i, be, n: (i, 0))),
        out_shape=jax.ShapeDtypeStruct((nb * rows, D_MODEL), F32),
        compiler_params=_cparams(("arbitrary",)),
        name="experts",
    )(blk_exp, nblk, xr, gf, wgu, wd)


def _combine_kernel(x1_ref, w0_ref, w1_ref, y_hbm, pos_hbm, o_ref, buf, idx, sem_rows, sem_idx):
    rows = buf.shape[1]
    tm = rows // 2
    slot = _gather_step(pl.program_id(0), pl.num_programs(0), y_hbm, pos_hbm, buf, idx, sem_rows, sem_idx)
    w0 = w0_ref[...]
    w1 = w1_ref[...]
    for c in range(D_MODEL // LANES):
        sl = slice(c * LANES, (c + 1) * LANES)
        o_ref[:, sl] = x1_ref[:, sl] + w0 * buf[slot, 0:tm, sl] + w1 * buf[slot, tm:rows, sl]


def _combine_call(x1, w0, w1, yr, pos_tiles, tm):
    t = x1.shape[0]
    rows = TOP_K * tm
    row = lambda w: pl.BlockSpec((tm, w), lambda i: (i, 0))
    return pl.pallas_call(
        _combine_kernel,
        grid=(t // tm,),
        in_specs=[row(D_MODEL), row(LANES), row(LANES),
                  pl.BlockSpec(memory_space=pl.ANY), pl.BlockSpec(memory_space=pl.ANY)],
        out_specs=row(D_MODEL),
        out_shape=jax.ShapeDtypeStruct((t, D_MODEL), F32),
        scratch_shapes=[pltpu.VMEM((2, rows, D_MODEL), F32), pltpu.SMEM((2 * rows,), jnp.int32),
                        pltpu.SemaphoreType.DMA((2,)), pltpu.SemaphoreType.DMA((2,))],
        compiler_params=_cparams(("arbitrary",)),
        name="combine",
    )(x1, w0, w1, yr, pos_tiles)


def _rot_cols(w):
    half = ROPE_DIM // 2
    return jnp.concatenate([-w[..., half:], w[..., :half]], axis=-1)


def _head_pad(nope, rope):
    z = jnp.zeros(rope.shape[:-1] + (LANES - B_QK_DIM,), rope.dtype)
    out = jnp.concatenate([nope, rope, z], axis=-1)
    return out.reshape(out.shape[:-2] + (out.shape[-2] * LANES,))


def _t5_bucket(dist):
    max_exact = REL_BUCKETS // 2
    d = jnp.maximum(dist, 0)
    ratio = jnp.maximum(d, 1).astype(F32) / max_exact
    large = max_exact + (jnp.log(ratio) / math.log(REL_MAX_DIST / max_exact)
                         * (REL_BUCKETS - max_exact)).astype(jnp.int32)
    large = jnp.minimum(large, REL_BUCKETS - 1)
    return jnp.where(d < max_exact, d, large)


def _swa_bias(rel_bias, tq):
    qi = jnp.arange(tq)[:, None]
    kj = jnp.arange(2 * tq)[None, :]
    dist = qi + tq - kj
    bias = rel_bias[_t5_bucket(dist)].astype(F32).transpose(2, 0, 1)
    mask = (dist >= 0) & (dist < WINDOW)
    return jnp.where(mask[None], bias, NEG)


def _rope_tables(seq):
    half = ROPE_DIM // 2
    inv_freq = ROPE_THETA ** (-jnp.arange(half, dtype=F32) / half)
    ang = jnp.arange(seq, dtype=F32)[:, None] * inv_freq[None, :]
    cos = jnp.concatenate([jnp.cos(ang), jnp.cos(ang)], axis=-1)
    sin = jnp.concatenate([jnp.sin(ang), jnp.sin(ang)], axis=-1)
    return cos, sin


def _lane_table(nope_part, rope_part):
    seq = rope_part.shape[0]
    nope_part = jnp.broadcast_to(nope_part, (seq, NOPE_DIM))
    return jnp.concatenate([nope_part, rope_part, jnp.zeros((seq, LANES - B_QK_DIM), F32)], axis=-1)


def _prep_layer(l, seq, cos, sin, attn_norm, w_in, b_gate, a_q_norm, a_k_norm, w_oa, q_a_norm, w_qb, kv_a_norm,
                w_kvb, b_q_norm, b_k_norm, w_ob, w_o, ffn_norm, w_rg, b_rg, w_re, b_re, w_eg, w_eu, w_ed):
    w = w_in[l]
    o = 0
    parts = []
    for sz in (A_WIDTH, A_KV_HEADS * A_HEAD_DIM, A_KV_HEADS * A_HEAD_DIM, Q_LORA, KV_LORA, ROPE_DIM, 2 * D_MODEL):
        parts.append(w[:, o:o + sz])
        o += sz
    wqa, wka, wva, wcq, wckv, wkpe, wg = parts
    perm = jnp.array([c + (A_HEADS // 2) * h for c in range(A_HEADS // 2) for h in range(2)])
    wqa = wqa.reshape(D_MODEL, A_HEADS, A_HEAD_DIM)[:, perm].reshape(D_MODEL, A_WIDTH)
    zl = jnp.zeros((D_MODEL, NOPE_DIM), F32)
    zr = jnp.zeros((D_MODEL, LANES - B_QK_DIM), F32)
    w1 = jnp.concatenate([wqa, wka, wva, wcq, wckv, zl, wkpe, zr, zl, _rot_cols(wkpe), zr, wg], axis=1).astype(BF16)

    wq = w_qb[l].reshape(Q_LORA, B_HEADS, B_QK_DIM)
    zq = jnp.zeros((Q_LORA, B_HEADS, NOPE_DIM), F32)
    wq2 = jnp.concatenate([_head_pad(wq[..., :NOPE_DIM], wq[..., NOPE_DIM:]),
                           _head_pad(zq, _rot_cols(wq[..., NOPE_DIM:]))], axis=1).astype(BF16)
    wkv = w_kvb[l].reshape(KV_LORA, B_HEADS, NOPE_DIM + B_V_DIM)
    wk2 = _head_pad(wkv[..., :NOPE_DIM], jnp.zeros((KV_LORA, B_HEADS, ROPE_DIM), F32)).astype(BF16)
    wv = wkv[..., NOPE_DIM:]
    zv = jnp.zeros_like(wv)
    wv2 = jnp.where((jnp.arange(B_HEADS) % 2 == 0)[None, :, None], jnp.concatenate([wv, zv], -1),
                    jnp.concatenate([zv, wv], -1)).reshape(KV_LORA, B_HEADS * LANES).astype(BF16)
    one_lane = jnp.where(jnp.arange(B_HEADS) % 2 == 0, V_ONE_EVEN, V_ONE_ODD)
    vone = (jnp.arange(LANES)[None, :] == one_lane[:, None]).astype(F32).reshape(1, B_HEADS * LANES)

    def rope_tabs(gain, scale):
        gn, gr = gain[:NOPE_DIM], gain[NOPE_DIM:]
        gr_rot = jnp.concatenate([gr[ROPE_DIM // 2:], gr[:ROPE_DIM // 2]])
        a = _lane_table(gn[None, :] * scale, gr[None, :] * cos * scale)
        b = _lane_table(jnp.zeros((1, NOPE_DIM), F32), gr_rot[None, :] * sin * scale)
        return a, b

    aq, bq = rope_tabs(b_q_norm[l], B_QK_DIM ** -0.5 * LOG2E)
    score_bound = math.sqrt(B_QK_DIM) * LOG2E * jnp.max(jnp.abs(b_q_norm[l])) * jnp.max(jnp.abs(b_k_norm[l]))
    ak, bk = rope_tabs(b_k_norm[l], 1.0)

    gqa = (jnp.tile(a_q_norm[l], 2) * A_HEAD_DIM ** -0.5)[None, :]
    gka = jnp.tile(a_k_norm[l], 2)[None, :]
    woa = w_oa[l].reshape(A_HEADS, A_HEAD_DIM, D_MODEL)[perm].reshape(A_WIDTH, D_MODEL).astype(BF16)

    wr = jnp.concatenate([w_rg[l], w_re[l], jnp.zeros((D_MODEL, LANES - N_GROUPS - N_EXPERTS), F32)], axis=1)
    wrh = wr.astype(BF16)
    wrl = (wr - wrh.astype(F32)).astype(BF16)
    br = jnp.concatenate([b_rg[l], b_re[l], jnp.zeros((LANES - N_GROUPS - N_EXPERTS,), F32)])[None, :]

    return dict(
        gn=attn_norm[l][None, :], w1=w1, wq2=wq2, wk2=wk2, wv2=wv2, vone=vone, gcq=q_a_norm[l][None, :],
        gckv=kv_a_norm[l][None, :], score_bound=score_bound, bg=b_gate[l][None, :], gqa=gqa, gka=gka,
        aq=aq, bq=bq, ak=ak, bk=bk,
        woa=woa, wob=w_ob[l].astype(BF16), wo=w_o[l].astype(BF16), gf=ffn_norm[l][None, :], wrh=wrh, wrl=wrl, br=br,
        wgu=jnp.concatenate([w_eg[l], w_eu[l]], axis=-1).astype(BF16), wd=w_ed[l].astype(BF16))


def _tril(n):
    return (jnp.arange(n)[:, None] >= jnp.arange(n)[None, :]).astype(jnp.int32)


def _dispatch(route, tile_cnt, rows):
    t = route.shape[0]
    nt = tile_cnt.shape[0]
    tile_cnt = tile_cnt[:, 0, :N_EXPERTS]
    counts = jnp.sum(tile_cnt, axis=0)
    padded = ((counts + rows - 1) // rows) * rows
    pends = jnp.dot(_tril(N_EXPERTS), padded)
    pstarts = pends - padded
    tile_base = pstarts[None, :] + jnp.dot(_tril(nt), tile_cnt) - tile_cnt
    base = jnp.broadcast_to(tile_base[:, None, None, :], (nt, t // nt, 1, N_EXPERTS)).reshape(t, 1, N_EXPERTS)
    onehot = route[:, :TOP_K, None] == jnp.arange(N_EXPERTS, dtype=jnp.int32)[None, None, :]
    dest = jnp.sum(jnp.where(onehot, base, 0), axis=-1) + route[:, TOP_K:2 * TOP_K]
    n_rows = t * TOP_K + N_EXPERTS * rows
    n_blocks = n_rows // rows
    blk_start = jnp.arange(n_blocks, dtype=jnp.int32)[:, None] * rows
    blk_exp = jnp.minimum(jnp.sum((pends[None, :] <= blk_start).astype(jnp.int32), axis=1), N_EXPERTS - 1)
    nblk = (pends[-1] // rows).astype(jnp.int32).reshape(1)
    return dest, blk_exp, nblk, n_rows


def _layer(x2d, bsz, seq, p, sinks, bias):
    t = x2d.shape[0]
    qa, ka, va, qb, kb, vb, gates = _proj_call(x2d, seq, p)
    oa = _swa_call(qa, ka, va, sinks, bias, bsz, seq)
    ob = _mla_call(qb, kb, vb, bsz, seq, p["score_bound"])
    x1, w0, w1, route, tile_cnt = _merge_call(x2d, oa, ob, gates, p)
    dest, blk_exp, nblk, n_rows = _dispatch(route, tile_cnt, MOE_ROWS)
    pos_tiles = dest.reshape(t // COMB_TM, COMB_TM, TOP_K).transpose(0, 2, 1).reshape(t * TOP_K)
    xr = _scatter_call(x1, pos_tiles, n_rows, COMB_TM)
    yr = _expert_call(xr, MOE_ROWS, blk_exp, nblk, p["gf"], p["wgu"], p["wd"])
    return _combine_call(x1, w0, w1, yr, pos_tiles, COMB_TM)


def kernel(x, attn_norm, w_in, b_gate, a_q_norm, a_k_norm, a_sinks, rel_bias, w_oa, q_a_norm, w_qb, kv_a_norm, w_kvb, b_q_norm, b_k_norm, w_ob, w_o, ffn_norm, w_router_group, b_router_group, w_router_expert, b_router_expert, w_exp_gate, w_exp_up, w_exp_down):
    bsz, seq, d = x.shape
    assert d == D_MODEL and seq % MLA_TQ == 0 and seq % PROJ_TM == 0 and (bsz * seq) % COMB_TM == 0
    depth = w_in.shape[0]
    cos, sin = _rope_tables(seq)
    bias = _swa_bias(rel_bias, SWA_TQ)
    x2d = x.reshape(bsz * seq, d)
    for l in range(depth):
        p = _prep_layer(l, seq, cos, sin, attn_norm, w_in, b_gate, a_q_norm, a_k_norm, w_oa, q_a_norm, w_qb,
                        kv_a_norm, w_kvb, b_q_norm, b_k_norm, w_ob, w_o, ffn_norm, w_router_group,
                        b_router_group, w_router_expert, b_router_expert, w_exp_gate, w_exp_up, w_exp_down)
        x2d = _layer(x2d, bsz, seq, p, a_sinks[l], bias)
    return x2d.reshape(bsz, seq, d)
```

```python
import functools
import math

import jax
import jax.numpy as jnp
from jax import lax
from jax.experimental import pallas as pl
from jax.experimental.pallas import tpu as pltpu

F32 = jnp.float32
BF16 = jnp.bfloat16

D_MODEL = 1024
A_HEADS = 8
A_KV_HEADS = 2
A_HEAD_DIM = 64
A_WIDTH = A_HEADS * A_HEAD_DIM
WINDOW = 128
REL_BUCKETS = 32
REL_MAX_DIST = 128
B_HEADS = 8
Q_LORA = 256
KV_LORA = 128
NOPE_DIM = 64
ROPE_DIM = 32
B_QK_DIM = NOPE_DIM + ROPE_DIM
B_V_DIM = 64
B_WIDTH = B_HEADS * B_V_DIM
ROPE_THETA = 10000.0
N_GROUPS = 4
EXPERTS_PER_GROUP = 8
N_EXPERTS = N_GROUPS * EXPERTS_PER_GROUP
TOP_K = 2
D_EXPERT = 256
EPS = 1e-6

LANES = 128
NEG = -1e30
LOG2E = math.log2(math.e)
V_ONE_EVEN = B_V_DIM
V_ONE_ODD = 0
EXP2_SAFE_BOUND = 100.0
VMEM_LIMIT = 56 * 1024 * 1024

C_QA = 0
C_KA = C_QA + A_WIDTH
C_VA = C_KA + LANES
C_CQ = C_VA + LANES
C_CKV = C_CQ + Q_LORA
C_KPE = C_CKV + KV_LORA
C_KPR = C_KPE + LANES
C_GATE = C_KPR + LANES
C_END = C_GATE + 2 * D_MODEL

PROJ_TM = 512
SWA_TQ = 128
SWA_SUB = 4
MLA_TQ = 512
MERGE_TM = 512
MOE_ROWS = 256
COMB_TM = 256


def _cparams(sem):
    return pltpu.CompilerParams(dimension_semantics=sem, vmem_limit_bytes=VMEM_LIMIT)


def _proj_kernel(x_ref, gn_ref, w1_ref, wq2_ref, wk2_ref, wv2_ref, vone_ref, gcq_ref, gckv_ref, bg_ref,
                 gqa_ref, gka_ref, aq_ref, bq_ref, ak_ref, bk_ref,
                 qa_ref, ka_ref, va_ref, qb_ref, kb_ref, vb_ref, gate_ref):
    tm = x_ref.shape[0]
    x = x_ref[...]
    h = (x * lax.rsqrt(jnp.mean(x * x, axis=-1, keepdims=True) + EPS) * gn_ref[...]).astype(BF16)

    def proj(a, b):
        return jnp.dot(h, w1_ref[:, a:b], preferred_element_type=F32)

    lo = lax.broadcasted_iota(jnp.int32, (tm, LANES), 1) < A_HEAD_DIM

    def pair_norm(t, gain):
        t2 = t * t
        s_lo = jnp.sum(jnp.where(lo, t2, 0.0), axis=-1, keepdims=True)
        s_hi = jnp.sum(jnp.where(lo, 0.0, t2), axis=-1, keepdims=True)
        r = jnp.where(lo, lax.rsqrt(s_lo / A_HEAD_DIM + EPS), lax.rsqrt(s_hi / A_HEAD_DIM + EPS))
        return (t * r * gain).astype(BF16)

    qa = proj(C_QA, C_KA)
    for c in range(A_WIDTH // LANES):
        sl = slice(c * LANES, (c + 1) * LANES)
        qa_ref[:, sl] = pair_norm(qa[:, sl], gqa_ref[...])
    kvp = proj(C_KA, C_CQ)
    ka_ref[...] = pair_norm(kvp[:, :LANES], gka_ref[...])
    va_ref[...] = kvp[:, LANES:].astype(BF16)

    lat = proj(C_CQ, C_GATE)
    cq = lat[:, :Q_LORA]
    cqn = (cq * lax.rsqrt(jnp.mean(cq * cq, axis=-1, keepdims=True) + EPS) * gcq_ref[...]).astype(BF16)
    ckv = lat[:, Q_LORA:Q_LORA + KV_LORA]
    ckvn = (ckv * lax.rsqrt(jnp.mean(ckv * ckv, axis=-1, keepdims=True) + EPS) * gckv_ref[...]).astype(BF16)
    kpe = lat[:, C_KPE - C_CQ:C_KPR - C_CQ]
    kpr = lat[:, C_KPR - C_CQ:C_GATE - C_CQ]

    aq, bq, ak, bk = aq_ref[...], bq_ref[...], ak_ref[...], bk_ref[...]
    q2 = jnp.dot(cqn, wq2_ref[...], preferred_element_type=F32)
    k2 = jnp.dot(ckvn, wk2_ref[...], preferred_element_type=F32)
    vb_ref[...] = (jnp.dot(ckvn, wv2_ref[...], preferred_element_type=F32) + vone_ref[...]).astype(BF16)
    kr = kpr * bk
    for hd in range(B_HEADS):
        sl = slice(hd * LANES, (hd + 1) * LANES)
        qp = q2[:, sl]
        qr = q2[:, B_HEADS * LANES + hd * LANES:B_HEADS * LANES + (hd + 1) * LANES]
        rq = lax.rsqrt(jnp.sum(qp * qp, axis=-1, keepdims=True) / B_QK_DIM + EPS)
        qb_ref[:, sl] = (rq * (qp * aq + qr * bq)).astype(BF16)
        kp = k2[:, sl] + kpe
        rk = lax.rsqrt(jnp.sum(kp * kp, axis=-1, keepdims=True) / B_QK_DIM + EPS)
        kb_ref[:, sl] = (rk * (kp * ak + kr)).astype(BF16)

    z = proj(C_GATE, C_END) + bg_ref[...]
    gate_ref[...] = (1.0 / (1.0 + jnp.exp(-z))).astype(BF16)


def _proj_call(x2d, seq, p):
    t = x2d.shape[0]
    tm = PROJ_TM
    nt = t // tm
    npos = seq // tm
    row = lambda w: pl.BlockSpec((tm, w), lambda i: (i, 0))
    full = lambda a: pl.BlockSpec(a.shape, lambda i: (0,) * a.ndim)
    pos = pl.BlockSpec((tm, LANES), lambda i: (i % npos, 0))
    consts = [p["gn"], p["w1"], p["wq2"], p["wk2"], p["wv2"], p["vone"], p["gcq"], p["gckv"], p["bg"], p["gqa"],
              p["gka"]]
    tabs = [p["aq"], p["bq"], p["ak"], p["bk"]]
    widths = [A_WIDTH, LANES, LANES, B_HEADS * LANES, B_HEADS * LANES, B_HEADS * LANES, 2 * D_MODEL]
    return pl.pallas_call(
        _proj_kernel,
        grid=(nt,),
        in_specs=[row(D_MODEL)] + [full(a) for a in consts] + [pos] * 4,
        out_specs=[row(w) for w in widths],
        out_shape=[jax.ShapeDtypeStruct((t, w), BF16) for w in widths],
        compiler_params=_cparams(("parallel",)),
        name="proj",
    )(x2d, *consts, *tabs)


def _swa_kernel(q_ref, kc_ref, kp_ref, vc_ref, vp_ref, bias_ref, sink_ref, o_ref, *, shift):
    tq = SWA_TQ
    groups = A_WIDTH // LANES
    first = pl.program_id(1) == 0
    one_col = (lax.broadcasted_iota(jnp.int32, (2 * tq, LANES), 1) == 0).astype(BF16)
    lo = lax.broadcasted_iota(jnp.int32, (tq, LANES), 1) < A_HEAD_DIM
    no_prev = jnp.logical_and(first, lax.broadcasted_iota(jnp.int32, (groups * tq, 2 * tq), 1) < tq)
    for sub in range(SWA_SUB):
        if sub == 0:
            k = jnp.concatenate([kp_ref[...], kc_ref[0:tq, :]], axis=0)
            v = jnp.concatenate([vp_ref[...], vc_ref[0:tq, :]], axis=0)
        else:
            k = kc_ref[(sub - 1) * tq:(sub + 1) * tq, :]
            v = vc_ref[(sub - 1) * tq:(sub + 1) * tq, :]
        v_ext = jnp.concatenate([v, one_col], axis=1)
        qrows = slice(sub * tq, (sub + 1) * tq)
        qs = [q_ref[qrows, c * LANES:(c + 1) * LANES] for c in range(groups)]
        res = []
        for g in range(A_KV_HEADS):
            keep = lo if g == 0 else jnp.logical_not(lo)
            qg = jnp.concatenate([jnp.where(keep, qc, jnp.zeros_like(qc)) for qc in qs], axis=0)
            s = lax.dot_general(qg, k, (((1,), (1,)), ((), ())), preferred_element_type=F32) + bias_ref[g]
            sink = sink_ref[g][:, :1]
            if shift:
                if sub == 0:
                    s = jnp.where(no_prev, NEG, s)
                m = jnp.maximum(jnp.max(s, axis=-1, keepdims=True), sink)
                e = jnp.exp2(s - m)
                sink_term = jnp.exp2(sink - m)
            else:
                e = jnp.exp2(s)
                if sub == 0:
                    e = jnp.where(no_prev, 0.0, e)
                sink_term = jnp.exp2(sink)
            pv = jnp.dot(e.astype(BF16), v_ext, preferred_element_type=F32)
            res.append(pv[:, :LANES] * (1.0 / (pv[:, LANES:LANES + 1] + sink_term)))
        for c in range(groups):
            rows = slice(c * tq, (c + 1) * tq)
            o_ref[qrows, c * LANES:(c + 1) * LANES] = jnp.where(lo, res[0][rows], res[1][rows]).astype(BF16)


def _swa_call(qa, ka, va, sink_tab, bias, bsz, seq, score_bound):
    t = qa.shape[0]
    tq = SWA_TQ
    step = SWA_SUB * tq
    ns = seq // step
    cur = lambda b, i: (b * ns + i, 0)
    prev = lambda b, i: (b * (seq // tq) + jnp.maximum(SWA_SUB * i - 1, 0), 0)
    full = lambda a: pl.BlockSpec(a.shape, lambda b, i: (0,) * a.ndim)

    def call(shift):
        return pl.pallas_call(
            functools.partial(_swa_kernel, shift=shift),
            grid=(bsz, ns),
            in_specs=[pl.BlockSpec((step, A_WIDTH), cur),
                      pl.BlockSpec((step, LANES), cur), pl.BlockSpec((tq, LANES), prev),
                      pl.BlockSpec((step, LANES), cur), pl.BlockSpec((tq, LANES), prev),
                      full(bias), full(sink_tab)],
            out_specs=pl.BlockSpec((step, A_WIDTH), cur),
            out_shape=jax.ShapeDtypeStruct((t, A_WIDTH), BF16),
            compiler_params=_cparams(("parallel", "parallel")),
            name="swa_shifted" if shift else "swa",
        )(qa, ka, ka, va, va, bias, sink_tab)

    return lax.cond(score_bound <= EXP2_SAFE_BOUND, lambda: call(False), lambda: call(True))


def _mla_kernel(q_ref, k_ref, v_ref, o_ref, acc_sc, m_sc, *, shift):
    tq = q_ref.shape[0]
    tk = tq
    qi = pl.program_id(2)
    acc_sc[...] = jnp.zeros(acc_sc.shape, F32)
    if shift:
        m_sc[...] = jnp.full(m_sc.shape, NEG, F32)

    def tile(j, r0, r1, nk, masked):
        keys = pl.ds(pl.multiple_of(j * tk, tk), nk)
        if masked:
            causal = (lax.broadcasted_iota(jnp.int32, (r1 - r0, nk), 1)
                      <= lax.broadcasted_iota(jnp.int32, (r1 - r0, nk), 0) + r0)
        for half in range(2):
            hs = slice(half * LANES, (half + 1) * LANES)
            s = lax.dot_general(q_ref[r0:r1, hs], k_ref[keys, hs], (((1,), (1,)), ((), ())),
                                preferred_element_type=F32)
            if shift:
                if masked:
                    s = jnp.where(causal, s, NEG)
                m_prev = m_sc[half, r0:r1]
                m_new = jnp.maximum(m_prev, jnp.max(s, axis=-1, keepdims=True))
                m_sc[half, r0:r1] = m_new
                e = jnp.exp2(s - m_new)
                acc_sc[half, r0:r1] = jnp.exp2(m_prev - m_new) * acc_sc[half, r0:r1]
            else:
                e = jnp.exp2(s)
                if masked:
                    e = jnp.where(causal, e, 0.0)
            acc_sc[half, r0:r1] += jnp.dot(e.astype(BF16), v_ref[keys, hs], preferred_element_type=F32)

    def body(j, carry):
        tile(j, 0, tq, tk, False)
        return carry

    lax.fori_loop(0, qi, body, 0)
    tile(qi, 0, tq, tk, True)
    a0 = acc_sc[0]
    a1 = acc_sc[1]
    lo = lax.broadcasted_iota(jnp.int32, (tq, LANES), 1) < B_V_DIM
    o_ref[...] = jnp.where(lo, a0 * (1.0 / a0[:, V_ONE_EVEN:V_ONE_EVEN + 1]),
                           a1 * (1.0 / a1[:, V_ONE_ODD:V_ONE_ODD + 1])).astype(BF16)


def _mla_call(qb, kb, vb, bsz, seq, score_bound):
    t = qb.shape[0]
    tq = MLA_TQ
    nq = seq // tq
    pairs = B_HEADS // 2

    def call(shift):
        return pl.pallas_call(
            functools.partial(_mla_kernel, shift=shift),
            grid=(bsz, pairs, nq),
            in_specs=[pl.BlockSpec((tq, 2 * LANES), lambda b, p, i: (b * nq + i, p)),
                      pl.BlockSpec((seq, 2 * LANES), lambda b, p, i: (b, p)),
                      pl.BlockSpec((seq, 2 * LANES), lambda b, p, i: (b, p))],
            out_specs=pl.BlockSpec((tq, LANES), lambda b, p, i: (b * nq + i, p)),
            out_shape=jax.ShapeDtypeStruct((t, B_WIDTH), BF16),
            scratch_shapes=[pltpu.VMEM((2, tq, LANES), F32), pltpu.VMEM((2, tq, 1), F32)],
            compiler_params=_cparams(("parallel", "parallel", "arbitrary")),
            name="mla_shifted" if shift else "mla",
        )(qb, kb, vb)

    return lax.cond(score_bound <= EXP2_SAFE_BOUND, lambda: call(False), lambda: call(True))


def _merge_kernel(x_ref, oa_ref, ob_ref, gate_ref, woa_ref, wob_ref, wo_ref, gf_ref, wr_ref, br_ref,
                  tri_ref, x1_ref, w0_ref, w1_ref, eid_ref, cnt_ref):
    tm = x_ref.shape[0]
    ya = jnp.dot(oa_ref[...], woa_ref[...], preferred_element_type=F32)
    yb = jnp.dot(ob_ref[...], wob_ref[...], preferred_element_type=F32)
    g = gate_ref[...].astype(F32)
    mix = (g[:, :D_MODEL] * ya + g[:, D_MODEL:] * yb).astype(BF16)
    x1 = x_ref[...] + jnp.dot(mix, wo_ref[...], preferred_element_type=F32)
    x1_ref[...] = x1

    hn = x1 * lax.rsqrt(jnp.mean(x1 * x1, axis=-1, keepdims=True) + EPS) * gf_ref[...]
    hh = hn.astype(BF16)
    hl = (hn - hh.astype(F32)).astype(BF16)
    ph = jnp.dot(hh, wr_ref[...], preferred_element_type=F32)
    pl_ = jnp.dot(hl, wr_ref[...], preferred_element_type=F32)
    logits = (ph[:, :LANES] + ph[:, LANES:]) + (pl_[:, :LANES] + pl_[:, LANES:]) + br_ref[...]

    lane = lax.broadcasted_iota(jnp.int32, (tm, LANES), 1).astype(F32)
    big = float(LANES)
    gmask = lane < N_GROUPS
    gl = jnp.where(gmask, logits, NEG)
    gmax = jnp.max(gl, axis=-1, keepdims=True)
    gsum = jnp.sum(jnp.where(gmask, jnp.exp(gl - gmax), 0.0), axis=-1, keepdims=True)
    g_p = 1.0 / gsum
    g_idx = jnp.min(jnp.where(gl == gmax, lane, big), axis=-1, keepdims=True)
    e_lo = N_GROUPS + EXPERTS_PER_GROUP * g_idx
    emask = jnp.logical_and(lane >= e_lo, lane < e_lo + EXPERTS_PER_GROUP)
    el = jnp.where(emask, logits, NEG)
    t1 = jnp.max(el, axis=-1, keepdims=True)
    i1 = jnp.min(jnp.where(el == t1, lane, big), axis=-1, keepdims=True)
    el2 = jnp.where(lane == i1, NEG, el)
    t2 = jnp.max(el2, axis=-1, keepdims=True)
    i2 = jnp.min(jnp.where(el2 == t2, lane, big), axis=-1, keepdims=True)
    e2 = jnp.exp(t2 - t1)
    w_first = g_p / (1.0 + e2)
    w0_ref[...] = jnp.broadcast_to(w_first, (tm, LANES))
    w1_ref[...] = jnp.broadcast_to(w_first * e2, (tm, LANES))
    ex1 = i1 - N_GROUPS
    ex2 = i2 - N_GROUPS

    oh1 = lane == ex1
    oh2 = lane == ex2
    cnt = (oh1.astype(F32) + oh2.astype(F32)).astype(BF16)
    before = jnp.dot(tri_ref[...], cnt, preferred_element_type=F32)
    r1 = jnp.sum(jnp.where(oh1, before, 0.0), axis=-1, keepdims=True)
    r2 = jnp.sum(jnp.where(oh2, before, 0.0), axis=-1, keepdims=True)
    ids = jnp.where(lane == 0.0, ex1, jnp.where(lane == 1.0, ex2, jnp.where(lane == 2.0, r1,
                                                                             jnp.where(lane == 3.0, r2, 0.0))))
    eid_ref[...] = ids.astype(jnp.int32)
    tile_cnt = jnp.sum(cnt.astype(F32), axis=0, keepdims=True)
    cnt_ref[...] = jnp.broadcast_to(tile_cnt, cnt_ref.shape[1:]).astype(jnp.int32)[None]


def _merge_call(x2d, oa, ob, gates, p):
    t = x2d.shape[0]
    tm = MERGE_TM
    nt = t // tm
    row = lambda w: pl.BlockSpec((tm, w), lambda i: (i, 0))
    full = lambda a: pl.BlockSpec(a.shape, lambda i: (0,) * a.ndim)
    tri = (jnp.arange(tm)[:, None] > jnp.arange(tm)[None, :]).astype(BF16)
    consts = [p["woa"], p["wob"], p["wo"], p["gf"], p["wr"], p["br"], tri]
    return pl.pallas_call(
        _merge_kernel,
        grid=(nt,),
        in_specs=[row(D_MODEL), row(A_WIDTH), row(B_WIDTH), row(2 * D_MODEL)] + [full(a) for a in consts],
        out_specs=[row(D_MODEL), row(LANES), row(LANES), row(LANES),
                   pl.BlockSpec((1, 8, LANES), lambda i: (i, 0, 0))],
        out_shape=[jax.ShapeDtypeStruct((t, D_MODEL), F32), jax.ShapeDtypeStruct((t, LANES), F32),
                   jax.ShapeDtypeStruct((t, LANES), F32), jax.ShapeDtypeStruct((t, LANES), jnp.int32),
                   jax.ShapeDtypeStruct((nt, 8, LANES), jnp.int32)],
        compiler_params=_cparams(("parallel",)),
        name="merge",
    )(x2d, oa, ob, gates, *consts)


def _rows_copy(src_hbm, row, dst, j, sem):
    return pltpu.make_async_copy(src_hbm.at[pl.ds(row, 1), :], dst.at[pl.ds(j, 1), :], sem)


def _start_rows(src_hbm, idx, base, dst, sem, n):
    def body(j, carry):
        _rows_copy(src_hbm, idx[base + j], dst, j, sem).start()
        return carry
    lax.fori_loop(0, n, body, 0, unroll=8)


WAIT_ROWS = 128


def _wait_rows(src_hbm, dst, sem, n):
    for c in range(n // WAIT_ROWS):
        rows = pl.ds(c * WAIT_ROWS, WAIT_ROWS)
        pltpu.make_async_copy(src_hbm.at[rows, :], dst.at[rows, :], sem).wait()


def _gather_step(i, n, src_hbm, idx_hbm, buf, idx, sem_rows, sem_idx):
    rows = buf.shape[1]
    slot = i % 2
    nslot = 1 - slot

    def idx_copy(b, s):
        return pltpu.make_async_copy(idx_hbm.at[pl.ds(b * rows, rows)], idx.at[pl.ds(s * rows, rows)], sem_idx.at[s])

    @pl.when(jnp.logical_and(i == 0, n > 0))
    def _():
        idx_copy(0, 0).start()
        idx_copy(0, 0).wait()
        _start_rows(src_hbm, idx, 0, buf.at[0], sem_rows.at[0], rows)

        @pl.when(n > 1)
        def _():
            idx_copy(1, 1).start()

    @pl.when(i + 1 < n)
    def _():
        idx_copy(i + 1, nslot).wait()
        _start_rows(src_hbm, idx, nslot * rows, buf.at[nslot], sem_rows.at[nslot], rows)

    @pl.when(i + 2 < n)
    def _():
        idx_copy(i + 2, slot).start()

    @pl.when(i < n)
    def _():
        _wait_rows(src_hbm, buf.at[slot], sem_rows.at[slot], rows)
    return slot


SCATTER_BUFS = 3


def _scatter_kernel(x_hbm, dest_hbm, init_hbm, xr_hbm, buf, idx, sem_x, sem_rows, sem_idx):
    del init_hbm
    i = pl.program_id(0)
    n = pl.num_programs(0)
    tm = buf.shape[1]
    rows = TOP_K * tm
    islot = i % 2
    xslot = i % SCATTER_BUFS

    def idx_copy(b, s):
        return pltpu.make_async_copy(dest_hbm.at[pl.ds(b * rows, rows)], idx.at[pl.ds(s * rows, rows)], sem_idx.at[s])

    def x_copy(b, s):
        return pltpu.make_async_copy(x_hbm.at[pl.ds(b * tm, tm), :], buf.at[s], sem_x.at[s])

    def wait_rows(s):
        for k in range(TOP_K):
            pltpu.make_async_copy(buf.at[s], xr_hbm.at[pl.ds(0, tm), :], sem_rows.at[s]).wait()

    @pl.when(i == 0)
    def _():
        idx_copy(0, 0).start()
        x_copy(0, 0).start()

    @pl.when(i >= 2)
    def _():
        wait_rows((i - 2) % SCATTER_BUFS)

    @pl.when(i + 1 < n)
    def _():
        idx_copy(i + 1, 1 - islot).start()
        x_copy(i + 1, (i + 1) % SCATTER_BUFS).start()

    idx_copy(i, islot).wait()
    x_copy(i, xslot).wait()

    def body(j, carry):
        src = buf.at[xslot, pl.ds(j, 1), :]
        for k in range(TOP_K):
            dst = xr_hbm.at[pl.ds(idx[islot * rows + k * tm + j], 1), :]
            pltpu.make_async_copy(src, dst, sem_rows.at[xslot]).start()
        return carry
    lax.fori_loop(0, tm, body, 0, unroll=8)

    @pl.when(i == n - 1)
    def _():
        @pl.when(n > 1)
        def _():
            wait_rows((i - 1) % SCATTER_BUFS)
        wait_rows(xslot)


def _scatter_call(x1, pos_tiles, n_rows, tm):
    t = x1.shape[0]
    rows = TOP_K * tm
    any_spec = pl.BlockSpec(memory_space=pl.ANY)
    return pl.pallas_call(
        _scatter_kernel,
        grid=(t // tm,),
        in_specs=[any_spec, any_spec, any_spec],
        out_specs=any_spec,
        out_shape=jax.ShapeDtypeStruct((n_rows, D_MODEL), F32),
        input_output_aliases={2: 0},
        scratch_shapes=[pltpu.VMEM((SCATTER_BUFS, tm, D_MODEL), F32), pltpu.SMEM((2 * rows,), jnp.int32),
                        pltpu.SemaphoreType.DMA((SCATTER_BUFS,)), pltpu.SemaphoreType.DMA((SCATTER_BUFS,)),
                        pltpu.SemaphoreType.DMA((2,))],
        compiler_params=_cparams(("arbitrary",)),
        name="scatter",
    )(x1, pos_tiles, jnp.zeros((n_rows, D_MODEL), F32))


def _expert_kernel(blk_exp_ref, nblk_ref, x_ref, gf_ref, wgu_ref, wd_ref, y_ref):
    del blk_exp_ref
    i = pl.program_id(0)

    @pl.when(i < nblk_ref[0])
    def _():
        x = x_ref[...]
        hn = (x * lax.rsqrt(jnp.mean(x * x, axis=-1, keepdims=True) + EPS) * gf_ref[...]).astype(BF16)
        gu = jnp.dot(hn, wgu_ref[0], preferred_element_type=F32)
        gt, up = gu[:, :D_EXPERT], gu[:, D_EXPERT:]
        act = (gt * (1.0 / (1.0 + jnp.exp(-gt))) * up).astype(BF16)
        y_ref[...] = jnp.dot(act, wd_ref[0], preferred_element_type=F32)

    @pl.when(i >= nblk_ref[0])
    def _():
        y_ref[...] = jnp.zeros(y_ref.shape, F32)


def _expert_call(xr, rows, blk_exp, nblk, gf, wgu, wd):
    nb = xr.shape[0] // rows
    return pl.pallas_call(
        _expert_kernel,
        grid_spec=pltpu.PrefetchScalarGridSpec(
            num_scalar_prefetch=2,
            grid=(nb,),
            in_specs=[pl.BlockSpec((rows, D_MODEL), lambda i, be, n: (i, 0)),
                      pl.BlockSpec(gf.shape, lambda i, be, n: (0, 0)),
                      pl.BlockSpec((1, D_MODEL, 2 * D_EXPERT), lambda i, be, n: (be[i], 0, 0)),
                      pl.BlockSpec((1, D_EXPERT, D_MODEL), lambda i, be, n: (be[i], 0, 0))],
            out_specs=pl.BlockSpec((rows, D_MODEL), lambda i, be, n: (i, 0))),
        out_shape=jax.ShapeDtypeStruct((nb * rows, D_MODEL), F32),
        compiler_params=_cparams(("arbitrary",)),
        name="experts",
    )(blk_exp, nblk, xr, gf, wgu, wd)


def _combine_kernel(x1_ref, w0_ref, w1_ref, y_hbm, pos_hbm, o_ref, buf, idx, sem_rows, sem_idx):
    rows = buf.shape[1]
    tm = rows // 2
    slot = _gather_step(pl.program_id(0), pl.num_programs(0), y_hbm, pos_hbm, buf, idx, sem_rows, sem_idx)
    w0 = w0_ref[...]
    w1 = w1_ref[...]
    for c in range(D_MODEL // LANES):
        sl = slice(c * LANES, (c + 1) * LANES)
        o_ref[:, sl] = x1_ref[:, sl] + w0 * buf[slot, 0:tm, sl] + w1 * buf[slot, tm:rows, sl]


def _combine_call(x1, w0, w1, yr, pos_tiles, tm):
    t = x1.shape[0]
    rows = TOP_K * tm
    row = lambda w: pl.BlockSpec((tm, w), lambda i: (i, 0))
    return pl.pallas_call(
        _combine_kernel,
        grid=(t // tm,),
        in_specs=[row(D_MODEL), row(LANES), row(LANES),
                  pl.BlockSpec(memory_space=pl.ANY), pl.BlockSpec(memory_space=pl.ANY)],
        out_specs=row(D_MODEL),
        out_shape=jax.ShapeDtypeStruct((t, D_MODEL), F32),
        scratch_shapes=[pltpu.VMEM((2, rows, D_MODEL), F32), pltpu.SMEM((2 * rows,), jnp.int32),
                        pltpu.SemaphoreType.DMA((2,)), pltpu.SemaphoreType.DMA((2,))],
        compiler_params=_cparams(("arbitrary",)),
        name="combine",
    )(x1, w0, w1, yr, pos_tiles)


def _rot_cols(w):
    half = ROPE_DIM // 2
    return jnp.concatenate([-w[..., half:], w[..., :half]], axis=-1)


def _head_pad(nope, rope):
    z = jnp.zeros(rope.shape[:-1] + (LANES - B_QK_DIM,), rope.dtype)
    out = jnp.concatenate([nope, rope, z], axis=-1)
    return out.reshape(out.shape[:-2] + (out.shape[-2] * LANES,))


def _t5_bucket(dist):
    max_exact = REL_BUCKETS // 2
    d = jnp.maximum(dist, 0)
    ratio = jnp.maximum(d, 1).astype(F32) / max_exact
    large = max_exact + (jnp.log(ratio) / math.log(REL_MAX_DIST / max_exact)
                         * (REL_BUCKETS - max_exact)).astype(jnp.int32)
    large = jnp.minimum(large, REL_BUCKETS - 1)
    return jnp.where(d < max_exact, d, large)


def _swa_bias(rel_bias, tq):
    qi = jnp.arange(tq)[:, None]
    kj = jnp.arange(2 * tq)[None, :]
    dist = qi + tq - kj
    bias = rel_bias[_t5_bucket(dist)].astype(F32).transpose(2, 0, 1) * LOG2E
    mask = (dist >= 0) & (dist < WINDOW)
    bias = jnp.where(mask[None], bias, NEG)
    return bias.reshape(A_KV_HEADS, (A_HEADS // A_KV_HEADS) * tq, 2 * tq)


def _rope_tables(seq):
    half = ROPE_DIM // 2
    inv_freq = ROPE_THETA ** (-jnp.arange(half, dtype=F32) / half)
    ang = jnp.arange(seq, dtype=F32)[:, None] * inv_freq[None, :]
    cos = jnp.concatenate([jnp.cos(ang), jnp.cos(ang)], axis=-1)
    sin = jnp.concatenate([jnp.sin(ang), jnp.sin(ang)], axis=-1)
    return cos, sin


def _lane_table(nope_part, rope_part):
    seq = rope_part.shape[0]
    nope_part = jnp.broadcast_to(nope_part, (seq, NOPE_DIM))
    return jnp.concatenate([nope_part, rope_part, jnp.zeros((seq, LANES - B_QK_DIM), F32)], axis=-1)


def _prep_layer(l, seq, cos, sin, attn_norm, w_in, b_gate, a_q_norm, a_k_norm, w_oa, q_a_norm, w_qb, kv_a_norm,
                w_kvb, b_q_norm, b_k_norm, w_ob, w_o, ffn_norm, w_rg, b_rg, w_re, b_re, w_eg, w_eu, w_ed):
    w = w_in[l]
    o = 0
    parts = []
    for sz in (A_WIDTH, A_KV_HEADS * A_HEAD_DIM, A_KV_HEADS * A_HEAD_DIM, Q_LORA, KV_LORA, ROPE_DIM, 2 * D_MODEL):
        parts.append(w[:, o:o + sz])
        o += sz
    wqa, wka, wva, wcq, wckv, wkpe, wg = parts
    perm = jnp.array([c + (A_HEADS // 2) * h for c in range(A_HEADS // 2) for h in range(2)])
    wqa = wqa.reshape(D_MODEL, A_HEADS, A_HEAD_DIM)[:, perm].reshape(D_MODEL, A_WIDTH)
    zl = jnp.zeros((D_MODEL, NOPE_DIM), F32)
    zr = jnp.zeros((D_MODEL, LANES - B_QK_DIM), F32)
    w1 = jnp.concatenate([wqa, wka, wva, wcq, wckv, zl, wkpe, zr, zl, _rot_cols(wkpe), zr, wg], axis=1).astype(BF16)

    wq = w_qb[l].reshape(Q_LORA, B_HEADS, B_QK_DIM)
    zq = jnp.zeros((Q_LORA, B_HEADS, NOPE_DIM), F32)
    wq2 = jnp.concatenate([_head_pad(wq[..., :NOPE_DIM], wq[..., NOPE_DIM:]),
                           _head_pad(zq, _rot_cols(wq[..., NOPE_DIM:]))], axis=1).astype(BF16)
    wkv = w_kvb[l].reshape(KV_LORA, B_HEADS, NOPE_DIM + B_V_DIM)
    wk2 = _head_pad(wkv[..., :NOPE_DIM], jnp.zeros((KV_LORA, B_HEADS, ROPE_DIM), F32)).astype(BF16)
    wv = wkv[..., NOPE_DIM:]
    zv = jnp.zeros_like(wv)
    wv2 = jnp.where((jnp.arange(B_HEADS) % 2 == 0)[None, :, None], jnp.concatenate([wv, zv], -1),
                    jnp.concatenate([zv, wv], -1)).reshape(KV_LORA, B_HEADS * LANES).astype(BF16)
    one_lane = jnp.where(jnp.arange(B_HEADS) % 2 == 0, V_ONE_EVEN, V_ONE_ODD)
    vone = (jnp.arange(LANES)[None, :] == one_lane[:, None]).astype(F32).reshape(1, B_HEADS * LANES)

    def rope_tabs(gain, scale):
        gn, gr = gain[:NOPE_DIM], gain[NOPE_DIM:]
        gr_rot = jnp.concatenate([gr[ROPE_DIM // 2:], gr[:ROPE_DIM // 2]])
        a = _lane_table(gn[None, :] * scale, gr[None, :] * cos * scale)
        b = _lane_table(jnp.zeros((1, NOPE_DIM), F32), gr_rot[None, :] * sin * scale)
        return a, b

    aq, bq = rope_tabs(b_q_norm[l], B_QK_DIM ** -0.5 * LOG2E)
    score_bound = math.sqrt(B_QK_DIM) * LOG2E * jnp.max(jnp.abs(b_q_norm[l])) * jnp.max(jnp.abs(b_k_norm[l]))
    ak, bk = rope_tabs(b_k_norm[l], 1.0)

    gqa = (jnp.tile(a_q_norm[l], 2) * (A_HEAD_DIM ** -0.5 * LOG2E))[None, :]
    gka = jnp.tile(a_k_norm[l], 2)[None, :]
    woa = w_oa[l].reshape(A_HEADS, A_HEAD_DIM, D_MODEL)[perm].reshape(A_WIDTH, D_MODEL).astype(BF16)

    wr = jnp.concatenate([w_rg[l], w_re[l], jnp.zeros((D_MODEL, LANES - N_GROUPS - N_EXPERTS), F32)], axis=1)
    wrh = wr.astype(BF16)
    wrl = (wr - wrh.astype(F32)).astype(BF16)
    br = jnp.concatenate([b_rg[l], b_re[l], jnp.zeros((LANES - N_GROUPS - N_EXPERTS,), F32)])[None, :]

    return dict(
        gn=attn_norm[l][None, :], w1=w1, wq2=wq2, wk2=wk2, wv2=wv2, vone=vone, gcq=q_a_norm[l][None, :],
        gckv=kv_a_norm[l][None, :], score_bound=score_bound, bg=b_gate[l][None, :], gqa=gqa, gka=gka,
        aq=aq, bq=bq, ak=ak, bk=bk,
        woa=woa, wob=w_ob[l].astype(BF16), wo=w_o[l].astype(BF16), gf=ffn_norm[l][None, :], wr=jnp.concatenate([wrh, wrl], axis=1), br=br,
        wgu=jnp.concatenate([w_eg[l], w_eu[l]], axis=-1).astype(BF16), wd=w_ed[l].astype(BF16))


def _tril(n):
    return (jnp.arange(n)[:, None] >= jnp.arange(n)[None, :]).astype(jnp.int32)


def _dispatch(route, tile_cnt, rows):
    t = route.shape[0]
    nt = tile_cnt.shape[0]
    tile_cnt = tile_cnt[:, 0, :N_EXPERTS]
    counts = jnp.sum(tile_cnt, axis=0)
    padded = ((counts + rows - 1) // rows) * rows
    pends = jnp.dot(_tril(N_EXPERTS), padded)
    pstarts = pends - padded
    tile_base = pstarts[None, :] + jnp.dot(_tril(nt), tile_cnt) - tile_cnt
    base = jnp.broadcast_to(tile_base[:, None, None, :], (nt, t // nt, 1, N_EXPERTS)).reshape(t, 1, N_EXPERTS)
    onehot = route[:, :TOP_K, None] == jnp.arange(N_EXPERTS, dtype=jnp.int32)[None, None, :]
    dest = jnp.sum(jnp.where(onehot, base, 0), axis=-1) + route[:, TOP_K:2 * TOP_K]
    n_rows = t * TOP_K + N_EXPERTS * rows
    n_blocks = n_rows // rows
    blk_start = jnp.arange(n_blocks, dtype=jnp.int32)[:, None] * rows
    blk_exp = jnp.minimum(jnp.sum((pends[None, :] <= blk_start).astype(jnp.int32), axis=1), N_EXPERTS - 1)
    nblk = (pends[-1] // rows).astype(jnp.int32).reshape(1)
    return dest, blk_exp, nblk, n_rows


def _swa_tables(sinks, rel_bias, a_q_norm, a_k_norm):
    per_group = A_HEADS // A_KV_HEADS
    tab = jnp.broadcast_to((sinks * LOG2E).reshape(A_KV_HEADS, per_group, 1, 1),
                           (A_KV_HEADS, per_group, SWA_TQ, LANES)).reshape(A_KV_HEADS, per_group * SWA_TQ, LANES)
    qk = math.sqrt(A_HEAD_DIM) * jnp.max(jnp.abs(a_q_norm)) * jnp.max(jnp.abs(a_k_norm))
    bound = LOG2E * jnp.maximum(qk + jnp.max(jnp.abs(rel_bias)), jnp.max(jnp.abs(sinks)))
    return tab, bound


def _layer(x2d, bsz, seq, p, sink_tab, swa_bound, bias):
    t = x2d.shape[0]
    qa, ka, va, qb, kb, vb, gates = _proj_call(x2d, seq, p)
    oa = _swa_call(qa, ka, va, sink_tab, bias, bsz, seq, swa_bound)
    ob = _mla_call(qb, kb, vb, bsz, seq, p["score_bound"])
    x1, w0, w1, route, tile_cnt = _merge_call(x2d, oa, ob, gates, p)
    dest, blk_exp, nblk, n_rows = _dispatch(route, tile_cnt, MOE_ROWS)
    pos_tiles = dest.reshape(t // COMB_TM, COMB_TM, TOP_K).transpose(0, 2, 1).reshape(t * TOP_K)
    xr = _scatter_call(x1, pos_tiles, n_rows, COMB_TM)
    yr = _expert_call(xr, MOE_ROWS, blk_exp, nblk, p["gf"], p["wgu"], p["wd"])
    return _combine_call(x1, w0, w1, yr, pos_tiles, COMB_TM)


def kernel(x, attn_norm, w_in, b_gate, a_q_norm, a_k_norm, a_sinks, rel_bias, w_oa, q_a_norm, w_qb, kv_a_norm, w_kvb, b_q_norm, b_k_norm, w_ob, w_o, ffn_norm, w_router_group, b_router_group, w_router_expert, b_router_expert, w_exp_gate, w_exp_up, w_exp_down):
    bsz, seq, d = x.shape
    assert d == D_MODEL and seq % MLA_TQ == 0 and seq % PROJ_TM == 0 and (bsz * seq) % COMB_TM == 0
    depth = w_in.shape[0]
    cos, sin = _rope_tables(seq)
    bias = _swa_bias(rel_bias, SWA_TQ)
    x2d = x.reshape(bsz * seq, d)
    for l in range(depth):
        p = _prep_layer(l, seq, cos, sin, attn_norm, w_in, b_gate, a_q_norm, a_k_norm, w_oa, q_a_norm, w_qb,
                        kv_a_norm, w_kvb, b_q_norm, b_k_norm, w_ob, w_o, ffn_norm, w_router_group,
                        b_router_group, w_router_expert, b_router_expert, w_exp_gate, w_exp_up, w_exp_down)
        sink_tab, swa_bound = _swa_tables(a_sinks[l], rel_bias, a_q_norm[l], a_k_norm[l])
        x2d = _layer(x2d, bsz, seq, p, sink_tab, swa_bound, bias)
    return x2d.reshape(bsz, seq, d)
```

```python
import functools
import math

import jax
import jax.numpy as jnp
from jax import lax
from jax.experimental import pallas as pl
from jax.experimental.pallas import tpu as pltpu

F32 = jnp.float32
BF16 = jnp.bfloat16

D_MODEL = 1024
A_HEADS = 8
A_KV_HEADS = 2
A_HEAD_DIM = 64
A_WIDTH = A_HEADS * A_HEAD_DIM
WINDOW = 128
REL_BUCKETS = 32
REL_MAX_DIST = 128
B_HEADS = 8
Q_LORA = 256
KV_LORA = 128
NOPE_DIM = 64
ROPE_DIM = 32
B_QK_DIM = NOPE_DIM + ROPE_DIM
B_V_DIM = 64
B_WIDTH = B_HEADS * B_V_DIM
ROPE_THETA = 10000.0
N_GROUPS = 4
EXPERTS_PER_GROUP = 8
N_EXPERTS = N_GROUPS * EXPERTS_PER_GROUP
TOP_K = 2
D_EXPERT = 256
EPS = 1e-6

LANES = 128
NEG = -1e30
LOG2E = math.log2(math.e)
V_ONE_EVEN = B_V_DIM
V_ONE_ODD = 0
EXP2_SAFE_BOUND = 100.0
VMEM_LIMIT = 56 * 1024 * 1024

C_QA = 0
C_KA = C_QA + A_WIDTH
C_VA = C_KA + LANES
C_CQ = C_VA + LANES
C_CKV = C_CQ + Q_LORA
C_KPE = C_CKV + KV_LORA
C_KPR = C_KPE + LANES
C_GATE = C_KPR + LANES
C_END = C_GATE + 2 * D_MODEL

PROJ_TM = 512
SWA_TQ = 128
SWA_SUB = 4
MLA_TQ = 1024
MLA_TK = 512
MERGE_TM = 512
MOE_ROWS = 512
COMB_TM = 256


def _cparams(sem):
    return pltpu.CompilerParams(dimension_semantics=sem, vmem_limit_bytes=VMEM_LIMIT)


def _proj_kernel(x_ref, gn_ref, w1_ref, wq2_ref, wk2_ref, wv2_ref, vone_ref, gcq_ref, gckv_ref, bg_ref,
                 gqa_ref, gka_ref, aq_ref, bq_ref, ak_ref, bk_ref,
                 qa_ref, ka_ref, va_ref, qb_ref, kb_ref, vb_ref, gate_ref):
    tm = x_ref.shape[0]
    x = x_ref[...]
    h = (x * lax.rsqrt(jnp.mean(x * x, axis=-1, keepdims=True) + EPS) * gn_ref[...]).astype(BF16)

    def proj(a, b):
        return jnp.dot(h, w1_ref[:, a:b], preferred_element_type=F32)

    lo = lax.broadcasted_iota(jnp.int32, (tm, LANES), 1) < A_HEAD_DIM

    def pair_norm(t, gain):
        t2 = t * t
        s_lo = jnp.sum(jnp.where(lo, t2, 0.0), axis=-1, keepdims=True)
        s_hi = jnp.sum(jnp.where(lo, 0.0, t2), axis=-1, keepdims=True)
        r = jnp.where(lo, lax.rsqrt(s_lo / A_HEAD_DIM + EPS), lax.rsqrt(s_hi / A_HEAD_DIM + EPS))
        return (t * r * gain).astype(BF16)

    qa = proj(C_QA, C_KA)
    for c in range(A_WIDTH // LANES):
        sl = slice(c * LANES, (c + 1) * LANES)
        qa_ref[:, sl] = pair_norm(qa[:, sl], gqa_ref[...])
    kvp = proj(C_KA, C_CQ)
    ka_ref[...] = pair_norm(kvp[:, :LANES], gka_ref[...])
    va_ref[...] = kvp[:, LANES:].astype(BF16)

    lat = proj(C_CQ, C_GATE)
    cq = lat[:, :Q_LORA]
    cqn = (cq * lax.rsqrt(jnp.mean(cq * cq, axis=-1, keepdims=True) + EPS) * gcq_ref[...]).astype(BF16)
    ckv = lat[:, Q_LORA:Q_LORA + KV_LORA]
    ckvn = (ckv * lax.rsqrt(jnp.mean(ckv * ckv, axis=-1, keepdims=True) + EPS) * gckv_ref[...]).astype(BF16)
    kpe = lat[:, C_KPE - C_CQ:C_KPR - C_CQ]
    kpr = lat[:, C_KPR - C_CQ:C_GATE - C_CQ]

    aq, bq, ak, bk = aq_ref[...], bq_ref[...], ak_ref[...], bk_ref[...]
    q2 = jnp.dot(cqn, wq2_ref[...], preferred_element_type=F32)
    k2 = jnp.dot(ckvn, wk2_ref[...], preferred_element_type=F32)
    vb_ref[...] = (jnp.dot(ckvn, wv2_ref[...], preferred_element_type=F32) + vone_ref[...]).astype(BF16)
    kr = kpr * bk
    for hd in range(B_HEADS):
        sl = slice(hd * LANES, (hd + 1) * LANES)
        qp = q2[:, sl]
        qr = q2[:, B_HEADS * LANES + hd * LANES:B_HEADS * LANES + (hd + 1) * LANES]
        rq = lax.rsqrt(jnp.sum(qp * qp, axis=-1, keepdims=True) / B_QK_DIM + EPS)
        qb_ref[:, sl] = (rq * (qp * aq + qr * bq)).astype(BF16)
        kp = k2[:, sl] + kpe
        rk = lax.rsqrt(jnp.sum(kp * kp, axis=-1, keepdims=True) / B_QK_DIM + EPS)
        kb_ref[:, sl] = (rk * (kp * ak + kr)).astype(BF16)

    z = proj(C_GATE, C_END) + bg_ref[...]
    gate_ref[...] = (1.0 / (1.0 + jnp.exp(-z))).astype(BF16)


def _proj_call(x2d, seq, p):
    t = x2d.shape[0]
    tm = PROJ_TM
    nt = t // tm
    npos = seq // tm
    row = lambda w: pl.BlockSpec((tm, w), lambda i: (i, 0))
    full = lambda a: pl.BlockSpec(a.shape, lambda i: (0,) * a.ndim)
    pos = pl.BlockSpec((tm, LANES), lambda i: (i % npos, 0))
    consts = [p["gn"], p["w1"], p["wq2"], p["wk2"], p["wv2"], p["vone"], p["gcq"], p["gckv"], p["bg"], p["gqa"],
              p["gka"]]
    tabs = [p["aq"], p["bq"], p["ak"], p["bk"]]
    widths = [A_WIDTH, LANES, LANES, B_HEADS * LANES, B_HEADS * LANES, B_HEADS * LANES, 2 * D_MODEL]
    return pl.pallas_call(
        _proj_kernel,
        grid=(nt,),
        in_specs=[row(D_MODEL)] + [full(a) for a in consts] + [pos] * 4,
        out_specs=[row(w) for w in widths],
        out_shape=[jax.ShapeDtypeStruct((t, w), BF16) for w in widths],
        compiler_params=_cparams(("parallel",)),
        name="proj",
    )(x2d, *consts, *tabs)


def _swa_kernel(q_ref, kc_ref, kp_ref, vc_ref, vp_ref, bias_ref, sink_ref, o_ref, *, shift):
    tq = SWA_TQ
    groups = A_WIDTH // LANES
    first = pl.program_id(1) == 0
    one_col = (lax.broadcasted_iota(jnp.int32, (2 * tq, LANES), 1) == 0).astype(BF16)
    lo = lax.broadcasted_iota(jnp.int32, (tq, LANES), 1) < A_HEAD_DIM
    no_prev = jnp.logical_and(first, lax.broadcasted_iota(jnp.int32, (groups * tq, 2 * tq), 1) < tq)
    for sub in range(SWA_SUB):
        if sub == 0:
            k = jnp.concatenate([kp_ref[...], kc_ref[0:tq, :]], axis=0)
            v = jnp.concatenate([vp_ref[...], vc_ref[0:tq, :]], axis=0)
        else:
            k = kc_ref[(sub - 1) * tq:(sub + 1) * tq, :]
            v = vc_ref[(sub - 1) * tq:(sub + 1) * tq, :]
        v_ext = jnp.concatenate([v, one_col], axis=1)
        qrows = slice(sub * tq, (sub + 1) * tq)
        qs = [q_ref[qrows, c * LANES:(c + 1) * LANES] for c in range(groups)]
        res = []
        for g in range(A_KV_HEADS):
            keep = lo if g == 0 else jnp.logical_not(lo)
            qg = jnp.concatenate([jnp.where(keep, qc, jnp.zeros_like(qc)) for qc in qs], axis=0)
            s = lax.dot_general(qg, k, (((1,), (1,)), ((), ())), preferred_element_type=F32) + bias_ref[g]
            sink = sink_ref[g][:, :1]
            if shift:
                if sub == 0:
                    s = jnp.where(no_prev, NEG, s)
                m = jnp.maximum(jnp.max(s, axis=-1, keepdims=True), sink)
                e = jnp.exp2(s - m)
                sink_term = jnp.exp2(sink - m)
            else:
                e = jnp.exp2(s)
                if sub == 0:
                    e = jnp.where(no_prev, 0.0, e)
                sink_term = jnp.exp2(sink)
            pv = jnp.dot(e.astype(BF16), v_ext, preferred_element_type=F32)
            res.append(pv[:, :LANES] * (1.0 / (pv[:, LANES:LANES + 1] + sink_term)))
        for c in range(groups):
            rows = slice(c * tq, (c + 1) * tq)
            o_ref[qrows, c * LANES:(c + 1) * LANES] = jnp.where(lo, res[0][rows], res[1][rows]).astype(BF16)


def _swa_call(qa, ka, va, sink_tab, bias, bsz, seq, score_bound):
    t = qa.shape[0]
    tq = SWA_TQ
    step = SWA_SUB * tq
    ns = seq // step
    cur = lambda b, i: (b * ns + i, 0)
    prev = lambda b, i: (b * (seq // tq) + jnp.maximum(SWA_SUB * i - 1, 0), 0)
    full = lambda a: pl.BlockSpec(a.shape, lambda b, i: (0,) * a.ndim)

    def call(shift):
        return pl.pallas_call(
            functools.partial(_swa_kernel, shift=shift),
            grid=(bsz, ns),
            in_specs=[pl.BlockSpec((step, A_WIDTH), cur),
                      pl.BlockSpec((step, LANES), cur), pl.BlockSpec((tq, LANES), prev),
                      pl.BlockSpec((step, LANES), cur), pl.BlockSpec((tq, LANES), prev),
                      full(bias), full(sink_tab)],
            out_specs=pl.BlockSpec((step, A_WIDTH), cur),
            out_shape=jax.ShapeDtypeStruct((t, A_WIDTH), BF16),
            compiler_params=_cparams(("parallel", "parallel")),
            name="swa_shifted" if shift else "swa",
        )(qa, ka, ka, va, va, bias, sink_tab)

    return lax.cond(score_bound <= EXP2_SAFE_BOUND, lambda: call(False), lambda: call(True))


def _mla_kernel(q_ref, k_ref, v_ref, o_ref, acc_sc, m_sc, *, shift):
    tq = q_ref.shape[0]
    tk = MLA_TK
    per_tile = tq // tk
    qi = pl.program_id(2)
    acc_sc[...] = jnp.zeros(acc_sc.shape, F32)
    if shift:
        m_sc[...] = jnp.full(m_sc.shape, NEG, F32)

    def tile(j, r0, r1, masked):
        keys = pl.ds(pl.multiple_of(j * tk, tk), tk)
        if masked:
            causal = (lax.broadcasted_iota(jnp.int32, (r1 - r0, tk), 1)
                      <= lax.broadcasted_iota(jnp.int32, (r1 - r0, tk), 0))
        for half in range(2):
            hs = slice(half * LANES, (half + 1) * LANES)
            s = lax.dot_general(q_ref[r0:r1, hs], k_ref[keys, hs], (((1,), (1,)), ((), ())),
                                preferred_element_type=F32)
            if shift:
                if masked:
                    s = jnp.where(causal, s, NEG)
                m_prev = m_sc[half, r0:r1]
                m_new = jnp.maximum(m_prev, jnp.max(s, axis=-1, keepdims=True))
                m_sc[half, r0:r1] = m_new
                e = jnp.exp2(s - m_new)
                acc_sc[half, r0:r1] = jnp.exp2(m_prev - m_new) * acc_sc[half, r0:r1]
            else:
                e = jnp.exp2(s)
                if masked:
                    e = jnp.where(causal, e, 0.0)
            acc_sc[half, r0:r1] += jnp.dot(e.astype(BF16), v_ref[keys, hs], preferred_element_type=F32)

    def body(j, carry):
        tile(j, 0, tq, False)
        return carry

    lax.fori_loop(0, qi * per_tile, body, 0)
    for d in range(per_tile):
        tile(qi * per_tile + d, d * tk, (d + 1) * tk, True)
        if (d + 1) * tk < tq:
            tile(qi * per_tile + d, (d + 1) * tk, tq, False)
    a0 = acc_sc[0]
    a1 = acc_sc[1]
    lo = lax.broadcasted_iota(jnp.int32, (tq, LANES), 1) < B_V_DIM
    o_ref[...] = jnp.where(lo, a0 * (1.0 / a0[:, V_ONE_EVEN:V_ONE_EVEN + 1]),
                           a1 * (1.0 / a1[:, V_ONE_ODD:V_ONE_ODD + 1])).astype(BF16)


def _mla_call(qb, kb, vb, bsz, seq, score_bound):
    t = qb.shape[0]
    tq = MLA_TQ
    nq = seq // tq
    pairs = B_HEADS // 2

    def call(shift):
        return pl.pallas_call(
            functools.partial(_mla_kernel, shift=shift),
            grid=(bsz, pairs, nq),
            in_specs=[pl.BlockSpec((tq, 2 * LANES), lambda b, p, i: (b * nq + i, p)),
                      pl.BlockSpec((seq, 2 * LANES), lambda b, p, i: (b, p)),
                      pl.BlockSpec((seq, 2 * LANES), lambda b, p, i: (b, p))],
            out_specs=pl.BlockSpec((tq, LANES), lambda b, p, i: (b * nq + i, p)),
            out_shape=jax.ShapeDtypeStruct((t, B_WIDTH), BF16),
            scratch_shapes=[pltpu.VMEM((2, tq, LANES), F32), pltpu.VMEM((2, tq, 1), F32)],
            compiler_params=_cparams(("parallel", "parallel", "arbitrary")),
            name="mla_shifted" if shift else "mla",
        )(qb, kb, vb)

    return lax.cond(score_bound <= EXP2_SAFE_BOUND, lambda: call(False), lambda: call(True))


def _merge_kernel(x_ref, oa_ref, ob_ref, gate_ref, woa_ref, wob_ref, wo_ref, gf_ref, wr_ref, br_ref,
                  tri_ref, x1_ref, w0_ref, w1_ref, eid_ref, cnt_ref):
    tm = x_ref.shape[0]
    ya = jnp.dot(oa_ref[...], woa_ref[...], preferred_element_type=F32)
    yb = jnp.dot(ob_ref[...], wob_ref[...], preferred_element_type=F32)
    g = gate_ref[...].astype(F32)
    mix = (g[:, :D_MODEL] * ya + g[:, D_MODEL:] * yb).astype(BF16)
    x1 = x_ref[...] + jnp.dot(mix, wo_ref[...], preferred_element_type=F32)
    x1_ref[...] = x1

    hn = x1 * lax.rsqrt(jnp.mean(x1 * x1, axis=-1, keepdims=True) + EPS) * gf_ref[...]
    hh = hn.astype(BF16)
    hl = (hn - hh.astype(F32)).astype(BF16)
    ph = jnp.dot(hh, wr_ref[...], preferred_element_type=F32)
    pl_ = jnp.dot(hl, wr_ref[...], preferred_element_type=F32)
    logits = (ph[:, :LANES] + ph[:, LANES:]) + (pl_[:, :LANES] + pl_[:, LANES:]) + br_ref[...]

    lane = lax.broadcasted_iota(jnp.int32, (tm, LANES), 1).astype(F32)
    big = float(LANES)
    gmask = lane < N_GROUPS
    gl = jnp.where(gmask, logits, NEG)
    gmax = jnp.max(gl, axis=-1, keepdims=True)
    gsum = jnp.sum(jnp.where(gmask, jnp.exp(gl - gmax), 0.0), axis=-1, keepdims=True)
    g_p = 1.0 / gsum
    g_idx = jnp.min(jnp.where(gl == gmax, lane, big), axis=-1, keepdims=True)
    e_lo = N_GROUPS + EXPERTS_PER_GROUP * g_idx
    emask = jnp.logical_and(lane >= e_lo, lane < e_lo + EXPERTS_PER_GROUP)
    el = jnp.where(emask, logits, NEG)
    t1 = jnp.max(el, axis=-1, keepdims=True)
    i1 = jnp.min(jnp.where(el == t1, lane, big), axis=-1, keepdims=True)
    el2 = jnp.where(lane == i1, NEG, el)
    t2 = jnp.max(el2, axis=-1, keepdims=True)
    i2 = jnp.min(jnp.where(el2 == t2, lane, big), axis=-1, keepdims=True)
    e2 = jnp.exp(t2 - t1)
    w_first = g_p / (1.0 + e2)
    w0_ref[...] = jnp.broadcast_to(w_first, (tm, LANES))
    w1_ref[...] = jnp.broadcast_to(w_first * e2, (tm, LANES))
    ex1 = i1 - N_GROUPS
    ex2 = i2 - N_GROUPS

    oh1 = lane == ex1
    oh2 = lane == ex2
    cnt = (oh1.astype(F32) + oh2.astype(F32)).astype(BF16)
    before = jnp.dot(tri_ref[...], cnt, preferred_element_type=F32)
    r1 = jnp.sum(jnp.where(oh1, before, 0.0), axis=-1, keepdims=True)
    r2 = jnp.sum(jnp.where(oh2, before, 0.0), axis=-1, keepdims=True)
    ids = jnp.where(lane == 0.0, ex1, jnp.where(lane == 1.0, ex2, jnp.where(lane == 2.0, r1,
                                                                             jnp.where(lane == 3.0, r2, 0.0))))
    eid_ref[...] = ids.astype(jnp.int32)
    tile_cnt = jnp.sum(cnt.astype(F32), axis=0, keepdims=True)
    cnt_ref[...] = jnp.broadcast_to(tile_cnt, cnt_ref.shape[1:]).astype(jnp.int32)[None]


def _merge_call(x2d, oa, ob, gates, p):
    t = x2d.shape[0]
    tm = MERGE_TM
    nt = t // tm
    row = lambda w: pl.BlockSpec((tm, w), lambda i: (i, 0))
    full = lambda a: pl.BlockSpec(a.shape, lambda i: (0,) * a.ndim)
    tri = (jnp.arange(tm)[:, None] > jnp.arange(tm)[None, :]).astype(BF16)
    consts = [p["woa"], p["wob"], p["wo"], p["gf"], p["wr"], p["br"], tri]
    return pl.pallas_call(
        _merge_kernel,
        grid=(nt,),
        in_specs=[row(D_MODEL), row(A_WIDTH), row(B_WIDTH), row(2 * D_MODEL)] + [full(a) for a in consts],
        out_specs=[row(D_MODEL), row(LANES), row(LANES), row(LANES),
                   pl.BlockSpec((1, 8, LANES), lambda i: (i, 0, 0))],
        out_shape=[jax.ShapeDtypeStruct((t, D_MODEL), F32), jax.ShapeDtypeStruct((t, LANES), F32),
                   jax.ShapeDtypeStruct((t, LANES), F32), jax.ShapeDtypeStruct((t, LANES), jnp.int32),
                   jax.ShapeDtypeStruct((nt, 8, LANES), jnp.int32)],
        compiler_params=_cparams(("parallel",)),
        name="merge",
    )(x2d, oa, ob, gates, *consts)


def _rows_copy(src_hbm, row, dst, j, sem):
    return pltpu.make_async_copy(src_hbm.at[pl.ds(row, 1), :], dst.at[pl.ds(j, 1), :], sem)


def _start_rows(src_hbm, idx, base, dst, sem, n):
    def body(j, carry):
        _rows_copy(src_hbm, idx[base + j], dst, j, sem).start()
        return carry
    lax.fori_loop(0, n, body, 0, unroll=8)


WAIT_ROWS = 128


def _wait_rows(src_hbm, dst, sem, n):
    for c in range(n // WAIT_ROWS):
        rows = pl.ds(c * WAIT_ROWS, WAIT_ROWS)
        pltpu.make_async_copy(src_hbm.at[rows, :], dst.at[rows, :], sem).wait()


def _gather_step(i, n, src_hbm, idx_hbm, buf, idx, sem_rows, sem_idx):
    rows = buf.shape[1]
    slot = i % 2
    nslot = 1 - slot

    def idx_copy(b, s):
        return pltpu.make_async_copy(idx_hbm.at[pl.ds(b * rows, rows)], idx.at[pl.ds(s * rows, rows)], sem_idx.at[s])

    @pl.when(jnp.logical_and(i == 0, n > 0))
    def _():
        idx_copy(0, 0).start()
        idx_copy(0, 0).wait()
        _start_rows(src_hbm, idx, 0, buf.at[0], sem_rows.at[0], rows)

        @pl.when(n > 1)
        def _():
            idx_copy(1, 1).start()

    @pl.when(i + 1 < n)
    def _():
        idx_copy(i + 1, nslot).wait()
        _start_rows(src_hbm, idx, nslot * rows, buf.at[nslot], sem_rows.at[nslot], rows)

    @pl.when(i + 2 < n)
    def _():
        idx_copy(i + 2, slot).start()

    @pl.when(i < n)
    def _():
        _wait_rows(src_hbm, buf.at[slot], sem_rows.at[slot], rows)
    return slot


SCATTER_BUFS = 3


def _scatter_kernel(zero_blk_ref, x_hbm, dest_hbm, xr_hbm, buf, idx, sem_x, sem_rows, sem_idx, sem_zero,
                    *, blk_rows):
    i = pl.program_id(0)
    n = pl.num_programs(0)
    tm = buf.shape[1]
    rows = TOP_K * tm
    islot = i % 2
    xslot = i % SCATTER_BUFS

    def idx_copy(b, s):
        return pltpu.make_async_copy(dest_hbm.at[pl.ds(b * rows, rows)], idx.at[pl.ds(s * rows, rows)], sem_idx.at[s])

    def x_copy(b, s):
        return pltpu.make_async_copy(x_hbm.at[pl.ds(b * tm, tm), :], buf.at[s], sem_x.at[s])

    def wait_rows(s):
        for k in range(TOP_K):
            pltpu.make_async_copy(buf.at[s], xr_hbm.at[pl.ds(0, tm), :], sem_rows.at[s]).wait()

    @pl.when(i == 0)
    def _():
        idx_copy(0, 0).start()
        x_copy(0, 0).start()
        zsrc = buf.at[SCATTER_BUFS - 1]
        zsrc[...] = jnp.zeros(zsrc.shape, F32)

        def zero_copies(e):
            b = zero_blk_ref[e]
            return b >= 0, [pltpu.make_async_copy(zsrc, xr_hbm.at[pl.ds(b * blk_rows + h * tm, tm), :], sem_zero)
                            for h in range(blk_rows // tm)]

        def start(e, carry):
            live, copies = zero_copies(e)

            @pl.when(live)
            def _():
                for c in copies:
                    c.start()
            return carry

        def wait(e, carry):
            live, copies = zero_copies(e)

            @pl.when(live)
            def _():
                for c in copies:
                    c.wait()
            return carry

        lax.fori_loop(0, zero_blk_ref.shape[0], start, 0)
        lax.fori_loop(0, zero_blk_ref.shape[0], wait, 0)

    @pl.when(i >= 2)
    def _():
        wait_rows((i - 2) % SCATTER_BUFS)

    @pl.when(i + 1 < n)
    def _():
        idx_copy(i + 1, 1 - islot).start()
        x_copy(i + 1, (i + 1) % SCATTER_BUFS).start()

    idx_copy(i, islot).wait()
    x_copy(i, xslot).wait()

    def body(j, carry):
        src = buf.at[xslot, pl.ds(j, 1), :]
        for k in range(TOP_K):
            dst = xr_hbm.at[pl.ds(idx[islot * rows + k * tm + j], 1), :]
            pltpu.make_async_copy(src, dst, sem_rows.at[xslot]).start()
        return carry
    lax.fori_loop(0, tm, body, 0, unroll=8)

    @pl.when(i == n - 1)
    def _():
        @pl.when(n > 1)
        def _():
            wait_rows((i - 1) % SCATTER_BUFS)
        wait_rows(xslot)


def _scatter_call(x1, pos_tiles, zero_blk, n_rows, tm, blk_rows):
    t = x1.shape[0]
    rows = TOP_K * tm
    assert blk_rows % tm == 0
    any_spec = pl.BlockSpec(memory_space=pl.ANY)
    return pl.pallas_call(
        functools.partial(_scatter_kernel, blk_rows=blk_rows),
        grid_spec=pltpu.PrefetchScalarGridSpec(
            num_scalar_prefetch=1,
            grid=(t // tm,),
            in_specs=[any_spec, any_spec],
            out_specs=any_spec,
            scratch_shapes=[pltpu.VMEM((SCATTER_BUFS, tm, D_MODEL), F32), pltpu.SMEM((2 * rows,), jnp.int32),
                            pltpu.SemaphoreType.DMA((SCATTER_BUFS,)), pltpu.SemaphoreType.DMA((SCATTER_BUFS,)),
                            pltpu.SemaphoreType.DMA((2,)), pltpu.SemaphoreType.DMA]),
        out_shape=jax.ShapeDtypeStruct((n_rows, D_MODEL), F32),
        compiler_params=_cparams(("arbitrary",)),
        name="scatter",
    )(zero_blk, x1, pos_tiles)


def _expert_kernel(blk_exp_ref, nblk_ref, nvalid_ref, x_ref, gf_ref, wgu_ref, wd_ref, y_ref):
    del blk_exp_ref
    i = pl.program_id(0)

    @pl.when(i < nblk_ref[0])
    def _():
        real = lax.broadcasted_iota(jnp.int32, x_ref.shape, 0) < nvalid_ref[i]
        x = jnp.where(real, x_ref[...], 0.0)
        hn = (x * lax.rsqrt(jnp.mean(x * x, axis=-1, keepdims=True) + EPS) * gf_ref[...]).astype(BF16)
        gu = jnp.dot(hn, wgu_ref[0], preferred_element_type=F32)
        gt, up = gu[:, :D_EXPERT], gu[:, D_EXPERT:]
        act = (gt * (1.0 / (1.0 + jnp.exp(-gt))) * up).astype(BF16)
        y_ref[...] = jnp.dot(act, wd_ref[0], preferred_element_type=F32)

    @pl.when(i >= nblk_ref[0])
    def _():
        y_ref[...] = jnp.zeros(y_ref.shape, F32)


def _expert_call(xr, rows, blk_exp, nblk, nvalid, gf, wgu, wd):
    nb = xr.shape[0] // rows
    return pl.pallas_call(
        _expert_kernel,
        grid_spec=pltpu.PrefetchScalarGridSpec(
            num_scalar_prefetch=3,
            grid=(nb,),
            in_specs=[pl.BlockSpec((rows, D_MODEL), lambda i, be, n, nv: (jnp.minimum(i, n[0] - 1), 0)),
                      pl.BlockSpec(gf.shape, lambda i, be, n, nv: (0, 0)),
                      pl.BlockSpec((1, D_MODEL, 2 * D_EXPERT), lambda i, be, n, nv: (be[i], 0, 0)),
                      pl.BlockSpec((1, D_EXPERT, D_MODEL), lambda i, be, n, nv: (be[i], 0, 0))],
            out_specs=pl.BlockSpec((rows, D_MODEL), lambda i, be, n, nv: (i, 0))),
        out_shape=jax.ShapeDtypeStruct((nb * rows, D_MODEL), F32),
        compiler_params=_cparams(("arbitrary",)),
        name="experts",
    )(blk_exp, nblk, nvalid, xr, gf, wgu, wd)


def _combine_kernel(x1_ref, w0_ref, w1_ref, y_hbm, pos_hbm, o_ref, buf, idx, sem_rows, sem_idx):
    rows = buf.shape[1]
    tm = rows // 2
    slot = _gather_step(pl.program_id(0), pl.num_programs(0), y_hbm, pos_hbm, buf, idx, sem_rows, sem_idx)
    w0 = w0_ref[...]
    w1 = w1_ref[...]
    for c in range(D_MODEL // LANES):
        sl = slice(c * LANES, (c + 1) * LANES)
        o_ref[:, sl] = x1_ref[:, sl] + w0 * buf[slot, 0:tm, sl] + w1 * buf[slot, tm:rows, sl]


def _combine_call(x1, w0, w1, yr, pos_tiles, tm):
    t = x1.shape[0]
    rows = TOP_K * tm
    row = lambda w: pl.BlockSpec((tm, w), lambda i: (i, 0))
    return pl.pallas_call(
        _combine_kernel,
        grid=(t // tm,),
        in_specs=[row(D_MODEL), row(LANES), row(LANES),
                  pl.BlockSpec(memory_space=pl.ANY), pl.BlockSpec(memory_space=pl.ANY)],
        out_specs=row(D_MODEL),
        out_shape=jax.ShapeDtypeStruct((t, D_MODEL), F32),
        scratch_shapes=[pltpu.VMEM((2, rows, D_MODEL), F32), pltpu.SMEM((2 * rows,), jnp.int32),
                        pltpu.SemaphoreType.DMA((2,)), pltpu.SemaphoreType.DMA((2,))],
        compiler_params=_cparams(("arbitrary",)),
        name="combine",
    )(x1, w0, w1, yr, pos_tiles)


def _rot_cols(w):
    half = ROPE_DIM // 2
    return jnp.concatenate([-w[..., half:], w[..., :half]], axis=-1)


def _head_pad(nope, rope):
    z = jnp.zeros(rope.shape[:-1] + (LANES - B_QK_DIM,), rope.dtype)
    out = jnp.concatenate([nope, rope, z], axis=-1)
    return out.reshape(out.shape[:-2] + (out.shape[-2] * LANES,))


def _t5_bucket(dist):
    max_exact = REL_BUCKETS // 2
    d = jnp.maximum(dist, 0)
    ratio = jnp.maximum(d, 1).astype(F32) / max_exact
    large = max_exact + (jnp.log(ratio) / math.log(REL_MAX_DIST / max_exact)
                         * (REL_BUCKETS - max_exact)).astype(jnp.int32)
    large = jnp.minimum(large, REL_BUCKETS - 1)
    return jnp.where(d < max_exact, d, large)


def _swa_bias(rel_bias, tq):
    qi = jnp.arange(tq)[:, None]
    kj = jnp.arange(2 * tq)[None, :]
    dist = qi + tq - kj
    bias = rel_bias[_t5_bucket(dist)].astype(F32).transpose(2, 0, 1) * LOG2E
    mask = (dist >= 0) & (dist < WINDOW)
    bias = jnp.where(mask[None], bias, NEG)
    return bias.reshape(A_KV_HEADS, (A_HEADS // A_KV_HEADS) * tq, 2 * tq)


def _rope_tables(seq):
    half = ROPE_DIM // 2
    inv_freq = ROPE_THETA ** (-jnp.arange(half, dtype=F32) / half)
    ang = jnp.arange(seq, dtype=F32)[:, None] * inv_freq[None, :]
    cos = jnp.concatenate([jnp.cos(ang), jnp.cos(ang)], axis=-1)
    sin = jnp.concatenate([jnp.sin(ang), jnp.sin(ang)], axis=-1)
    return cos, sin


def _lane_table(nope_part, rope_part):
    seq = rope_part.shape[0]
    nope_part = jnp.broadcast_to(nope_part, (seq, NOPE_DIM))
    return jnp.concatenate([nope_part, rope_part, jnp.zeros((seq, LANES - B_QK_DIM), F32)], axis=-1)


def _prep_layer(l, seq, cos, sin, attn_norm, w_in, b_gate, a_q_norm, a_k_norm, w_oa, q_a_norm, w_qb, kv_a_norm,
                w_kvb, b_q_norm, b_k_norm, w_ob, w_o, ffn_norm, w_rg, b_rg, w_re, b_re, w_eg, w_eu, w_ed):
    w = w_in[l]
    o = 0
    parts = []
    for sz in (A_WIDTH, A_KV_HEADS * A_HEAD_DIM, A_KV_HEADS * A_HEAD_DIM, Q_LORA, KV_LORA, ROPE_DIM, 2 * D_MODEL):
        parts.append(w[:, o:o + sz])
        o += sz
    wqa, wka, wva, wcq, wckv, wkpe, wg = parts
    perm = jnp.array([c + (A_HEADS // 2) * h for c in range(A_HEADS // 2) for h in range(2)])
    wqa = wqa.reshape(D_MODEL, A_HEADS, A_HEAD_DIM)[:, perm].reshape(D_MODEL, A_WIDTH)
    zl = jnp.zeros((D_MODEL, NOPE_DIM), F32)
    zr = jnp.zeros((D_MODEL, LANES - B_QK_DIM), F32)
    w1 = jnp.concatenate([wqa, wka, wva, wcq, wckv, zl, wkpe, zr, zl, _rot_cols(wkpe), zr, wg], axis=1).astype(BF16)

    wq = w_qb[l].reshape(Q_LORA, B_HEADS, B_QK_DIM)
    zq = jnp.zeros((Q_LORA, B_HEADS, NOPE_DIM), F32)
    wq2 = jnp.concatenate([_head_pad(wq[..., :NOPE_DIM], wq[..., NOPE_DIM:]),
                           _head_pad(zq, _rot_cols(wq[..., NOPE_DIM:]))], axis=1).astype(BF16)
    wkv = w_kvb[l].reshape(KV_LORA, B_HEADS, NOPE_DIM + B_V_DIM)
    wk2 = _head_pad(wkv[..., :NOPE_DIM], jnp.zeros((KV_LORA, B_HEADS, ROPE_DIM), F32)).astype(BF16)
    wv = wkv[..., NOPE_DIM:]
    zv = jnp.zeros_like(wv)
    wv2 = jnp.where((jnp.arange(B_HEADS) % 2 == 0)[None, :, None], jnp.concatenate([wv, zv], -1),
                    jnp.concatenate([zv, wv], -1)).reshape(KV_LORA, B_HEADS * LANES).astype(BF16)
    one_lane = jnp.where(jnp.arange(B_HEADS) % 2 == 0, V_ONE_EVEN, V_ONE_ODD)
    vone = (jnp.arange(LANES)[None, :] == one_lane[:, None]).astype(F32).reshape(1, B_HEADS * LANES)

    def rope_tabs(gain, scale):
        gn, gr = gain[:NOPE_DIM], gain[NOPE_DIM:]
        gr_rot = jnp.concatenate([gr[ROPE_DIM // 2:], gr[:ROPE_DIM // 2]])
        a = _lane_table(gn[None, :] * scale, gr[None, :] * cos * scale)
        b = _lane_table(jnp.zeros((1, NOPE_DIM), F32), gr_rot[None, :] * sin * scale)
        return a, b

    aq, bq = rope_tabs(b_q_norm[l], B_QK_DIM ** -0.5 * LOG2E)
    score_bound = math.sqrt(B_QK_DIM) * LOG2E * jnp.max(jnp.abs(b_q_norm[l])) * jnp.max(jnp.abs(b_k_norm[l]))
    ak, bk = rope_tabs(b_k_norm[l], 1.0)

    gqa = (jnp.tile(a_q_norm[l], 2) * (A_HEAD_DIM ** -0.5 * LOG2E))[None, :]
    gka = jnp.tile(a_k_norm[l], 2)[None, :]
    woa = w_oa[l].reshape(A_HEADS, A_HEAD_DIM, D_MODEL)[perm].reshape(A_WIDTH, D_MODEL).astype(BF16)

    wr = jnp.concatenate([w_rg[l], w_re[l], jnp.zeros((D_MODEL, LANES - N_GROUPS - N_EXPERTS), F32)], axis=1)
    wrh = wr.astype(BF16)
    wrl = (wr - wrh.astype(F32)).astype(BF16)
    br = jnp.concatenate([b_rg[l], b_re[l], jnp.zeros((LANES - N_GROUPS - N_EXPERTS,), F32)])[None, :]

    return dict(
        gn=attn_norm[l][None, :], w1=w1, wq2=wq2, wk2=wk2, wv2=wv2, vone=vone, gcq=q_a_norm[l][None, :],
        gckv=kv_a_norm[l][None, :], score_bound=score_bound, bg=b_gate[l][None, :], gqa=gqa, gka=gka,
        aq=aq, bq=bq, ak=ak, bk=bk,
        woa=woa, wob=w_ob[l].astype(BF16), wo=w_o[l].astype(BF16), gf=ffn_norm[l][None, :], wr=jnp.concatenate([wrh, wrl], axis=1), br=br,
        wgu=jnp.concatenate([w_eg[l], w_eu[l]], axis=-1).astype(BF16), wd=w_ed[l].astype(BF16))


def _tril(n):
    return (jnp.arange(n)[:, None] >= jnp.arange(n)[None, :]).astype(jnp.int32)


def _dispatch(route, tile_cnt, rows):
    t = route.shape[0]
    nt = tile_cnt.shape[0]
    tile_cnt = tile_cnt[:, 0, :N_EXPERTS]
    counts = jnp.sum(tile_cnt, axis=0)
    padded = ((counts + rows - 1) // rows) * rows
    pends = jnp.dot(_tril(N_EXPERTS), padded)
    pstarts = pends - padded
    tile_base = pstarts[None, :] + jnp.dot(_tril(nt), tile_cnt) - tile_cnt
    base = jnp.broadcast_to(tile_base[:, None, None, :], (nt, t // nt, 1, N_EXPERTS)).reshape(t, 1, N_EXPERTS)
    onehot = route[:, :TOP_K, None] == jnp.arange(N_EXPERTS, dtype=jnp.int32)[None, None, :]
    dest = jnp.sum(jnp.where(onehot, base, 0), axis=-1) + route[:, TOP_K:2 * TOP_K]
    n_rows = t * TOP_K + N_EXPERTS * rows
    n_blocks = n_rows // rows
    blk_start = jnp.arange(n_blocks, dtype=jnp.int32)[:, None] * rows
    blk_exp = jnp.minimum(jnp.sum((pends[None, :] <= blk_start).astype(jnp.int32), axis=1), N_EXPERTS - 1)
    nblk = (pends[-1] // rows).astype(jnp.int32).reshape(1)
    nvalid = jnp.clip((pstarts + counts)[blk_exp] - blk_start[:, 0], 0, rows).astype(jnp.int32)
    last_blk = jnp.where(padded > 0, pends // rows - 1, -1)
    trailing = nblk[0] + jnp.arange(N_EXPERTS, dtype=jnp.int32)
    zero_blk = jnp.concatenate([last_blk, jnp.where(trailing < n_blocks, trailing, -1)]).astype(jnp.int32)
    return dest, blk_exp, nblk, nvalid, zero_blk, n_rows


def _swa_tables(sinks, rel_bias, a_q_norm, a_k_norm):
    per_group = A_HEADS // A_KV_HEADS
    tab = jnp.broadcast_to((sinks * LOG2E).reshape(A_KV_HEADS, per_group, 1, 1),
                           (A_KV_HEADS, per_group, SWA_TQ, LANES)).reshape(A_KV_HEADS, per_group * SWA_TQ, LANES)
    qk = math.sqrt(A_HEAD_DIM) * jnp.max(jnp.abs(a_q_norm)) * jnp.max(jnp.abs(a_k_norm))
    bound = LOG2E * jnp.maximum(qk + jnp.max(jnp.abs(rel_bias)), jnp.max(jnp.abs(sinks)))
    return tab, bound


def _layer(x2d, bsz, seq, p, sink_tab, swa_bound, bias):
    t = x2d.shape[0]
    qa, ka, va, qb, kb, vb, gates = _proj_call(x2d, seq, p)
    oa = _swa_call(qa, ka, va, sink_tab, bias, bsz, seq, swa_bound)
    ob = _mla_call(qb, kb, vb, bsz, seq, p["score_bound"])
    x1, w0, w1, route, tile_cnt = _merge_call(x2d, oa, ob, gates, p)
    dest, blk_exp, nblk, nvalid, zero_blk, n_rows = _dispatch(route, tile_cnt, MOE_ROWS)
    pos_tiles = dest.reshape(t // COMB_TM, COMB_TM, TOP_K).transpose(0, 2, 1).reshape(t * TOP_K)
    xr = _scatter_call(x1, pos_tiles, zero_blk, n_rows, COMB_TM, MOE_ROWS)
    yr = _expert_call(xr, MOE_ROWS, blk_exp, nblk, nvalid, p["gf"], p["wgu"], p["wd"])
    return _combine_call(x1, w0, w1, yr, pos_tiles, COMB_TM)


def kernel(x, attn_norm, w_in, b_gate, a_q_norm, a_k_norm, a_sinks, rel_bias, w_oa, q_a_norm, w_qb, kv_a_norm, w_kvb, b_q_norm, b_k_norm, w_ob, w_o, ffn_norm, w_router_group, b_router_group, w_router_expert, b_router_expert, w_exp_gate, w_exp_up, w_exp_down):
    bsz, seq, d = x.shape
    assert d == D_MODEL and seq % MLA_TQ == 0 and seq % PROJ_TM == 0 and (bsz * seq) % COMB_TM == 0
    depth = w_in.shape[0]
    cos, sin = _rope_tables(seq)
    bias = _swa_bias(rel_bias, SWA_TQ)
    x2d = x.reshape(bsz * seq, d)
    for l in range(depth):
        p = _prep_layer(l, seq, cos, sin, attn_norm, w_in, b_gate, a_q_norm, a_k_norm, w_oa, q_a_norm, w_qb,
                        kv_a_norm, w_kvb, b_q_norm, b_k_norm, w_ob, w_o, ffn_norm, w_router_group,
                        b_router_group, w_router_expert, b_router_expert, w_exp_gate, w_exp_up, w_exp_down)
        sink_tab, swa_bound = _swa_tables(a_sinks[l], rel_bias, a_q_norm[l], a_k_norm[l])
        x2d = _layer(x2d, bsz, seq, p, sink_tab, swa_bound, bias)
    return x2d.reshape(bsz, seq, d)
```

```python
import functools
import math

import jax
import jax.numpy as jnp
from jax import lax
from jax.experimental import pallas as pl
from jax.experimental.pallas import tpu as pltpu

F32 = jnp.float32
BF16 = jnp.bfloat16

D_MODEL = 1024
A_HEADS = 8
A_KV_HEADS = 2
A_HEAD_DIM = 64
A_WIDTH = A_HEADS * A_HEAD_DIM
WINDOW = 128
REL_BUCKETS = 32
REL_MAX_DIST = 128
B_HEADS = 8
Q_LORA = 256
KV_LORA = 128
NOPE_DIM = 64
ROPE_DIM = 32
B_QK_DIM = NOPE_DIM + ROPE_DIM
B_V_DIM = 64
B_WIDTH = B_HEADS * B_V_DIM
ROPE_THETA = 10000.0
N_GROUPS = 4
EXPERTS_PER_GROUP = 8
N_EXPERTS = N_GROUPS * EXPERTS_PER_GROUP
TOP_K = 2
D_EXPERT = 256
EPS = 1e-6

LANES = 128
NEG = -1e30
LOG2E = math.log2(math.e)
V_ONE_EVEN = B_V_DIM
V_ONE_ODD = 0
EXP2_SAFE_BOUND = 100.0
VMEM_LIMIT = 56 * 1024 * 1024

C_QA = 0
C_KA = C_QA + A_WIDTH
C_VA = C_KA + LANES
C_CQ = C_VA + LANES
C_CKV = C_CQ + Q_LORA
C_KPE = C_CKV + KV_LORA
C_KPR = C_KPE + LANES
C_GATE = C_KPR + LANES
C_END = C_GATE + 2 * D_MODEL

PROJ_TM = 512
SWA_TQ = 128
SWA_SUB = 4
MLA_TQ = 1024
MLA_TK = 512
MERGE_TM = 512
MOE_ROWS = 512
COMB_TM = 256


def _cparams(sem):
    return pltpu.CompilerParams(dimension_semantics=sem, vmem_limit_bytes=VMEM_LIMIT)


def _proj_kernel(x_ref, gn_ref, w1_ref, wq2_ref, wk2_ref, wv2_ref, vone_ref, gcq_ref, gckv_ref, bg_ref,
                 gqa_ref, gka_ref, aq_ref, bq_ref, ak_ref, bk_ref,
                 qa_ref, ka_ref, va_ref, qb_ref, kb_ref, vb_ref, gate_ref):
    tm = x_ref.shape[0]
    x = x_ref[...]
    h = (x * lax.rsqrt(jnp.mean(x * x, axis=-1, keepdims=True) + EPS) * gn_ref[...]).astype(BF16)

    def proj(a, b):
        return jnp.dot(h, w1_ref[:, a:b], preferred_element_type=F32)

    lo = lax.broadcasted_iota(jnp.int32, (tm, LANES), 1) < A_HEAD_DIM

    def pair_norm(t, gain):
        t2 = t * t
        s_lo = jnp.sum(jnp.where(lo, t2, 0.0), axis=-1, keepdims=True)
        s_hi = jnp.sum(jnp.where(lo, 0.0, t2), axis=-1, keepdims=True)
        r = jnp.where(lo, lax.rsqrt(s_lo / A_HEAD_DIM + EPS), lax.rsqrt(s_hi / A_HEAD_DIM + EPS))
        return (t * r * gain).astype(BF16)

    qa = proj(C_QA, C_KA)
    for c in range(A_WIDTH // LANES):
        sl = slice(c * LANES, (c + 1) * LANES)
        qa_ref[:, sl] = pair_norm(qa[:, sl], gqa_ref[...])
    kvp = proj(C_KA, C_CQ)
    ka_ref[...] = pair_norm(kvp[:, :LANES], gka_ref[...])
    va_ref[...] = kvp[:, LANES:].astype(BF16)

    lat = proj(C_CQ, C_GATE)
    cq = lat[:, :Q_LORA]
    cqn = (cq * lax.rsqrt(jnp.mean(cq * cq, axis=-1, keepdims=True) + EPS) * gcq_ref[...]).astype(BF16)
    ckv = lat[:, Q_LORA:Q_LORA + KV_LORA]
    ckvn = (ckv * lax.rsqrt(jnp.mean(ckv * ckv, axis=-1, keepdims=True) + EPS) * gckv_ref[...]).astype(BF16)
    kpe = lat[:, C_KPE - C_CQ:C_KPR - C_CQ]
    kpr = lat[:, C_KPR - C_CQ:C_GATE - C_CQ]

    aq, bq, ak, bk = aq_ref[...], bq_ref[...], ak_ref[...], bk_ref[...]
    q2 = jnp.dot(cqn, wq2_ref[...], preferred_element_type=F32)
    k2 = jnp.dot(ckvn, wk2_ref[...], preferred_element_type=F32)
    vb_ref[...] = (jnp.dot(ckvn, wv2_ref[...], preferred_element_type=F32) + vone_ref[...]).astype(BF16)
    kr = kpr * bk
    for hd in range(B_HEADS):
        sl = slice(hd * LANES, (hd + 1) * LANES)
        qp = q2[:, sl]
        qr = q2[:, B_HEADS * LANES + hd * LANES:B_HEADS * LANES + (hd + 1) * LANES]
        rq = lax.rsqrt(jnp.sum(qp * qp, axis=-1, keepdims=True) / B_QK_DIM + EPS)
        qb_ref[:, sl] = (rq * (qp * aq + qr * bq)).astype(BF16)
        kp = k2[:, sl] + kpe
        rk = lax.rsqrt(jnp.sum(kp * kp, axis=-1, keepdims=True) / B_QK_DIM + EPS)
        kb_ref[:, sl] = (rk * (kp * ak + kr)).astype(BF16)

    z = proj(C_GATE, C_END) + bg_ref[...]
    gate_ref[...] = (1.0 / (1.0 + jnp.exp(-z))).astype(BF16)


def _proj_call(x2d, seq, p):
    t = x2d.shape[0]
    tm = PROJ_TM
    nt = t // tm
    npos = seq // tm
    row = lambda w: pl.BlockSpec((tm, w), lambda i: (i, 0))
    full = lambda a: pl.BlockSpec(a.shape, lambda i: (0,) * a.ndim)
    pos = pl.BlockSpec((tm, LANES), lambda i: (i % npos, 0))
    consts = [p["gn"], p["w1"], p["wq2"], p["wk2"], p["wv2"], p["vone"], p["gcq"], p["gckv"], p["bg"], p["gqa"],
              p["gka"]]
    tabs = [p["aq"], p["bq"], p["ak"], p["bk"]]
    widths = [A_WIDTH, LANES, LANES, B_HEADS * LANES, B_HEADS * LANES, B_HEADS * LANES, 2 * D_MODEL]
    return pl.pallas_call(
        _proj_kernel,
        grid=(nt,),
        in_specs=[row(D_MODEL)] + [full(a) for a in consts] + [pos] * 4,
        out_specs=[row(w) for w in widths],
        out_shape=[jax.ShapeDtypeStruct((t, w), BF16) for w in widths],
        compiler_params=_cparams(("parallel",)),
        name="proj",
    )(x2d, *consts, *tabs)


def _swa_kernel(q_ref, kc_ref, kp_ref, vc_ref, vp_ref, bias_ref, sink_ref, o_ref, *, shift):
    tq = SWA_TQ
    groups = A_WIDTH // LANES
    first = pl.program_id(1) == 0
    one_col = (lax.broadcasted_iota(jnp.int32, (2 * tq, LANES), 1) == 0).astype(BF16)
    lo = lax.broadcasted_iota(jnp.int32, (tq, LANES), 1) < A_HEAD_DIM
    no_prev = jnp.logical_and(first, lax.broadcasted_iota(jnp.int32, (groups * tq, 2 * tq), 1) < tq)
    for sub in range(SWA_SUB):
        if sub == 0:
            k = jnp.concatenate([kp_ref[...], kc_ref[0:tq, :]], axis=0)
            v = jnp.concatenate([vp_ref[...], vc_ref[0:tq, :]], axis=0)
        else:
            k = kc_ref[(sub - 1) * tq:(sub + 1) * tq, :]
            v = vc_ref[(sub - 1) * tq:(sub + 1) * tq, :]
        v_ext = jnp.concatenate([v, one_col], axis=1)
        qrows = slice(sub * tq, (sub + 1) * tq)
        qs = [q_ref[qrows, c * LANES:(c + 1) * LANES] for c in range(groups)]
        res = []
        for g in range(A_KV_HEADS):
            keep = lo if g == 0 else jnp.logical_not(lo)
            qg = jnp.concatenate([jnp.where(keep, qc, jnp.zeros_like(qc)) for qc in qs], axis=0)
            s = lax.dot_general(qg, k, (((1,), (1,)), ((), ())), preferred_element_type=F32) + bias_ref[g]
            sink = sink_ref[g][:, :1]
            if shift:
                if sub == 0:
                    s = jnp.where(no_prev, NEG, s)
                m = jnp.maximum(jnp.max(s, axis=-1, keepdims=True), sink)
                e = jnp.exp2(s - m)
                sink_term = jnp.exp2(sink - m)
            else:
                e = jnp.exp2(s)
                if sub == 0:
                    e = jnp.where(no_prev, 0.0, e)
                sink_term = jnp.exp2(sink)
            pv = jnp.dot(e.astype(BF16), v_ext, preferred_element_type=F32)
            res.append(pv[:, :LANES] * (1.0 / (pv[:, LANES:LANES + 1] + sink_term)))
        for c in range(groups):
            rows = slice(c * tq, (c + 1) * tq)
            o_ref[qrows, c * LANES:(c + 1) * LANES] = jnp.where(lo, res[0][rows], res[1][rows]).astype(BF16)


def _swa_call(qa, ka, va, sink_tab, bias, bsz, seq, score_bound):
    t = qa.shape[0]
    tq = SWA_TQ
    step = SWA_SUB * tq
    ns = seq // step
    cur = lambda b, i: (b * ns + i, 0)
    prev = lambda b, i: (b * (seq // tq) + jnp.maximum(SWA_SUB * i - 1, 0), 0)
    full = lambda a: pl.BlockSpec(a.shape, lambda b, i: (0,) * a.ndim)

    def call(shift):
        return pl.pallas_call(
            functools.partial(_swa_kernel, shift=shift),
            grid=(bsz, ns),
            in_specs=[pl.BlockSpec((step, A_WIDTH), cur),
                      pl.BlockSpec((step, LANES), cur), pl.BlockSpec((tq, LANES), prev),
                      pl.BlockSpec((step, LANES), cur), pl.BlockSpec((tq, LANES), prev),
                      full(bias), full(sink_tab)],
            out_specs=pl.BlockSpec((step, A_WIDTH), cur),
            out_shape=jax.ShapeDtypeStruct((t, A_WIDTH), BF16),
            compiler_params=_cparams(("parallel", "parallel")),
            name="swa_shifted" if shift else "swa",
        )(qa, ka, ka, va, va, bias, sink_tab)

    return lax.cond(score_bound <= EXP2_SAFE_BOUND, lambda: call(False), lambda: call(True))


def _mla_kernel(q_ref, k_ref, v_ref, o_ref, acc_sc, m_sc, *, shift):
    tq = q_ref.shape[0]
    tk = MLA_TK
    per_tile = tq // tk
    qi = pl.program_id(2)
    acc_sc[...] = jnp.zeros(acc_sc.shape, F32)
    if shift:
        m_sc[...] = jnp.full(m_sc.shape, NEG, F32)

    def tile(j, r0, r1, masked):
        keys = pl.ds(pl.multiple_of(j * tk, tk), tk)
        if masked:
            causal = (lax.broadcasted_iota(jnp.int32, (r1 - r0, tk), 1)
                      <= lax.broadcasted_iota(jnp.int32, (r1 - r0, tk), 0))
        for half in range(2):
            hs = slice(half * LANES, (half + 1) * LANES)
            s = lax.dot_general(q_ref[r0:r1, hs], k_ref[keys, hs], (((1,), (1,)), ((), ())),
                                preferred_element_type=F32)
            if shift:
                if masked:
                    s = jnp.where(causal, s, NEG)
                m_prev = m_sc[half, r0:r1]
                m_new = jnp.maximum(m_prev, jnp.max(s, axis=-1, keepdims=True))
                m_sc[half, r0:r1] = m_new
                e = jnp.exp2(s - m_new)
                acc_sc[half, r0:r1] = jnp.exp2(m_prev - m_new) * acc_sc[half, r0:r1]
            else:
                e = jnp.exp2(s)
                if masked:
                    e = jnp.where(causal, e, 0.0)
            acc_sc[half, r0:r1] += jnp.dot(e.astype(BF16), v_ref[keys, hs], preferred_element_type=F32)

    def body(j, carry):
        tile(j, 0, tq, False)
        return carry

    lax.fori_loop(0, qi * per_tile, body, 0)
    for d in range(per_tile):
        tile(qi * per_tile + d, d * tk, (d + 1) * tk, True)
        if (d + 1) * tk < tq:
            tile(qi * per_tile + d, (d + 1) * tk, tq, False)
    a0 = acc_sc[0]
    a1 = acc_sc[1]
    lo = lax.broadcasted_iota(jnp.int32, (tq, LANES), 1) < B_V_DIM
    o_ref[...] = jnp.where(lo, a0 * (1.0 / a0[:, V_ONE_EVEN:V_ONE_EVEN + 1]),
                           a1 * (1.0 / a1[:, V_ONE_ODD:V_ONE_ODD + 1])).astype(BF16)


def _mla_call(qb, kb, vb, bsz, seq, score_bound):
    t = qb.shape[0]
    tq = MLA_TQ
    nq = seq // tq
    pairs = B_HEADS // 2

    def call(shift):
        return pl.pallas_call(
            functools.partial(_mla_kernel, shift=shift),
            grid=(bsz, pairs, nq),
            in_specs=[pl.BlockSpec((tq, 2 * LANES), lambda b, p, i: (b * nq + i, p)),
                      pl.BlockSpec((seq, 2 * LANES), lambda b, p, i: (b, p)),
                      pl.BlockSpec((seq, 2 * LANES), lambda b, p, i: (b, p))],
            out_specs=pl.BlockSpec((tq, LANES), lambda b, p, i: (b * nq + i, p)),
            out_shape=jax.ShapeDtypeStruct((t, B_WIDTH), BF16),
            scratch_shapes=[pltpu.VMEM((2, tq, LANES), F32), pltpu.VMEM((2, tq, 1), F32)],
            compiler_params=_cparams(("parallel", "parallel", "arbitrary")),
            name="mla_shifted" if shift else "mla",
        )(qb, kb, vb)

    return lax.cond(score_bound <= EXP2_SAFE_BOUND, lambda: call(False), lambda: call(True))


def _merge_kernel(x_ref, oa_ref, ob_ref, gate_ref, woa_ref, wob_ref, wo_ref, gf_ref, wr_ref, br_ref,
                  tri_ref, x1_ref, w0_ref, w1_ref, eid_ref, cnt_ref):
    tm = x_ref.shape[0]
    ya = jnp.dot(oa_ref[...], woa_ref[...], preferred_element_type=F32)
    yb = jnp.dot(ob_ref[...], wob_ref[...], preferred_element_type=F32)
    g = gate_ref[...].astype(F32)
    mix = (g[:, :D_MODEL] * ya + g[:, D_MODEL:] * yb).astype(BF16)
    x1 = x_ref[...] + jnp.dot(mix, wo_ref[...], preferred_element_type=F32)
    x1_ref[...] = x1

    hn = x1 * lax.rsqrt(jnp.mean(x1 * x1, axis=-1, keepdims=True) + EPS) * gf_ref[...]
    hh = hn.astype(BF16)
    hl = (hn - hh.astype(F32)).astype(BF16)
    ph = jnp.dot(hh, wr_ref[...], preferred_element_type=F32)
    pl_ = jnp.dot(hl, wr_ref[...], preferred_element_type=F32)
    logits = (ph[:, :LANES] + ph[:, LANES:]) + (pl_[:, :LANES] + pl_[:, LANES:]) + br_ref[...]

    lane = lax.broadcasted_iota(jnp.int32, (tm, LANES), 1).astype(F32)
    big = float(LANES)
    gmask = lane < N_GROUPS
    gl = jnp.where(gmask, logits, NEG)
    gmax = jnp.max(gl, axis=-1, keepdims=True)
    gsum = jnp.sum(jnp.where(gmask, jnp.exp(gl - gmax), 0.0), axis=-1, keepdims=True)
    g_p = 1.0 / gsum
    g_idx = jnp.min(jnp.where(gl == gmax, lane, big), axis=-1, keepdims=True)
    e_lo = N_GROUPS + EXPERTS_PER_GROUP * g_idx
    emask = jnp.logical_and(lane >= e_lo, lane < e_lo + EXPERTS_PER_GROUP)
    el = jnp.where(emask, logits, NEG)
    t1 = jnp.max(el, axis=-1, keepdims=True)
    i1 = jnp.min(jnp.where(el == t1, lane, big), axis=-1, keepdims=True)
    el2 = jnp.where(lane == i1, NEG, el)
    t2 = jnp.max(el2, axis=-1, keepdims=True)
    i2 = jnp.min(jnp.where(el2 == t2, lane, big), axis=-1, keepdims=True)
    e2 = jnp.exp(t2 - t1)
    w_first = g_p / (1.0 + e2)
    w0_ref[...] = jnp.broadcast_to(w_first, (tm, LANES))
    w1_ref[...] = jnp.broadcast_to(w_first * e2, (tm, LANES))
    ex1 = i1 - N_GROUPS
    ex2 = i2 - N_GROUPS

    oh1 = lane == ex1
    oh2 = lane == ex2
    cnt = (oh1.astype(F32) + oh2.astype(F32)).astype(BF16)
    before = jnp.dot(tri_ref[...], cnt, preferred_element_type=F32)
    r1 = jnp.sum(jnp.where(oh1, before, 0.0), axis=-1, keepdims=True)
    r2 = jnp.sum(jnp.where(oh2, before, 0.0), axis=-1, keepdims=True)
    ids = jnp.where(lane == 0.0, ex1, jnp.where(lane == 1.0, ex2, jnp.where(lane == 2.0, r1,
                                                                             jnp.where(lane == 3.0, r2, 0.0))))
    eid_ref[...] = ids.astype(jnp.int32)
    tile_cnt = jnp.sum(cnt.astype(F32), axis=0, keepdims=True)
    cnt_ref[...] = jnp.broadcast_to(tile_cnt, cnt_ref.shape[1:]).astype(jnp.int32)[None]


def _merge_call(x2d, oa, ob, gates, p):
    t = x2d.shape[0]
    tm = MERGE_TM
    nt = t // tm
    row = lambda w: pl.BlockSpec((tm, w), lambda i: (i, 0))
    full = lambda a: pl.BlockSpec(a.shape, lambda i: (0,) * a.ndim)
    tri = (jnp.arange(tm)[:, None] > jnp.arange(tm)[None, :]).astype(BF16)
    consts = [p["woa"], p["wob"], p["wo"], p["gf"], p["wr"], p["br"], tri]
    return pl.pallas_call(
        _merge_kernel,
        grid=(nt,),
        in_specs=[row(D_MODEL), row(A_WIDTH), row(B_WIDTH), row(2 * D_MODEL)] + [full(a) for a in consts],
        out_specs=[row(D_MODEL), row(LANES), row(LANES), row(LANES),
                   pl.BlockSpec((1, 8, LANES), lambda i: (i, 0, 0))],
        out_shape=[jax.ShapeDtypeStruct((t, D_MODEL), F32), jax.ShapeDtypeStruct((t, LANES), F32),
                   jax.ShapeDtypeStruct((t, LANES), F32), jax.ShapeDtypeStruct((t, LANES), jnp.int32),
                   jax.ShapeDtypeStruct((nt, 8, LANES), jnp.int32)],
        compiler_params=_cparams(("parallel",)),
        name="merge",
    )(x2d, oa, ob, gates, *consts)


def _rows_copy(src_hbm, row, dst, j, sem):
    return pltpu.make_async_copy(src_hbm.at[row], dst.at[j], sem)


def _start_rows(src_hbm, idx, base, dst, sem, n):
    for j in range(n):
        _rows_copy(src_hbm, idx[base + j], dst, j, sem).start()


WAIT_ROWS = 128


def _wait_rows(src_hbm, dst, sem, n):
    for c in range(n // WAIT_ROWS):
        rows = pl.ds(c * WAIT_ROWS, WAIT_ROWS)
        pltpu.make_async_copy(src_hbm.at[rows], dst.at[rows], sem).wait()


def _gather_step(i, n, src_hbm, idx_hbm, buf, idx, sem_rows, sem_idx):
    rows = buf.shape[1]
    slot = i % 2
    nslot = 1 - slot

    def idx_copy(b, s):
        return pltpu.make_async_copy(idx_hbm.at[pl.ds(b * rows, rows)], idx.at[pl.ds(s * rows, rows)], sem_idx.at[s])

    @pl.when(jnp.logical_and(i == 0, n > 0))
    def _():
        idx_copy(0, 0).start()
        idx_copy(0, 0).wait()
        _start_rows(src_hbm, idx, 0, buf.at[0], sem_rows.at[0], rows)

        @pl.when(n > 1)
        def _():
            idx_copy(1, 1).start()

    @pl.when(i + 1 < n)
    def _():
        idx_copy(i + 1, nslot).wait()
        _start_rows(src_hbm, idx, nslot * rows, buf.at[nslot], sem_rows.at[nslot], rows)

    @pl.when(i + 2 < n)
    def _():
        idx_copy(i + 2, slot).start()

    @pl.when(i < n)
    def _():
        _wait_rows(src_hbm, buf.at[slot], sem_rows.at[slot], rows)
    return slot


SCATTER_BUFS = 3


def _scatter_kernel(zero_blk_ref, x_hbm, dest_hbm, xr_hbm, buf, rbuf, idx, sem_x, sem_rows, sem_idx, sem_zero,
                    *, blk_rows):
    i = pl.program_id(0)
    n = pl.num_programs(0)
    tm = buf.shape[1]
    rows = TOP_K * tm
    islot = i % 2
    xslot = i % SCATTER_BUFS

    def idx_copy(b, s):
        return pltpu.make_async_copy(dest_hbm.at[pl.ds(b * rows, rows)], idx.at[pl.ds(s * rows, rows)], sem_idx.at[s])

    def x_copy(b, s):
        return pltpu.make_async_copy(x_hbm.at[pl.ds(b * tm, tm), :], buf.at[s], sem_x.at[s])

    def wait_rows(s):
        for k in range(TOP_K):
            pltpu.make_async_copy(rbuf.at[s], xr_hbm.at[pl.ds(0, tm)], sem_rows.at[s]).wait()

    @pl.when(i == 0)
    def _():
        idx_copy(0, 0).start()
        x_copy(0, 0).start()
        zsrc = rbuf.at[SCATTER_BUFS - 1]
        zsrc[...] = jnp.zeros(zsrc.shape, F32)

        def zero_copies(e):
            b = zero_blk_ref[e]
            return b >= 0, [pltpu.make_async_copy(zsrc, xr_hbm.at[pl.ds(b * blk_rows + h * tm, tm)], sem_zero)
                            for h in range(blk_rows // tm)]

        def start(e, carry):
            live, copies = zero_copies(e)

            @pl.when(live)
            def _():
                for c in copies:
                    c.start()
            return carry

        def wait(e, carry):
            live, copies = zero_copies(e)

            @pl.when(live)
            def _():
                for c in copies:
                    c.wait()
            return carry

        lax.fori_loop(0, zero_blk_ref.shape[0], start, 0)
        lax.fori_loop(0, zero_blk_ref.shape[0], wait, 0)

    @pl.when(i >= 2)
    def _():
        wait_rows((i - 2) % SCATTER_BUFS)

    @pl.when(i + 1 < n)
    def _():
        idx_copy(i + 1, 1 - islot).start()
        x_copy(i + 1, (i + 1) % SCATTER_BUFS).start()

    idx_copy(i, islot).wait()
    x_copy(i, xslot).wait()

    rbuf[xslot, :, 0, :] = buf[xslot]
    for j in range(tm):
        src = rbuf.at[xslot, j]
        for k in range(TOP_K):
            dst = xr_hbm.at[idx[islot * rows + k * tm + j]]
            pltpu.make_async_copy(src, dst, sem_rows.at[xslot]).start()

    @pl.when(i == n - 1)
    def _():
        @pl.when(n > 1)
        def _():
            wait_rows((i - 1) % SCATTER_BUFS)
        wait_rows(xslot)


def _scatter_call(x1, pos_tiles, zero_blk, n_rows, tm, blk_rows):
    t = x1.shape[0]
    rows = TOP_K * tm
    assert blk_rows % tm == 0
    any_spec = pl.BlockSpec(memory_space=pl.ANY)
    return pl.pallas_call(
        functools.partial(_scatter_kernel, blk_rows=blk_rows),
        grid_spec=pltpu.PrefetchScalarGridSpec(
            num_scalar_prefetch=1,
            grid=(t // tm,),
            in_specs=[any_spec, any_spec],
            out_specs=any_spec,
            scratch_shapes=[pltpu.VMEM((SCATTER_BUFS, tm, D_MODEL), F32),
                            pltpu.VMEM((SCATTER_BUFS, tm, 1, D_MODEL), F32), pltpu.SMEM((2 * rows,), jnp.int32),
                            pltpu.SemaphoreType.DMA((SCATTER_BUFS,)), pltpu.SemaphoreType.DMA((SCATTER_BUFS,)),
                            pltpu.SemaphoreType.DMA((2,)), pltpu.SemaphoreType.DMA]),
        out_shape=jax.ShapeDtypeStruct((n_rows, 1, D_MODEL), F32),
        compiler_params=_cparams(("arbitrary",)),
        name="scatter",
    )(zero_blk, x1, pos_tiles)


def _expert_kernel(blk_exp_ref, nblk_ref, nvalid_ref, x_ref, gf_ref, wgu_ref, wd_ref, y_ref):
    del blk_exp_ref
    i = pl.program_id(0)

    @pl.when(i < nblk_ref[0])
    def _():
        x = x_ref[:, 0, :]
        real = lax.broadcasted_iota(jnp.int32, x.shape, 0) < nvalid_ref[i]
        x = jnp.where(real, x, 0.0)
        hn = (x * lax.rsqrt(jnp.mean(x * x, axis=-1, keepdims=True) + EPS) * gf_ref[...]).astype(BF16)
        gu = jnp.dot(hn, wgu_ref[0], preferred_element_type=F32)
        gt, up = gu[:, :D_EXPERT], gu[:, D_EXPERT:]
        act = (gt * (1.0 / (1.0 + jnp.exp(-gt))) * up).astype(BF16)
        y_ref[:, 0, :] = jnp.dot(act, wd_ref[0], preferred_element_type=F32)

    @pl.when(i >= nblk_ref[0])
    def _():
        y_ref[...] = jnp.zeros(y_ref.shape, F32)


def _expert_call(xr, rows, blk_exp, nblk, nvalid, gf, wgu, wd):
    nb = xr.shape[0] // rows
    return pl.pallas_call(
        _expert_kernel,
        grid_spec=pltpu.PrefetchScalarGridSpec(
            num_scalar_prefetch=3,
            grid=(nb,),
            in_specs=[pl.BlockSpec((rows, 1, D_MODEL), lambda i, be, n, nv: (jnp.minimum(i, n[0] - 1), 0, 0)),
                      pl.BlockSpec(gf.shape, lambda i, be, n, nv: (0, 0)),
                      pl.BlockSpec((1, D_MODEL, 2 * D_EXPERT), lambda i, be, n, nv: (be[i], 0, 0)),
                      pl.BlockSpec((1, D_EXPERT, D_MODEL), lambda i, be, n, nv: (be[i], 0, 0))],
            out_specs=pl.BlockSpec((rows, 1, D_MODEL), lambda i, be, n, nv: (i, 0, 0))),
        out_shape=jax.ShapeDtypeStruct((nb * rows, 1, D_MODEL), F32),
        compiler_params=_cparams(("arbitrary",)),
        name="experts",
    )(blk_exp, nblk, nvalid, xr, gf, wgu, wd)


def _combine_kernel(x1_ref, w0_ref, w1_ref, y_hbm, pos_hbm, o_ref, buf, idx, sem_rows, sem_idx):
    rows = buf.shape[1]
    tm = rows // 2
    slot = _gather_step(pl.program_id(0), pl.num_programs(0), y_hbm, pos_hbm, buf, idx, sem_rows, sem_idx)
    w0 = w0_ref[...]
    w1 = w1_ref[...]
    for c in range(D_MODEL // LANES):
        sl = slice(c * LANES, (c + 1) * LANES)
        o_ref[:, sl] = x1_ref[:, sl] + w0 * buf[slot, 0:tm, 0, sl] + w1 * buf[slot, tm:rows, 0, sl]


def _combine_call(x1, w0, w1, yr, pos_tiles, tm):
    t = x1.shape[0]
    rows = TOP_K * tm
    row = lambda w: pl.BlockSpec((tm, w), lambda i: (i, 0))
    return pl.pallas_call(
        _combine_kernel,
        grid=(t // tm,),
        in_specs=[row(D_MODEL), row(LANES), row(LANES),
                  pl.BlockSpec(memory_space=pl.ANY), pl.BlockSpec(memory_space=pl.ANY)],
        out_specs=row(D_MODEL),
        out_shape=jax.ShapeDtypeStruct((t, D_MODEL), F32),
        scratch_shapes=[pltpu.VMEM((2, rows, 1, D_MODEL), F32), pltpu.SMEM((2 * rows,), jnp.int32),
                        pltpu.SemaphoreType.DMA((2,)), pltpu.SemaphoreType.DMA((2,))],
        compiler_params=_cparams(("arbitrary",)),
        name="combine",
    )(x1, w0, w1, yr, pos_tiles)


def _rot_cols(w):
    half = ROPE_DIM // 2
    return jnp.concatenate([-w[..., half:], w[..., :half]], axis=-1)


def _head_pad(nope, rope):
    z = jnp.zeros(rope.shape[:-1] + (LANES - B_QK_DIM,), rope.dtype)
    out = jnp.concatenate([nope, rope, z], axis=-1)
    return out.reshape(out.shape[:-2] + (out.shape[-2] * LANES,))


def _t5_bucket(dist):
    max_exact = REL_BUCKETS // 2
    d = jnp.maximum(dist, 0)
    ratio = jnp.maximum(d, 1).astype(F32) / max_exact
    large = max_exact + (jnp.log(ratio) / math.log(REL_MAX_DIST / max_exact)
                         * (REL_BUCKETS - max_exact)).astype(jnp.int32)
    large = jnp.minimum(large, REL_BUCKETS - 1)
    return jnp.where(d < max_exact, d, large)


def _swa_bias(rel_bias, tq):
    qi = jnp.arange(tq)[:, None]
    kj = jnp.arange(2 * tq)[None, :]
    dist = qi + tq - kj
    bias = rel_bias[_t5_bucket(dist)].astype(F32).transpose(2, 0, 1) * LOG2E
    mask = (dist >= 0) & (dist < WINDOW)
    bias = jnp.where(mask[None], bias, NEG)
    return bias.reshape(A_KV_HEADS, (A_HEADS // A_KV_HEADS) * tq, 2 * tq)


def _rope_tables(seq):
    half = ROPE_DIM // 2
    inv_freq = ROPE_THETA ** (-jnp.arange(half, dtype=F32) / half)
    ang = jnp.arange(seq, dtype=F32)[:, None] * inv_freq[None, :]
    cos = jnp.concatenate([jnp.cos(ang), jnp.cos(ang)], axis=-1)
    sin = jnp.concatenate([jnp.sin(ang), jnp.sin(ang)], axis=-1)
    return cos, sin


def _lane_table(nope_part, rope_part):
    seq = rope_part.shape[0]
    nope_part = jnp.broadcast_to(nope_part, (seq, NOPE_DIM))
    return jnp.concatenate([nope_part, rope_part, jnp.zeros((seq, LANES - B_QK_DIM), F32)], axis=-1)


def _prep_layer(l, seq, cos, sin, attn_norm, w_in, b_gate, a_q_norm, a_k_norm, w_oa, q_a_norm, w_qb, kv_a_norm,
                w_kvb, b_q_norm, b_k_norm, w_ob, w_o, ffn_norm, w_rg, b_rg, w_re, b_re, w_eg, w_eu, w_ed):
    w = w_in[l]
    o = 0
    parts = []
    for sz in (A_WIDTH, A_KV_HEADS * A_HEAD_DIM, A_KV_HEADS * A_HEAD_DIM, Q_LORA, KV_LORA, ROPE_DIM, 2 * D_MODEL):
        parts.append(w[:, o:o + sz])
        o += sz
    wqa, wka, wva, wcq, wckv, wkpe, wg = parts
    perm = jnp.array([c + (A_HEADS // 2) * h for c in range(A_HEADS // 2) for h in range(2)])
    wqa = wqa.reshape(D_MODEL, A_HEADS, A_HEAD_DIM)[:, perm].reshape(D_MODEL, A_WIDTH)
    zl = jnp.zeros((D_MODEL, NOPE_DIM), F32)
    zr = jnp.zeros((D_MODEL, LANES - B_QK_DIM), F32)
    w1 = jnp.concatenate([wqa, wka, wva, wcq, wckv, zl, wkpe, zr, zl, _rot_cols(wkpe), zr, wg], axis=1).astype(BF16)

    wq = w_qb[l].reshape(Q_LORA, B_HEADS, B_QK_DIM)
    zq = jnp.zeros((Q_LORA, B_HEADS, NOPE_DIM), F32)
    wq2 = jnp.concatenate([_head_pad(wq[..., :NOPE_DIM], wq[..., NOPE_DIM:]),
                           _head_pad(zq, _rot_cols(wq[..., NOPE_DIM:]))], axis=1).astype(BF16)
    wkv = w_kvb[l].reshape(KV_LORA, B_HEADS, NOPE_DIM + B_V_DIM)
    wk2 = _head_pad(wkv[..., :NOPE_DIM], jnp.zeros((KV_LORA, B_HEADS, ROPE_DIM), F32)).astype(BF16)
    wv = wkv[..., NOPE_DIM:]
    zv = jnp.zeros_like(wv)
    wv2 = jnp.where((jnp.arange(B_HEADS) % 2 == 0)[None, :, None], jnp.concatenate([wv, zv], -1),
                    jnp.concatenate([zv, wv], -1)).reshape(KV_LORA, B_HEADS * LANES).astype(BF16)
    one_lane = jnp.where(jnp.arange(B_HEADS) % 2 == 0, V_ONE_EVEN, V_ONE_ODD)
    vone = (jnp.arange(LANES)[None, :] == one_lane[:, None]).astype(F32).reshape(1, B_HEADS * LANES)

    def rope_tabs(gain, scale):
        gn, gr = gain[:NOPE_DIM], gain[NOPE_DIM:]
        gr_rot = jnp.concatenate([gr[ROPE_DIM // 2:], gr[:ROPE_DIM // 2]])
        a = _lane_table(gn[None, :] * scale, gr[None, :] * cos * scale)
        b = _lane_table(jnp.zeros((1, NOPE_DIM), F32), gr_rot[None, :] * sin * scale)
        return a, b

    aq, bq = rope_tabs(b_q_norm[l], B_QK_DIM ** -0.5 * LOG2E)
    score_bound = math.sqrt(B_QK_DIM) * LOG2E * jnp.max(jnp.abs(b_q_norm[l])) * jnp.max(jnp.abs(b_k_norm[l]))
    ak, bk = rope_tabs(b_k_norm[l], 1.0)

    gqa = (jnp.tile(a_q_norm[l], 2) * (A_HEAD_DIM ** -0.5 * LOG2E))[None, :]
    gka = jnp.tile(a_k_norm[l], 2)[None, :]
    woa = w_oa[l].reshape(A_HEADS, A_HEAD_DIM, D_MODEL)[perm].reshape(A_WIDTH, D_MODEL).astype(BF16)

    wr = jnp.concatenate([w_rg[l], w_re[l], jnp.zeros((D_MODEL, LANES - N_GROUPS - N_EXPERTS), F32)], axis=1)
    wrh = wr.astype(BF16)
    wrl = (wr - wrh.astype(F32)).astype(BF16)
    br = jnp.concatenate([b_rg[l], b_re[l], jnp.zeros((LANES - N_GROUPS - N_EXPERTS,), F32)])[None, :]

    return dict(
        gn=attn_norm[l][None, :], w1=w1, wq2=wq2, wk2=wk2, wv2=wv2, vone=vone, gcq=q_a_norm[l][None, :],
        gckv=kv_a_norm[l][None, :], score_bound=score_bound, bg=b_gate[l][None, :], gqa=gqa, gka=gka,
        aq=aq, bq=bq, ak=ak, bk=bk,
        woa=woa, wob=w_ob[l].astype(BF16), wo=w_o[l].astype(BF16), gf=ffn_norm[l][None, :], wr=jnp.concatenate([wrh, wrl], axis=1), br=br,
        wgu=jnp.concatenate([w_eg[l], w_eu[l]], axis=-1).astype(BF16), wd=w_ed[l].astype(BF16))


def _tril(n):
    return (jnp.arange(n)[:, None] >= jnp.arange(n)[None, :]).astype(jnp.int32)


def _dispatch(route, tile_cnt, rows):
    t = route.shape[0]
    nt = tile_cnt.shape[0]
    tile_cnt = tile_cnt[:, 0, :N_EXPERTS]
    counts = jnp.sum(tile_cnt, axis=0)
    padded = ((counts + rows - 1) // rows) * rows
    pends = jnp.dot(_tril(N_EXPERTS), padded)
    pstarts = pends - padded
    tile_base = pstarts[None, :] + jnp.dot(_tril(nt), tile_cnt) - tile_cnt
    base = jnp.broadcast_to(tile_base[:, None, None, :], (nt, t // nt, 1, N_EXPERTS)).reshape(t, 1, N_EXPERTS)
    onehot = route[:, :TOP_K, None] == jnp.arange(N_EXPERTS, dtype=jnp.int32)[None, None, :]
    dest = jnp.sum(jnp.where(onehot, base, 0), axis=-1) + route[:, TOP_K:2 * TOP_K]
    n_rows = t * TOP_K + N_EXPERTS * rows
    n_blocks = n_rows // rows
    blk_start = jnp.arange(n_blocks, dtype=jnp.int32)[:, None] * rows
    blk_exp = jnp.minimum(jnp.sum((pends[None, :] <= blk_start).astype(jnp.int32), axis=1), N_EXPERTS - 1)
    nblk = (pends[-1] // rows).astype(jnp.int32).reshape(1)
    nvalid = jnp.clip((pstarts + counts)[blk_exp] - blk_start[:, 0], 0, rows).astype(jnp.int32)
    last_blk = jnp.where(padded > 0, pends // rows - 1, -1)
    trailing = nblk[0] + jnp.arange(N_EXPERTS, dtype=jnp.int32)
    zero_blk = jnp.concatenate([last_blk, jnp.where(trailing < n_blocks, trailing, -1)]).astype(jnp.int32)
    return dest, blk_exp, nblk, nvalid, zero_blk, n_rows


def _swa_tables(sinks, rel_bias, a_q_norm, a_k_norm):
    per_group = A_HEADS // A_KV_HEADS
    tab = jnp.broadcast_to((sinks * LOG2E).reshape(A_KV_HEADS, per_group, 1, 1),
                           (A_KV_HEADS, per_group, SWA_TQ, LANES)).reshape(A_KV_HEADS, per_group * SWA_TQ, LANES)
    qk = math.sqrt(A_HEAD_DIM) * jnp.max(jnp.abs(a_q_norm)) * jnp.max(jnp.abs(a_k_norm))
    bound = LOG2E * jnp.maximum(qk + jnp.max(jnp.abs(rel_bias)), jnp.max(jnp.abs(sinks)))
    return tab, bound


def _layer(x2d, bsz, seq, p, sink_tab, swa_bound, bias):
    t = x2d.shape[0]
    qa, ka, va, qb, kb, vb, gates = _proj_call(x2d, seq, p)
    oa = _swa_call(qa, ka, va, sink_tab, bias, bsz, seq, swa_bound)
    ob = _mla_call(qb, kb, vb, bsz, seq, p["score_bound"])
    x1, w0, w1, route, tile_cnt = _merge_call(x2d, oa, ob, gates, p)
    dest, blk_exp, nblk, nvalid, zero_blk, n_rows = _dispatch(route, tile_cnt, MOE_ROWS)
    pos_tiles = dest.reshape(t // COMB_TM, COMB_TM, TOP_K).transpose(0, 2, 1).reshape(t * TOP_K)
    xr = _scatter_call(x1, pos_tiles, zero_blk, n_rows, COMB_TM, MOE_ROWS)
    yr = _expert_call(xr, MOE_ROWS, blk_exp, nblk, nvalid, p["gf"], p["wgu"], p["wd"])
    return _combine_call(x1, w0, w1, yr, pos_tiles, COMB_TM)


def kernel(x, attn_norm, w_in, b_gate, a_q_norm, a_k_norm, a_sinks, rel_bias, w_oa, q_a_norm, w_qb, kv_a_norm, w_kvb, b_q_norm, b_k_norm, w_ob, w_o, ffn_norm, w_router_group, b_router_group, w_router_expert, b_router_expert, w_exp_gate, w_exp_up, w_exp_down):
    bsz, seq, d = x.shape
    assert d == D_MODEL and seq % MLA_TQ == 0 and seq % PROJ_TM == 0 and (bsz * seq) % COMB_TM == 0
    depth = w_in.shape[0]
    cos, sin = _rope_tables(seq)
    bias = _swa_bias(rel_bias, SWA_TQ)
    x2d = x.reshape(bsz * seq, d)
    for l in range(depth):
        p = _prep_layer(l, seq, cos, sin, attn_norm, w_in, b_gate, a_q_norm, a_k_norm, w_oa, q_a_norm, w_qb,
                        kv_a_norm, w_kvb, b_q_norm, b_k_norm, w_ob, w_o, ffn_norm, w_router_group,
                        b_router_group, w_router_expert, b_router_expert, w_exp_gate, w_exp_up, w_exp_down)
        sink_tab, swa_bound = _swa_tables(a_sinks[l], rel_bias, a_q_norm[l], a_k_norm[l])
        x2d = _layer(x2d, bsz, seq, p, sink_tab, swa_bound, bias)
    return x2d.reshape(bsz, seq, d)
```

```python
import functools
import math

import jax
import jax.numpy as jnp
from jax import lax
from jax.experimental import pallas as pl
from jax.experimental.pallas import tpu as pltpu

F32 = jnp.float32
BF16 = jnp.bfloat16

D_MODEL = 1024
A_HEADS = 8
A_KV_HEADS = 2
A_HEAD_DIM = 64
A_WIDTH = A_HEADS * A_HEAD_DIM
WINDOW = 128
REL_BUCKETS = 32
REL_MAX_DIST = 128
B_HEADS = 8
Q_LORA = 256
KV_LORA = 128
NOPE_DIM = 64
ROPE_DIM = 32
B_QK_DIM = NOPE_DIM + ROPE_DIM
B_V_DIM = 64
B_WIDTH = B_HEADS * B_V_DIM
ROPE_THETA = 10000.0
N_GROUPS = 4
EXPERTS_PER_GROUP = 8
N_EXPERTS = N_GROUPS * EXPERTS_PER_GROUP
TOP_K = 2
D_EXPERT = 256
EPS = 1e-6

LANES = 128
NEG = -1e30
LOG2E = math.log2(math.e)
V_ONE_EVEN = B_V_DIM
V_ONE_ODD = 0
EXP2_SAFE_BOUND = 100.0
VMEM_LIMIT = 56 * 1024 * 1024

C_QA = 0
C_KA = C_QA + A_WIDTH
C_VA = C_KA + LANES
C_CQ = C_VA + LANES
C_CKV = C_CQ + Q_LORA
C_KPE = C_CKV + KV_LORA
C_KPR = C_KPE + LANES
C_GATE = C_KPR + LANES
C_END = C_GATE + 2 * D_MODEL

PROJ_TM = 512
SWA_TQ = 128
SWA_SUB = 4
MLA_TQ = 1024
MLA_TK = 512
MERGE_TM = 512
MOE_ROWS = 512
COMB_TM = 256


def _cparams(sem):
    return pltpu.CompilerParams(dimension_semantics=sem, vmem_limit_bytes=VMEM_LIMIT)


def _proj_kernel(x_ref, gn_ref, w1_ref, wq2_ref, wk2_ref, wv2_ref, vone_ref, gcq_ref, gckv_ref, bg_ref,
                 gqa_ref, gka_ref, aq_ref, bq_ref, ak_ref, bk_ref,
                 qa_ref, ka_ref, va_ref, qb_ref, kb_ref, vb_ref, gate_ref):
    tm = x_ref.shape[0]
    x = x_ref[...]
    h = (x * lax.rsqrt(jnp.mean(x * x, axis=-1, keepdims=True) + EPS) * gn_ref[...]).astype(BF16)

    def proj(a, b):
        return jnp.dot(h, w1_ref[:, a:b], preferred_element_type=F32)

    lo = lax.broadcasted_iota(jnp.int32, (tm, LANES), 1) < A_HEAD_DIM

    def pair_norm(t, gain):
        t2 = t * t
        s_lo = jnp.sum(jnp.where(lo, t2, 0.0), axis=-1, keepdims=True)
        s_hi = jnp.sum(jnp.where(lo, 0.0, t2), axis=-1, keepdims=True)
        r = jnp.where(lo, lax.rsqrt(s_lo / A_HEAD_DIM + EPS), lax.rsqrt(s_hi / A_HEAD_DIM + EPS))
        return (t * r * gain).astype(BF16)

    qa = proj(C_QA, C_KA)
    for c in range(A_WIDTH // LANES):
        sl = slice(c * LANES, (c + 1) * LANES)
        qa_ref[:, sl] = pair_norm(qa[:, sl], gqa_ref[...])
    kvp = proj(C_KA, C_CQ)
    ka_ref[...] = pair_norm(kvp[:, :LANES], gka_ref[...])
    va_ref[...] = kvp[:, LANES:].astype(BF16)

    lat = proj(C_CQ, C_GATE)
    cq = lat[:, :Q_LORA]
    cqn = (cq * lax.rsqrt(jnp.mean(cq * cq, axis=-1, keepdims=True) + EPS) * gcq_ref[...]).astype(BF16)
    ckv = lat[:, Q_LORA:Q_LORA + KV_LORA]
    ckvn = (ckv * lax.rsqrt(jnp.mean(ckv * ckv, axis=-1, keepdims=True) + EPS) * gckv_ref[...]).astype(BF16)
    kpe = lat[:, C_KPE - C_CQ:C_KPR - C_CQ]
    kpr = lat[:, C_KPR - C_CQ:C_GATE - C_CQ]

    aq, bq, ak, bk = aq_ref[...], bq_ref[...], ak_ref[...], bk_ref[...]
    q2 = jnp.dot(cqn, wq2_ref[...], preferred_element_type=F32)
    k2 = jnp.dot(ckvn, wk2_ref[...], preferred_element_type=F32)
    vb_ref[...] = (jnp.dot(ckvn, wv2_ref[...], preferred_element_type=F32) + vone_ref[...]).astype(BF16)
    kr = kpr * bk
    for hd in range(B_HEADS):
        sl = slice(hd * LANES, (hd + 1) * LANES)
        qp = q2[:, sl]
        qr = q2[:, B_HEADS * LANES + hd * LANES:B_HEADS * LANES + (hd + 1) * LANES]
        rq = lax.rsqrt(jnp.sum(qp * qp, axis=-1, keepdims=True) / B_QK_DIM + EPS)
        qb_ref[:, sl] = (rq * (qp * aq + qr * bq)).astype(BF16)
        kp = k2[:, sl] + kpe
        rk = lax.rsqrt(jnp.sum(kp * kp, axis=-1, keepdims=True) / B_QK_DIM + EPS)
        kb_ref[:, sl] = (rk * (kp * ak + kr)).astype(BF16)

    z = proj(C_GATE, C_END) + bg_ref[...]
    gate_ref[...] = (1.0 / (1.0 + jnp.exp(-z))).astype(BF16)


def _proj_call(x2d, seq, p):
    t = x2d.shape[0]
    tm = PROJ_TM
    nt = t // tm
    npos = seq // tm
    row = lambda w: pl.BlockSpec((tm, w), lambda i: (i, 0))
    full = lambda a: pl.BlockSpec(a.shape, lambda i: (0,) * a.ndim)
    pos = pl.BlockSpec((tm, LANES), lambda i: (i % npos, 0))
    consts = [p["gn"], p["w1"], p["wq2"], p["wk2"], p["wv2"], p["vone"], p["gcq"], p["gckv"], p["bg"], p["gqa"],
              p["gka"]]
    tabs = [p["aq"], p["bq"], p["ak"], p["bk"]]
    widths = [A_WIDTH, LANES, LANES, B_HEADS * LANES, B_HEADS * LANES, B_HEADS * LANES, 2 * D_MODEL]
    return pl.pallas_call(
        _proj_kernel,
        grid=(nt,),
        in_specs=[row(D_MODEL)] + [full(a) for a in consts] + [pos] * 4,
        out_specs=[row(w) for w in widths],
        out_shape=[jax.ShapeDtypeStruct((t, w), BF16) for w in widths],
        compiler_params=_cparams(("parallel",)),
        name="proj",
    )(x2d, *consts, *tabs)


def _swa_kernel(q_ref, kc_ref, kp_ref, vc_ref, vp_ref, bias_ref, sink_ref, o_ref, *, shift):
    tq = SWA_TQ
    groups = A_WIDTH // LANES
    first = pl.program_id(1) == 0
    one_col = (lax.broadcasted_iota(jnp.int32, (2 * tq, LANES), 1) == 0).astype(BF16)
    lo = lax.broadcasted_iota(jnp.int32, (tq, LANES), 1) < A_HEAD_DIM
    no_prev = jnp.logical_and(first, lax.broadcasted_iota(jnp.int32, (groups * tq, 2 * tq), 1) < tq)
    for sub in range(SWA_SUB):
        if sub == 0:
            k = jnp.concatenate([kp_ref[...], kc_ref[0:tq, :]], axis=0)
            v = jnp.concatenate([vp_ref[...], vc_ref[0:tq, :]], axis=0)
        else:
            k = kc_ref[(sub - 1) * tq:(sub + 1) * tq, :]
            v = vc_ref[(sub - 1) * tq:(sub + 1) * tq, :]
        v_ext = jnp.concatenate([v, one_col], axis=1)
        qrows = slice(sub * tq, (sub + 1) * tq)
        qs = [q_ref[qrows, c * LANES:(c + 1) * LANES] for c in range(groups)]
        res = []
        for g in range(A_KV_HEADS):
            keep = lo if g == 0 else jnp.logical_not(lo)
            qg = jnp.concatenate([jnp.where(keep, qc, jnp.zeros_like(qc)) for qc in qs], axis=0)
            s = lax.dot_general(qg, k, (((1,), (1,)), ((), ())), preferred_element_type=F32) + bias_ref[g]
            sink = sink_ref[g][:, :1]
            if shift:
                if sub == 0:
                    s = jnp.where(no_prev, NEG, s)
                m = jnp.maximum(jnp.max(s, axis=-1, keepdims=True), sink)
                e = jnp.exp2(s - m)
                sink_term = jnp.exp2(sink - m)
            else:
                e = jnp.exp2(s)
                if sub == 0:
                    e = jnp.where(no_prev, 0.0, e)
                sink_term = jnp.exp2(sink)
            pv = jnp.dot(e.astype(BF16), v_ext, preferred_element_type=F32)
            res.append(pv[:, :LANES] * (1.0 / (pv[:, LANES:LANES + 1] + sink_term)))
        for c in range(groups):
            rows = slice(c * tq, (c + 1) * tq)
            o_ref[qrows, c * LANES:(c + 1) * LANES] = jnp.where(lo, res[0][rows], res[1][rows]).astype(BF16)


def _swa_call(qa, ka, va, sink_tab, bias, bsz, seq, score_bound):
    t = qa.shape[0]
    tq = SWA_TQ
    step = SWA_SUB * tq
    ns = seq // step
    cur = lambda b, i: (b * ns + i, 0)
    prev = lambda b, i: (b * (seq // tq) + jnp.maximum(SWA_SUB * i - 1, 0), 0)
    full = lambda a: pl.BlockSpec(a.shape, lambda b, i: (0,) * a.ndim)

    def call(shift):
        return pl.pallas_call(
            functools.partial(_swa_kernel, shift=shift),
            grid=(bsz, ns),
            in_specs=[pl.BlockSpec((step, A_WIDTH), cur),
                      pl.BlockSpec((step, LANES), cur), pl.BlockSpec((tq, LANES), prev),
                      pl.BlockSpec((step, LANES), cur), pl.BlockSpec((tq, LANES), prev),
                      full(bias), full(sink_tab)],
            out_specs=pl.BlockSpec((step, A_WIDTH), cur),
            out_shape=jax.ShapeDtypeStruct((t, A_WIDTH), BF16),
            compiler_params=_cparams(("parallel", "parallel")),
            name="swa_shifted" if shift else "swa",
        )(qa, ka, ka, va, va, bias, sink_tab)

    return lax.cond(score_bound <= EXP2_SAFE_BOUND, lambda: call(False), lambda: call(True))


def _mla_kernel(q_ref, k_ref, v_ref, o_ref, acc_sc, m_sc, *, shift):
    tq = q_ref.shape[0]
    tk = MLA_TK
    per_tile = tq // tk
    qi = pl.program_id(2)
    acc_sc[...] = jnp.zeros(acc_sc.shape, F32)
    if shift:
        m_sc[...] = jnp.full(m_sc.shape, NEG, F32)

    def tile(j, r0, r1, masked):
        keys = pl.ds(pl.multiple_of(j * tk, tk), tk)
        if masked:
            causal = (lax.broadcasted_iota(jnp.int32, (r1 - r0, tk), 1)
                      <= lax.broadcasted_iota(jnp.int32, (r1 - r0, tk), 0))
        for half in range(2):
            hs = slice(half * LANES, (half + 1) * LANES)
            s = lax.dot_general(q_ref[r0:r1, hs], k_ref[keys, hs], (((1,), (1,)), ((), ())),
                                preferred_element_type=F32)
            if shift:
                if masked:
                    s = jnp.where(causal, s, NEG)
                m_prev = m_sc[half, r0:r1]
                m_new = jnp.maximum(m_prev, jnp.max(s, axis=-1, keepdims=True))
                m_sc[half, r0:r1] = m_new
                e = jnp.exp2(s - m_new)
                acc_sc[half, r0:r1] = jnp.exp2(m_prev - m_new) * acc_sc[half, r0:r1]
            else:
                e = jnp.exp2(s)
                if masked:
                    e = jnp.where(causal, e, 0.0)
            acc_sc[half, r0:r1] += jnp.dot(e.astype(BF16), v_ref[keys, hs], preferred_element_type=F32)

    def body(j, carry):
        tile(j, 0, tq, False)
        return carry

    lax.fori_loop(0, qi * per_tile, body, 0)
    for d in range(per_tile):
        tile(qi * per_tile + d, d * tk, (d + 1) * tk, True)
        if (d + 1) * tk < tq:
            tile(qi * per_tile + d, (d + 1) * tk, tq, False)
    a0 = acc_sc[0]
    a1 = acc_sc[1]
    lo = lax.broadcasted_iota(jnp.int32, (tq, LANES), 1) < B_V_DIM
    o_ref[...] = jnp.where(lo, a0 * (1.0 / a0[:, V_ONE_EVEN:V_ONE_EVEN + 1]),
                           a1 * (1.0 / a1[:, V_ONE_ODD:V_ONE_ODD + 1])).astype(BF16)


def _mla_call(qb, kb, vb, bsz, seq, score_bound):
    t = qb.shape[0]
    tq = MLA_TQ
    nq = seq // tq
    pairs = B_HEADS // 2

    def call(shift):
        return pl.pallas_call(
            functools.partial(_mla_kernel, shift=shift),
            grid=(bsz, pairs, nq),
            in_specs=[pl.BlockSpec((tq, 2 * LANES), lambda b, p, i: (b * nq + i, p)),
                      pl.BlockSpec((seq, 2 * LANES), lambda b, p, i: (b, p)),
                      pl.BlockSpec((seq, 2 * LANES), lambda b, p, i: (b, p))],
            out_specs=pl.BlockSpec((tq, LANES), lambda b, p, i: (b * nq + i, p)),
            out_shape=jax.ShapeDtypeStruct((t, B_WIDTH), BF16),
            scratch_shapes=[pltpu.VMEM((2, tq, LANES), F32), pltpu.VMEM((2, tq, 1), F32)],
            compiler_params=_cparams(("parallel", "parallel", "arbitrary")),
            name="mla_shifted" if shift else "mla",
        )(qb, kb, vb)

    return lax.cond(score_bound <= EXP2_SAFE_BOUND, lambda: call(False), lambda: call(True))


def _merge_kernel(x_ref, oa_ref, ob_ref, gate_ref, woa_ref, wob_ref, wo_ref, gf_ref, wr_ref, br_ref,
                  tri_ref, x1_ref, w0_ref, w1_ref, eid_ref, cnt_ref):
    tm = x_ref.shape[0]
    ya = jnp.dot(oa_ref[...], woa_ref[...], preferred_element_type=F32)
    yb = jnp.dot(ob_ref[...], wob_ref[...], preferred_element_type=F32)
    g = gate_ref[...].astype(F32)
    mix = (g[:, :D_MODEL] * ya + g[:, D_MODEL:] * yb).astype(BF16)
    x1 = x_ref[...] + jnp.dot(mix, wo_ref[...], preferred_element_type=F32)
    x1_ref[...] = x1

    hn = x1 * lax.rsqrt(jnp.mean(x1 * x1, axis=-1, keepdims=True) + EPS) * gf_ref[...]
    hh = hn.astype(BF16)
    hl = (hn - hh.astype(F32)).astype(BF16)
    ph = jnp.dot(hh, wr_ref[...], preferred_element_type=F32)
    pl_ = jnp.dot(hl, wr_ref[...], preferred_element_type=F32)
    logits = (ph[:, :LANES] + ph[:, LANES:]) + (pl_[:, :LANES] + pl_[:, LANES:]) + br_ref[...]

    lane = lax.broadcasted_iota(jnp.int32, (tm, LANES), 1).astype(F32)
    big = float(LANES)
    gmask = lane < N_GROUPS
    gl = jnp.where(gmask, logits, NEG)
    gmax = jnp.max(gl, axis=-1, keepdims=True)
    gsum = jnp.sum(jnp.where(gmask, jnp.exp(gl - gmax), 0.0), axis=-1, keepdims=True)
    g_p = 1.0 / gsum
    g_idx = jnp.min(jnp.where(gl == gmax, lane, big), axis=-1, keepdims=True)
    e_lo = N_GROUPS + EXPERTS_PER_GROUP * g_idx
    emask = jnp.logical_and(lane >= e_lo, lane < e_lo + EXPERTS_PER_GROUP)
    el = jnp.where(emask, logits, NEG)
    t1 = jnp.max(el, axis=-1, keepdims=True)
    i1 = jnp.min(jnp.where(el == t1, lane, big), axis=-1, keepdims=True)
    el2 = jnp.where(lane == i1, NEG, el)
    t2 = jnp.max(el2, axis=-1, keepdims=True)
    i2 = jnp.min(jnp.where(el2 == t2, lane, big), axis=-1, keepdims=True)
    e2 = jnp.exp(t2 - t1)
    w_first = g_p / (1.0 + e2)
    w0_ref[...] = jnp.broadcast_to(w_first, (tm, LANES))
    w1_ref[...] = jnp.broadcast_to(w_first * e2, (tm, LANES))
    ex1 = i1 - N_GROUPS
    ex2 = i2 - N_GROUPS

    oh1 = lane == ex1
    oh2 = lane == ex2
    cnt = (oh1.astype(F32) + oh2.astype(F32)).astype(BF16)
    before = jnp.dot(tri_ref[...], cnt, preferred_element_type=F32)
    r1 = jnp.sum(jnp.where(oh1, before, 0.0), axis=-1, keepdims=True)
    r2 = jnp.sum(jnp.where(oh2, before, 0.0), axis=-1, keepdims=True)
    ids = jnp.where(lane == 0.0, ex1, jnp.where(lane == 1.0, ex2, jnp.where(lane == 2.0, r1,
                                                                             jnp.where(lane == 3.0, r2, 0.0))))
    eid_ref[...] = ids.astype(jnp.int32)
    tile_cnt = jnp.sum(cnt.astype(F32), axis=0, keepdims=True)
    cnt_ref[...] = jnp.broadcast_to(tile_cnt, cnt_ref.shape[1:]).astype(jnp.int32)[None]


def _merge_call(x2d, oa, ob, gates, p):
    t = x2d.shape[0]
    tm = MERGE_TM
    nt = t // tm
    row = lambda w: pl.BlockSpec((tm, w), lambda i: (i, 0))
    full = lambda a: pl.BlockSpec(a.shape, lambda i: (0,) * a.ndim)
    tri = (jnp.arange(tm)[:, None] > jnp.arange(tm)[None, :]).astype(BF16)
    consts = [p["woa"], p["wob"], p["wo"], p["gf"], p["wr"], p["br"], tri]
    return pl.pallas_call(
        _merge_kernel,
        grid=(nt,),
        in_specs=[row(D_MODEL), row(A_WIDTH), row(B_WIDTH), row(2 * D_MODEL)] + [full(a) for a in consts],
        out_specs=[row(D_MODEL), row(LANES), row(LANES), row(LANES),
                   pl.BlockSpec((1, 8, LANES), lambda i: (i, 0, 0))],
        out_shape=[jax.ShapeDtypeStruct((t, D_MODEL), F32), jax.ShapeDtypeStruct((t, LANES), F32),
                   jax.ShapeDtypeStruct((t, LANES), F32), jax.ShapeDtypeStruct((t, LANES), jnp.int32),
                   jax.ShapeDtypeStruct((nt, 8, LANES), jnp.int32)],
        compiler_params=_cparams(("parallel",)),
        name="merge",
    )(x2d, oa, ob, gates, *consts)


def _rows_copy(src_hbm, row, dst, j, sem):
    return pltpu.make_async_copy(src_hbm.at[row], dst.at[j], sem)


def _start_rows(src_hbm, idx, base, dst, sem, n):
    for j in range(n):
        _rows_copy(src_hbm, idx[base + j], dst, j, sem).start(priority=j % 2)


WAIT_ROWS = 128


def _wait_rows(src_hbm, dst, sem, n):
    for c in range(n // WAIT_ROWS):
        rows = pl.ds(c * WAIT_ROWS, WAIT_ROWS)
        pltpu.make_async_copy(src_hbm.at[rows], dst.at[rows], sem).wait()


def _gather_step(i, n, src_hbm, idx_hbm, buf, idx, sem_rows, sem_idx):
    rows = buf.shape[1]
    slot = i % 2
    nslot = 1 - slot

    def idx_copy(b, s):
        return pltpu.make_async_copy(idx_hbm.at[pl.ds(b * rows, rows)], idx.at[pl.ds(s * rows, rows)], sem_idx.at[s])

    @pl.when(jnp.logical_and(i == 0, n > 0))
    def _():
        idx_copy(0, 0).start()
        idx_copy(0, 0).wait()
        _start_rows(src_hbm, idx, 0, buf.at[0], sem_rows.at[0], rows)

        @pl.when(n > 1)
        def _():
            idx_copy(1, 1).start()

    @pl.when(i + 1 < n)
    def _():
        idx_copy(i + 1, nslot).wait()
        _start_rows(src_hbm, idx, nslot * rows, buf.at[nslot], sem_rows.at[nslot], rows)

    @pl.when(i + 2 < n)
    def _():
        idx_copy(i + 2, slot).start()

    @pl.when(i < n)
    def _():
        _wait_rows(src_hbm, buf.at[slot], sem_rows.at[slot], rows)
    return slot


SCATTER_BUFS = 3


def _scatter_kernel(zero_blk_ref, x_hbm, dest_hbm, xr_hbm, buf, rbuf, idx, sem_x, sem_rows, sem_idx, sem_zero,
                    *, blk_rows):
    i = pl.program_id(0)
    n = pl.num_programs(0)
    tm = buf.shape[1]
    rows = TOP_K * tm
    islot = i % 2
    xslot = i % SCATTER_BUFS

    def idx_copy(b, s):
        return pltpu.make_async_copy(dest_hbm.at[pl.ds(b * rows, rows)], idx.at[pl.ds(s * rows, rows)], sem_idx.at[s])

    def x_copy(b, s):
        return pltpu.make_async_copy(x_hbm.at[pl.ds(b * tm, tm), :], buf.at[s], sem_x.at[s])

    def wait_rows(s):
        for k in range(TOP_K):
            pltpu.make_async_copy(rbuf.at[s], xr_hbm.at[pl.ds(0, tm)], sem_rows.at[s]).wait()

    @pl.when(i == 0)
    def _():
        idx_copy(0, 0).start()
        x_copy(0, 0).start()
        zsrc = rbuf.at[SCATTER_BUFS - 1]
        zsrc[...] = jnp.zeros(zsrc.shape, F32)

        def zero_copies(e):
            b = zero_blk_ref[e]
            return b >= 0, [pltpu.make_async_copy(zsrc, xr_hbm.at[pl.ds(b * blk_rows + h * tm, tm)], sem_zero)
                            for h in range(blk_rows // tm)]

        def start(e, carry):
            live, copies = zero_copies(e)

            @pl.when(live)
            def _():
                for c in copies:
                    c.start()
            return carry

        def wait(e, carry):
            live, copies = zero_copies(e)

            @pl.when(live)
            def _():
                for c in copies:
                    c.wait()
            return carry

        lax.fori_loop(0, zero_blk_ref.shape[0], start, 0)
        lax.fori_loop(0, zero_blk_ref.shape[0], wait, 0)

    @pl.when(i >= 2)
    def _():
        wait_rows((i - 2) % SCATTER_BUFS)

    @pl.when(i + 1 < n)
    def _():
        idx_copy(i + 1, 1 - islot).start()
        x_copy(i + 1, (i + 1) % SCATTER_BUFS).start()

    idx_copy(i, islot).wait()
    x_copy(i, xslot).wait()

    rbuf[xslot, :, 0, :] = buf[xslot]
    for j in range(tm):
        src = rbuf.at[xslot, j]
        for k in range(TOP_K):
            dst = xr_hbm.at[idx[islot * rows + k * tm + j]]
            pltpu.make_async_copy(src, dst, sem_rows.at[xslot]).start(priority=k % 2)

    @pl.when(i == n - 1)
    def _():
        @pl.when(n > 1)
        def _():
            wait_rows((i - 1) % SCATTER_BUFS)
        wait_rows(xslot)


def _scatter_call(x1, pos_tiles, zero_blk, n_rows, tm, blk_rows):
    t = x1.shape[0]
    rows = TOP_K * tm
    assert blk_rows % tm == 0
    any_spec = pl.BlockSpec(memory_space=pl.ANY)
    return pl.pallas_call(
        functools.partial(_scatter_kernel, blk_rows=blk_rows),
        grid_spec=pltpu.PrefetchScalarGridSpec(
            num_scalar_prefetch=1,
            grid=(t // tm,),
            in_specs=[any_spec, any_spec],
            out_specs=any_spec,
            scratch_shapes=[pltpu.VMEM((SCATTER_BUFS, tm, D_MODEL), F32),
                            pltpu.VMEM((SCATTER_BUFS, tm, 1, D_MODEL), F32), pltpu.SMEM((2 * rows,), jnp.int32),
                            pltpu.SemaphoreType.DMA((SCATTER_BUFS,)), pltpu.SemaphoreType.DMA((SCATTER_BUFS,)),
                            pltpu.SemaphoreType.DMA((2,)), pltpu.SemaphoreType.DMA]),
        out_shape=jax.ShapeDtypeStruct((n_rows, 1, D_MODEL), F32),
        compiler_params=_cparams(("arbitrary",)),
        name="scatter",
    )(zero_blk, x1, pos_tiles)


def _expert_kernel(blk_exp_ref, nblk_ref, nvalid_ref, x_ref, gf_ref, wgu_ref, wd_ref, y_ref):
    del blk_exp_ref
    i = pl.program_id(0)

    @pl.when(i < nblk_ref[0])
    def _():
        x = x_ref[:, 0, :]
        real = lax.broadcasted_iota(jnp.int32, x.shape, 0) < nvalid_ref[i]
        x = jnp.where(real, x, 0.0)
        hn = (x * lax.rsqrt(jnp.mean(x * x, axis=-1, keepdims=True) + EPS) * gf_ref[...]).astype(BF16)
        gu = jnp.dot(hn, wgu_ref[0], preferred_element_type=F32)
        gt, up = gu[:, :D_EXPERT], gu[:, D_EXPERT:]
        act = (gt * (1.0 / (1.0 + jnp.exp(-gt))) * up).astype(BF16)
        y_ref[:, 0, :] = jnp.dot(act, wd_ref[0], preferred_element_type=F32)

    @pl.when(i >= nblk_ref[0])
    def _():
        y_ref[...] = jnp.zeros(y_ref.shape, F32)


def _expert_call(xr, rows, blk_exp, nblk, nvalid, gf, wgu, wd):
    nb = xr.shape[0] // rows
    return pl.pallas_call(
        _expert_kernel,
        grid_spec=pltpu.PrefetchScalarGridSpec(
            num_scalar_prefetch=3,
            grid=(nb,),
            in_specs=[pl.BlockSpec((rows, 1, D_MODEL), lambda i, be, n, nv: (jnp.minimum(i, n[0] - 1), 0, 0)),
                      pl.BlockSpec(gf.shape, lambda i, be, n, nv: (0, 0)),
                      pl.BlockSpec((1, D_MODEL, 2 * D_EXPERT), lambda i, be, n, nv: (be[i], 0, 0)),
                      pl.BlockSpec((1, D_EXPERT, D_MODEL), lambda i, be, n, nv: (be[i], 0, 0))],
            out_specs=pl.BlockSpec((rows, 1, D_MODEL), lambda i, be, n, nv: (i, 0, 0))),
        out_shape=jax.ShapeDtypeStruct((nb * rows, 1, D_MODEL), F32),
        compiler_params=_cparams(("arbitrary",)),
        name="experts",
    )(blk_exp, nblk, nvalid, xr, gf, wgu, wd)


def _combine_kernel(x1_ref, w0_ref, w1_ref, y_hbm, pos_hbm, o_ref, buf, idx, sem_rows, sem_idx):
    rows = buf.shape[1]
    tm = rows // 2
    slot = _gather_step(pl.program_id(0), pl.num_programs(0), y_hbm, pos_hbm, buf, idx, sem_rows, sem_idx)
    w0 = w0_ref[...]
    w1 = w1_ref[...]
    for c in range(D_MODEL // LANES):
        sl = slice(c * LANES, (c + 1) * LANES)
        o_ref[:, sl] = x1_ref[:, sl] + w0 * buf[slot, 0:tm, 0, sl] + w1 * buf[slot, tm:rows, 0, sl]


def _combine_call(x1, w0, w1, yr, pos_tiles, tm):
    t = x1.shape[0]
    rows = TOP_K * tm
    row = lambda w: pl.BlockSpec((tm, w), lambda i: (i, 0))
    return pl.pallas_call(
        _combine_kernel,
        grid=(t // tm,),
        in_specs=[row(D_MODEL), row(LANES), row(LANES),
                  pl.BlockSpec(memory_space=pl.ANY), pl.BlockSpec(memory_space=pl.ANY)],
        out_specs=row(D_MODEL),
        out_shape=jax.ShapeDtypeStruct((t, D_MODEL), F32),
        scratch_shapes=[pltpu.VMEM((2, rows, 1, D_MODEL), F32), pltpu.SMEM((2 * rows,), jnp.int32),
                        pltpu.SemaphoreType.DMA((2,)), pltpu.SemaphoreType.DMA((2,))],
        compiler_params=_cparams(("arbitrary",)),
        name="combine",
    )(x1, w0, w1, yr, pos_tiles)


def _rot_cols(w):
    half = ROPE_DIM // 2
    return jnp.concatenate([-w[..., half:], w[..., :half]], axis=-1)


def _head_pad(nope, rope):
    z = jnp.zeros(rope.shape[:-1] + (LANES - B_QK_DIM,), rope.dtype)
    out = jnp.concatenate([nope, rope, z], axis=-1)
    return out.reshape(out.shape[:-2] + (out.shape[-2] * LANES,))


def _t5_bucket(dist):
    max_exact = REL_BUCKETS // 2
    d = jnp.maximum(dist, 0)
    ratio = jnp.maximum(d, 1).astype(F32) / max_exact
    large = max_exact + (jnp.log(ratio) / math.log(REL_MAX_DIST / max_exact)
                         * (REL_BUCKETS - max_exact)).astype(jnp.int32)
    large = jnp.minimum(large, REL_BUCKETS - 1)
    return jnp.where(d < max_exact, d, large)


def _swa_bias(rel_bias, tq):
    qi = jnp.arange(tq)[:, None]
    kj = jnp.arange(2 * tq)[None, :]
    dist = qi + tq - kj
    onehot = (_t5_bucket(dist)[:, :, None] == jnp.arange(REL_BUCKETS)[None, None, :]).astype(F32)
    bias = jnp.einsum("qkb,bh->hqk", onehot, rel_bias.astype(F32), precision=lax.Precision.HIGHEST) * LOG2E
    mask = (dist >= 0) & (dist < WINDOW)
    bias = jnp.where(mask[None], bias, NEG)
    return bias.reshape(A_KV_HEADS, (A_HEADS // A_KV_HEADS) * tq, 2 * tq)


def _rope_tables(seq):
    half = ROPE_DIM // 2
    inv_freq = ROPE_THETA ** (-jnp.arange(half, dtype=F32) / half)
    ang = jnp.arange(seq, dtype=F32)[:, None] * inv_freq[None, :]
    cos = jnp.concatenate([jnp.cos(ang), jnp.cos(ang)], axis=-1)
    sin = jnp.concatenate([jnp.sin(ang), jnp.sin(ang)], axis=-1)
    return cos, sin


def _lane_table(nope_part, rope_part):
    seq = rope_part.shape[0]
    nope_part = jnp.broadcast_to(nope_part, (seq, NOPE_DIM))
    return jnp.concatenate([nope_part, rope_part, jnp.zeros((seq, LANES - B_QK_DIM), F32)], axis=-1)


def _prep_layer(l, seq, cos, sin, attn_norm, w_in, b_gate, a_q_norm, a_k_norm, w_oa, q_a_norm, w_qb, kv_a_norm,
                w_kvb, b_q_norm, b_k_norm, w_ob, w_o, ffn_norm, w_rg, b_rg, w_re, b_re, w_eg, w_eu, w_ed):
    w = w_in[l]
    o = 0
    parts = []
    for sz in (A_WIDTH, A_KV_HEADS * A_HEAD_DIM, A_KV_HEADS * A_HEAD_DIM, Q_LORA, KV_LORA, ROPE_DIM, 2 * D_MODEL):
        parts.append(w[:, o:o + sz])
        o += sz
    wqa, wka, wva, wcq, wckv, wkpe, wg = parts
    perm = jnp.array([c + (A_HEADS // 2) * h for c in range(A_HEADS // 2) for h in range(2)])
    wqa = wqa.reshape(D_MODEL, A_HEADS, A_HEAD_DIM)[:, perm].reshape(D_MODEL, A_WIDTH)
    zl = jnp.zeros((D_MODEL, NOPE_DIM), F32)
    zr = jnp.zeros((D_MODEL, LANES - B_QK_DIM), F32)
    w1 = jnp.concatenate([wqa, wka, wva, wcq, wckv, zl, wkpe, zr, zl, _rot_cols(wkpe), zr, wg], axis=1).astype(BF16)

    wq = w_qb[l].reshape(Q_LORA, B_HEADS, B_QK_DIM)
    zq = jnp.zeros((Q_LORA, B_HEADS, NOPE_DIM), F32)
    wq2 = jnp.concatenate([_head_pad(wq[..., :NOPE_DIM], wq[..., NOPE_DIM:]),
                           _head_pad(zq, _rot_cols(wq[..., NOPE_DIM:]))], axis=1).astype(BF16)
    wkv = w_kvb[l].reshape(KV_LORA, B_HEADS, NOPE_DIM + B_V_DIM)
    wk2 = _head_pad(wkv[..., :NOPE_DIM], jnp.zeros((KV_LORA, B_HEADS, ROPE_DIM), F32)).astype(BF16)
    wv = wkv[..., NOPE_DIM:]
    zv = jnp.zeros_like(wv)
    wv2 = jnp.where((jnp.arange(B_HEADS) % 2 == 0)[None, :, None], jnp.concatenate([wv, zv], -1),
                    jnp.concatenate([zv, wv], -1)).reshape(KV_LORA, B_HEADS * LANES).astype(BF16)
    one_lane = jnp.where(jnp.arange(B_HEADS) % 2 == 0, V_ONE_EVEN, V_ONE_ODD)
    vone = (jnp.arange(LANES)[None, :] == one_lane[:, None]).astype(F32).reshape(1, B_HEADS * LANES)

    def rope_tabs(gain, scale):
        gn, gr = gain[:NOPE_DIM], gain[NOPE_DIM:]
        gr_rot = jnp.concatenate([gr[ROPE_DIM // 2:], gr[:ROPE_DIM // 2]])
        a = _lane_table(gn[None, :] * scale, gr[None, :] * cos * scale)
        b = _lane_table(jnp.zeros((1, NOPE_DIM), F32), gr_rot[None, :] * sin * scale)
        return a, b

    aq, bq = rope_tabs(b_q_norm[l], B_QK_DIM ** -0.5 * LOG2E)
    score_bound = math.sqrt(B_QK_DIM) * LOG2E * jnp.max(jnp.abs(b_q_norm[l])) * jnp.max(jnp.abs(b_k_norm[l]))
    ak, bk = rope_tabs(b_k_norm[l], 1.0)

    gqa = (jnp.tile(a_q_norm[l], 2) * (A_HEAD_DIM ** -0.5 * LOG2E))[None, :]
    gka = jnp.tile(a_k_norm[l], 2)[None, :]
    woa = w_oa[l].reshape(A_HEADS, A_HEAD_DIM, D_MODEL)[perm].reshape(A_WIDTH, D_MODEL).astype(BF16)

    wr = jnp.concatenate([w_rg[l], w_re[l], jnp.zeros((D_MODEL, LANES - N_GROUPS - N_EXPERTS), F32)], axis=1)
    wrh = wr.astype(BF16)
    wrl = (wr - wrh.astype(F32)).astype(BF16)
    br = jnp.concatenate([b_rg[l], b_re[l], jnp.zeros((LANES - N_GROUPS - N_EXPERTS,), F32)])[None, :]

    return dict(
        gn=attn_norm[l][None, :], w1=w1, wq2=wq2, wk2=wk2, wv2=wv2, vone=vone, gcq=q_a_norm[l][None, :],
        gckv=kv_a_norm[l][None, :], score_bound=score_bound, bg=b_gate[l][None, :], gqa=gqa, gka=gka,
        aq=aq, bq=bq, ak=ak, bk=bk,
        woa=woa, wob=w_ob[l].astype(BF16), wo=w_o[l].astype(BF16), gf=ffn_norm[l][None, :], wr=jnp.concatenate([wrh, wrl], axis=1), br=br,
        wgu=jnp.concatenate([w_eg[l], w_eu[l]], axis=-1).astype(BF16), wd=w_ed[l].astype(BF16))


def _tril(n):
    return (jnp.arange(n)[:, None] >= jnp.arange(n)[None, :]).astype(jnp.int32)


def _dispatch(route, tile_cnt, rows):
    t = route.shape[0]
    nt = tile_cnt.shape[0]
    tile_cnt = tile_cnt[:, 0, :N_EXPERTS]
    counts = jnp.sum(tile_cnt, axis=0)
    padded = ((counts + rows - 1) // rows) * rows
    pends = jnp.dot(_tril(N_EXPERTS), padded)
    pstarts = pends - padded
    tile_base = pstarts[None, :] + jnp.dot(_tril(nt), tile_cnt) - tile_cnt
    base = jnp.broadcast_to(tile_base[:, None, None, :], (nt, t // nt, 1, N_EXPERTS)).reshape(t, 1, N_EXPERTS)
    onehot = route[:, :TOP_K, None] == jnp.arange(N_EXPERTS, dtype=jnp.int32)[None, None, :]
    dest = jnp.sum(jnp.where(onehot, base, 0), axis=-1) + route[:, TOP_K:2 * TOP_K]
    n_rows = t * TOP_K + N_EXPERTS * rows
    n_blocks = n_rows // rows
    blk_start = jnp.arange(n_blocks, dtype=jnp.int32)[:, None] * rows
    blk_exp = jnp.minimum(jnp.sum((pends[None, :] <= blk_start).astype(jnp.int32), axis=1), N_EXPERTS - 1)
    nblk = (pends[-1] // rows).astype(jnp.int32).reshape(1)
    nvalid = jnp.clip((pstarts + counts)[blk_exp] - blk_start[:, 0], 0, rows).astype(jnp.int32)
    last_blk = jnp.where(padded > 0, pends // rows - 1, -1)
    trailing = nblk[0] + jnp.arange(N_EXPERTS, dtype=jnp.int32)
    zero_blk = jnp.concatenate([last_blk, jnp.where(trailing < n_blocks, trailing, -1)]).astype(jnp.int32)
    return dest, blk_exp, nblk, nvalid, zero_blk, n_rows


def _swa_tables(sinks, rel_bias, a_q_norm, a_k_norm):
    per_group = A_HEADS // A_KV_HEADS
    tab = jnp.broadcast_to((sinks * LOG2E).reshape(A_KV_HEADS, per_group, 1, 1),
                           (A_KV_HEADS, per_group, SWA_TQ, LANES)).reshape(A_KV_HEADS, per_group * SWA_TQ, LANES)
    qk = math.sqrt(A_HEAD_DIM) * jnp.max(jnp.abs(a_q_norm)) * jnp.max(jnp.abs(a_k_norm))
    bound = LOG2E * jnp.maximum(qk + jnp.max(jnp.abs(rel_bias)), jnp.max(jnp.abs(sinks)))
    return tab, bound


def _layer(x2d, bsz, seq, p, sink_tab, swa_bound, bias):
    t = x2d.shape[0]
    qa, ka, va, qb, kb, vb, gates = _proj_call(x2d, seq, p)
    oa = _swa_call(qa, ka, va, sink_tab, bias, bsz, seq, swa_bound)
    ob = _mla_call(qb, kb, vb, bsz, seq, p["score_bound"])
    x1, w0, w1, route, tile_cnt = _merge_call(x2d, oa, ob, gates, p)
    dest, blk_exp, nblk, nvalid, zero_blk, n_rows = _dispatch(route, tile_cnt, MOE_ROWS)
    pos_tiles = dest.reshape(t // COMB_TM, COMB_TM, TOP_K).transpose(0, 2, 1).reshape(t * TOP_K)
    xr = _scatter_call(x1, pos_tiles, zero_blk, n_rows, COMB_TM, MOE_ROWS)
    yr = _expert_call(xr, MOE_ROWS, blk_exp, nblk, nvalid, p["gf"], p["wgu"], p["wd"])
    return _combine_call(x1, w0, w1, yr, pos_tiles, COMB_TM)


def kernel(x, attn_norm, w_in, b_gate, a_q_norm, a_k_norm, a_sinks, rel_bias, w_oa, q_a_norm, w_qb, kv_a_norm, w_kvb, b_q_norm, b_k_norm, w_ob, w_o, ffn_norm, w_router_group, b_router_group, w_router_expert, b_router_expert, w_exp_gate, w_exp_up, w_exp_down):
    bsz, seq, d = x.shape
    assert d == D_MODEL and seq % MLA_TQ == 0 and seq % PROJ_TM == 0 and (bsz * seq) % COMB_TM == 0
    depth = w_in.shape[0]
    cos, sin = _rope_tables(seq)
    bias = _swa_bias(rel_bias, SWA_TQ)
    x2d = x.reshape(bsz * seq, d)
    for l in range(depth):
        p = _prep_layer(l, seq, cos, sin, attn_norm, w_in, b_gate, a_q_norm, a_k_norm, w_oa, q_a_norm, w_qb,
                        kv_a_norm, w_kvb, b_q_norm, b_k_norm, w_ob, w_o, ffn_norm, w_router_group,
                        b_router_group, w_router_expert, b_router_expert, w_exp_gate, w_exp_up, w_exp_down)
        sink_tab, swa_bound = _swa_tables(a_sinks[l], rel_bias, a_q_norm[l], a_k_norm[l])
        x2d = _layer(x2d, bsz, seq, p, sink_tab, swa_bound, bias)
    return x2d.reshape(bsz, seq, d)
```

```python
import functools
import math

import jax
import jax.numpy as jnp
from jax import lax
from jax.experimental import pallas as pl
from jax.experimental.pallas import tpu as pltpu

F32 = jnp.float32
BF16 = jnp.bfloat16

D_MODEL = 1024
A_HEADS = 8
A_KV_HEADS = 2
A_HEAD_DIM = 64
A_WIDTH = A_HEADS * A_HEAD_DIM
WINDOW = 128
REL_BUCKETS = 32
REL_MAX_DIST = 128
B_HEADS = 8
Q_LORA = 256
KV_LORA = 128
NOPE_DIM = 64
ROPE_DIM = 32
B_QK_DIM = NOPE_DIM + ROPE_DIM
B_V_DIM = 64
B_WIDTH = B_HEADS * B_V_DIM
ROPE_THETA = 10000.0
N_GROUPS = 4
EXPERTS_PER_GROUP = 8
N_EXPERTS = N_GROUPS * EXPERTS_PER_GROUP
TOP_K = 2
D_EXPERT = 256
EPS = 1e-6

LANES = 128
NEG = -1e30
LOG2E = math.log2(math.e)
V_ONE_EVEN = B_V_DIM
V_ONE_ODD = 0
EXP2_SAFE_BOUND = 100.0
VMEM_LIMIT = 56 * 1024 * 1024

C_QA = 0
C_KA = C_QA + A_WIDTH
C_VA = C_KA + LANES
C_CQ = C_VA + LANES
C_CKV = C_CQ + Q_LORA
C_KPE = C_CKV + KV_LORA
C_KPR = C_KPE + LANES
C_GATE = C_KPR + LANES
C_END = C_GATE + 2 * D_MODEL

PROJ_TM = 512
SWA_TQ = 128
SWA_SUB = 4
MLA_TQ = 1024
MLA_TK = 512
MERGE_TM = 512
MOE_ROWS = 512
COMB_TM = 256


def _cparams(sem):
    return pltpu.CompilerParams(dimension_semantics=sem, vmem_limit_bytes=VMEM_LIMIT)


def _proj_kernel(x_ref, gn_ref, w1_ref, wq2_ref, wk2_ref, wv2_ref, vone_ref, gcq_ref, gckv_ref, bg_ref,
                 gqa_ref, gka_ref, aq_ref, bq_ref, ak_ref, bk_ref,
                 qa_ref, ka_ref, va_ref, qb_ref, kb_ref, vb_ref, gate_ref):
    tm = x_ref.shape[0]
    x = x_ref[...]
    h = (x * lax.rsqrt(jnp.mean(x * x, axis=-1, keepdims=True) + EPS) * gn_ref[...]).astype(BF16)

    def proj(a, b):
        return jnp.dot(h, w1_ref[:, a:b], preferred_element_type=F32)

    lo = lax.broadcasted_iota(jnp.int32, (tm, LANES), 1) < A_HEAD_DIM

    def pair_norm(t, gain):
        t2 = t * t
        s_lo = jnp.sum(jnp.where(lo, t2, 0.0), axis=-1, keepdims=True)
        s_hi = jnp.sum(jnp.where(lo, 0.0, t2), axis=-1, keepdims=True)
        r = jnp.where(lo, lax.rsqrt(s_lo / A_HEAD_DIM + EPS), lax.rsqrt(s_hi / A_HEAD_DIM + EPS))
        return (t * r * gain).astype(BF16)

    qa = proj(C_QA, C_KA)
    for c in range(A_WIDTH // LANES):
        sl = slice(c * LANES, (c + 1) * LANES)
        qa_ref[:, sl] = pair_norm(qa[:, sl], gqa_ref[...])
    kvp = proj(C_KA, C_CQ)
    ka_ref[...] = pair_norm(kvp[:, :LANES], gka_ref[...])
    va_ref[...] = kvp[:, LANES:].astype(BF16)

    lat = proj(C_CQ, C_GATE)
    cq = lat[:, :Q_LORA]
    cqn = (cq * lax.rsqrt(jnp.mean(cq * cq, axis=-1, keepdims=True) + EPS) * gcq_ref[...]).astype(BF16)
    ckv = lat[:, Q_LORA:Q_LORA + KV_LORA]
    ckvn = (ckv * lax.rsqrt(jnp.mean(ckv * ckv, axis=-1, keepdims=True) + EPS) * gckv_ref[...]).astype(BF16)
    kpe = lat[:, C_KPE - C_CQ:C_KPR - C_CQ]
    kpr = lat[:, C_KPR - C_CQ:C_GATE - C_CQ]

    aq, bq, ak, bk = aq_ref[...], bq_ref[...], ak_ref[...], bk_ref[...]
    q2 = jnp.dot(cqn, wq2_ref[...], preferred_element_type=F32)
    k2 = jnp.dot(ckvn, wk2_ref[...], preferred_element_type=F32)
    vb_ref[...] = (jnp.dot(ckvn, wv2_ref[...], preferred_element_type=F32) + vone_ref[...]).astype(BF16)
    kr = kpr * bk
    for hd in range(B_HEADS):
        sl = slice(hd * LANES, (hd + 1) * LANES)
        qp = q2[:, sl]
        qr = q2[:, B_HEADS * LANES + hd * LANES:B_HEADS * LANES + (hd + 1) * LANES]
        rq = lax.rsqrt(jnp.sum(qp * qp, axis=-1, keepdims=True) / B_QK_DIM + EPS)
        qb_ref[:, sl] = (rq * (qp * aq + qr * bq)).astype(BF16)
        kp = k2[:, sl] + kpe
        rk = lax.rsqrt(jnp.sum(kp * kp, axis=-1, keepdims=True) / B_QK_DIM + EPS)
        kb_ref[:, sl] = (rk * (kp * ak + kr)).astype(BF16)

    z = proj(C_GATE, C_END) + bg_ref[...]
    gate_ref[...] = (0.5 * jnp.tanh(0.5 * z) + 0.5).astype(BF16)


def _proj_call(x2d, seq, p):
    t = x2d.shape[0]
    tm = PROJ_TM
    nt = t // tm
    npos = seq // tm
    row = lambda w: pl.BlockSpec((tm, w), lambda i: (i, 0))
    full = lambda a: pl.BlockSpec(a.shape, lambda i: (0,) * a.ndim)
    pos = pl.BlockSpec((tm, LANES), lambda i: (i % npos, 0))
    consts = [p["gn"], p["w1"], p["wq2"], p["wk2"], p["wv2"], p["vone"], p["gcq"], p["gckv"], p["bg"], p["gqa"],
              p["gka"]]
    tabs = [p["aq"], p["bq"], p["ak"], p["bk"]]
    widths = [A_WIDTH, LANES, LANES, B_HEADS * LANES, B_HEADS * LANES, B_HEADS * LANES, 2 * D_MODEL]
    return pl.pallas_call(
        _proj_kernel,
        grid=(nt,),
        in_specs=[row(D_MODEL)] + [full(a) for a in consts] + [pos] * 4,
        out_specs=[row(w) for w in widths],
        out_shape=[jax.ShapeDtypeStruct((t, w), BF16) for w in widths],
        compiler_params=_cparams(("parallel",)),
        name="proj",
    )(x2d, *consts, *tabs)


def _swa_kernel(q_ref, kc_ref, kp_ref, vc_ref, vp_ref, bias_ref, sink_ref, o_ref, *, shift):
    tq = SWA_TQ
    groups = A_WIDTH // LANES
    first = pl.program_id(1) == 0
    one_col = (lax.broadcasted_iota(jnp.int32, (2 * tq, LANES), 1) == 0).astype(BF16)
    lo = lax.broadcasted_iota(jnp.int32, (tq, LANES), 1) < A_HEAD_DIM
    no_prev = jnp.logical_and(first, lax.broadcasted_iota(jnp.int32, (groups * tq, 2 * tq), 1) < tq)
    for sub in range(SWA_SUB):
        if sub == 0:
            k = jnp.concatenate([kp_ref[...], kc_ref[0:tq, :]], axis=0)
            v = jnp.concatenate([vp_ref[...], vc_ref[0:tq, :]], axis=0)
        else:
            k = kc_ref[(sub - 1) * tq:(sub + 1) * tq, :]
            v = vc_ref[(sub - 1) * tq:(sub + 1) * tq, :]
        v_ext = jnp.concatenate([v, one_col], axis=1)
        qrows = slice(sub * tq, (sub + 1) * tq)
        qs = [q_ref[qrows, c * LANES:(c + 1) * LANES] for c in range(groups)]
        res = []
        for g in range(A_KV_HEADS):
            keep = lo if g == 0 else jnp.logical_not(lo)
            qg = jnp.concatenate([jnp.where(keep, qc, jnp.zeros_like(qc)) for qc in qs], axis=0)
            s = lax.dot_general(qg, k, (((1,), (1,)), ((), ())), preferred_element_type=F32) + bias_ref[g]
            sink = sink_ref[g][:, :1]
            if shift:
                if sub == 0:
                    s = jnp.where(no_prev, NEG, s)
                m = jnp.maximum(jnp.max(s, axis=-1, keepdims=True), sink)
                e = jnp.exp2(s - m)
                sink_term = jnp.exp2(sink - m)
            else:
                e = jnp.exp2(s)
                if sub == 0:
                    e = jnp.where(no_prev, 0.0, e)
                sink_term = jnp.exp2(sink)
            pv = jnp.dot(e.astype(BF16), v_ext, preferred_element_type=F32)
            res.append(pv[:, :LANES] * (1.0 / (pv[:, LANES:LANES + 1] + sink_term)))
        for c in range(groups):
            rows = slice(c * tq, (c + 1) * tq)
            o_ref[qrows, c * LANES:(c + 1) * LANES] = jnp.where(lo, res[0][rows], res[1][rows]).astype(BF16)


def _swa_call(qa, ka, va, sink_tab, bias, bsz, seq, score_bound):
    t = qa.shape[0]
    tq = SWA_TQ
    step = SWA_SUB * tq
    ns = seq // step
    cur = lambda b, i: (b * ns + i, 0)
    prev = lambda b, i: (b * (seq // tq) + jnp.maximum(SWA_SUB * i - 1, 0), 0)
    full = lambda a: pl.BlockSpec(a.shape, lambda b, i: (0,) * a.ndim)

    def call(shift):
        return pl.pallas_call(
            functools.partial(_swa_kernel, shift=shift),
            grid=(bsz, ns),
            in_specs=[pl.BlockSpec((step, A_WIDTH), cur),
                      pl.BlockSpec((step, LANES), cur), pl.BlockSpec((tq, LANES), prev),
                      pl.BlockSpec((step, LANES), cur), pl.BlockSpec((tq, LANES), prev),
                      full(bias), full(sink_tab)],
            out_specs=pl.BlockSpec((step, A_WIDTH), cur),
            out_shape=jax.ShapeDtypeStruct((t, A_WIDTH), BF16),
            compiler_params=_cparams(("parallel", "parallel")),
            name="swa_shifted" if shift else "swa",
        )(qa, ka, ka, va, va, bias, sink_tab)

    return lax.cond(score_bound <= EXP2_SAFE_BOUND, lambda: call(False), lambda: call(True))


def _mla_kernel(q_ref, k_ref, v_ref, o_ref, acc_sc, m_sc, *, shift):
    tq = q_ref.shape[0]
    tk = MLA_TK
    per_tile = tq // tk
    qi = pl.program_id(2)
    acc_sc[...] = jnp.zeros(acc_sc.shape, F32)
    if shift:
        m_sc[...] = jnp.full(m_sc.shape, NEG, F32)

    def tile(j, r0, r1, masked):
        keys = pl.ds(pl.multiple_of(j * tk, tk), tk)
        if masked:
            causal = (lax.broadcasted_iota(jnp.int32, (r1 - r0, tk), 1)
                      <= lax.broadcasted_iota(jnp.int32, (r1 - r0, tk), 0))
        for half in range(2):
            hs = slice(half * LANES, (half + 1) * LANES)
            s = lax.dot_general(q_ref[r0:r1, hs], k_ref[keys, hs], (((1,), (1,)), ((), ())),
                                preferred_element_type=F32)
            if shift:
                if masked:
                    s = jnp.where(causal, s, NEG)
                m_prev = m_sc[half, r0:r1]
                m_new = jnp.maximum(m_prev, jnp.max(s, axis=-1, keepdims=True))
                m_sc[half, r0:r1] = m_new
                e = jnp.exp2(s - m_new)
                acc_sc[half, r0:r1] = jnp.exp2(m_prev - m_new) * acc_sc[half, r0:r1]
            else:
                e = jnp.exp2(s)
                if masked:
                    e = jnp.where(causal, e, 0.0)
            acc_sc[half, r0:r1] += jnp.dot(e.astype(BF16), v_ref[keys, hs], preferred_element_type=F32)

    def body(j, carry):
        tile(j, 0, tq, False)
        return carry

    lax.fori_loop(0, qi * per_tile, body, 0)
    for d in range(per_tile):
        tile(qi * per_tile + d, d * tk, (d + 1) * tk, True)
        if (d + 1) * tk < tq:
            tile(qi * per_tile + d, (d + 1) * tk, tq, False)
    a0 = acc_sc[0]
    a1 = acc_sc[1]
    lo = lax.broadcasted_iota(jnp.int32, (tq, LANES), 1) < B_V_DIM
    o_ref[...] = jnp.where(lo, a0 * (1.0 / a0[:, V_ONE_EVEN:V_ONE_EVEN + 1]),
                           a1 * (1.0 / a1[:, V_ONE_ODD:V_ONE_ODD + 1])).astype(BF16)


def _mla_call(qb, kb, vb, bsz, seq, score_bound):
    t = qb.shape[0]
    tq = MLA_TQ
    nq = seq // tq
    pairs = B_HEADS // 2

    def call(shift):
        return pl.pallas_call(
            functools.partial(_mla_kernel, shift=shift),
            grid=(bsz, pairs, nq),
            in_specs=[pl.BlockSpec((tq, 2 * LANES), lambda b, p, i: (b * nq + i, p)),
                      pl.BlockSpec((seq, 2 * LANES), lambda b, p, i: (b, p)),
                      pl.BlockSpec((seq, 2 * LANES), lambda b, p, i: (b, p))],
            out_specs=pl.BlockSpec((tq, LANES), lambda b, p, i: (b * nq + i, p)),
            out_shape=jax.ShapeDtypeStruct((t, B_WIDTH), BF16),
            scratch_shapes=[pltpu.VMEM((2, tq, LANES), F32), pltpu.VMEM((2, tq, 1), F32)],
            compiler_params=_cparams(("parallel", "parallel", "arbitrary")),
            name="mla_shifted" if shift else "mla",
        )(qb, kb, vb)

    return lax.cond(score_bound <= EXP2_SAFE_BOUND, lambda: call(False), lambda: call(True))


def _merge_kernel(x_ref, oa_ref, ob_ref, gate_ref, woa_ref, wob_ref, wo_ref, gf_ref, wr_ref, br_ref,
                  tri_ref, x1_ref, w0_ref, w1_ref, eid_ref, cnt_ref):
    tm = x_ref.shape[0]
    ya = jnp.dot(oa_ref[...], woa_ref[...], preferred_element_type=F32)
    yb = jnp.dot(ob_ref[...], wob_ref[...], preferred_element_type=F32)
    g = gate_ref[...].astype(F32)
    mix = (g[:, :D_MODEL] * ya + g[:, D_MODEL:] * yb).astype(BF16)
    x1 = x_ref[...] + jnp.dot(mix, wo_ref[...], preferred_element_type=F32)
    x1_ref[...] = x1

    hn = x1 * lax.rsqrt(jnp.mean(x1 * x1, axis=-1, keepdims=True) + EPS) * gf_ref[...]
    hh = hn.astype(BF16)
    hl = (hn - hh.astype(F32)).astype(BF16)
    ph = jnp.dot(hh, wr_ref[...], preferred_element_type=F32)
    pl_ = jnp.dot(hl, wr_ref[...], preferred_element_type=F32)
    logits = (ph[:, :LANES] + ph[:, LANES:]) + (pl_[:, :LANES] + pl_[:, LANES:]) + br_ref[...]

    lane = lax.broadcasted_iota(jnp.int32, (tm, LANES), 1).astype(F32)
    big = float(LANES)
    gmask = lane < N_GROUPS
    gl = jnp.where(gmask, logits, NEG)
    gmax = jnp.max(gl, axis=-1, keepdims=True)
    gsum = jnp.sum(jnp.where(gmask, jnp.exp(gl - gmax), 0.0), axis=-1, keepdims=True)
    g_p = 1.0 / gsum
    g_idx = jnp.min(jnp.where(gl == gmax, lane, big), axis=-1, keepdims=True)
    e_lo = N_GROUPS + EXPERTS_PER_GROUP * g_idx
    emask = jnp.logical_and(lane >= e_lo, lane < e_lo + EXPERTS_PER_GROUP)
    el = jnp.where(emask, logits, NEG)
    t1 = jnp.max(el, axis=-1, keepdims=True)
    i1 = jnp.min(jnp.where(el == t1, lane, big), axis=-1, keepdims=True)
    el2 = jnp.where(lane == i1, NEG, el)
    t2 = jnp.max(el2, axis=-1, keepdims=True)
    i2 = jnp.min(jnp.where(el2 == t2, lane, big), axis=-1, keepdims=True)
    e2 = jnp.exp(t2 - t1)
    w_first = g_p / (1.0 + e2)
    w0_ref[...] = jnp.broadcast_to(w_first, (tm, LANES))
    w1_ref[...] = jnp.broadcast_to(w_first * e2, (tm, LANES))
    ex1 = i1 - N_GROUPS
    ex2 = i2 - N_GROUPS

    oh1 = lane == ex1
    oh2 = lane == ex2
    cnt = (oh1.astype(F32) + oh2.astype(F32)).astype(BF16)
    before = jnp.dot(tri_ref[...], cnt, preferred_element_type=F32)
    r1 = jnp.sum(jnp.where(oh1, before, 0.0), axis=-1, keepdims=True)
    r2 = jnp.sum(jnp.where(oh2, before, 0.0), axis=-1, keepdims=True)
    ids = jnp.where(lane == 0.0, ex1, jnp.where(lane == 1.0, ex2, jnp.where(lane == 2.0, r1,
                                                                             jnp.where(lane == 3.0, r2, 0.0))))
    eid_ref[...] = ids.astype(jnp.int32)
    tile_cnt = jnp.sum(cnt.astype(F32), axis=0, keepdims=True)
    cnt_ref[...] = jnp.broadcast_to(tile_cnt, cnt_ref.shape[1:]).astype(jnp.int32)[None]


def _merge_call(x2d, oa, ob, gates, p):
    t = x2d.shape[0]
    tm = MERGE_TM
    nt = t // tm
    row = lambda w: pl.BlockSpec((tm, w), lambda i: (i, 0))
    full = lambda a: pl.BlockSpec(a.shape, lambda i: (0,) * a.ndim)
    tri = (jnp.arange(tm)[:, None] > jnp.arange(tm)[None, :]).astype(BF16)
    consts = [p["woa"], p["wob"], p["wo"], p["gf"], p["wr"], p["br"], tri]
    return pl.pallas_call(
        _merge_kernel,
        grid=(nt,),
        in_specs=[row(D_MODEL), row(A_WIDTH), row(B_WIDTH), row(2 * D_MODEL)] + [full(a) for a in consts],
        out_specs=[row(D_MODEL), row(LANES), row(LANES), row(LANES),
                   pl.BlockSpec((1, 8, LANES), lambda i: (i, 0, 0))],
        out_shape=[jax.ShapeDtypeStruct((t, D_MODEL), F32), jax.ShapeDtypeStruct((t, LANES), F32),
                   jax.ShapeDtypeStruct((t, LANES), F32), jax.ShapeDtypeStruct((t, LANES), jnp.int32),
                   jax.ShapeDtypeStruct((nt, 8, LANES), jnp.int32)],
        compiler_params=_cparams(("parallel",)),
        name="merge",
    )(x2d, oa, ob, gates, *consts)


def _rows_copy(src_hbm, row, dst, j, sem):
    return pltpu.make_async_copy(src_hbm.at[row], dst.at[j], sem)


def _start_rows(src_hbm, idx, base, dst, sem, n):
    for j in range(n):
        _rows_copy(src_hbm, idx[base + j], dst, j, sem).start(priority=j % 2)


WAIT_ROWS = 128


def _wait_rows(src_hbm, dst, sem, n):
    for c in range(n // WAIT_ROWS):
        rows = pl.ds(c * WAIT_ROWS, WAIT_ROWS)
        pltpu.make_async_copy(src_hbm.at[rows], dst.at[rows], sem).wait()


def _gather_step(i, n, src_hbm, idx_hbm, buf, idx, sem_rows, sem_idx):
    rows = buf.shape[1]
    slot = i % 2
    nslot = 1 - slot

    def idx_copy(b, s):
        return pltpu.make_async_copy(idx_hbm.at[pl.ds(b * rows, rows)], idx.at[pl.ds(s * rows, rows)], sem_idx.at[s])

    @pl.when(jnp.logical_and(i == 0, n > 0))
    def _():
        idx_copy(0, 0).start()
        idx_copy(0, 0).wait()
        _start_rows(src_hbm, idx, 0, buf.at[0], sem_rows.at[0], rows)

        @pl.when(n > 1)
        def _():
            idx_copy(1, 1).start()

    @pl.when(i + 1 < n)
    def _():
        idx_copy(i + 1, nslot).wait()
        _start_rows(src_hbm, idx, nslot * rows, buf.at[nslot], sem_rows.at[nslot], rows)

    @pl.when(i + 2 < n)
    def _():
        idx_copy(i + 2, slot).start()

    @pl.when(i < n)
    def _():
        _wait_rows(src_hbm, buf.at[slot], sem_rows.at[slot], rows)
    return slot


SCATTER_BUFS = 3


def _pack_bf16_pair(a, b):
    abits = lax.bitcast_convert_type(a.astype(BF16).astype(F32), jnp.uint32)
    bbits = lax.bitcast_convert_type(b.astype(BF16).astype(F32), jnp.uint32)
    return (abits >> 16) | (bbits & jnp.uint32(0xFFFF0000))


def _unpack_bf16_pair(w):
    lo = lax.bitcast_convert_type(w << 16, F32).astype(BF16)
    hi = lax.bitcast_convert_type(w & jnp.uint32(0xFFFF0000), F32).astype(BF16)
    return lo, hi


def _scatter_kernel(zero_blk_ref, x_hbm, dest_hbm, gf_ref, xr_hbm, buf, rbuf, idx, sem_x, sem_rows, sem_idx,
                    sem_zero, *, blk_rows):
    i = pl.program_id(0)
    n = pl.num_programs(0)
    tm = buf.shape[1]
    rows = TOP_K * tm
    islot = i % 2
    xslot = i % SCATTER_BUFS

    def idx_copy(b, s):
        return pltpu.make_async_copy(dest_hbm.at[pl.ds(b * rows, rows)], idx.at[pl.ds(s * rows, rows)], sem_idx.at[s])

    def x_copy(b, s):
        return pltpu.make_async_copy(x_hbm.at[pl.ds(b * tm, tm), :], buf.at[s], sem_x.at[s])

    def wait_rows(s):
        for k in range(TOP_K):
            pltpu.make_async_copy(rbuf.at[s], xr_hbm.at[pl.ds(0, tm)], sem_rows.at[s]).wait()

    @pl.when(i == 0)
    def _():
        idx_copy(0, 0).start()
        x_copy(0, 0).start()
        zsrc = rbuf.at[SCATTER_BUFS - 1]
        zsrc[...] = jnp.zeros(zsrc.shape, zsrc.dtype)

        def zero_copies(e):
            b = zero_blk_ref[e]
            return b >= 0, [pltpu.make_async_copy(zsrc, xr_hbm.at[pl.ds(b * blk_rows + h * tm, tm)], sem_zero)
                            for h in range(blk_rows // tm)]

        def start(e, carry):
            live, copies = zero_copies(e)

            @pl.when(live)
            def _():
                for c in copies:
                    c.start()
            return carry

        def wait(e, carry):
            live, copies = zero_copies(e)

            @pl.when(live)
            def _():
                for c in copies:
                    c.wait()
            return carry

        lax.fori_loop(0, zero_blk_ref.shape[0], start, 0)
        lax.fori_loop(0, zero_blk_ref.shape[0], wait, 0)

    @pl.when(i >= 2)
    def _():
        wait_rows((i - 2) % SCATTER_BUFS)

    @pl.when(i + 1 < n)
    def _():
        idx_copy(i + 1, 1 - islot).start()
        x_copy(i + 1, (i + 1) % SCATTER_BUFS).start()

    idx_copy(i, islot).wait()
    x_copy(i, xslot).wait()

    x = buf[xslot]
    hn = x * lax.rsqrt(jnp.mean(x * x, axis=-1, keepdims=True) + EPS) * gf_ref[...]
    rbuf[xslot, :, 0, :] = _pack_bf16_pair(hn[:, :D_MODEL // 2], hn[:, D_MODEL // 2:])
    for j in range(tm):
        src = rbuf.at[xslot, j]
        for k in range(TOP_K):
            dst = xr_hbm.at[idx[islot * rows + k * tm + j]]
            pltpu.make_async_copy(src, dst, sem_rows.at[xslot]).start(priority=k % 2)

    @pl.when(i == n - 1)
    def _():
        @pl.when(n > 1)
        def _():
            wait_rows((i - 1) % SCATTER_BUFS)
        wait_rows(xslot)


def _scatter_call(x1, pos_tiles, zero_blk, gf, n_rows, tm, blk_rows):
    t = x1.shape[0]
    rows = TOP_K * tm
    assert blk_rows % tm == 0
    any_spec = pl.BlockSpec(memory_space=pl.ANY)
    return pl.pallas_call(
        functools.partial(_scatter_kernel, blk_rows=blk_rows),
        grid_spec=pltpu.PrefetchScalarGridSpec(
            num_scalar_prefetch=1,
            grid=(t // tm,),
            in_specs=[any_spec, any_spec, pl.BlockSpec(gf.shape, lambda i, zb: (0, 0))],
            out_specs=any_spec,
            scratch_shapes=[pltpu.VMEM((SCATTER_BUFS, tm, D_MODEL), F32),
                            pltpu.VMEM((SCATTER_BUFS, tm, 1, D_MODEL // 2), jnp.uint32),
                            pltpu.SMEM((2 * rows,), jnp.int32),
                            pltpu.SemaphoreType.DMA((SCATTER_BUFS,)), pltpu.SemaphoreType.DMA((SCATTER_BUFS,)),
                            pltpu.SemaphoreType.DMA((2,)), pltpu.SemaphoreType.DMA]),
        out_shape=jax.ShapeDtypeStruct((n_rows, 1, D_MODEL // 2), jnp.uint32),
        compiler_params=_cparams(("arbitrary",)),
        name="scatter",
    )(zero_blk, x1, pos_tiles, gf)


def _expert_kernel(blk_exp_ref, nblk_ref, nvalid_ref, x_ref, wgu_ref, wd_ref, y_ref):
    del blk_exp_ref
    i = pl.program_id(0)

    @pl.when(i < nblk_ref[0])
    def _():
        w = x_ref[:, 0, :]
        real = lax.broadcasted_iota(jnp.int32, w.shape, 0) < nvalid_ref[i]
        lo, hi = _unpack_bf16_pair(jnp.where(real, w, jnp.uint32(0)))
        half = D_MODEL // 2
        gu = (jnp.dot(lo, wgu_ref[0, :half, :], preferred_element_type=F32)
              + jnp.dot(hi, wgu_ref[0, half:, :], preferred_element_type=F32))
        gt, up = gu[:, :D_EXPERT], gu[:, D_EXPERT:]
        act = (gt * (0.5 * jnp.tanh(0.5 * gt) + 0.5) * up).astype(BF16)
        y_ref[:, 0, :] = jnp.dot(act, wd_ref[0], preferred_element_type=F32)

    @pl.when(i >= nblk_ref[0])
    def _():
        y_ref[...] = jnp.zeros(y_ref.shape, F32)


def _expert_call(xr, rows, blk_exp, nblk, nvalid, wgu, wd):
    nb = xr.shape[0] // rows
    return pl.pallas_call(
        _expert_kernel,
        grid_spec=pltpu.PrefetchScalarGridSpec(
            num_scalar_prefetch=3,
            grid=(nb,),
            in_specs=[pl.BlockSpec((rows, 1, D_MODEL // 2), lambda i, be, n, nv: (jnp.minimum(i, n[0] - 1), 0, 0)),
                      pl.BlockSpec((1, D_MODEL, 2 * D_EXPERT), lambda i, be, n, nv: (be[i], 0, 0)),
                      pl.BlockSpec((1, D_EXPERT, D_MODEL), lambda i, be, n, nv: (be[i], 0, 0))],
            out_specs=pl.BlockSpec((rows, 1, D_MODEL), lambda i, be, n, nv: (i, 0, 0))),
        out_shape=jax.ShapeDtypeStruct((nb * rows, 1, D_MODEL), F32),
        compiler_params=_cparams(("arbitrary",)),
        name="experts",
    )(blk_exp, nblk, nvalid, xr, wgu, wd)


def _combine_kernel(x1_ref, w0_ref, w1_ref, y_hbm, pos_hbm, o_ref, buf, idx, sem_rows, sem_idx):
    rows = buf.shape[1]
    tm = rows // 2
    slot = _gather_step(pl.program_id(0), pl.num_programs(0), y_hbm, pos_hbm, buf, idx, sem_rows, sem_idx)
    w0 = w0_ref[...]
    w1 = w1_ref[...]
    for c in range(D_MODEL // LANES):
        sl = slice(c * LANES, (c + 1) * LANES)
        o_ref[:, sl] = x1_ref[:, sl] + w0 * buf[slot, 0:tm, 0, sl] + w1 * buf[slot, tm:rows, 0, sl]


def _combine_call(x1, w0, w1, yr, pos_tiles, tm):
    t = x1.shape[0]
    rows = TOP_K * tm
    row = lambda w: pl.BlockSpec((tm, w), lambda i: (i, 0))
    return pl.pallas_call(
        _combine_kernel,
        grid=(t // tm,),
        in_specs=[row(D_MODEL), row(LANES), row(LANES),
                  pl.BlockSpec(memory_space=pl.ANY), pl.BlockSpec(memory_space=pl.ANY)],
        out_specs=row(D_MODEL),
        out_shape=jax.ShapeDtypeStruct((t, D_MODEL), F32),
        scratch_shapes=[pltpu.VMEM((2, rows, 1, D_MODEL), F32), pltpu.SMEM((2 * rows,), jnp.int32),
                        pltpu.SemaphoreType.DMA((2,)), pltpu.SemaphoreType.DMA((2,))],
        compiler_params=_cparams(("arbitrary",)),
        name="combine",
    )(x1, w0, w1, yr, pos_tiles)


def _rot_cols(w):
    half = ROPE_DIM // 2
    return jnp.concatenate([-w[..., half:], w[..., :half]], axis=-1)


def _head_pad(nope, rope):
    z = jnp.zeros(rope.shape[:-1] + (LANES - B_QK_DIM,), rope.dtype)
    out = jnp.concatenate([nope, rope, z], axis=-1)
    return out.reshape(out.shape[:-2] + (out.shape[-2] * LANES,))


def _t5_bucket(dist):
    max_exact = REL_BUCKETS // 2
    d = jnp.maximum(dist, 0)
    ratio = jnp.maximum(d, 1).astype(F32) / max_exact
    large = max_exact + (jnp.log(ratio) / math.log(REL_MAX_DIST / max_exact)
                         * (REL_BUCKETS - max_exact)).astype(jnp.int32)
    large = jnp.minimum(large, REL_BUCKETS - 1)
    return jnp.where(d < max_exact, d, large)


def _swa_bias(rel_bias, tq):
    qi = jnp.arange(tq)[:, None]
    kj = jnp.arange(2 * tq)[None, :]
    dist = qi + tq - kj
    onehot = (_t5_bucket(dist)[:, :, None] == jnp.arange(REL_BUCKETS)[None, None, :]).astype(F32)
    bias = jnp.einsum("qkb,bh->hqk", onehot, rel_bias.astype(F32), precision=lax.Precision.HIGHEST) * LOG2E
    mask = (dist >= 0) & (dist < WINDOW)
    bias = jnp.where(mask[None], bias, NEG)
    return bias.reshape(A_KV_HEADS, (A_HEADS // A_KV_HEADS) * tq, 2 * tq)


def _rope_tables(seq):
    half = ROPE_DIM // 2
    inv_freq = ROPE_THETA ** (-jnp.arange(half, dtype=F32) / half)
    ang = jnp.arange(seq, dtype=F32)[:, None] * inv_freq[None, :]
    cos = jnp.concatenate([jnp.cos(ang), jnp.cos(ang)], axis=-1)
    sin = jnp.concatenate([jnp.sin(ang), jnp.sin(ang)], axis=-1)
    return cos, sin


def _lane_table(nope_part, rope_part):
    seq = rope_part.shape[0]
    nope_part = jnp.broadcast_to(nope_part, (seq, NOPE_DIM))
    return jnp.concatenate([nope_part, rope_part, jnp.zeros((seq, LANES - B_QK_DIM), F32)], axis=-1)


def _prep_layer(l, seq, cos, sin, attn_norm, w_in, b_gate, a_q_norm, a_k_norm, w_oa, q_a_norm, w_qb, kv_a_norm,
                w_kvb, b_q_norm, b_k_norm, w_ob, w_o, ffn_norm, w_rg, b_rg, w_re, b_re, w_eg, w_eu, w_ed):
    w = w_in[l]
    o = 0
    parts = []
    for sz in (A_WIDTH, A_KV_HEADS * A_HEAD_DIM, A_KV_HEADS * A_HEAD_DIM, Q_LORA, KV_LORA, ROPE_DIM, 2 * D_MODEL):
        parts.append(w[:, o:o + sz])
        o += sz
    wqa, wka, wva, wcq, wckv, wkpe, wg = parts
    perm = jnp.array([c + (A_HEADS // 2) * h for c in range(A_HEADS // 2) for h in range(2)])
    wqa = wqa.reshape(D_MODEL, A_HEADS, A_HEAD_DIM)[:, perm].reshape(D_MODEL, A_WIDTH)
    zl = jnp.zeros((D_MODEL, NOPE_DIM), F32)
    zr = jnp.zeros((D_MODEL, LANES - B_QK_DIM), F32)
    w1 = jnp.concatenate([wqa, wka, wva, wcq, wckv, zl, wkpe, zr, zl, _rot_cols(wkpe), zr, wg], axis=1).astype(BF16)

    wq = w_qb[l].reshape(Q_LORA, B_HEADS, B_QK_DIM)
    zq = jnp.zeros((Q_LORA, B_HEADS, NOPE_DIM), F32)
    wq2 = jnp.concatenate([_head_pad(wq[..., :NOPE_DIM], wq[..., NOPE_DIM:]),
                           _head_pad(zq, _rot_cols(wq[..., NOPE_DIM:]))], axis=1).astype(BF16)
    wkv = w_kvb[l].reshape(KV_LORA, B_HEADS, NOPE_DIM + B_V_DIM)
    wk2 = _head_pad(wkv[..., :NOPE_DIM], jnp.zeros((KV_LORA, B_HEADS, ROPE_DIM), F32)).astype(BF16)
    wv = wkv[..., NOPE_DIM:]
    zv = jnp.zeros_like(wv)
    wv2 = jnp.where((jnp.arange(B_HEADS) % 2 == 0)[None, :, None], jnp.concatenate([wv, zv], -1),
                    jnp.concatenate([zv, wv], -1)).reshape(KV_LORA, B_HEADS * LANES).astype(BF16)
    one_lane = jnp.where(jnp.arange(B_HEADS) % 2 == 0, V_ONE_EVEN, V_ONE_ODD)
    vone = (jnp.arange(LANES)[None, :] == one_lane[:, None]).astype(F32).reshape(1, B_HEADS * LANES)

    def rope_tabs(gain, scale):
        gn, gr = gain[:NOPE_DIM], gain[NOPE_DIM:]
        gr_rot = jnp.concatenate([gr[ROPE_DIM // 2:], gr[:ROPE_DIM // 2]])
        a = _lane_table(gn[None, :] * scale, gr[None, :] * cos * scale)
        b = _lane_table(jnp.zeros((1, NOPE_DIM), F32), gr_rot[None, :] * sin * scale)
        return a, b

    aq, bq = rope_tabs(b_q_norm[l], B_QK_DIM ** -0.5 * LOG2E)
    score_bound = math.sqrt(B_QK_DIM) * LOG2E * jnp.max(jnp.abs(b_q_norm[l])) * jnp.max(jnp.abs(b_k_norm[l]))
    ak, bk = rope_tabs(b_k_norm[l], 1.0)

    gqa = (jnp.tile(a_q_norm[l], 2) * (A_HEAD_DIM ** -0.5 * LOG2E))[None, :]
    gka = jnp.tile(a_k_norm[l], 2)[None, :]
    woa = w_oa[l].reshape(A_HEADS, A_HEAD_DIM, D_MODEL)[perm].reshape(A_WIDTH, D_MODEL).astype(BF16)

    wr = jnp.concatenate([w_rg[l], w_re[l], jnp.zeros((D_MODEL, LANES - N_GROUPS - N_EXPERTS), F32)], axis=1)
    wrh = wr.astype(BF16)
    wrl = (wr - wrh.astype(F32)).astype(BF16)
    br = jnp.concatenate([b_rg[l], b_re[l], jnp.zeros((LANES - N_GROUPS - N_EXPERTS,), F32)])[None, :]

    return dict(
        gn=attn_norm[l][None, :], w1=w1, wq2=wq2, wk2=wk2, wv2=wv2, vone=vone, gcq=q_a_norm[l][None, :],
        gckv=kv_a_norm[l][None, :], score_bound=score_bound, bg=b_gate[l][None, :], gqa=gqa, gka=gka,
        aq=aq, bq=bq, ak=ak, bk=bk,
        woa=woa, wob=w_ob[l].astype(BF16), wo=w_o[l].astype(BF16), gf=ffn_norm[l][None, :], wr=jnp.concatenate([wrh, wrl], axis=1), br=br,
        wgu=jnp.concatenate([w_eg[l], w_eu[l]], axis=-1).astype(BF16), wd=w_ed[l].astype(BF16))


def _tril(n):
    return (jnp.arange(n)[:, None] >= jnp.arange(n)[None, :]).astype(jnp.int32)


def _dispatch(route, tile_cnt, rows):
    t = route.shape[0]
    nt = tile_cnt.shape[0]
    tile_cnt = tile_cnt[:, 0, :N_EXPERTS]
    counts = jnp.sum(tile_cnt, axis=0)
    padded = ((counts + rows - 1) // rows) * rows
    pends = jnp.dot(_tril(N_EXPERTS), padded)
    pstarts = pends - padded
    tile_base = pstarts[None, :] + jnp.dot(_tril(nt), tile_cnt) - tile_cnt
    base = jnp.broadcast_to(tile_base[:, None, None, :], (nt, t // nt, 1, N_EXPERTS)).reshape(t, 1, N_EXPERTS)
    onehot = route[:, :TOP_K, None] == jnp.arange(N_EXPERTS, dtype=jnp.int32)[None, None, :]
    dest = jnp.sum(jnp.where(onehot, base, 0), axis=-1) + route[:, TOP_K:2 * TOP_K]
    n_rows = t * TOP_K + N_EXPERTS * rows
    n_blocks = n_rows // rows
    blk_start = jnp.arange(n_blocks, dtype=jnp.int32)[:, None] * rows
    blk_exp = jnp.minimum(jnp.sum((pends[None, :] <= blk_start).astype(jnp.int32), axis=1), N_EXPERTS - 1)
    nblk = (pends[-1] // rows).astype(jnp.int32).reshape(1)
    nvalid = jnp.clip((pstarts + counts)[blk_exp] - blk_start[:, 0], 0, rows).astype(jnp.int32)
    last_blk = jnp.where(padded > 0, pends // rows - 1, -1)
    trailing = nblk[0] + jnp.arange(N_EXPERTS, dtype=jnp.int32)
    zero_blk = jnp.concatenate([last_blk, jnp.where(trailing < n_blocks, trailing, -1)]).astype(jnp.int32)
    return dest, blk_exp, nblk, nvalid, zero_blk, n_rows


def _swa_tables(sinks, rel_bias, a_q_norm, a_k_norm):
    per_group = A_HEADS // A_KV_HEADS
    tab = jnp.broadcast_to((sinks * LOG2E).reshape(A_KV_HEADS, per_group, 1, 1),
                           (A_KV_HEADS, per_group, SWA_TQ, LANES)).reshape(A_KV_HEADS, per_group * SWA_TQ, LANES)
    qk = math.sqrt(A_HEAD_DIM) * jnp.max(jnp.abs(a_q_norm)) * jnp.max(jnp.abs(a_k_norm))
    bound = LOG2E * jnp.maximum(qk + jnp.max(jnp.abs(rel_bias)), jnp.max(jnp.abs(sinks)))
    return tab, bound


def _pos_tiles(dest, tm):
    t = dest.shape[0]
    return dest.reshape(t // tm, tm, TOP_K).transpose(0, 2, 1).reshape(t * TOP_K)


def _layer(x2d, bsz, seq, p, sink_tab, swa_bound, bias):
    qa, ka, va, qb, kb, vb, gates = _proj_call(x2d, seq, p)
    oa = _swa_call(qa, ka, va, sink_tab, bias, bsz, seq, swa_bound)
    ob = _mla_call(qb, kb, vb, bsz, seq, p["score_bound"])
    x1, w0, w1, route, tile_cnt = _merge_call(x2d, oa, ob, gates, p)
    dest, blk_exp, nblk, nvalid, zero_blk, n_rows = _dispatch(route, tile_cnt, MOE_ROWS)
    pos_tiles = _pos_tiles(dest, COMB_TM)
    xr = _scatter_call(x1, pos_tiles, zero_blk, p["gf"], n_rows, COMB_TM, MOE_ROWS)
    yr = _expert_call(xr, MOE_ROWS, blk_exp, nblk, nvalid, p["wgu"], p["wd"])
    return _combine_call(x1, w0, w1, yr, pos_tiles, COMB_TM)


def kernel(x, attn_norm, w_in, b_gate, a_q_norm, a_k_norm, a_sinks, rel_bias, w_oa, q_a_norm, w_qb, kv_a_norm, w_kvb, b_q_norm, b_k_norm, w_ob, w_o, ffn_norm, w_router_group, b_router_group, w_router_expert, b_router_expert, w_exp_gate, w_exp_up, w_exp_down):
    bsz, seq, d = x.shape
    assert d == D_MODEL and seq % MLA_TQ == 0 and seq % PROJ_TM == 0 and (bsz * seq) % COMB_TM == 0
    depth = w_in.shape[0]
    cos, sin = _rope_tables(seq)
    bias = _swa_bias(rel_bias, SWA_TQ)
    x2d = x.reshape(bsz * seq, d)
    for l in range(depth):
        p = _prep_layer(l, seq, cos, sin, attn_norm, w_in, b_gate, a_q_norm, a_k_norm, w_oa, q_a_norm, w_qb,
                        kv_a_norm, w_kvb, b_q_norm, b_k_norm, w_ob, w_o, ffn_norm, w_router_group,
                        b_router_group, w_router_expert, b_router_expert, w_exp_gate, w_exp_up, w_exp_down)
        sink_tab, swa_bound = _swa_tables(a_sinks[l], rel_bias, a_q_norm[l], a_k_norm[l])
        x2d = _layer(x2d, bsz, seq, p, sink_tab, swa_bound, bias)
    return x2d.reshape(bsz, seq, d)
```

```python
import functools
import math

import jax
import jax.numpy as jnp
from jax import lax
from jax.experimental import pallas as pl
from jax.experimental.pallas import tpu as pltpu

F32 = jnp.float32
BF16 = jnp.bfloat16

D_MODEL = 1024
A_HEADS = 8
A_KV_HEADS = 2
A_HEAD_DIM = 64
A_WIDTH = A_HEADS * A_HEAD_DIM
WINDOW = 128
REL_BUCKETS = 32
REL_MAX_DIST = 128
B_HEADS = 8
Q_LORA = 256
KV_LORA = 128
NOPE_DIM = 64
ROPE_DIM = 32
B_QK_DIM = NOPE_DIM + ROPE_DIM
B_V_DIM = 64
B_WIDTH = B_HEADS * B_V_DIM
ROPE_THETA = 10000.0
N_GROUPS = 4
EXPERTS_PER_GROUP = 8
N_EXPERTS = N_GROUPS * EXPERTS_PER_GROUP
TOP_K = 2
D_EXPERT = 256
EPS = 1e-6

LANES = 128
NEG = -1e30
LOG2E = math.log2(math.e)
V_ONE_EVEN = B_V_DIM
V_ONE_ODD = 0
EXP2_SAFE_BOUND = 100.0
VMEM_LIMIT = 56 * 1024 * 1024

C_QA = 0
C_KA = C_QA + A_WIDTH
C_VA = C_KA + LANES
C_CQ = C_VA + LANES
C_CKV = C_CQ + Q_LORA
C_KPE = C_CKV + KV_LORA
C_KPR = C_KPE + LANES
C_GATE = C_KPR + LANES
C_END = C_GATE + 2 * D_MODEL

PROJ_TM = 512
SWA_TQ = 128
SWA_SUB = 4
MLA_TQ = 1024
MLA_TK = 512
MERGE_TM = 512
MOE_ROWS = 512
COMB_TM = 256


def _cparams(sem):
    return pltpu.CompilerParams(dimension_semantics=sem, vmem_limit_bytes=VMEM_LIMIT)


def _proj_kernel(x_ref, gn_ref, w1_ref, wq2_ref, wk2_ref, wv2_ref, vone_ref, gcq_ref, gckv_ref, bg_ref,
                 gqa_ref, gka_ref, aq_ref, bq_ref, ak_ref, bk_ref,
                 qa_ref, ka_ref, va_ref, qb_ref, kb_ref, vb_ref, gate_ref):
    tm = x_ref.shape[0]
    x = x_ref[...]
    h = (x * lax.rsqrt(jnp.mean(x * x, axis=-1, keepdims=True) + EPS) * gn_ref[...]).astype(BF16)

    def proj(a, b):
        return jnp.dot(h, w1_ref[:, a:b], preferred_element_type=F32)

    lo = lax.broadcasted_iota(jnp.int32, (tm, LANES), 1) < A_HEAD_DIM

    def pair_norm(t, gain):
        t2 = t * t
        s_lo = jnp.sum(jnp.where(lo, t2, 0.0), axis=-1, keepdims=True)
        s_hi = jnp.sum(jnp.where(lo, 0.0, t2), axis=-1, keepdims=True)
        r = jnp.where(lo, lax.rsqrt(s_lo / A_HEAD_DIM + EPS), lax.rsqrt(s_hi / A_HEAD_DIM + EPS))
        return (t * r * gain).astype(BF16)

    qa = proj(C_QA, C_KA)
    for c in range(A_WIDTH // LANES):
        sl = slice(c * LANES, (c + 1) * LANES)
        qa_ref[:, sl] = pair_norm(qa[:, sl], gqa_ref[...])
    kvp = proj(C_KA, C_CQ)
    ka_ref[...] = pair_norm(kvp[:, :LANES], gka_ref[...])
    va_ref[...] = kvp[:, LANES:].astype(BF16)

    lat = proj(C_CQ, C_GATE)
    cq = lat[:, :Q_LORA]
    cqn = (cq * lax.rsqrt(jnp.mean(cq * cq, axis=-1, keepdims=True) + EPS) * gcq_ref[...]).astype(BF16)
    ckv = lat[:, Q_LORA:Q_LORA + KV_LORA]
    ckvn = (ckv * lax.rsqrt(jnp.mean(ckv * ckv, axis=-1, keepdims=True) + EPS) * gckv_ref[...]).astype(BF16)
    kpe = lat[:, C_KPE - C_CQ:C_KPR - C_CQ]
    kpr = lat[:, C_KPR - C_CQ:C_GATE - C_CQ]

    aq, bq, ak, bk = aq_ref[...], bq_ref[...], ak_ref[...], bk_ref[...]
    q2 = jnp.dot(cqn, wq2_ref[...], preferred_element_type=F32)
    k2 = jnp.dot(ckvn, wk2_ref[...], preferred_element_type=F32)
    vb_ref[...] = (jnp.dot(ckvn, wv2_ref[...], preferred_element_type=F32) + vone_ref[...]).astype(BF16)
    kr = kpr * bk
    for hd in range(B_HEADS):
        sl = slice(hd * LANES, (hd + 1) * LANES)
        qp = q2[:, sl]
        qr = q2[:, B_HEADS * LANES + hd * LANES:B_HEADS * LANES + (hd + 1) * LANES]
        rq = lax.rsqrt(jnp.sum(qp * qp, axis=-1, keepdims=True) / B_QK_DIM + EPS)
        qb_ref[:, sl] = (rq * (qp * aq + qr * bq)).astype(BF16)
        kp = k2[:, sl] + kpe
        rk = lax.rsqrt(jnp.sum(kp * kp, axis=-1, keepdims=True) / B_QK_DIM + EPS)
        kb_ref[:, sl] = (rk * (kp * ak + kr)).astype(BF16)

    z = proj(C_GATE, C_END) + bg_ref[...]
    gate_ref[...] = (0.5 * jnp.tanh(0.5 * z) + 0.5).astype(BF16)


def _proj_call(x2d, seq, p):
    t = x2d.shape[0]
    tm = PROJ_TM
    nt = t // tm
    npos = seq // tm
    row = lambda w: pl.BlockSpec((tm, w), lambda i: (i, 0))
    full = lambda a: pl.BlockSpec(a.shape, lambda i: (0,) * a.ndim)
    pos = pl.BlockSpec((tm, LANES), lambda i: (i % npos, 0))
    consts = [p["gn"], p["w1"], p["wq2"], p["wk2"], p["wv2"], p["vone"], p["gcq"], p["gckv"], p["bg"], p["gqa"],
              p["gka"]]
    tabs = [p["aq"], p["bq"], p["ak"], p["bk"]]
    widths = [A_WIDTH, LANES, LANES, B_HEADS * LANES, B_HEADS * LANES, B_HEADS * LANES, 2 * D_MODEL]
    return pl.pallas_call(
        _proj_kernel,
        grid=(nt,),
        in_specs=[row(D_MODEL)] + [full(a) for a in consts] + [pos] * 4,
        out_specs=[row(w) for w in widths],
        out_shape=[jax.ShapeDtypeStruct((t, w), BF16) for w in widths],
        compiler_params=_cparams(("parallel",)),
        name="proj",
    )(x2d, *consts, *tabs)


def _swa_kernel(q_ref, kc_ref, kp_ref, vc_ref, vp_ref, bias_ref, sink_ref, o_ref, *, shift):
    tq = SWA_TQ
    groups = A_WIDTH // LANES
    first = pl.program_id(1) == 0
    one_col = (lax.broadcasted_iota(jnp.int32, (2 * tq, LANES), 1) == 0).astype(BF16)
    lo = lax.broadcasted_iota(jnp.int32, (tq, LANES), 1) < A_HEAD_DIM
    no_prev = jnp.logical_and(first, lax.broadcasted_iota(jnp.int32, (groups * tq, 2 * tq), 1) < tq)
    for sub in range(SWA_SUB):
        if sub == 0:
            k = jnp.concatenate([kp_ref[...], kc_ref[0:tq, :]], axis=0)
            v = jnp.concatenate([vp_ref[...], vc_ref[0:tq, :]], axis=0)
        else:
            k = kc_ref[(sub - 1) * tq:(sub + 1) * tq, :]
            v = vc_ref[(sub - 1) * tq:(sub + 1) * tq, :]
        v_ext = jnp.concatenate([v, one_col], axis=1)
        qrows = slice(sub * tq, (sub + 1) * tq)
        qs = [q_ref[qrows, c * LANES:(c + 1) * LANES] for c in range(groups)]
        res = []
        for g in range(A_KV_HEADS):
            keep = lo if g == 0 else jnp.logical_not(lo)
            qg = jnp.concatenate([jnp.where(keep, qc, jnp.zeros_like(qc)) for qc in qs], axis=0)
            s = lax.dot_general(qg, k, (((1,), (1,)), ((), ())), preferred_element_type=F32) + bias_ref[g]
            sink = sink_ref[g][:, :1]
            if shift:
                if sub == 0:
                    s = jnp.where(no_prev, NEG, s)
                m = jnp.maximum(jnp.max(s, axis=-1, keepdims=True), sink)
                e = jnp.exp2(s - m)
                sink_term = jnp.exp2(sink - m)
            else:
                e = jnp.exp2(s)
                if sub == 0:
                    e = jnp.where(no_prev, 0.0, e)
                sink_term = jnp.exp2(sink)
            pv = jnp.dot(e.astype(BF16), v_ext, preferred_element_type=F32)
            res.append(pv[:, :LANES] * (1.0 / (pv[:, LANES:LANES + 1] + sink_term)))
        for c in range(groups):
            rows = slice(c * tq, (c + 1) * tq)
            o_ref[qrows, c * LANES:(c + 1) * LANES] = jnp.where(lo, res[0][rows], res[1][rows]).astype(BF16)


def _swa_call(qa, ka, va, sink_tab, bias, bsz, seq, score_bound):
    t = qa.shape[0]
    tq = SWA_TQ
    step = SWA_SUB * tq
    ns = seq // step
    cur = lambda b, i: (b * ns + i, 0)
    prev = lambda b, i: (b * (seq // tq) + jnp.maximum(SWA_SUB * i - 1, 0), 0)
    full = lambda a: pl.BlockSpec(a.shape, lambda b, i: (0,) * a.ndim)

    def call(shift):
        return pl.pallas_call(
            functools.partial(_swa_kernel, shift=shift),
            grid=(bsz, ns),
            in_specs=[pl.BlockSpec((step, A_WIDTH), cur),
                      pl.BlockSpec((step, LANES), cur), pl.BlockSpec((tq, LANES), prev),
                      pl.BlockSpec((step, LANES), cur), pl.BlockSpec((tq, LANES), prev),
                      full(bias), full(sink_tab)],
            out_specs=pl.BlockSpec((step, A_WIDTH), cur),
            out_shape=jax.ShapeDtypeStruct((t, A_WIDTH), BF16),
            compiler_params=_cparams(("parallel", "parallel")),
            name="swa_shifted" if shift else "swa",
        )(qa, ka, ka, va, va, bias, sink_tab)

    return lax.cond(score_bound <= EXP2_SAFE_BOUND, lambda: call(False), lambda: call(True))


def _mla_kernel(q_ref, k_ref, v_ref, o_ref, acc_sc, m_sc, *, shift):
    tq = q_ref.shape[0]
    tk = MLA_TK
    per_tile = tq // tk
    qi = pl.program_id(2)
    acc_sc[...] = jnp.zeros(acc_sc.shape, F32)
    if shift:
        m_sc[...] = jnp.full(m_sc.shape, NEG, F32)

    def tile(j, r0, r1, masked):
        keys = pl.ds(pl.multiple_of(j * tk, tk), tk)
        if masked:
            causal = (lax.broadcasted_iota(jnp.int32, (r1 - r0, tk), 1)
                      <= lax.broadcasted_iota(jnp.int32, (r1 - r0, tk), 0))
        for half in range(2):
            hs = slice(half * LANES, (half + 1) * LANES)
            s = lax.dot_general(q_ref[r0:r1, hs], k_ref[keys, hs], (((1,), (1,)), ((), ())),
                                preferred_element_type=F32)
            if shift:
                if masked:
                    s = jnp.where(causal, s, NEG)
                m_prev = m_sc[half, r0:r1]
                m_new = jnp.maximum(m_prev, jnp.max(s, axis=-1, keepdims=True))
                m_sc[half, r0:r1] = m_new
                e = jnp.exp2(s - m_new)
                acc_sc[half, r0:r1] = jnp.exp2(m_prev - m_new) * acc_sc[half, r0:r1]
            else:
                e = jnp.exp2(s)
                if masked:
                    e = jnp.where(causal, e, 0.0)
            acc_sc[half, r0:r1] += jnp.dot(e.astype(BF16), v_ref[keys, hs], preferred_element_type=F32)

    def body(j, carry):
        tile(j, 0, tq, False)
        return carry

    lax.fori_loop(0, qi * per_tile, body, 0)
    for d in range(per_tile):
        tile(qi * per_tile + d, d * tk, (d + 1) * tk, True)
        if (d + 1) * tk < tq:
            tile(qi * per_tile + d, (d + 1) * tk, tq, False)
    a0 = acc_sc[0]
    a1 = acc_sc[1]
    lo = lax.broadcasted_iota(jnp.int32, (tq, LANES), 1) < B_V_DIM
    o_ref[...] = jnp.where(lo, a0 * (1.0 / a0[:, V_ONE_EVEN:V_ONE_EVEN + 1]),
                           a1 * (1.0 / a1[:, V_ONE_ODD:V_ONE_ODD + 1])).astype(BF16)


def _mla_call(qb, kb, vb, bsz, seq, score_bound):
    t = qb.shape[0]
    tq = MLA_TQ
    nq = seq // tq
    pairs = B_HEADS // 2

    def call(shift):
        return pl.pallas_call(
            functools.partial(_mla_kernel, shift=shift),
            grid=(bsz, pairs, nq),
            in_specs=[pl.BlockSpec((tq, 2 * LANES), lambda b, p, i: (b * nq + i, p)),
                      pl.BlockSpec((seq, 2 * LANES), lambda b, p, i: (b, p)),
                      pl.BlockSpec((seq, 2 * LANES), lambda b, p, i: (b, p))],
            out_specs=pl.BlockSpec((tq, LANES), lambda b, p, i: (b * nq + i, p)),
            out_shape=jax.ShapeDtypeStruct((t, B_WIDTH), BF16),
            scratch_shapes=[pltpu.VMEM((2, tq, LANES), F32), pltpu.VMEM((2, tq, 1), F32)],
            compiler_params=_cparams(("parallel", "parallel", "arbitrary")),
            name="mla_shifted" if shift else "mla",
        )(qb, kb, vb)

    return lax.cond(score_bound <= EXP2_SAFE_BOUND, lambda: call(False), lambda: call(True))


def _merge_kernel(x_ref, oa_ref, ob_ref, gate_ref, woa_ref, wob_ref, wo_ref, gf_ref, wr_ref, br_ref,
                  tri_ref, x1_ref, w0_ref, w1_ref, eid_ref, cnt_ref):
    tm = x_ref.shape[0]
    ya = jnp.dot(oa_ref[...], woa_ref[...], preferred_element_type=F32)
    yb = jnp.dot(ob_ref[...], wob_ref[...], preferred_element_type=F32)
    g = gate_ref[...].astype(F32)
    mix = (g[:, :D_MODEL] * ya + g[:, D_MODEL:] * yb).astype(BF16)
    x1 = x_ref[...] + jnp.dot(mix, wo_ref[...], preferred_element_type=F32)
    x1_ref[...] = x1

    hn = x1 * lax.rsqrt(jnp.mean(x1 * x1, axis=-1, keepdims=True) + EPS) * gf_ref[...]
    hh = hn.astype(BF16)
    hl = (hn - hh.astype(F32)).astype(BF16)
    ph = jnp.dot(hh, wr_ref[...], preferred_element_type=F32)
    pl_ = jnp.dot(hl, wr_ref[...], preferred_element_type=F32)
    logits = (ph[:, :LANES] + ph[:, LANES:]) + (pl_[:, :LANES] + pl_[:, LANES:]) + br_ref[...]

    lane = lax.broadcasted_iota(jnp.int32, (tm, LANES), 1).astype(F32)
    big = float(LANES)
    gmask = lane < N_GROUPS
    gl = jnp.where(gmask, logits, NEG)
    gmax = jnp.max(gl, axis=-1, keepdims=True)
    gsum = jnp.sum(jnp.where(gmask, jnp.exp(gl - gmax), 0.0), axis=-1, keepdims=True)
    g_p = 1.0 / gsum
    g_idx = jnp.min(jnp.where(gl == gmax, lane, big), axis=-1, keepdims=True)
    e_lo = N_GROUPS + EXPERTS_PER_GROUP * g_idx
    emask = jnp.logical_and(lane >= e_lo, lane < e_lo + EXPERTS_PER_GROUP)
    el = jnp.where(emask, logits, NEG)
    t1 = jnp.max(el, axis=-1, keepdims=True)
    i1 = jnp.min(jnp.where(el == t1, lane, big), axis=-1, keepdims=True)
    el2 = jnp.where(lane == i1, NEG, el)
    t2 = jnp.max(el2, axis=-1, keepdims=True)
    i2 = jnp.min(jnp.where(el2 == t2, lane, big), axis=-1, keepdims=True)
    e2 = jnp.exp(t2 - t1)
    w_first = g_p / (1.0 + e2)
    w0_ref[...] = jnp.broadcast_to(w_first, (tm, LANES))
    w1_ref[...] = jnp.broadcast_to(w_first * e2, (tm, LANES))
    ex1 = i1 - N_GROUPS
    ex2 = i2 - N_GROUPS

    oh1 = lane == ex1
    oh2 = lane == ex2
    cnt = (oh1.astype(F32) + oh2.astype(F32)).astype(BF16)
    before = jnp.dot(tri_ref[...], cnt, preferred_element_type=F32)
    r1 = jnp.sum(jnp.where(oh1, before, 0.0), axis=-1, keepdims=True)
    r2 = jnp.sum(jnp.where(oh2, before, 0.0), axis=-1, keepdims=True)
    ids = jnp.where(lane == 0.0, ex1, jnp.where(lane == 1.0, ex2, jnp.where(lane == 2.0, r1,
                                                                             jnp.where(lane == 3.0, r2, 0.0))))
    eid_ref[...] = ids.astype(jnp.int32)
    tile_cnt = jnp.sum(cnt.astype(F32), axis=0, keepdims=True)
    cnt_ref[...] = jnp.broadcast_to(tile_cnt, cnt_ref.shape[1:]).astype(jnp.int32)[None]


def _merge_call(x2d, oa, ob, gates, p):
    t = x2d.shape[0]
    tm = MERGE_TM
    nt = t // tm
    row = lambda w: pl.BlockSpec((tm, w), lambda i: (i, 0))
    full = lambda a: pl.BlockSpec(a.shape, lambda i: (0,) * a.ndim)
    tri = (jnp.arange(tm)[:, None] > jnp.arange(tm)[None, :]).astype(BF16)
    consts = [p["woa"], p["wob"], p["wo"], p["gf"], p["wr"], p["br"], tri]
    return pl.pallas_call(
        _merge_kernel,
        grid=(nt,),
        in_specs=[row(D_MODEL), row(A_WIDTH), row(B_WIDTH), row(2 * D_MODEL)] + [full(a) for a in consts],
        out_specs=[row(D_MODEL), row(LANES), row(LANES), row(LANES),
                   pl.BlockSpec((1, 8, LANES), lambda i: (i, 0, 0))],
        out_shape=[jax.ShapeDtypeStruct((t, D_MODEL), F32), jax.ShapeDtypeStruct((t, LANES), F32),
                   jax.ShapeDtypeStruct((t, LANES), F32), jax.ShapeDtypeStruct((t, LANES), jnp.int32),
                   jax.ShapeDtypeStruct((nt, 8, LANES), jnp.int32)],
        compiler_params=_cparams(("parallel",)),
        name="merge",
    )(x2d, oa, ob, gates, *consts)


def _rows_copy(src_hbm, row, dst, j, sem):
    return pltpu.make_async_copy(src_hbm.at[row], dst.at[j], sem)


def _start_rows(src_hbm, idx, base, dst, sem, n):
    for j in range(n):
        _rows_copy(src_hbm, idx[base + j], dst, j, sem).start(priority=j % 2)


WAIT_ROWS = 128


def _wait_rows(src_hbm, dst, sem, n):
    for c in range(n // WAIT_ROWS):
        rows = pl.ds(c * WAIT_ROWS, WAIT_ROWS)
        pltpu.make_async_copy(src_hbm.at[rows], dst.at[rows], sem).wait()


def _gather_step(i, n, src_hbm, idx_hbm, buf, idx, sem_rows, sem_idx):
    rows = buf.shape[1]
    slot = i % 2
    nslot = 1 - slot

    def idx_copy(b, s):
        return pltpu.make_async_copy(idx_hbm.at[pl.ds(b * rows, rows)], idx.at[pl.ds(s * rows, rows)], sem_idx.at[s])

    @pl.when(jnp.logical_and(i == 0, n > 0))
    def _():
        idx_copy(0, 0).start()
        idx_copy(0, 0).wait()
        _start_rows(src_hbm, idx, 0, buf.at[0], sem_rows.at[0], rows)

        @pl.when(n > 1)
        def _():
            idx_copy(1, 1).start()

    @pl.when(i + 1 < n)
    def _():
        idx_copy(i + 1, nslot).wait()
        _start_rows(src_hbm, idx, nslot * rows, buf.at[nslot], sem_rows.at[nslot], rows)

    @pl.when(i + 2 < n)
    def _():
        idx_copy(i + 2, slot).start()

    @pl.when(i < n)
    def _():
        _wait_rows(src_hbm, buf.at[slot], sem_rows.at[slot], rows)
    return slot


SCATTER_BUFS = 3


def _pack_bf16_pair(a, b):
    abits = lax.bitcast_convert_type(a.astype(BF16).astype(F32), jnp.uint32)
    bbits = lax.bitcast_convert_type(b.astype(BF16).astype(F32), jnp.uint32)
    return (abits >> 16) | (bbits & jnp.uint32(0xFFFF0000))


def _unpack_bf16_pair(w):
    lo = lax.bitcast_convert_type(w << 16, F32).astype(BF16)
    hi = lax.bitcast_convert_type(w & jnp.uint32(0xFFFF0000), F32).astype(BF16)
    return lo, hi


def _scatter_kernel(zero_blk_ref, x_hbm, dest_hbm, gf_ref, xr_hbm, buf, rbuf, idx, sem_x, sem_rows, sem_idx,
                    sem_zero, *, blk_rows):
    i = pl.program_id(0)
    n = pl.num_programs(0)
    tm = buf.shape[1]
    rows = TOP_K * tm
    islot = i % 2
    xslot = i % SCATTER_BUFS

    def idx_copy(b, s):
        return pltpu.make_async_copy(dest_hbm.at[pl.ds(b * rows, rows)], idx.at[pl.ds(s * rows, rows)], sem_idx.at[s])

    def x_copy(b, s):
        return pltpu.make_async_copy(x_hbm.at[pl.ds(b * tm, tm), :], buf.at[s], sem_x.at[s])

    def wait_rows(s):
        for k in range(TOP_K):
            pltpu.make_async_copy(rbuf.at[s], xr_hbm.at[pl.ds(0, tm)], sem_rows.at[s]).wait()

    @pl.when(i == 0)
    def _():
        idx_copy(0, 0).start()
        x_copy(0, 0).start()
        zsrc = rbuf.at[SCATTER_BUFS - 1]
        zsrc[...] = jnp.zeros(zsrc.shape, zsrc.dtype)

        def zero_copies(e):
            b = zero_blk_ref[e]
            return b >= 0, [pltpu.make_async_copy(zsrc, xr_hbm.at[pl.ds(b * blk_rows + h * tm, tm)], sem_zero)
                            for h in range(blk_rows // tm)]

        def start(e, carry):
            live, copies = zero_copies(e)

            @pl.when(live)
            def _():
                for c in copies:
                    c.start()
            return carry

        def wait(e, carry):
            live, copies = zero_copies(e)

            @pl.when(live)
            def _():
                for c in copies:
                    c.wait()
            return carry

        lax.fori_loop(0, zero_blk_ref.shape[0], start, 0)
        lax.fori_loop(0, zero_blk_ref.shape[0], wait, 0)

    @pl.when(i >= 2)
    def _():
        wait_rows((i - 2) % SCATTER_BUFS)

    @pl.when(i + 1 < n)
    def _():
        idx_copy(i + 1, 1 - islot).start()
        x_copy(i + 1, (i + 1) % SCATTER_BUFS).start()

    idx_copy(i, islot).wait()
    x_copy(i, xslot).wait()

    x = buf[xslot]
    hn = x * lax.rsqrt(jnp.mean(x * x, axis=-1, keepdims=True) + EPS) * gf_ref[...]
    rbuf[xslot, :, 0, :] = _pack_bf16_pair(hn[:, :D_MODEL // 2], hn[:, D_MODEL // 2:])
    for j in range(tm):
        src = rbuf.at[xslot, j]
        for k in range(TOP_K):
            dst = xr_hbm.at[idx[islot * rows + k * tm + j]]
            pltpu.make_async_copy(src, dst, sem_rows.at[xslot]).start(priority=k % 2)

    @pl.when(i == n - 1)
    def _():
        @pl.when(n > 1)
        def _():
            wait_rows((i - 1) % SCATTER_BUFS)
        wait_rows(xslot)


def _scatter_call(x1, pos_tiles, zero_blk, gf, n_rows, tm, blk_rows):
    t = x1.shape[0]
    rows = TOP_K * tm
    assert blk_rows % tm == 0
    any_spec = pl.BlockSpec(memory_space=pl.ANY)
    return pl.pallas_call(
        functools.partial(_scatter_kernel, blk_rows=blk_rows),
        grid_spec=pltpu.PrefetchScalarGridSpec(
            num_scalar_prefetch=1,
            grid=(t // tm,),
            in_specs=[any_spec, any_spec, pl.BlockSpec(gf.shape, lambda i, zb: (0, 0))],
            out_specs=any_spec,
            scratch_shapes=[pltpu.VMEM((SCATTER_BUFS, tm, D_MODEL), F32),
                            pltpu.VMEM((SCATTER_BUFS, tm, 1, D_MODEL // 2), jnp.uint32),
                            pltpu.SMEM((2 * rows,), jnp.int32),
                            pltpu.SemaphoreType.DMA((SCATTER_BUFS,)), pltpu.SemaphoreType.DMA((SCATTER_BUFS,)),
                            pltpu.SemaphoreType.DMA((2,)), pltpu.SemaphoreType.DMA]),
        out_shape=jax.ShapeDtypeStruct((n_rows, 1, D_MODEL // 2), jnp.uint32),
        compiler_params=_cparams(("arbitrary",)),
        name="scatter",
    )(zero_blk, x1, pos_tiles, gf)


def _expert_kernel(blk_exp_ref, nblk_ref, nvalid_ref, x_ref, wg_ref, wu_ref, wd_ref, y_ref):
    del blk_exp_ref
    i = pl.program_id(0)

    @pl.when(i < nblk_ref[0])
    def _():
        w = x_ref[:, 0, :]
        real = lax.broadcasted_iota(jnp.int32, w.shape, 0) < nvalid_ref[i]
        lo, hi = _unpack_bf16_pair(jnp.where(real, w, jnp.uint32(0)))
        half = D_MODEL // 2
        def up_proj(w_ref):
            return (jnp.dot(lo, w_ref[0, :half, :], preferred_element_type=F32)
                    + jnp.dot(hi, w_ref[0, half:, :], preferred_element_type=F32))

        gt, up = up_proj(wg_ref), up_proj(wu_ref)
        act = (gt * (0.5 * jnp.tanh(0.5 * gt) + 0.5) * up).astype(BF16)
        y = jnp.dot(act, wd_ref[0], preferred_element_type=F32)
        y_ref[:, 0, :] = _pack_bf16_pair(y[:, :half], y[:, half:])

    @pl.when(i >= nblk_ref[0])
    def _():
        y_ref[...] = jnp.zeros(y_ref.shape, y_ref.dtype)


def _expert_call(xr, rows, blk_exp, nblk, nvalid, wg, wu, wd):
    nb = xr.shape[0] // rows
    return pl.pallas_call(
        _expert_kernel,
        grid_spec=pltpu.PrefetchScalarGridSpec(
            num_scalar_prefetch=3,
            grid=(nb,),
            in_specs=[pl.BlockSpec((rows, 1, D_MODEL // 2), lambda i, be, n, nv: (jnp.minimum(i, n[0] - 1), 0, 0)),
                      pl.BlockSpec((1, D_MODEL, D_EXPERT), lambda i, be, n, nv: (be[i], 0, 0)),
                      pl.BlockSpec((1, D_MODEL, D_EXPERT), lambda i, be, n, nv: (be[i], 0, 0)),
                      pl.BlockSpec((1, D_EXPERT, D_MODEL), lambda i, be, n, nv: (be[i], 0, 0))],
            out_specs=pl.BlockSpec((rows, 1, D_MODEL // 2), lambda i, be, n, nv: (i, 0, 0))),
        out_shape=jax.ShapeDtypeStruct((nb * rows, 1, D_MODEL // 2), jnp.uint32),
        compiler_params=_cparams(("arbitrary",)),
        name="experts",
    )(blk_exp, nblk, nvalid, xr, wg, wu, wd)


def _combine_kernel(x1_ref, w0_ref, w1_ref, y_hbm, pos_hbm, o_ref, buf, idx, sem_rows, sem_idx):
    rows = buf.shape[1]
    tm = rows // 2
    slot = _gather_step(pl.program_id(0), pl.num_programs(0), y_hbm, pos_hbm, buf, idx, sem_rows, sem_idx)
    w0 = w0_ref[...]
    w1 = w1_ref[...]
    half = D_MODEL // 2
    for c in range(half // LANES):
        sl = slice(c * LANES, (c + 1) * LANES)
        g0 = buf[slot, 0:tm, 0, sl]
        g1 = buf[slot, tm:rows, 0, sl]
        for part, off in ((lambda g: g << 16, 0), (lambda g: g & jnp.uint32(0xFFFF0000), half)):
            so = slice(off + c * LANES, off + (c + 1) * LANES)
            o_ref[:, so] = (x1_ref[:, so] + w0 * lax.bitcast_convert_type(part(g0), F32)
                            + w1 * lax.bitcast_convert_type(part(g1), F32))


def _combine_call(x1, w0, w1, yr, pos_tiles, tm):
    t = x1.shape[0]
    rows = TOP_K * tm
    row = lambda w: pl.BlockSpec((tm, w), lambda i: (i, 0))
    return pl.pallas_call(
        _combine_kernel,
        grid=(t // tm,),
        in_specs=[row(D_MODEL), row(LANES), row(LANES),
                  pl.BlockSpec(memory_space=pl.ANY), pl.BlockSpec(memory_space=pl.ANY)],
        out_specs=row(D_MODEL),
        out_shape=jax.ShapeDtypeStruct((t, D_MODEL), F32),
        scratch_shapes=[pltpu.VMEM((2, rows, 1, D_MODEL // 2), jnp.uint32), pltpu.SMEM((2 * rows,), jnp.int32),
                        pltpu.SemaphoreType.DMA((2,)), pltpu.SemaphoreType.DMA((2,))],
        compiler_params=_cparams(("arbitrary",)),
        name="combine",
    )(x1, w0, w1, yr, pos_tiles)


def _rot_cols(w):
    half = ROPE_DIM // 2
    return jnp.concatenate([-w[..., half:], w[..., :half]], axis=-1)


def _head_pad(nope, rope):
    z = jnp.zeros(rope.shape[:-1] + (LANES - B_QK_DIM,), rope.dtype)
    out = jnp.concatenate([nope, rope, z], axis=-1)
    return out.reshape(out.shape[:-2] + (out.shape[-2] * LANES,))


def _t5_bucket(dist):
    max_exact = REL_BUCKETS // 2
    d = jnp.maximum(dist, 0)
    ratio = jnp.maximum(d, 1).astype(F32) / max_exact
    large = max_exact + (jnp.log(ratio) / math.log(REL_MAX_DIST / max_exact)
                         * (REL_BUCKETS - max_exact)).astype(jnp.int32)
    large = jnp.minimum(large, REL_BUCKETS - 1)
    return jnp.where(d < max_exact, d, large)


def _swa_bias(rel_bias, tq):
    qi = jnp.arange(tq)[:, None]
    kj = jnp.arange(2 * tq)[None, :]
    dist = qi + tq - kj
    onehot = (_t5_bucket(dist)[:, :, None] == jnp.arange(REL_BUCKETS)[None, None, :]).astype(F32)
    bias = jnp.einsum("qkb,bh->hqk", onehot, rel_bias.astype(F32), precision=lax.Precision.HIGHEST) * LOG2E
    mask = (dist >= 0) & (dist < WINDOW)
    bias = jnp.where(mask[None], bias, NEG)
    return bias.reshape(A_KV_HEADS, (A_HEADS // A_KV_HEADS) * tq, 2 * tq)


def _rope_tables(seq):
    half = ROPE_DIM // 2
    inv_freq = ROPE_THETA ** (-jnp.arange(half, dtype=F32) / half)
    ang = jnp.arange(seq, dtype=F32)[:, None] * inv_freq[None, :]
    cos = jnp.concatenate([jnp.cos(ang), jnp.cos(ang)], axis=-1)
    sin = jnp.concatenate([jnp.sin(ang), jnp.sin(ang)], axis=-1)
    return cos, sin


def _lane_table(nope_part, rope_part):
    seq = rope_part.shape[0]
    nope_part = jnp.broadcast_to(nope_part, (seq, NOPE_DIM))
    return jnp.concatenate([nope_part, rope_part, jnp.zeros((seq, LANES - B_QK_DIM), F32)], axis=-1)


def _prep_layer(l, seq, cos, sin, attn_norm, w_in, b_gate, a_q_norm, a_k_norm, w_oa, q_a_norm, w_qb, kv_a_norm,
                w_kvb, b_q_norm, b_k_norm, w_ob, w_o, ffn_norm, w_rg, b_rg, w_re, b_re, w_eg, w_eu, w_ed):
    w = w_in[l]
    o = 0
    parts = []
    for sz in (A_WIDTH, A_KV_HEADS * A_HEAD_DIM, A_KV_HEADS * A_HEAD_DIM, Q_LORA, KV_LORA, ROPE_DIM, 2 * D_MODEL):
        parts.append(w[:, o:o + sz])
        o += sz
    wqa, wka, wva, wcq, wckv, wkpe, wg = parts
    perm = jnp.array([c + (A_HEADS // 2) * h for c in range(A_HEADS // 2) for h in range(2)])
    wqa = wqa.reshape(D_MODEL, A_HEADS, A_HEAD_DIM)[:, perm].reshape(D_MODEL, A_WIDTH)
    zl = jnp.zeros((D_MODEL, NOPE_DIM), F32)
    zr = jnp.zeros((D_MODEL, LANES - B_QK_DIM), F32)
    w1 = jnp.concatenate([wqa, wka, wva, wcq, wckv, zl, wkpe, zr, zl, _rot_cols(wkpe), zr, wg], axis=1).astype(BF16)

    wq = w_qb[l].reshape(Q_LORA, B_HEADS, B_QK_DIM)
    zq = jnp.zeros((Q_LORA, B_HEADS, NOPE_DIM), F32)
    wq2 = jnp.concatenate([_head_pad(wq[..., :NOPE_DIM], wq[..., NOPE_DIM:]),
                           _head_pad(zq, _rot_cols(wq[..., NOPE_DIM:]))], axis=1).astype(BF16)
    wkv = w_kvb[l].reshape(KV_LORA, B_HEADS, NOPE_DIM + B_V_DIM)
    wk2 = _head_pad(wkv[..., :NOPE_DIM], jnp.zeros((KV_LORA, B_HEADS, ROPE_DIM), F32)).astype(BF16)
    wv = wkv[..., NOPE_DIM:]
    zv = jnp.zeros_like(wv)
    wv2 = jnp.where((jnp.arange(B_HEADS) % 2 == 0)[None, :, None], jnp.concatenate([wv, zv], -1),
                    jnp.concatenate([zv, wv], -1)).reshape(KV_LORA, B_HEADS * LANES).astype(BF16)
    one_lane = jnp.where(jnp.arange(B_HEADS) % 2 == 0, V_ONE_EVEN, V_ONE_ODD)
    vone = (jnp.arange(LANES)[None, :] == one_lane[:, None]).astype(F32).reshape(1, B_HEADS * LANES)

    def rope_tabs(gain, scale):
        gn, gr = gain[:NOPE_DIM], gain[NOPE_DIM:]
        gr_rot = jnp.concatenate([gr[ROPE_DIM // 2:], gr[:ROPE_DIM // 2]])
        a = _lane_table(gn[None, :] * scale, gr[None, :] * cos * scale)
        b = _lane_table(jnp.zeros((1, NOPE_DIM), F32), gr_rot[None, :] * sin * scale)
        return a, b

    aq, bq = rope_tabs(b_q_norm[l], B_QK_DIM ** -0.5 * LOG2E)
    score_bound = math.sqrt(B_QK_DIM) * LOG2E * jnp.max(jnp.abs(b_q_norm[l])) * jnp.max(jnp.abs(b_k_norm[l]))
    ak, bk = rope_tabs(b_k_norm[l], 1.0)

    gqa = (jnp.tile(a_q_norm[l], 2) * (A_HEAD_DIM ** -0.5 * LOG2E))[None, :]
    gka = jnp.tile(a_k_norm[l], 2)[None, :]
    woa = w_oa[l].reshape(A_HEADS, A_HEAD_DIM, D_MODEL)[perm].reshape(A_WIDTH, D_MODEL).astype(BF16)

    wr = jnp.concatenate([w_rg[l], w_re[l], jnp.zeros((D_MODEL, LANES - N_GROUPS - N_EXPERTS), F32)], axis=1)
    wrh = wr.astype(BF16)
    wrl = (wr - wrh.astype(F32)).astype(BF16)
    br = jnp.concatenate([b_rg[l], b_re[l], jnp.zeros((LANES - N_GROUPS - N_EXPERTS,), F32)])[None, :]

    return dict(
        gn=attn_norm[l][None, :], w1=w1, wq2=wq2, wk2=wk2, wv2=wv2, vone=vone, gcq=q_a_norm[l][None, :],
        gckv=kv_a_norm[l][None, :], score_bound=score_bound, bg=b_gate[l][None, :], gqa=gqa, gka=gka,
        aq=aq, bq=bq, ak=ak, bk=bk,
        woa=woa, wob=w_ob[l].astype(BF16), wo=w_o[l].astype(BF16), gf=ffn_norm[l][None, :], wr=jnp.concatenate([wrh, wrl], axis=1), br=br,
        wg=w_eg[l].astype(BF16), wu=w_eu[l].astype(BF16), wd=w_ed[l].astype(BF16))


def _tril(n):
    return (jnp.arange(n)[:, None] >= jnp.arange(n)[None, :]).astype(jnp.int32)


def _dispatch(route, tile_cnt, rows):
    t = route.shape[0]
    nt = tile_cnt.shape[0]
    tile_cnt = tile_cnt[:, 0, :N_EXPERTS]
    counts = jnp.sum(tile_cnt, axis=0)
    padded = ((counts + rows - 1) // rows) * rows
    pends = jnp.dot(_tril(N_EXPERTS), padded)
    pstarts = pends - padded
    tile_base = pstarts[None, :] + jnp.dot(_tril(nt), tile_cnt) - tile_cnt
    base = jnp.broadcast_to(tile_base[:, None, None, :], (nt, t // nt, 1, N_EXPERTS)).reshape(t, 1, N_EXPERTS)
    onehot = route[:, :TOP_K, None] == jnp.arange(N_EXPERTS, dtype=jnp.int32)[None, None, :]
    dest = jnp.sum(jnp.where(onehot, base, 0), axis=-1) + route[:, TOP_K:2 * TOP_K]
    n_rows = t * TOP_K + N_EXPERTS * rows
    n_blocks = n_rows // rows
    blk_start = jnp.arange(n_blocks, dtype=jnp.int32)[:, None] * rows
    blk_exp = jnp.minimum(jnp.sum((pends[None, :] <= blk_start).astype(jnp.int32), axis=1), N_EXPERTS - 1)
    nblk = (pends[-1] // rows).astype(jnp.int32).reshape(1)
    nvalid = jnp.clip((pstarts + counts)[blk_exp] - blk_start[:, 0], 0, rows).astype(jnp.int32)
    last_blk = jnp.where(padded > 0, pends // rows - 1, -1)
    trailing = nblk[0] + jnp.arange(N_EXPERTS, dtype=jnp.int32)
    zero_blk = jnp.concatenate([last_blk, jnp.where(trailing < n_blocks, trailing, -1)]).astype(jnp.int32)
    return dest, blk_exp, nblk, nvalid, zero_blk, n_rows


def _swa_tables(sinks, rel_bias, a_q_norm, a_k_norm):
    per_group = A_HEADS // A_KV_HEADS
    tab = jnp.broadcast_to((sinks * LOG2E).reshape(A_KV_HEADS, per_group, 1, 1),
                           (A_KV_HEADS, per_group, SWA_TQ, LANES)).reshape(A_KV_HEADS, per_group * SWA_TQ, LANES)
    qk = math.sqrt(A_HEAD_DIM) * jnp.max(jnp.abs(a_q_norm)) * jnp.max(jnp.abs(a_k_norm))
    bound = LOG2E * jnp.maximum(qk + jnp.max(jnp.abs(rel_bias)), jnp.max(jnp.abs(sinks)))
    return tab, bound


def _pos_tiles(dest, tm):
    t = dest.shape[0]
    return dest.reshape(t // tm, tm, TOP_K).transpose(0, 2, 1).reshape(t * TOP_K)


def _layer(x2d, bsz, seq, p, sink_tab, swa_bound, bias):
    qa, ka, va, qb, kb, vb, gates = _proj_call(x2d, seq, p)
    oa = _swa_call(qa, ka, va, sink_tab, bias, bsz, seq, swa_bound)
    ob = _mla_call(qb, kb, vb, bsz, seq, p["score_bound"])
    x1, w0, w1, route, tile_cnt = _merge_call(x2d, oa, ob, gates, p)
    dest, blk_exp, nblk, nvalid, zero_blk, n_rows = _dispatch(route, tile_cnt, MOE_ROWS)
    pos_tiles = _pos_tiles(dest, COMB_TM)
    xr = _scatter_call(x1, pos_tiles, zero_blk, p["gf"], n_rows, COMB_TM, MOE_ROWS)
    yr = _expert_call(xr, MOE_ROWS, blk_exp, nblk, nvalid, p["wg"], p["wu"], p["wd"])
    return _combine_call(x1, w0, w1, yr, pos_tiles, COMB_TM)


def kernel(x, attn_norm, w_in, b_gate, a_q_norm, a_k_norm, a_sinks, rel_bias, w_oa, q_a_norm, w_qb, kv_a_norm, w_kvb, b_q_norm, b_k_norm, w_ob, w_o, ffn_norm, w_router_group, b_router_group, w_router_expert, b_router_expert, w_exp_gate, w_exp_up, w_exp_down):
    bsz, seq, d = x.shape
    assert d == D_MODEL and seq % MLA_TQ == 0 and seq % PROJ_TM == 0 and (bsz * seq) % COMB_TM == 0
    depth = w_in.shape[0]
    cos, sin = _rope_tables(seq)
    bias = _swa_bias(rel_bias, SWA_TQ)
    x2d = x.reshape(bsz * seq, d)
    for l in range(depth):
        p = _prep_layer(l, seq, cos, sin, attn_norm, w_in, b_gate, a_q_norm, a_k_norm, w_oa, q_a_norm, w_qb,
                        kv_a_norm, w_kvb, b_q_norm, b_k_norm, w_ob, w_o, ffn_norm, w_router_group,
                        b_router_group, w_router_expert, b_router_expert, w_exp_gate, w_exp_up, w_exp_down)
        sink_tab, swa_bound = _swa_tables(a_sinks[l], rel_bias, a_q_norm[l], a_k_norm[l])
        x2d = _layer(x2d, bsz, seq, p, sink_tab, swa_bound, bias)
    return x2d.reshape(bsz, seq, d)
```

```python
import functools
import math

import jax
import jax.numpy as jnp
from jax import lax
from jax.experimental import pallas as pl
from jax.experimental.pallas import tpu as pltpu

F32 = jnp.float32
BF16 = jnp.bfloat16

D_MODEL = 1024
A_HEADS = 8
A_KV_HEADS = 2
A_HEAD_DIM = 64
A_WIDTH = A_HEADS * A_HEAD_DIM
WINDOW = 128
REL_BUCKETS = 32
REL_MAX_DIST = 128
B_HEADS = 8
Q_LORA = 256
KV_LORA = 128
NOPE_DIM = 64
ROPE_DIM = 32
B_QK_DIM = NOPE_DIM + ROPE_DIM
B_V_DIM = 64
B_WIDTH = B_HEADS * B_V_DIM
ROPE_THETA = 10000.0
N_GROUPS = 4
EXPERTS_PER_GROUP = 8
N_EXPERTS = N_GROUPS * EXPERTS_PER_GROUP
TOP_K = 2
D_EXPERT = 256
EPS = 1e-6

LANES = 128
NEG = -1e30
LOG2E = math.log2(math.e)
V_ONE_EVEN = B_V_DIM
V_ONE_ODD = 0
EXP2_SAFE_BOUND = 100.0
VMEM_LIMIT = 56 * 1024 * 1024

C_QA = 0
C_KA = C_QA + A_WIDTH
C_VA = C_KA + LANES
C_CQ = C_VA + LANES
C_CKV = C_CQ + Q_LORA
C_KPE = C_CKV + KV_LORA
C_KPR = C_KPE + LANES
C_GATE = C_KPR + LANES
C_END = C_GATE + 2 * D_MODEL

PROJ_TM = 512
SWA_TQ = 128
SWA_SUB = 4
MLA_TQ = 1024
MLA_TK = 512
MERGE_TM = 1024
MOE_ROWS = 512
COMB_TM = 256


def _cparams(sem):
    return pltpu.CompilerParams(dimension_semantics=sem, vmem_limit_bytes=VMEM_LIMIT)


def _proj_kernel(x_ref, gn_ref, w1_ref, wq2_ref, wk2_ref, wv2_ref, vone_ref, gcq_ref, gckv_ref, bg_ref,
                 gqa_ref, gka_ref, aq_ref, bq_ref, ak_ref, bk_ref,
                 qa_ref, ka_ref, va_ref, qb_ref, kb_ref, vb_ref, gate_ref):
    tm = x_ref.shape[0]
    x = x_ref[...]
    h = (x * lax.rsqrt(jnp.mean(x * x, axis=-1, keepdims=True) + EPS) * gn_ref[...]).astype(BF16)

    def proj(a, b):
        return jnp.dot(h, w1_ref[:, a:b], preferred_element_type=F32)

    lo = lax.broadcasted_iota(jnp.int32, (tm, LANES), 1) < A_HEAD_DIM

    def pair_norm(t, gain):
        t2 = t * t
        s_lo = jnp.sum(jnp.where(lo, t2, 0.0), axis=-1, keepdims=True)
        s_hi = jnp.sum(jnp.where(lo, 0.0, t2), axis=-1, keepdims=True)
        r = jnp.where(lo, lax.rsqrt(s_lo / A_HEAD_DIM + EPS), lax.rsqrt(s_hi / A_HEAD_DIM + EPS))
        return (t * r * gain).astype(BF16)

    qa = proj(C_QA, C_KA)
    for c in range(A_WIDTH // LANES):
        sl = slice(c * LANES, (c + 1) * LANES)
        qa_ref[:, sl] = pair_norm(qa[:, sl], gqa_ref[...])
    kvp = proj(C_KA, C_CQ)
    ka_ref[...] = pair_norm(kvp[:, :LANES], gka_ref[...])
    va_ref[...] = kvp[:, LANES:].astype(BF16)

    lat = proj(C_CQ, C_GATE)
    cq = lat[:, :Q_LORA]
    cqn = (cq * lax.rsqrt(jnp.mean(cq * cq, axis=-1, keepdims=True) + EPS) * gcq_ref[...]).astype(BF16)
    ckv = lat[:, Q_LORA:Q_LORA + KV_LORA]
    ckvn = (ckv * lax.rsqrt(jnp.mean(ckv * ckv, axis=-1, keepdims=True) + EPS) * gckv_ref[...]).astype(BF16)
    kpe = lat[:, C_KPE - C_CQ:C_KPR - C_CQ]
    kpr = lat[:, C_KPR - C_CQ:C_GATE - C_CQ]

    aq, bq, ak, bk = aq_ref[...], bq_ref[...], ak_ref[...], bk_ref[...]
    q2 = jnp.dot(cqn, wq2_ref[...], preferred_element_type=F32)
    k2 = jnp.dot(ckvn, wk2_ref[...], preferred_element_type=F32)
    vb_ref[...] = (jnp.dot(ckvn, wv2_ref[...], preferred_element_type=F32) + vone_ref[...]).astype(BF16)
    kr = kpr * bk
    for hd in range(B_HEADS):
        sl = slice(hd * LANES, (hd + 1) * LANES)
        qp = q2[:, sl]
        qr = q2[:, B_HEADS * LANES + hd * LANES:B_HEADS * LANES + (hd + 1) * LANES]
        rq = lax.rsqrt(jnp.sum(qp * qp, axis=-1, keepdims=True) / B_QK_DIM + EPS)
        qb_ref[:, sl] = (rq * (qp * aq + qr * bq)).astype(BF16)
        kp = k2[:, sl] + kpe
        rk = lax.rsqrt(jnp.sum(kp * kp, axis=-1, keepdims=True) / B_QK_DIM + EPS)
        kb_ref[:, sl] = (rk * (kp * ak + kr)).astype(BF16)

    z = proj(C_GATE, C_END) + bg_ref[...]
    gate_ref[...] = (0.5 * jnp.tanh(0.5 * z) + 0.5).astype(BF16)


def _proj_call(x2d, seq, p):
    t = x2d.shape[0]
    tm = PROJ_TM
    nt = t // tm
    npos = seq // tm
    row = lambda w: pl.BlockSpec((tm, w), lambda i: (i, 0))
    full = lambda a: pl.BlockSpec(a.shape, lambda i: (0,) * a.ndim)
    pos = pl.BlockSpec((tm, LANES), lambda i: (i % npos, 0))
    consts = [p["gn"], p["w1"], p["wq2"], p["wk2"], p["wv2"], p["vone"], p["gcq"], p["gckv"], p["bg"], p["gqa"],
              p["gka"]]
    tabs = [p["aq"], p["bq"], p["ak"], p["bk"]]
    widths = [A_WIDTH, LANES, LANES, B_HEADS * LANES, B_HEADS * LANES, B_HEADS * LANES, 2 * D_MODEL]
    return pl.pallas_call(
        _proj_kernel,
        grid=(nt,),
        in_specs=[row(D_MODEL)] + [full(a) for a in consts] + [pos] * 4,
        out_specs=[row(w) for w in widths],
        out_shape=[jax.ShapeDtypeStruct((t, w), BF16) for w in widths],
        compiler_params=_cparams(("parallel",)),
        name="proj",
    )(x2d, *consts, *tabs)


def _swa_kernel(q_ref, kc_ref, kp_ref, vc_ref, vp_ref, bias_ref, sink_ref, o_ref, *, shift):
    tq = SWA_TQ
    groups = A_WIDTH // LANES
    first = pl.program_id(1) == 0
    one_col = (lax.broadcasted_iota(jnp.int32, (2 * tq, LANES), 1) == 0).astype(BF16)
    lo = lax.broadcasted_iota(jnp.int32, (tq, LANES), 1) < A_HEAD_DIM
    no_prev = jnp.logical_and(first, lax.broadcasted_iota(jnp.int32, (groups * tq, 2 * tq), 1) < tq)
    for sub in range(SWA_SUB):
        if sub == 0:
            k = jnp.concatenate([kp_ref[...], kc_ref[0:tq, :]], axis=0)
            v = jnp.concatenate([vp_ref[...], vc_ref[0:tq, :]], axis=0)
        else:
            k = kc_ref[(sub - 1) * tq:(sub + 1) * tq, :]
            v = vc_ref[(sub - 1) * tq:(sub + 1) * tq, :]
        v_ext = jnp.concatenate([v, one_col], axis=1)
        qrows = slice(sub * tq, (sub + 1) * tq)
        qs = [q_ref[qrows, c * LANES:(c + 1) * LANES] for c in range(groups)]
        res = []
        for g in range(A_KV_HEADS):
            keep = lo if g == 0 else jnp.logical_not(lo)
            qg = jnp.concatenate([jnp.where(keep, qc, jnp.zeros_like(qc)) for qc in qs], axis=0)
            s = lax.dot_general(qg, k, (((1,), (1,)), ((), ())), preferred_element_type=F32) + bias_ref[g]
            sink = sink_ref[g][:, :1]
            if shift:
                if sub == 0:
                    s = jnp.where(no_prev, NEG, s)
                m = jnp.maximum(jnp.max(s, axis=-1, keepdims=True), sink)
                e = jnp.exp2(s - m)
                sink_term = jnp.exp2(sink - m)
            else:
                e = jnp.exp2(s)
                if sub == 0:
                    e = jnp.where(no_prev, 0.0, e)
                sink_term = jnp.exp2(sink)
            pv = jnp.dot(e.astype(BF16), v_ext, preferred_element_type=F32)
            res.append(pv[:, :LANES] * (1.0 / (pv[:, LANES:LANES + 1] + sink_term)))
        for c in range(groups):
            rows = slice(c * tq, (c + 1) * tq)
            o_ref[qrows, c * LANES:(c + 1) * LANES] = jnp.where(lo, res[0][rows], res[1][rows]).astype(BF16)


def _swa_call(qa, ka, va, sink_tab, bias, bsz, seq, score_bound):
    t = qa.shape[0]
    tq = SWA_TQ
    step = SWA_SUB * tq
    ns = seq // step
    cur = lambda b, i: (b * ns + i, 0)
    prev = lambda b, i: (b * (seq // tq) + jnp.maximum(SWA_SUB * i - 1, 0), 0)
    full = lambda a: pl.BlockSpec(a.shape, lambda b, i: (0,) * a.ndim)

    def call(shift):
        return pl.pallas_call(
            functools.partial(_swa_kernel, shift=shift),
            grid=(bsz, ns),
            in_specs=[pl.BlockSpec((step, A_WIDTH), cur),
                      pl.BlockSpec((step, LANES), cur), pl.BlockSpec((tq, LANES), prev),
                      pl.BlockSpec((step, LANES), cur), pl.BlockSpec((tq, LANES), prev),
                      full(bias), full(sink_tab)],
            out_specs=pl.BlockSpec((step, A_WIDTH), cur),
            out_shape=jax.ShapeDtypeStruct((t, A_WIDTH), BF16),
            compiler_params=_cparams(("parallel", "parallel")),
            name="swa_shifted" if shift else "swa",
        )(qa, ka, ka, va, va, bias, sink_tab)

    return lax.cond(score_bound <= EXP2_SAFE_BOUND, lambda: call(False), lambda: call(True))


def _mla_kernel(q_ref, k_ref, v_ref, o_ref, acc_sc, m_sc, *, shift):
    tq = q_ref.shape[0]
    tk = MLA_TK
    per_tile = tq // tk
    qi = pl.program_id(2)
    acc_sc[...] = jnp.zeros(acc_sc.shape, F32)
    if shift:
        m_sc[...] = jnp.full(m_sc.shape, NEG, F32)

    def tile(j, r0, r1, masked):
        keys = pl.ds(pl.multiple_of(j * tk, tk), tk)
        if masked:
            causal = (lax.broadcasted_iota(jnp.int32, (r1 - r0, tk), 1)
                      <= lax.broadcasted_iota(jnp.int32, (r1 - r0, tk), 0))
        for half in range(2):
            hs = slice(half * LANES, (half + 1) * LANES)
            s = lax.dot_general(q_ref[r0:r1, hs], k_ref[keys, hs], (((1,), (1,)), ((), ())),
                                preferred_element_type=F32)
            if shift:
                if masked:
                    s = jnp.where(causal, s, NEG)
                m_prev = m_sc[half, r0:r1]
                m_new = jnp.maximum(m_prev, jnp.max(s, axis=-1, keepdims=True))
                m_sc[half, r0:r1] = m_new
                e = jnp.exp2(s - m_new)
                acc_sc[half, r0:r1] = jnp.exp2(m_prev - m_new) * acc_sc[half, r0:r1]
            else:
                e = jnp.exp2(s)
                if masked:
                    e = jnp.where(causal, e, 0.0)
            acc_sc[half, r0:r1] += jnp.dot(e.astype(BF16), v_ref[keys, hs], preferred_element_type=F32)

    def body(j, carry):
        tile(j, 0, tq, False)
        return carry

    lax.fori_loop(0, qi * per_tile, body, 0)
    for d in range(per_tile):
        tile(qi * per_tile + d, d * tk, (d + 1) * tk, True)
        if (d + 1) * tk < tq:
            tile(qi * per_tile + d, (d + 1) * tk, tq, False)
    a0 = acc_sc[0]
    a1 = acc_sc[1]
    lo = lax.broadcasted_iota(jnp.int32, (tq, LANES), 1) < B_V_DIM
    o_ref[...] = jnp.where(lo, a0 * (1.0 / a0[:, V_ONE_EVEN:V_ONE_EVEN + 1]),
                           a1 * (1.0 / a1[:, V_ONE_ODD:V_ONE_ODD + 1])).astype(BF16)


def _mla_call(qb, kb, vb, bsz, seq, score_bound):
    t = qb.shape[0]
    tq = MLA_TQ
    nq = seq // tq
    pairs = B_HEADS // 2

    def call(shift):
        return pl.pallas_call(
            functools.partial(_mla_kernel, shift=shift),
            grid=(bsz, pairs, nq),
            in_specs=[pl.BlockSpec((tq, 2 * LANES), lambda b, p, i: (b * nq + i, p)),
                      pl.BlockSpec((seq, 2 * LANES), lambda b, p, i: (b, p)),
                      pl.BlockSpec((seq, 2 * LANES), lambda b, p, i: (b, p))],
            out_specs=pl.BlockSpec((tq, LANES), lambda b, p, i: (b * nq + i, p)),
            out_shape=jax.ShapeDtypeStruct((t, B_WIDTH), BF16),
            scratch_shapes=[pltpu.VMEM((2, tq, LANES), F32), pltpu.VMEM((2, tq, 1), F32)],
            compiler_params=_cparams(("parallel", "parallel", "arbitrary")),
            name="mla_shifted" if shift else "mla",
        )(qb, kb, vb)

    return lax.cond(score_bound <= EXP2_SAFE_BOUND, lambda: call(False), lambda: call(True))


def _merge_kernel(x_ref, oa_ref, ob_ref, gate_ref, woa_ref, wob_ref, wo_ref, gf_ref, wr_ref, br_ref,
                  tri_ref, x1_ref, w0_ref, w1_ref, eid_ref, cnt_ref):
    tm = x_ref.shape[0]
    ya = jnp.dot(oa_ref[...], woa_ref[...], preferred_element_type=F32)
    yb = jnp.dot(ob_ref[...], wob_ref[...], preferred_element_type=F32)
    g = gate_ref[...].astype(F32)
    mix = (g[:, :D_MODEL] * ya + g[:, D_MODEL:] * yb).astype(BF16)
    x1 = x_ref[...] + jnp.dot(mix, wo_ref[...], preferred_element_type=F32)
    x1_ref[...] = x1

    hn = x1 * lax.rsqrt(jnp.mean(x1 * x1, axis=-1, keepdims=True) + EPS) * gf_ref[...]
    hh = hn.astype(BF16)
    hl = (hn - hh.astype(F32)).astype(BF16)
    ph = jnp.dot(hh, wr_ref[...], preferred_element_type=F32)
    pl_ = jnp.dot(hl, wr_ref[...], preferred_element_type=F32)
    logits = (ph[:, :LANES] + ph[:, LANES:]) + (pl_[:, :LANES] + pl_[:, LANES:]) + br_ref[...]

    lane = lax.broadcasted_iota(jnp.int32, (tm, LANES), 1).astype(F32)
    big = float(LANES)
    gmask = lane < N_GROUPS
    gl = jnp.where(gmask, logits, NEG)
    gmax = jnp.max(gl, axis=-1, keepdims=True)
    gsum = jnp.sum(jnp.where(gmask, jnp.exp(gl - gmax), 0.0), axis=-1, keepdims=True)
    g_p = 1.0 / gsum
    g_idx = jnp.min(jnp.where(gl == gmax, lane, big), axis=-1, keepdims=True)
    e_lo = N_GROUPS + EXPERTS_PER_GROUP * g_idx
    emask = jnp.logical_and(lane >= e_lo, lane < e_lo + EXPERTS_PER_GROUP)
    el = jnp.where(emask, logits, NEG)
    t1 = jnp.max(el, axis=-1, keepdims=True)
    i1 = jnp.min(jnp.where(el == t1, lane, big), axis=-1, keepdims=True)
    el2 = jnp.where(lane == i1, NEG, el)
    t2 = jnp.max(el2, axis=-1, keepdims=True)
    i2 = jnp.min(jnp.where(el2 == t2, lane, big), axis=-1, keepdims=True)
    e2 = jnp.exp(t2 - t1)
    w_first = g_p / (1.0 + e2)
    w0_ref[...] = jnp.broadcast_to(w_first, (tm, LANES))
    w1_ref[...] = jnp.broadcast_to(w_first * e2, (tm, LANES))
    ex1 = i1 - N_GROUPS
    ex2 = i2 - N_GROUPS

    oh1 = lane == ex1
    oh2 = lane == ex2
    cnt = (oh1.astype(F32) + oh2.astype(F32)).astype(BF16)
    before = jnp.dot(tri_ref[...], cnt, preferred_element_type=F32)
    r1 = jnp.sum(jnp.where(oh1, before, 0.0), axis=-1, keepdims=True)
    r2 = jnp.sum(jnp.where(oh2, before, 0.0), axis=-1, keepdims=True)
    ids = jnp.where(lane == 0.0, ex1, jnp.where(lane == 1.0, ex2, jnp.where(lane == 2.0, r1,
                                                                             jnp.where(lane == 3.0, r2, 0.0))))
    eid_ref[...] = ids.astype(jnp.int32)
    tile_cnt = jnp.sum(cnt.astype(F32), axis=0, keepdims=True)
    cnt_ref[...] = jnp.broadcast_to(tile_cnt, cnt_ref.shape[1:]).astype(jnp.int32)[None]


def _merge_call(x2d, oa, ob, gates, p):
    t = x2d.shape[0]
    tm = MERGE_TM
    nt = t // tm
    row = lambda w: pl.BlockSpec((tm, w), lambda i: (i, 0))
    full = lambda a: pl.BlockSpec(a.shape, lambda i: (0,) * a.ndim)
    tri = (jnp.arange(tm)[:, None] > jnp.arange(tm)[None, :]).astype(BF16)
    consts = [p["woa"], p["wob"], p["wo"], p["gf"], p["wr"], p["br"], tri]
    return pl.pallas_call(
        _merge_kernel,
        grid=(nt,),
        in_specs=[row(D_MODEL), row(A_WIDTH), row(B_WIDTH), row(2 * D_MODEL)] + [full(a) for a in consts],
        out_specs=[row(D_MODEL), row(LANES), row(LANES), row(LANES),
                   pl.BlockSpec((1, 8, LANES), lambda i: (i, 0, 0))],
        out_shape=[jax.ShapeDtypeStruct((t, D_MODEL), F32), jax.ShapeDtypeStruct((t, LANES), F32),
                   jax.ShapeDtypeStruct((t, LANES), F32), jax.ShapeDtypeStruct((t, LANES), jnp.int32),
                   jax.ShapeDtypeStruct((nt, 8, LANES), jnp.int32)],
        compiler_params=_cparams(("parallel",)),
        name="merge",
    )(x2d, oa, ob, gates, *consts)


def _rows_copy(src_hbm, row, dst, j, sem):
    return pltpu.make_async_copy(src_hbm.at[row], dst.at[j], sem)


def _start_rows(src_hbm, idx, base, dst, sem, n):
    for j in range(n):
        _rows_copy(src_hbm, idx[base + j], dst, j, sem).start(priority=j % 2)


WAIT_ROWS = 128


def _wait_rows(src_hbm, dst, sem, n):
    for c in range(n // WAIT_ROWS):
        rows = pl.ds(c * WAIT_ROWS, WAIT_ROWS)
        pltpu.make_async_copy(src_hbm.at[rows], dst.at[rows], sem).wait()


def _gather_step(i, n, src_hbm, idx_hbm, buf, idx, sem_rows, sem_idx):
    rows = buf.shape[1]
    slot = i % 2
    nslot = 1 - slot

    def idx_copy(b, s):
        return pltpu.make_async_copy(idx_hbm.at[pl.ds(b * rows, rows)], idx.at[pl.ds(s * rows, rows)], sem_idx.at[s])

    @pl.when(jnp.logical_and(i == 0, n > 0))
    def _():
        idx_copy(0, 0).start()
        idx_copy(0, 0).wait()
        _start_rows(src_hbm, idx, 0, buf.at[0], sem_rows.at[0], rows)

        @pl.when(n > 1)
        def _():
            idx_copy(1, 1).start()

    @pl.when(i + 1 < n)
    def _():
        idx_copy(i + 1, nslot).wait()
        _start_rows(src_hbm, idx, nslot * rows, buf.at[nslot], sem_rows.at[nslot], rows)

    @pl.when(i + 2 < n)
    def _():
        idx_copy(i + 2, slot).start()

    @pl.when(i < n)
    def _():
        _wait_rows(src_hbm, buf.at[slot], sem_rows.at[slot], rows)
    return slot


SCATTER_BUFS = 3


def _pack_bf16_pair(a, b):
    abits = lax.bitcast_convert_type(a.astype(BF16).astype(F32), jnp.uint32)
    bbits = lax.bitcast_convert_type(b.astype(BF16).astype(F32), jnp.uint32)
    return lax.bitcast_convert_type((abits >> 16) | (bbits & jnp.uint32(0xFFFF0000)), F32)


def _unpack_bf16_pair(w):
    bits = lax.bitcast_convert_type(w, jnp.uint32)
    lo = lax.bitcast_convert_type(bits << 16, F32)
    hi = lax.bitcast_convert_type(bits & jnp.uint32(0xFFFF0000), F32)
    return lo, hi


def _scatter_kernel(zero_blk_ref, x_hbm, dest_hbm, gf_ref, xr_hbm, buf, rbuf, idx, sem_x, sem_rows, sem_idx,
                    sem_zero, *, blk_rows):
    i = pl.program_id(0)
    n = pl.num_programs(0)
    tm = buf.shape[1]
    rows = TOP_K * tm
    islot = i % 2
    xslot = i % SCATTER_BUFS

    def idx_copy(b, s):
        return pltpu.make_async_copy(dest_hbm.at[pl.ds(b * rows, rows)], idx.at[pl.ds(s * rows, rows)], sem_idx.at[s])

    def x_copy(b, s):
        return pltpu.make_async_copy(x_hbm.at[pl.ds(b * tm, tm), :], buf.at[s], sem_x.at[s])

    def wait_rows(s):
        for k in range(TOP_K):
            pltpu.make_async_copy(rbuf.at[s], xr_hbm.at[pl.ds(0, tm)], sem_rows.at[s]).wait()

    @pl.when(i == 0)
    def _():
        idx_copy(0, 0).start()
        x_copy(0, 0).start()
        zsrc = rbuf.at[SCATTER_BUFS - 1]
        zsrc[...] = jnp.zeros(zsrc.shape, zsrc.dtype)

        def zero_copies(e):
            b = zero_blk_ref[e]
            return b >= 0, [pltpu.make_async_copy(zsrc, xr_hbm.at[pl.ds(b * blk_rows + h * tm, tm)], sem_zero)
                            for h in range(blk_rows // tm)]

        def start(e, carry):
            live, copies = zero_copies(e)

            @pl.when(live)
            def _():
                for c in copies:
                    c.start()
            return carry

        def wait(e, carry):
            live, copies = zero_copies(e)

            @pl.when(live)
            def _():
                for c in copies:
                    c.wait()
            return carry

        lax.fori_loop(0, zero_blk_ref.shape[0], start, 0)
        lax.fori_loop(0, zero_blk_ref.shape[0], wait, 0)

    @pl.when(i >= 2)
    def _():
        wait_rows((i - 2) % SCATTER_BUFS)

    @pl.when(i + 1 < n)
    def _():
        idx_copy(i + 1, 1 - islot).start()
        x_copy(i + 1, (i + 1) % SCATTER_BUFS).start()

    idx_copy(i, islot).wait()
    x_copy(i, xslot).wait()

    x = buf[xslot]
    hn = x * lax.rsqrt(jnp.mean(x * x, axis=-1, keepdims=True) + EPS) * gf_ref[...]
    rbuf[xslot, :, 0, :] = _pack_bf16_pair(hn[:, :D_MODEL // 2], hn[:, D_MODEL // 2:])
    for j in range(tm):
        src = rbuf.at[xslot, j]
        for k in range(TOP_K):
            dst = xr_hbm.at[idx[islot * rows + k * tm + j]]
            pltpu.make_async_copy(src, dst, sem_rows.at[xslot]).start(priority=k % 2)

    @pl.when(i == n - 1)
    def _():
        @pl.when(n > 1)
        def _():
            wait_rows((i - 1) % SCATTER_BUFS)
        wait_rows(xslot)


def _scatter_call(x1, pos_tiles, zero_blk, gf, n_rows, tm, blk_rows):
    t = x1.shape[0]
    rows = TOP_K * tm
    assert blk_rows % tm == 0
    any_spec = pl.BlockSpec(memory_space=pl.ANY)
    return pl.pallas_call(
        functools.partial(_scatter_kernel, blk_rows=blk_rows),
        grid_spec=pltpu.PrefetchScalarGridSpec(
            num_scalar_prefetch=1,
            grid=(t // tm,),
            in_specs=[any_spec, any_spec, pl.BlockSpec(gf.shape, lambda i, zb: (0, 0))],
            out_specs=any_spec,
            scratch_shapes=[pltpu.VMEM((SCATTER_BUFS, tm, D_MODEL), F32),
                            pltpu.VMEM((SCATTER_BUFS, tm, 1, D_MODEL // 2), F32),
                            pltpu.SMEM((2 * rows,), jnp.int32),
                            pltpu.SemaphoreType.DMA((SCATTER_BUFS,)), pltpu.SemaphoreType.DMA((SCATTER_BUFS,)),
                            pltpu.SemaphoreType.DMA((2,)), pltpu.SemaphoreType.DMA]),
        out_shape=jax.ShapeDtypeStruct((n_rows, 1, D_MODEL // 2), F32),
        compiler_params=_cparams(("arbitrary",)),
        name="scatter",
    )(zero_blk, x1, pos_tiles, gf)


def _expert_kernel(blk_exp_ref, nblk_ref, nvalid_ref, x_ref, wg_ref, wu_ref, wd_ref, y_ref):
    del blk_exp_ref
    i = pl.program_id(0)

    @pl.when(i < nblk_ref[0])
    def _():
        w = x_ref[:, 0, :]
        real = lax.broadcasted_iota(jnp.int32, w.shape, 0) < nvalid_ref[i]
        lo, hi = [v.astype(BF16) for v in _unpack_bf16_pair(jnp.where(real, w, 0.0))]
        half = D_MODEL // 2

        def up_proj(w_ref):
            return (jnp.dot(lo, w_ref[0, :half, :], preferred_element_type=F32)
                    + jnp.dot(hi, w_ref[0, half:, :], preferred_element_type=F32))

        gt, up = up_proj(wg_ref), up_proj(wu_ref)
        act = (gt * (0.5 * jnp.tanh(0.5 * gt) + 0.5) * up).astype(BF16)
        y_ref[:, 0, :] = jnp.dot(act, wd_ref[0], preferred_element_type=F32)

    @pl.when(i >= nblk_ref[0])
    def _():
        y_ref[...] = jnp.zeros(y_ref.shape, y_ref.dtype)


def _expert_call(xr, rows, blk_exp, nblk, nvalid, wg, wu, wd):
    nb = xr.shape[0] // rows
    return pl.pallas_call(
        _expert_kernel,
        grid_spec=pltpu.PrefetchScalarGridSpec(
            num_scalar_prefetch=3,
            grid=(nb,),
            in_specs=[pl.BlockSpec((rows, 1, D_MODEL // 2), lambda i, be, n, nv: (jnp.minimum(i, n[0] - 1), 0, 0)),
                      pl.BlockSpec((1, D_MODEL, D_EXPERT), lambda i, be, n, nv: (be[i], 0, 0)),
                      pl.BlockSpec((1, D_MODEL, D_EXPERT), lambda i, be, n, nv: (be[i], 0, 0)),
                      pl.BlockSpec((1, D_EXPERT, D_MODEL), lambda i, be, n, nv: (be[i], 0, 0))],
            out_specs=pl.BlockSpec((rows, 1, D_MODEL), lambda i, be, n, nv: (i, 0, 0))),
        out_shape=jax.ShapeDtypeStruct((nb * rows, 1, D_MODEL), F32),
        compiler_params=_cparams(("arbitrary",)),
        name="experts",
    )(blk_exp, nblk, nvalid, xr, wg, wu, wd)


def _combine_kernel(x1_ref, w0_ref, w1_ref, y_hbm, pos_hbm, o_ref, buf, idx, sem_rows, sem_idx):
    rows = buf.shape[1]
    tm = rows // 2
    slot = _gather_step(pl.program_id(0), pl.num_programs(0), y_hbm, pos_hbm, buf, idx, sem_rows, sem_idx)
    w0 = w0_ref[...]
    w1 = w1_ref[...]
    for c in range(D_MODEL // LANES):
        sl = slice(c * LANES, (c + 1) * LANES)
        o_ref[:, sl] = x1_ref[:, sl] + w0 * buf[slot, 0:tm, 0, sl] + w1 * buf[slot, tm:rows, 0, sl]


def _combine_call(x1, w0, w1, yr, pos_tiles, tm):
    t = x1.shape[0]
    rows = TOP_K * tm
    row = lambda w: pl.BlockSpec((tm, w), lambda i: (i, 0))
    return pl.pallas_call(
        _combine_kernel,
        grid=(t // tm,),
        in_specs=[row(D_MODEL), row(LANES), row(LANES),
                  pl.BlockSpec(memory_space=pl.ANY), pl.BlockSpec(memory_space=pl.ANY)],
        out_specs=row(D_MODEL),
        out_shape=jax.ShapeDtypeStruct((t, D_MODEL), F32),
        scratch_shapes=[pltpu.VMEM((2, rows, 1, D_MODEL), F32), pltpu.SMEM((2 * rows,), jnp.int32),
                        pltpu.SemaphoreType.DMA((2,)), pltpu.SemaphoreType.DMA((2,))],
        compiler_params=_cparams(("arbitrary",)),
        name="combine",
    )(x1, w0, w1, yr, pos_tiles)


def _rot_cols(w):
    half = ROPE_DIM // 2
    return jnp.concatenate([-w[..., half:], w[..., :half]], axis=-1)


def _head_pad(nope, rope):
    z = jnp.zeros(rope.shape[:-1] + (LANES - B_QK_DIM,), rope.dtype)
    out = jnp.concatenate([nope, rope, z], axis=-1)
    return out.reshape(out.shape[:-2] + (out.shape[-2] * LANES,))


def _t5_bucket(dist):
    max_exact = REL_BUCKETS // 2
    d = jnp.maximum(dist, 0)
    ratio = jnp.maximum(d, 1).astype(F32) / max_exact
    large = max_exact + (jnp.log(ratio) / math.log(REL_MAX_DIST / max_exact)
                         * (REL_BUCKETS - max_exact)).astype(jnp.int32)
    large = jnp.minimum(large, REL_BUCKETS - 1)
    return jnp.where(d < max_exact, d, large)


def _swa_bias(rel_bias, tq):
    qi = jnp.arange(tq)[:, None]
    kj = jnp.arange(2 * tq)[None, :]
    dist = qi + tq - kj
    onehot = (_t5_bucket(dist)[:, :, None] == jnp.arange(REL_BUCKETS)[None, None, :]).astype(F32)
    bias = jnp.einsum("qkb,bh->hqk", onehot, rel_bias.astype(F32), precision=lax.Precision.HIGHEST) * LOG2E
    mask = (dist >= 0) & (dist < WINDOW)
    bias = jnp.where(mask[None], bias, NEG)
    return bias.reshape(A_KV_HEADS, (A_HEADS // A_KV_HEADS) * tq, 2 * tq)


def _rope_tables(seq):
    half = ROPE_DIM // 2
    inv_freq = ROPE_THETA ** (-jnp.arange(half, dtype=F32) / half)
    ang = jnp.arange(seq, dtype=F32)[:, None] * inv_freq[None, :]
    cos = jnp.concatenate([jnp.cos(ang), jnp.cos(ang)], axis=-1)
    sin = jnp.concatenate([jnp.sin(ang), jnp.sin(ang)], axis=-1)
    return cos, sin


def _lane_table(nope_part, rope_part):
    seq = rope_part.shape[0]
    nope_part = jnp.broadcast_to(nope_part, (seq, NOPE_DIM))
    return jnp.concatenate([nope_part, rope_part, jnp.zeros((seq, LANES - B_QK_DIM), F32)], axis=-1)


def _prep_layer(l, seq, cos, sin, attn_norm, w_in, b_gate, a_q_norm, a_k_norm, w_oa, q_a_norm, w_qb, kv_a_norm,
                w_kvb, b_q_norm, b_k_norm, w_ob, w_o, ffn_norm, w_rg, b_rg, w_re, b_re, w_eg, w_eu, w_ed):
    w = w_in[l]
    o = 0
    parts = []
    for sz in (A_WIDTH, A_KV_HEADS * A_HEAD_DIM, A_KV_HEADS * A_HEAD_DIM, Q_LORA, KV_LORA, ROPE_DIM, 2 * D_MODEL):
        parts.append(w[:, o:o + sz])
        o += sz
    wqa, wka, wva, wcq, wckv, wkpe, wg = parts
    perm = jnp.array([c + (A_HEADS // 2) * h for c in range(A_HEADS // 2) for h in range(2)])
    wqa = wqa.reshape(D_MODEL, A_HEADS, A_HEAD_DIM)[:, perm].reshape(D_MODEL, A_WIDTH)
    zl = jnp.zeros((D_MODEL, NOPE_DIM), F32)
    zr = jnp.zeros((D_MODEL, LANES - B_QK_DIM), F32)
    w1 = jnp.concatenate([wqa, wka, wva, wcq, wckv, zl, wkpe, zr, zl, _rot_cols(wkpe), zr, wg], axis=1).astype(BF16)

    wq = w_qb[l].reshape(Q_LORA, B_HEADS, B_QK_DIM)
    zq = jnp.zeros((Q_LORA, B_HEADS, NOPE_DIM), F32)
    wq2 = jnp.concatenate([_head_pad(wq[..., :NOPE_DIM], wq[..., NOPE_DIM:]),
                           _head_pad(zq, _rot_cols(wq[..., NOPE_DIM:]))], axis=1).astype(BF16)
    wkv = w_kvb[l].reshape(KV_LORA, B_HEADS, NOPE_DIM + B_V_DIM)
    wk2 = _head_pad(wkv[..., :NOPE_DIM], jnp.zeros((KV_LORA, B_HEADS, ROPE_DIM), F32)).astype(BF16)
    wv = wkv[..., NOPE_DIM:]
    zv = jnp.zeros_like(wv)
    wv2 = jnp.where((jnp.arange(B_HEADS) % 2 == 0)[None, :, None], jnp.concatenate([wv, zv], -1),
                    jnp.concatenate([zv, wv], -1)).reshape(KV_LORA, B_HEADS * LANES).astype(BF16)
    one_lane = jnp.where(jnp.arange(B_HEADS) % 2 == 0, V_ONE_EVEN, V_ONE_ODD)
    vone = (jnp.arange(LANES)[None, :] == one_lane[:, None]).astype(F32).reshape(1, B_HEADS * LANES)

    def rope_tabs(gain, scale):
        gn, gr = gain[:NOPE_DIM], gain[NOPE_DIM:]
        gr_rot = jnp.concatenate([gr[ROPE_DIM // 2:], gr[:ROPE_DIM // 2]])
        a = _lane_table(gn[None, :] * scale, gr[None, :] * cos * scale)
        b = _lane_table(jnp.zeros((1, NOPE_DIM), F32), gr_rot[None, :] * sin * scale)
        return a, b

    aq, bq = rope_tabs(b_q_norm[l], B_QK_DIM ** -0.5 * LOG2E)
    score_bound = math.sqrt(B_QK_DIM) * LOG2E * jnp.max(jnp.abs(b_q_norm[l])) * jnp.max(jnp.abs(b_k_norm[l]))
    ak, bk = rope_tabs(b_k_norm[l], 1.0)

    gqa = (jnp.tile(a_q_norm[l], 2) * (A_HEAD_DIM ** -0.5 * LOG2E))[None, :]
    gka = jnp.tile(a_k_norm[l], 2)[None, :]
    woa = w_oa[l].reshape(A_HEADS, A_HEAD_DIM, D_MODEL)[perm].reshape(A_WIDTH, D_MODEL).astype(BF16)

    wr = jnp.concatenate([w_rg[l], w_re[l], jnp.zeros((D_MODEL, LANES - N_GROUPS - N_EXPERTS), F32)], axis=1)
    wrh = wr.astype(BF16)
    wrl = (wr - wrh.astype(F32)).astype(BF16)
    br = jnp.concatenate([b_rg[l], b_re[l], jnp.zeros((LANES - N_GROUPS - N_EXPERTS,), F32)])[None, :]

    return dict(
        gn=attn_norm[l][None, :], w1=w1, wq2=wq2, wk2=wk2, wv2=wv2, vone=vone, gcq=q_a_norm[l][None, :],
        gckv=kv_a_norm[l][None, :], score_bound=score_bound, bg=b_gate[l][None, :], gqa=gqa, gka=gka,
        aq=aq, bq=bq, ak=ak, bk=bk,
        woa=woa, wob=w_ob[l].astype(BF16), wo=w_o[l].astype(BF16), gf=ffn_norm[l][None, :], wr=jnp.concatenate([wrh, wrl], axis=1), br=br,
        wg=w_eg[l].astype(BF16), wu=w_eu[l].astype(BF16), wd=w_ed[l].astype(BF16))


def _tril(n):
    return (jnp.arange(n)[:, None] >= jnp.arange(n)[None, :]).astype(jnp.int32)


def _dispatch(route, tile_cnt, rows):
    t = route.shape[0]
    nt = tile_cnt.shape[0]
    tile_cnt = tile_cnt[:, 0, :N_EXPERTS]
    counts = jnp.sum(tile_cnt, axis=0)
    padded = ((counts + rows - 1) // rows) * rows
    pends = jnp.dot(_tril(N_EXPERTS), padded)
    pstarts = pends - padded
    tile_base = pstarts[None, :] + jnp.dot(_tril(nt), tile_cnt) - tile_cnt
    base = jnp.broadcast_to(tile_base[:, None, None, :], (nt, t // nt, 1, N_EXPERTS)).reshape(t, 1, N_EXPERTS)
    onehot = route[:, :TOP_K, None] == jnp.arange(N_EXPERTS, dtype=jnp.int32)[None, None, :]
    dest = jnp.sum(jnp.where(onehot, base, 0), axis=-1) + route[:, TOP_K:2 * TOP_K]
    n_rows = t * TOP_K + N_EXPERTS * rows
    n_blocks = n_rows // rows
    blk_start = jnp.arange(n_blocks, dtype=jnp.int32)[:, None] * rows
    blk_exp = jnp.minimum(jnp.sum((pends[None, :] <= blk_start).astype(jnp.int32), axis=1), N_EXPERTS - 1)
    nblk = (pends[-1] // rows).astype(jnp.int32).reshape(1)
    nvalid = jnp.clip((pstarts + counts)[blk_exp] - blk_start[:, 0], 0, rows).astype(jnp.int32)
    last_blk = jnp.where(padded > 0, pends // rows - 1, -1)
    trailing = nblk[0] + jnp.arange(N_EXPERTS, dtype=jnp.int32)
    zero_blk = jnp.concatenate([last_blk, jnp.where(trailing < n_blocks, trailing, -1)]).astype(jnp.int32)
    return dest, blk_exp, nblk, nvalid, zero_blk, n_rows


def _swa_tables(sinks, rel_bias, a_q_norm, a_k_norm):
    per_group = A_HEADS // A_KV_HEADS
    tab = jnp.broadcast_to((sinks * LOG2E).reshape(A_KV_HEADS, per_group, 1, 1),
                           (A_KV_HEADS, per_group, SWA_TQ, LANES)).reshape(A_KV_HEADS, per_group * SWA_TQ, LANES)
    qk = math.sqrt(A_HEAD_DIM) * jnp.max(jnp.abs(a_q_norm)) * jnp.max(jnp.abs(a_k_norm))
    bound = LOG2E * jnp.maximum(qk + jnp.max(jnp.abs(rel_bias)), jnp.max(jnp.abs(sinks)))
    return tab, bound


def _pos_tiles(dest, tm):
    t = dest.shape[0]
    return dest.reshape(t // tm, tm, TOP_K).transpose(0, 2, 1).reshape(t * TOP_K)


def _layer(x2d, bsz, seq, p, sink_tab, swa_bound, bias):
    qa, ka, va, qb, kb, vb, gates = _proj_call(x2d, seq, p)
    oa = _swa_call(qa, ka, va, sink_tab, bias, bsz, seq, swa_bound)
    ob = _mla_call(qb, kb, vb, bsz, seq, p["score_bound"])
    x1, w0, w1, route, tile_cnt = _merge_call(x2d, oa, ob, gates, p)
    dest, blk_exp, nblk, nvalid, zero_blk, n_rows = _dispatch(route, tile_cnt, MOE_ROWS)
    pos_tiles = _pos_tiles(dest, COMB_TM)
    xr = _scatter_call(x1, pos_tiles, zero_blk, p["gf"], n_rows, COMB_TM, MOE_ROWS)
    yr = _expert_call(xr, MOE_ROWS, blk_exp, nblk, nvalid, p["wg"], p["wu"], p["wd"])
    return _combine_call(x1, w0, w1, yr, pos_tiles, COMB_TM)


def kernel(x, attn_norm, w_in, b_gate, a_q_norm, a_k_norm, a_sinks, rel_bias, w_oa, q_a_norm, w_qb, kv_a_norm, w_kvb, b_q_norm, b_k_norm, w_ob, w_o, ffn_norm, w_router_group, b_router_group, w_router_expert, b_router_expert, w_exp_gate, w_exp_up, w_exp_down):
    bsz, seq, d = x.shape
    assert d == D_MODEL and seq % MLA_TQ == 0 and seq % PROJ_TM == 0 and (bsz * seq) % COMB_TM == 0
    depth = w_in.shape[0]
    cos, sin = _rope_tables(seq)
    bias = _swa_bias(rel_bias, SWA_TQ)
    x2d = x.reshape(bsz * seq, d)
    for l in range(depth):
        p = _prep_layer(l, seq, cos, sin, attn_norm, w_in, b_gate, a_q_norm, a_k_norm, w_oa, q_a_norm, w_qb,
                        kv_a_norm, w_kvb, b_q_norm, b_k_norm, w_ob, w_o, ffn_norm, w_router_group,
                        b_router_group, w_router_expert, b_router_expert, w_exp_gate, w_exp_up, w_exp_down)
        sink_tab, swa_bound = _swa_tables(a_sinks[l], rel_bias, a_q_norm[l], a_k_norm[l])
        x2d = _layer(x2d, bsz, seq, p, sink_tab, swa_bound, bias)
    return x2d.reshape(bsz, seq, d)
```

```python
import functools
import math

import jax
import jax.numpy as jnp
from jax import lax
from jax.experimental import pallas as pl
from jax.experimental.pallas import tpu as pltpu

F32 = jnp.float32
BF16 = jnp.bfloat16

D_MODEL = 1024
A_HEADS = 8
A_KV_HEADS = 2
A_HEAD_DIM = 64
A_WIDTH = A_HEADS * A_HEAD_DIM
WINDOW = 128
REL_BUCKETS = 32
REL_MAX_DIST = 128
B_HEADS = 8
Q_LORA = 256
KV_LORA = 128
NOPE_DIM = 64
ROPE_DIM = 32
B_QK_DIM = NOPE_DIM + ROPE_DIM
B_V_DIM = 64
B_WIDTH = B_HEADS * B_V_DIM
ROPE_THETA = 10000.0
N_GROUPS = 4
EXPERTS_PER_GROUP = 8
N_EXPERTS = N_GROUPS * EXPERTS_PER_GROUP
TOP_K = 2
D_EXPERT = 256
EPS = 1e-6

LANES = 128
NEG = -1e30
LOG2E = math.log2(math.e)
V_ONE_EVEN = B_V_DIM
V_ONE_ODD = 0
EXP2_SAFE_BOUND = 100.0
VMEM_LIMIT = 56 * 1024 * 1024

C_QA = 0
C_KA = C_QA + A_WIDTH
C_VA = C_KA + LANES
C_CQ = C_VA + LANES
C_CKV = C_CQ + Q_LORA
C_KPE = C_CKV + KV_LORA
C_KPR = C_KPE + LANES
C_GATE = C_KPR + LANES
C_END = C_GATE + 2 * D_MODEL

PROJ_TM = 512
SWA_TQ = 128
SWA_SUB = 8
MLA_TQ = 1024
MLA_TK = 512
MERGE_TM = 1024
MOE_ROWS = 512
COMB_TM = 256


def _cparams(sem):
    return pltpu.CompilerParams(dimension_semantics=sem, vmem_limit_bytes=VMEM_LIMIT)


def _proj_kernel(x_ref, gn_ref, w1_ref, wq2_ref, wk2_ref, wv2_ref, vone_ref, gcq_ref, gckv_ref, bg_ref,
                 gqa_ref, gka_ref, aq_ref, bq_ref, ak_ref, bk_ref,
                 qa_ref, ka_ref, va_ref, qb_ref, kb_ref, vb_ref, gate_ref):
    tm = x_ref.shape[0]
    x = x_ref[...]
    h = (x * lax.rsqrt(jnp.mean(x * x, axis=-1, keepdims=True) + EPS) * gn_ref[...]).astype(BF16)

    def proj(a, b):
        return jnp.dot(h, w1_ref[:, a:b], preferred_element_type=F32)

    lo = lax.broadcasted_iota(jnp.int32, (tm, LANES), 1) < A_HEAD_DIM

    def pair_norm(t, gain):
        t2 = t * t
        s_lo = jnp.sum(jnp.where(lo, t2, 0.0), axis=-1, keepdims=True)
        s_hi = jnp.sum(jnp.where(lo, 0.0, t2), axis=-1, keepdims=True)
        r = jnp.where(lo, lax.rsqrt(s_lo / A_HEAD_DIM + EPS), lax.rsqrt(s_hi / A_HEAD_DIM + EPS))
        return (t * r * gain).astype(BF16)

    qa = proj(C_QA, C_KA)
    for c in range(A_WIDTH // LANES):
        sl = slice(c * LANES, (c + 1) * LANES)
        qa_ref[:, sl] = pair_norm(qa[:, sl], gqa_ref[...])
    kvp = proj(C_KA, C_CQ)
    ka_ref[...] = pair_norm(kvp[:, :LANES], gka_ref[...])
    va_ref[...] = kvp[:, LANES:].astype(BF16)

    lat = proj(C_CQ, C_GATE)
    cq = lat[:, :Q_LORA]
    cqn = (cq * lax.rsqrt(jnp.mean(cq * cq, axis=-1, keepdims=True) + EPS) * gcq_ref[...]).astype(BF16)
    ckv = lat[:, Q_LORA:Q_LORA + KV_LORA]
    ckvn = (ckv * lax.rsqrt(jnp.mean(ckv * ckv, axis=-1, keepdims=True) + EPS) * gckv_ref[...]).astype(BF16)
    kpe = lat[:, C_KPE - C_CQ:C_KPR - C_CQ]
    kpr = lat[:, C_KPR - C_CQ:C_GATE - C_CQ]

    aq, bq, ak, bk = aq_ref[...], bq_ref[...], ak_ref[...], bk_ref[...]
    q2 = jnp.dot(cqn, wq2_ref[...], preferred_element_type=F32)
    k2 = jnp.dot(ckvn, wk2_ref[...], preferred_element_type=F32)
    vb_ref[...] = (jnp.dot(ckvn, wv2_ref[...], preferred_element_type=F32) + vone_ref[...]).astype(BF16)
    kr = kpr * bk
    for hd in range(B_HEADS):
        sl = slice(hd * LANES, (hd + 1) * LANES)
        qp = q2[:, sl]
        qr = q2[:, B_HEADS * LANES + hd * LANES:B_HEADS * LANES + (hd + 1) * LANES]
        rq = lax.rsqrt(jnp.sum(qp * qp, axis=-1, keepdims=True) / B_QK_DIM + EPS)
        qb_ref[:, sl] = (rq * (qp * aq + qr * bq)).astype(BF16)
        kp = k2[:, sl] + kpe
        rk = lax.rsqrt(jnp.sum(kp * kp, axis=-1, keepdims=True) / B_QK_DIM + EPS)
        kb_ref[:, sl] = (rk * (kp * ak + kr)).astype(BF16)

    gate_ref[...] = (0.5 * jnp.tanh(proj(C_GATE, C_END) + bg_ref[...]) + 0.5).astype(BF16)


def _proj_call(x2d, seq, p):
    t = x2d.shape[0]
    tm = PROJ_TM
    nt = t // tm
    npos = seq // tm
    row = lambda w: pl.BlockSpec((tm, w), lambda i: (i, 0))
    full = lambda a: pl.BlockSpec(a.shape, lambda i: (0,) * a.ndim)
    pos = pl.BlockSpec((tm, LANES), lambda i: (i % npos, 0))
    consts = [p["gn"], p["w1"], p["wq2"], p["wk2"], p["wv2"], p["vone"], p["gcq"], p["gckv"], p["bg"], p["gqa"],
              p["gka"]]
    tabs = [p["aq"], p["bq"], p["ak"], p["bk"]]
    widths = [A_WIDTH, LANES, LANES, B_HEADS * LANES, B_HEADS * LANES, B_HEADS * LANES, 2 * D_MODEL]
    return pl.pallas_call(
        _proj_kernel,
        grid=(nt,),
        in_specs=[row(D_MODEL)] + [full(a) for a in consts] + [pos] * 4,
        out_specs=[row(w) for w in widths],
        out_shape=[jax.ShapeDtypeStruct((t, w), BF16) for w in widths],
        compiler_params=_cparams(("parallel",)),
        name="proj",
    )(x2d, *consts, *tabs)


def _swa_kernel(q_ref, kc_ref, kp_ref, vc_ref, vp_ref, bias_ref, sink_ref, o_ref, *, shift):
    tq = SWA_TQ
    groups = A_WIDTH // LANES
    first = pl.program_id(1) == 0
    one_col = (lax.broadcasted_iota(jnp.int32, (2 * tq, LANES), 1) == 0).astype(BF16)
    lo = lax.broadcasted_iota(jnp.int32, (tq, LANES), 1) < A_HEAD_DIM
    no_prev = jnp.logical_and(first, lax.broadcasted_iota(jnp.int32, (groups * tq, 2 * tq), 1) < tq)
    for sub in range(SWA_SUB):
        if sub == 0:
            k = jnp.concatenate([kp_ref[...], kc_ref[0:tq, :]], axis=0)
            v = jnp.concatenate([vp_ref[...], vc_ref[0:tq, :]], axis=0)
        else:
            k = kc_ref[(sub - 1) * tq:(sub + 1) * tq, :]
            v = vc_ref[(sub - 1) * tq:(sub + 1) * tq, :]
        v_ext = jnp.concatenate([v, one_col], axis=1)
        qrows = slice(sub * tq, (sub + 1) * tq)
        qs = [q_ref[qrows, c * LANES:(c + 1) * LANES] for c in range(groups)]
        res = []
        for g in range(A_KV_HEADS):
            keep = lo if g == 0 else jnp.logical_not(lo)
            qg = jnp.concatenate([jnp.where(keep, qc, jnp.zeros_like(qc)) for qc in qs], axis=0)
            s = lax.dot_general(qg, k, (((1,), (1,)), ((), ())), preferred_element_type=F32) + bias_ref[g]
            sink = sink_ref[g][:, :1]
            if shift:
                if sub == 0:
                    s = jnp.where(no_prev, NEG, s)
                m = jnp.maximum(jnp.max(s, axis=-1, keepdims=True), sink)
                e = jnp.exp2(s - m)
                sink_term = jnp.exp2(sink - m)
            else:
                e = jnp.exp2(s)
                if sub == 0:
                    e = jnp.where(no_prev, 0.0, e)
                sink_term = jnp.exp2(sink)
            pv = jnp.dot(e.astype(BF16), v_ext, preferred_element_type=F32)
            res.append(pv[:, :LANES] * (1.0 / (pv[:, LANES:LANES + 1] + sink_term)))
        for c in range(groups):
            rows = slice(c * tq, (c + 1) * tq)
            o_ref[qrows, c * LANES:(c + 1) * LANES] = jnp.where(lo, res[0][rows], res[1][rows]).astype(BF16)


def _swa_call(qa, ka, va, sink_tab, bias, bsz, seq, score_bound):
    t = qa.shape[0]
    tq = SWA_TQ
    step = SWA_SUB * tq
    ns = seq // step
    cur = lambda b, i: (b * ns + i, 0)
    prev = lambda b, i: (b * (seq // tq) + jnp.maximum(SWA_SUB * i - 1, 0), 0)
    full = lambda a: pl.BlockSpec(a.shape, lambda b, i: (0,) * a.ndim)

    def call(shift):
        return pl.pallas_call(
            functools.partial(_swa_kernel, shift=shift),
            grid=(bsz, ns),
            in_specs=[pl.BlockSpec((step, A_WIDTH), cur),
                      pl.BlockSpec((step, LANES), cur), pl.BlockSpec((tq, LANES), prev),
                      pl.BlockSpec((step, LANES), cur), pl.BlockSpec((tq, LANES), prev),
                      full(bias), full(sink_tab)],
            out_specs=pl.BlockSpec((step, A_WIDTH), cur),
            out_shape=jax.ShapeDtypeStruct((t, A_WIDTH), BF16),
            compiler_params=_cparams(("parallel", "parallel")),
            name="swa_shifted" if shift else "swa",
        )(qa, ka, ka, va, va, bias, sink_tab)

    return lax.cond(score_bound <= EXP2_SAFE_BOUND, lambda: call(False), lambda: call(True))


def _mla_kernel(q_ref, k_ref, v_ref, o_ref, acc_sc, m_sc, *, shift):
    tq = q_ref.shape[0]
    tk = MLA_TK
    per_tile = tq // tk
    qi = pl.program_id(2)
    acc_sc[...] = jnp.zeros(acc_sc.shape, F32)
    if shift:
        m_sc[...] = jnp.full(m_sc.shape, NEG, F32)

    def tile(j, r0, r1, masked):
        keys = pl.ds(pl.multiple_of(j * tk, tk), tk)
        if masked:
            causal = (lax.broadcasted_iota(jnp.int32, (r1 - r0, tk), 1)
                      <= lax.broadcasted_iota(jnp.int32, (r1 - r0, tk), 0))
        for half in range(2):
            hs = slice(half * LANES, (half + 1) * LANES)
            s = lax.dot_general(q_ref[r0:r1, hs], k_ref[keys, hs], (((1,), (1,)), ((), ())),
                                preferred_element_type=F32)
            if shift:
                if masked:
                    s = jnp.where(causal, s, NEG)
                m_prev = m_sc[half, r0:r1]
                m_new = jnp.maximum(m_prev, jnp.max(s, axis=-1, keepdims=True))
                m_sc[half, r0:r1] = m_new
                e = jnp.exp2(s - m_new)
                acc_sc[half, r0:r1] = jnp.exp2(m_prev - m_new) * acc_sc[half, r0:r1]
            else:
                e = jnp.exp2(s)
                if masked:
                    e = jnp.where(causal, e, 0.0)
            acc_sc[half, r0:r1] += jnp.dot(e.astype(BF16), v_ref[keys, hs], preferred_element_type=F32)

    def body(j, carry):
        tile(j, 0, tq, False)
        return carry

    lax.fori_loop(0, qi * per_tile, body, 0)
    for d in range(per_tile):
        tile(qi * per_tile + d, d * tk, (d + 1) * tk, True)
        if (d + 1) * tk < tq:
            tile(qi * per_tile + d, (d + 1) * tk, tq, False)
    a0 = acc_sc[0]
    a1 = acc_sc[1]
    lo = lax.broadcasted_iota(jnp.int32, (tq, LANES), 1) < B_V_DIM
    o_ref[...] = jnp.where(lo, a0 * (1.0 / a0[:, V_ONE_EVEN:V_ONE_EVEN + 1]),
                           a1 * (1.0 / a1[:, V_ONE_ODD:V_ONE_ODD + 1])).astype(BF16)


def _mla_call(qb, kb, vb, bsz, seq, score_bound):
    t = qb.shape[0]
    tq = MLA_TQ
    nq = seq // tq
    pairs = B_HEADS // 2

    def call(shift):
        return pl.pallas_call(
            functools.partial(_mla_kernel, shift=shift),
            grid=(bsz, pairs, nq),
            in_specs=[pl.BlockSpec((tq, 2 * LANES), lambda b, p, i: (b * nq + i, p)),
                      pl.BlockSpec((seq, 2 * LANES), lambda b, p, i: (b, p)),
                      pl.BlockSpec((seq, 2 * LANES), lambda b, p, i: (b, p))],
            out_specs=pl.BlockSpec((tq, LANES), lambda b, p, i: (b * nq + i, p)),
            out_shape=jax.ShapeDtypeStruct((t, B_WIDTH), BF16),
            scratch_shapes=[pltpu.VMEM((2, tq, LANES), F32), pltpu.VMEM((2, tq, 1), F32)],
            compiler_params=_cparams(("parallel", "parallel", "arbitrary")),
            name="mla_shifted" if shift else "mla",
        )(qb, kb, vb)

    return lax.cond(score_bound <= EXP2_SAFE_BOUND, lambda: call(False), lambda: call(True))


def _merge_kernel(x_ref, oa_ref, ob_ref, gate_ref, woa_ref, wob_ref, wo_ref, gf_ref, wr_ref, br_ref,
                  tri_ref, x1_ref, w0_ref, w1_ref, eid_ref, cnt_ref):
    tm = x_ref.shape[0]
    ya = jnp.dot(oa_ref[...], woa_ref[...], preferred_element_type=F32)
    yb = jnp.dot(ob_ref[...], wob_ref[...], preferred_element_type=F32)
    g = gate_ref[...].astype(F32)
    mix = (g[:, :D_MODEL] * ya + g[:, D_MODEL:] * yb).astype(BF16)
    x1 = x_ref[...] + jnp.dot(mix, wo_ref[...], preferred_element_type=F32)
    x1_ref[...] = x1

    hn = x1 * lax.rsqrt(jnp.mean(x1 * x1, axis=-1, keepdims=True) + EPS) * gf_ref[...]
    hh = hn.astype(BF16)
    hl = (hn - hh.astype(F32)).astype(BF16)
    ph = jnp.dot(hh, wr_ref[...], preferred_element_type=F32)
    pl_ = jnp.dot(hl, wr_ref[...], preferred_element_type=F32)
    logits = (ph[:, :LANES] + ph[:, LANES:]) + (pl_[:, :LANES] + pl_[:, LANES:]) + br_ref[...]

    lane = lax.broadcasted_iota(jnp.int32, (tm, LANES), 1).astype(F32)
    big = float(LANES)
    gmask = lane < N_GROUPS
    gl = jnp.where(gmask, logits, NEG)
    gmax = jnp.max(gl, axis=-1, keepdims=True)
    gsum = jnp.sum(jnp.where(gmask, jnp.exp(gl - gmax), 0.0), axis=-1, keepdims=True)
    g_p = 1.0 / gsum
    g_idx = jnp.min(jnp.where(gl == gmax, lane, big), axis=-1, keepdims=True)
    e_lo = N_GROUPS + EXPERTS_PER_GROUP * g_idx
    emask = jnp.logical_and(lane >= e_lo, lane < e_lo + EXPERTS_PER_GROUP)
    el = jnp.where(emask, logits, NEG)
    t1 = jnp.max(el, axis=-1, keepdims=True)
    i1 = jnp.min(jnp.where(el == t1, lane, big), axis=-1, keepdims=True)
    el2 = jnp.where(lane == i1, NEG, el)
    t2 = jnp.max(el2, axis=-1, keepdims=True)
    i2 = jnp.min(jnp.where(el2 == t2, lane, big), axis=-1, keepdims=True)
    e2 = jnp.exp(t2 - t1)
    w_first = g_p / (1.0 + e2)
    w0_ref[...] = jnp.broadcast_to(w_first, (tm, LANES))
    w1_ref[...] = jnp.broadcast_to(w_first * e2, (tm, LANES))
    ex1 = i1 - N_GROUPS
    ex2 = i2 - N_GROUPS

    oh1 = lane == ex1
    oh2 = lane == ex2
    cnt = (oh1.astype(F32) + oh2.astype(F32)).astype(BF16)
    before = jnp.dot(tri_ref[...], cnt, preferred_element_type=F32)
    r1 = jnp.sum(jnp.where(oh1, before, 0.0), axis=-1, keepdims=True)
    r2 = jnp.sum(jnp.where(oh2, before, 0.0), axis=-1, keepdims=True)
    ids = jnp.where(lane == 0.0, ex1, jnp.where(lane == 1.0, ex2, jnp.where(lane == 2.0, r1,
                                                                             jnp.where(lane == 3.0, r2, 0.0))))
    eid_ref[...] = ids.astype(jnp.int32)
    tile_cnt = jnp.sum(cnt.astype(F32), axis=0, keepdims=True)
    cnt_ref[...] = jnp.broadcast_to(tile_cnt, cnt_ref.shape[1:]).astype(jnp.int32)[None]


def _merge_call(x2d, oa, ob, gates, p):
    t = x2d.shape[0]
    tm = MERGE_TM
    nt = t // tm
    row = lambda w: pl.BlockSpec((tm, w), lambda i: (i, 0))
    full = lambda a: pl.BlockSpec(a.shape, lambda i: (0,) * a.ndim)
    tri = (jnp.arange(tm)[:, None] > jnp.arange(tm)[None, :]).astype(BF16)
    consts = [p["woa"], p["wob"], p["wo"], p["gf"], p["wr"], p["br"], tri]
    return pl.pallas_call(
        _merge_kernel,
        grid=(nt,),
        in_specs=[row(D_MODEL), row(A_WIDTH), row(B_WIDTH), row(2 * D_MODEL)] + [full(a) for a in consts],
        out_specs=[row(D_MODEL), row(LANES), row(LANES), row(LANES),
                   pl.BlockSpec((1, 8, LANES), lambda i: (i, 0, 0))],
        out_shape=[jax.ShapeDtypeStruct((t, D_MODEL), F32), jax.ShapeDtypeStruct((t, LANES), F32),
                   jax.ShapeDtypeStruct((t, LANES), F32), jax.ShapeDtypeStruct((t, LANES), jnp.int32),
                   jax.ShapeDtypeStruct((nt, 8, LANES), jnp.int32)],
        compiler_params=_cparams(("parallel",)),
        name="merge",
    )(x2d, oa, ob, gates, *consts)


def _rows_copy(src_hbm, row, dst, j, sem):
    return pltpu.make_async_copy(src_hbm.at[row], dst.at[j], sem)


def _start_rows(src_hbm, idx, base, dst, sem, n):
    for j in range(n):
        _rows_copy(src_hbm, idx[base + j], dst, j, sem).start(priority=j % 2)


WAIT_ROWS = 128


def _wait_rows(src_hbm, dst, sem, n):
    for c in range(n // WAIT_ROWS):
        rows = pl.ds(c * WAIT_ROWS, WAIT_ROWS)
        pltpu.make_async_copy(src_hbm.at[rows], dst.at[rows], sem).wait()


def _gather_step(i, n, src_hbm, idx_hbm, buf, idx, sem_rows, sem_idx):
    rows = buf.shape[1]
    slot = i % 2
    nslot = 1 - slot

    def idx_copy(b, s):
        return pltpu.make_async_copy(idx_hbm.at[pl.ds(b * rows, rows)], idx.at[pl.ds(s * rows, rows)], sem_idx.at[s])

    @pl.when(jnp.logical_and(i == 0, n > 0))
    def _():
        idx_copy(0, 0).start()
        idx_copy(0, 0).wait()
        _start_rows(src_hbm, idx, 0, buf.at[0], sem_rows.at[0], rows)

        @pl.when(n > 1)
        def _():
            idx_copy(1, 1).start()

    @pl.when(i + 1 < n)
    def _():
        idx_copy(i + 1, nslot).wait()
        _start_rows(src_hbm, idx, nslot * rows, buf.at[nslot], sem_rows.at[nslot], rows)

    @pl.when(i + 2 < n)
    def _():
        idx_copy(i + 2, slot).start()

    @pl.when(i < n)
    def _():
        _wait_rows(src_hbm, buf.at[slot], sem_rows.at[slot], rows)
    return slot


SCATTER_BUFS = 3


def _pack_bf16_pair(a, b):
    abits = lax.bitcast_convert_type(a.astype(BF16).astype(F32), jnp.uint32)
    bbits = lax.bitcast_convert_type(b.astype(BF16).astype(F32), jnp.uint32)
    return lax.bitcast_convert_type((abits >> 16) | (bbits & jnp.uint32(0xFFFF0000)), F32)


def _unpack_bf16_pair(w):
    bits = lax.bitcast_convert_type(w, jnp.uint32)
    lo = lax.bitcast_convert_type(bits << 16, F32)
    hi = lax.bitcast_convert_type(bits & jnp.uint32(0xFFFF0000), F32)
    return lo, hi


def _scatter_kernel(zero_blk_ref, x_hbm, dest_hbm, gf_ref, xr_hbm, buf, rbuf, idx, sem_x, sem_rows, sem_idx,
                    sem_zero, *, blk_rows):
    i = pl.program_id(0)
    n = pl.num_programs(0)
    tm = buf.shape[1]
    rows = TOP_K * tm
    islot = i % 2
    xslot = i % SCATTER_BUFS

    def idx_copy(b, s):
        return pltpu.make_async_copy(dest_hbm.at[pl.ds(b * rows, rows)], idx.at[pl.ds(s * rows, rows)], sem_idx.at[s])

    def x_copy(b, s):
        return pltpu.make_async_copy(x_hbm.at[pl.ds(b * tm, tm), :], buf.at[s], sem_x.at[s])

    def wait_rows(s):
        for k in range(TOP_K):
            pltpu.make_async_copy(rbuf.at[s], xr_hbm.at[pl.ds(0, tm)], sem_rows.at[s]).wait()

    @pl.when(i == 0)
    def _():
        idx_copy(0, 0).start()
        x_copy(0, 0).start()
        zsrc = rbuf.at[SCATTER_BUFS - 1]
        zsrc[...] = jnp.zeros(zsrc.shape, zsrc.dtype)

        def zero_copies(e):
            b = zero_blk_ref[e]
            return b >= 0, [pltpu.make_async_copy(zsrc, xr_hbm.at[pl.ds(b * blk_rows + h * tm, tm)], sem_zero)
                            for h in range(blk_rows // tm)]

        def start(e, carry):
            live, copies = zero_copies(e)

            @pl.when(live)
            def _():
                for c in copies:
                    c.start()
            return carry

        def wait(e, carry):
            live, copies = zero_copies(e)

            @pl.when(live)
            def _():
                for c in copies:
                    c.wait()
            return carry

        lax.fori_loop(0, zero_blk_ref.shape[0], start, 0)
        lax.fori_loop(0, zero_blk_ref.shape[0], wait, 0)

    @pl.when(i >= 2)
    def _():
        wait_rows((i - 2) % SCATTER_BUFS)

    @pl.when(i + 1 < n)
    def _():
        idx_copy(i + 1, 1 - islot).start()
        x_copy(i + 1, (i + 1) % SCATTER_BUFS).start()

    idx_copy(i, islot).wait()
    x_copy(i, xslot).wait()

    x = buf[xslot]
    hn = x * lax.rsqrt(jnp.mean(x * x, axis=-1, keepdims=True) + EPS) * gf_ref[...]
    rbuf[xslot, :, 0, :] = _pack_bf16_pair(hn[:, :D_MODEL // 2], hn[:, D_MODEL // 2:])
    for j in range(tm):
        src = rbuf.at[xslot, j]
        for k in range(TOP_K):
            dst = xr_hbm.at[idx[islot * rows + k * tm + j]]
            pltpu.make_async_copy(src, dst, sem_rows.at[xslot]).start(priority=k % 2)

    @pl.when(i == n - 1)
    def _():
        @pl.when(n > 1)
        def _():
            wait_rows((i - 1) % SCATTER_BUFS)
        wait_rows(xslot)


def _scatter_call(x1, pos_tiles, zero_blk, gf, n_rows, tm, blk_rows):
    t = x1.shape[0]
    rows = TOP_K * tm
    assert blk_rows % tm == 0
    any_spec = pl.BlockSpec(memory_space=pl.ANY)
    return pl.pallas_call(
        functools.partial(_scatter_kernel, blk_rows=blk_rows),
        grid_spec=pltpu.PrefetchScalarGridSpec(
            num_scalar_prefetch=1,
            grid=(t // tm,),
            in_specs=[any_spec, any_spec, pl.BlockSpec(gf.shape, lambda i, zb: (0, 0))],
            out_specs=any_spec,
            scratch_shapes=[pltpu.VMEM((SCATTER_BUFS, tm, D_MODEL), F32),
                            pltpu.VMEM((SCATTER_BUFS, tm, 1, D_MODEL // 2), F32),
                            pltpu.SMEM((2 * rows,), jnp.int32),
                            pltpu.SemaphoreType.DMA((SCATTER_BUFS,)), pltpu.SemaphoreType.DMA((SCATTER_BUFS,)),
                            pltpu.SemaphoreType.DMA((2,)), pltpu.SemaphoreType.DMA]),
        out_shape=jax.ShapeDtypeStruct((n_rows, 1, D_MODEL // 2), F32),
        compiler_params=_cparams(("arbitrary",)),
        name="scatter",
    )(zero_blk, x1, pos_tiles, gf)


def _expert_kernel(blk_exp_ref, nblk_ref, nvalid_ref, x_ref, wg_ref, wu_ref, wd_ref, y_ref):
    del blk_exp_ref
    i = pl.program_id(0)

    @pl.when(i < nblk_ref[0])
    def _():
        w = x_ref[:, 0, :]
        real = lax.broadcasted_iota(jnp.int32, w.shape, 0) < nvalid_ref[i]
        lo, hi = [v.astype(BF16) for v in _unpack_bf16_pair(jnp.where(real, w, 0.0))]
        half = D_MODEL // 2

        def up_proj(w_ref):
            return (jnp.dot(lo, w_ref[0, :half, :], preferred_element_type=F32)
                    + jnp.dot(hi, w_ref[0, half:, :], preferred_element_type=F32))

        gt, up = up_proj(wg_ref), up_proj(wu_ref)
        act = (gt * (0.5 * jnp.tanh(0.5 * gt) + 0.5) * up).astype(BF16)
        y_ref[:, 0, :] = jnp.dot(act, wd_ref[0], preferred_element_type=F32)

    @pl.when(i >= nblk_ref[0])
    def _():
        y_ref[...] = jnp.zeros(y_ref.shape, y_ref.dtype)


def _expert_call(xr, rows, blk_exp, nblk, nvalid, wg, wu, wd):
    nb = xr.shape[0] // rows
    return pl.pallas_call(
        _expert_kernel,
        grid_spec=pltpu.PrefetchScalarGridSpec(
            num_scalar_prefetch=3,
            grid=(nb,),
            in_specs=[pl.BlockSpec((rows, 1, D_MODEL // 2), lambda i, be, n, nv: (jnp.minimum(i, n[0] - 1), 0, 0)),
                      pl.BlockSpec((1, D_MODEL, D_EXPERT), lambda i, be, n, nv: (be[i], 0, 0)),
                      pl.BlockSpec((1, D_MODEL, D_EXPERT), lambda i, be, n, nv: (be[i], 0, 0)),
                      pl.BlockSpec((1, D_EXPERT, D_MODEL), lambda i, be, n, nv: (be[i], 0, 0))],
            out_specs=pl.BlockSpec((rows, 1, D_MODEL), lambda i, be, n, nv: (i, 0, 0))),
        out_shape=jax.ShapeDtypeStruct((nb * rows, 1, D_MODEL), F32),
        compiler_params=_cparams(("arbitrary",)),
        name="experts",
    )(blk_exp, nblk, nvalid, xr, wg, wu, wd)


def _combine_kernel(x1_ref, w0_ref, w1_ref, y_hbm, pos_hbm, o_ref, buf, idx, sem_rows, sem_idx):
    rows = buf.shape[1]
    tm = rows // 2
    slot = _gather_step(pl.program_id(0), pl.num_programs(0), y_hbm, pos_hbm, buf, idx, sem_rows, sem_idx)
    w0 = w0_ref[...]
    w1 = w1_ref[...]
    for c in range(D_MODEL // LANES):
        sl = slice(c * LANES, (c + 1) * LANES)
        o_ref[:, sl] = x1_ref[:, sl] + w0 * buf[slot, 0:tm, 0, sl] + w1 * buf[slot, tm:rows, 0, sl]


def _combine_call(x1, w0, w1, yr, pos_tiles, tm):
    t = x1.shape[0]
    rows = TOP_K * tm
    row = lambda w: pl.BlockSpec((tm, w), lambda i: (i, 0))
    return pl.pallas_call(
        _combine_kernel,
        grid=(t // tm,),
        in_specs=[row(D_MODEL), row(LANES), row(LANES),
                  pl.BlockSpec(memory_space=pl.ANY), pl.BlockSpec(memory_space=pl.ANY)],
        out_specs=row(D_MODEL),
        out_shape=jax.ShapeDtypeStruct((t, D_MODEL), F32),
        scratch_shapes=[pltpu.VMEM((2, rows, 1, D_MODEL), F32), pltpu.SMEM((2 * rows,), jnp.int32),
                        pltpu.SemaphoreType.DMA((2,)), pltpu.SemaphoreType.DMA((2,))],
        compiler_params=_cparams(("arbitrary",)),
        name="combine",
    )(x1, w0, w1, yr, pos_tiles)


def _rot_cols(w):
    half = ROPE_DIM // 2
    return jnp.concatenate([-w[..., half:], w[..., :half]], axis=-1)


def _head_pad(nope, rope):
    z = jnp.zeros(rope.shape[:-1] + (LANES - B_QK_DIM,), rope.dtype)
    out = jnp.concatenate([nope, rope, z], axis=-1)
    return out.reshape(out.shape[:-2] + (out.shape[-2] * LANES,))


def _t5_bucket(dist):
    max_exact = REL_BUCKETS // 2
    d = jnp.maximum(dist, 0)
    ratio = jnp.maximum(d, 1).astype(F32) / max_exact
    large = max_exact + (jnp.log(ratio) / math.log(REL_MAX_DIST / max_exact)
                         * (REL_BUCKETS - max_exact)).astype(jnp.int32)
    large = jnp.minimum(large, REL_BUCKETS - 1)
    return jnp.where(d < max_exact, d, large)


def _swa_bias(rel_bias, tq):
    qi = jnp.arange(tq)[:, None]
    kj = jnp.arange(2 * tq)[None, :]
    dist = qi + tq - kj
    onehot = (_t5_bucket(dist)[:, :, None] == jnp.arange(REL_BUCKETS)[None, None, :]).astype(F32)
    bias = jnp.einsum("qkb,bh->hqk", onehot, rel_bias.astype(F32), precision=lax.Precision.HIGHEST) * LOG2E
    mask = (dist >= 0) & (dist < WINDOW)
    bias = jnp.where(mask[None], bias, NEG)
    return bias.reshape(A_KV_HEADS, (A_HEADS // A_KV_HEADS) * tq, 2 * tq)


def _rope_tables(seq):
    half = ROPE_DIM // 2
    inv_freq = ROPE_THETA ** (-jnp.arange(half, dtype=F32) / half)
    ang = jnp.arange(seq, dtype=F32)[:, None] * inv_freq[None, :]
    cos = jnp.concatenate([jnp.cos(ang), jnp.cos(ang)], axis=-1)
    sin = jnp.concatenate([jnp.sin(ang), jnp.sin(ang)], axis=-1)
    return cos, sin


def _lane_table(nope_part, rope_part):
    seq = rope_part.shape[0]
    nope_part = jnp.broadcast_to(nope_part, (seq, NOPE_DIM))
    return jnp.concatenate([nope_part, rope_part, jnp.zeros((seq, LANES - B_QK_DIM), F32)], axis=-1)


def _prep_layer(l, seq, cos, sin, attn_norm, w_in, b_gate, a_q_norm, a_k_norm, w_oa, q_a_norm, w_qb, kv_a_norm,
                w_kvb, b_q_norm, b_k_norm, w_ob, w_o, ffn_norm, w_rg, b_rg, w_re, b_re, w_eg, w_eu, w_ed):
    w = w_in[l]
    o = 0
    parts = []
    for sz in (A_WIDTH, A_KV_HEADS * A_HEAD_DIM, A_KV_HEADS * A_HEAD_DIM, Q_LORA, KV_LORA, ROPE_DIM, 2 * D_MODEL):
        parts.append(w[:, o:o + sz])
        o += sz
    wqa, wka, wva, wcq, wckv, wkpe, wg = parts
    perm = jnp.array([c + (A_HEADS // 2) * h for c in range(A_HEADS // 2) for h in range(2)])
    wqa = wqa.reshape(D_MODEL, A_HEADS, A_HEAD_DIM)[:, perm].reshape(D_MODEL, A_WIDTH)
    zl = jnp.zeros((D_MODEL, NOPE_DIM), F32)
    zr = jnp.zeros((D_MODEL, LANES - B_QK_DIM), F32)
    w1 = jnp.concatenate([wqa, wka, wva, wcq, wckv, zl, wkpe, zr, zl, _rot_cols(wkpe), zr, 0.5 * wg],
                         axis=1).astype(BF16)

    wq = w_qb[l].reshape(Q_LORA, B_HEADS, B_QK_DIM)
    zq = jnp.zeros((Q_LORA, B_HEADS, NOPE_DIM), F32)
    wq2 = jnp.concatenate([_head_pad(wq[..., :NOPE_DIM], wq[..., NOPE_DIM:]),
                           _head_pad(zq, _rot_cols(wq[..., NOPE_DIM:]))], axis=1).astype(BF16)
    wkv = w_kvb[l].reshape(KV_LORA, B_HEADS, NOPE_DIM + B_V_DIM)
    wk2 = _head_pad(wkv[..., :NOPE_DIM], jnp.zeros((KV_LORA, B_HEADS, ROPE_DIM), F32)).astype(BF16)
    wv = wkv[..., NOPE_DIM:]
    zv = jnp.zeros_like(wv)
    wv2 = jnp.where((jnp.arange(B_HEADS) % 2 == 0)[None, :, None], jnp.concatenate([wv, zv], -1),
                    jnp.concatenate([zv, wv], -1)).reshape(KV_LORA, B_HEADS * LANES).astype(BF16)
    one_lane = jnp.where(jnp.arange(B_HEADS) % 2 == 0, V_ONE_EVEN, V_ONE_ODD)
    vone = (jnp.arange(LANES)[None, :] == one_lane[:, None]).astype(F32).reshape(1, B_HEADS * LANES)

    def rope_tabs(gain, scale):
        gn, gr = gain[:NOPE_DIM], gain[NOPE_DIM:]
        gr_rot = jnp.concatenate([gr[ROPE_DIM // 2:], gr[:ROPE_DIM // 2]])
        a = _lane_table(gn[None, :] * scale, gr[None, :] * cos * scale)
        b = _lane_table(jnp.zeros((1, NOPE_DIM), F32), gr_rot[None, :] * sin * scale)
        return a, b

    aq, bq = rope_tabs(b_q_norm[l], B_QK_DIM ** -0.5 * LOG2E)
    score_bound = math.sqrt(B_QK_DIM) * LOG2E * jnp.max(jnp.abs(b_q_norm[l])) * jnp.max(jnp.abs(b_k_norm[l]))
    ak, bk = rope_tabs(b_k_norm[l], 1.0)

    gqa = (jnp.tile(a_q_norm[l], 2) * (A_HEAD_DIM ** -0.5 * LOG2E))[None, :]
    gka = jnp.tile(a_k_norm[l], 2)[None, :]
    woa = w_oa[l].reshape(A_HEADS, A_HEAD_DIM, D_MODEL)[perm].reshape(A_WIDTH, D_MODEL).astype(BF16)

    wr = jnp.concatenate([w_rg[l], w_re[l], jnp.zeros((D_MODEL, LANES - N_GROUPS - N_EXPERTS), F32)], axis=1)
    wrh = wr.astype(BF16)
    wrl = (wr - wrh.astype(F32)).astype(BF16)
    br = jnp.concatenate([b_rg[l], b_re[l], jnp.zeros((LANES - N_GROUPS - N_EXPERTS,), F32)])[None, :]

    return dict(
        gn=attn_norm[l][None, :], w1=w1, wq2=wq2, wk2=wk2, wv2=wv2, vone=vone, gcq=q_a_norm[l][None, :],
        gckv=kv_a_norm[l][None, :], score_bound=score_bound, bg=0.5 * b_gate[l][None, :], gqa=gqa, gka=gka,
        aq=aq, bq=bq, ak=ak, bk=bk,
        woa=woa, wob=w_ob[l].astype(BF16), wo=w_o[l].astype(BF16), gf=ffn_norm[l][None, :], wr=jnp.concatenate([wrh, wrl], axis=1), br=br,
        wg=w_eg[l].astype(BF16), wu=w_eu[l].astype(BF16), wd=w_ed[l].astype(BF16))


def _tril(n):
    return (jnp.arange(n)[:, None] >= jnp.arange(n)[None, :]).astype(jnp.int32)


def _dispatch(route, tile_cnt, rows):
    t = route.shape[0]
    nt = tile_cnt.shape[0]
    tile_cnt = tile_cnt[:, 0, :N_EXPERTS]
    counts = jnp.sum(tile_cnt, axis=0)
    padded = ((counts + rows - 1) // rows) * rows
    pends = jnp.dot(_tril(N_EXPERTS), padded)
    pstarts = pends - padded
    tile_base = pstarts[None, :] + jnp.dot(_tril(nt), tile_cnt) - tile_cnt
    base = jnp.broadcast_to(tile_base[:, None, None, :], (nt, t // nt, 1, N_EXPERTS)).reshape(t, 1, N_EXPERTS)
    onehot = route[:, :TOP_K, None] == jnp.arange(N_EXPERTS, dtype=jnp.int32)[None, None, :]
    dest = jnp.sum(jnp.where(onehot, base, 0), axis=-1) + route[:, TOP_K:2 * TOP_K]
    n_rows = t * TOP_K + N_EXPERTS * rows
    n_blocks = n_rows // rows
    blk_start = jnp.arange(n_blocks, dtype=jnp.int32)[:, None] * rows
    blk_exp = jnp.minimum(jnp.sum((pends[None, :] <= blk_start).astype(jnp.int32), axis=1), N_EXPERTS - 1)
    nblk = (pends[-1] // rows).astype(jnp.int32).reshape(1)
    nvalid = jnp.clip((pstarts + counts)[blk_exp] - blk_start[:, 0], 0, rows).astype(jnp.int32)
    last_blk = jnp.where(padded > 0, pends // rows - 1, -1)
    trailing = nblk[0] + jnp.arange(N_EXPERTS, dtype=jnp.int32)
    zero_blk = jnp.concatenate([last_blk, jnp.where(trailing < n_blocks, trailing, -1)]).astype(jnp.int32)
    return dest, blk_exp, nblk, nvalid, zero_blk, n_rows


def _swa_tables(sinks, rel_bias, a_q_norm, a_k_norm):
    per_group = A_HEADS // A_KV_HEADS
    tab = jnp.broadcast_to((sinks * LOG2E).reshape(A_KV_HEADS, per_group, 1, 1),
                           (A_KV_HEADS, per_group, SWA_TQ, LANES)).reshape(A_KV_HEADS, per_group * SWA_TQ, LANES)
    qk = math.sqrt(A_HEAD_DIM) * jnp.max(jnp.abs(a_q_norm)) * jnp.max(jnp.abs(a_k_norm))
    bound = LOG2E * jnp.maximum(qk + jnp.max(jnp.abs(rel_bias)), jnp.max(jnp.abs(sinks)))
    return tab, bound


def _pos_tiles(dest, tm):
    t = dest.shape[0]
    return dest.reshape(t // tm, tm, TOP_K).transpose(0, 2, 1).reshape(t * TOP_K)


def _layer(x2d, bsz, seq, p, sink_tab, swa_bound, bias):
    qa, ka, va, qb, kb, vb, gates = _proj_call(x2d, seq, p)
    oa = _swa_call(qa, ka, va, sink_tab, bias, bsz, seq, swa_bound)
    ob = _mla_call(qb, kb, vb, bsz, seq, p["score_bound"])
    x1, w0, w1, route, tile_cnt = _merge_call(x2d, oa, ob, gates, p)
    dest, blk_exp, nblk, nvalid, zero_blk, n_rows = _dispatch(route, tile_cnt, MOE_ROWS)
    pos_tiles = _pos_tiles(dest, COMB_TM)
    xr = _scatter_call(x1, pos_tiles, zero_blk, p["gf"], n_rows, COMB_TM, MOE_ROWS)
    yr = _expert_call(xr, MOE_ROWS, blk_exp, nblk, nvalid, p["wg"], p["wu"], p["wd"])
    return _combine_call(x1, w0, w1, yr, pos_tiles, COMB_TM)


def kernel(x, attn_norm, w_in, b_gate, a_q_norm, a_k_norm, a_sinks, rel_bias, w_oa, q_a_norm, w_qb, kv_a_norm, w_kvb, b_q_norm, b_k_norm, w_ob, w_o, ffn_norm, w_router_group, b_router_group, w_router_expert, b_router_expert, w_exp_gate, w_exp_up, w_exp_down):
    bsz, seq, d = x.shape
    assert d == D_MODEL and seq % MLA_TQ == 0 and seq % PROJ_TM == 0 and (bsz * seq) % COMB_TM == 0
    depth = w_in.shape[0]
    cos, sin = _rope_tables(seq)
    bias = _swa_bias(rel_bias, SWA_TQ)
    x2d = x.reshape(bsz * seq, d)
    for l in range(depth):
        p = _prep_layer(l, seq, cos, sin, attn_norm, w_in, b_gate, a_q_norm, a_k_norm, w_oa, q_a_norm, w_qb,
                        kv_a_norm, w_kvb, b_q_norm, b_k_norm, w_ob, w_o, ffn_norm, w_router_group,
                        b_router_group, w_router_expert, b_router_expert, w_exp_gate, w_exp_up, w_exp_down)
        sink_tab, swa_bound = _swa_tables(a_sinks[l], rel_bias, a_q_norm[l], a_k_norm[l])
        x2d = _layer(x2d, bsz, seq, p, sink_tab, swa_bound, bias)
    return x2d.reshape(bsz, seq, d)
```

```python
import functools
import math

import jax
import jax.numpy as jnp
from jax import lax
from jax.experimental import pallas as pl
from jax.experimental.pallas import tpu as pltpu

F32 = jnp.float32
BF16 = jnp.bfloat16

D_MODEL = 1024
A_HEADS = 8
A_KV_HEADS = 2
A_HEAD_DIM = 64
A_WIDTH = A_HEADS * A_HEAD_DIM
WINDOW = 128
REL_BUCKETS = 32
REL_MAX_DIST = 128
B_HEADS = 8
Q_LORA = 256
KV_LORA = 128
NOPE_DIM = 64
ROPE_DIM = 32
B_QK_DIM = NOPE_DIM + ROPE_DIM
B_V_DIM = 64
B_WIDTH = B_HEADS * B_V_DIM
ROPE_THETA = 10000.0
N_GROUPS = 4
EXPERTS_PER_GROUP = 8
N_EXPERTS = N_GROUPS * EXPERTS_PER_GROUP
TOP_K = 2
D_EXPERT = 256
EPS = 1e-6

LANES = 128
NEG = -1e30
LOG2E = math.log2(math.e)
V_ONE_EVEN = B_V_DIM
V_ONE_ODD = 0
EXP2_SAFE_BOUND = 100.0
VMEM_LIMIT = 56 * 1024 * 1024

C_QA = 0
C_KA = C_QA + A_WIDTH
C_VA = C_KA + LANES
C_CQ = C_VA + LANES
C_CKV = C_CQ + Q_LORA
C_KPE = C_CKV + KV_LORA
C_KPR = C_KPE + LANES
C_GATE = C_KPR + LANES
C_END = C_GATE + 2 * D_MODEL

PROJ_TM = 512
SWA_TQ = 128
SWA_SUB = 8
MLA_TQ = 1024
MLA_TK = 512
MERGE_TM = 1024
MOE_ROWS = 512
COMB_TM = 256


def _cparams(sem):
    return pltpu.CompilerParams(dimension_semantics=sem, vmem_limit_bytes=VMEM_LIMIT)


def _proj_kernel(x_ref, gn_ref, w1_ref, wq2_ref, wk2_ref, wv2_ref, vone_ref, gcq_ref, gckv_ref, bg_ref,
                 gqa_ref, gka_ref, aq_ref, bq_ref, ak_ref, bk_ref,
                 qa_ref, ka_ref, va_ref, qb_ref, kb_ref, vb_ref, gate_ref):
    tm = x_ref.shape[0]
    x = x_ref[...]
    h = (x * lax.rsqrt(jnp.mean(x * x, axis=-1, keepdims=True) + EPS) * gn_ref[...]).astype(BF16)

    def proj(a, b):
        return jnp.dot(h, w1_ref[:, a:b], preferred_element_type=F32)

    lo = lax.broadcasted_iota(jnp.int32, (tm, LANES), 1) < A_HEAD_DIM

    def pair_norm(t, gain):
        t2 = t * t
        s_lo = jnp.sum(jnp.where(lo, t2, 0.0), axis=-1, keepdims=True)
        s_hi = jnp.sum(jnp.where(lo, 0.0, t2), axis=-1, keepdims=True)
        r = jnp.where(lo, lax.rsqrt(s_lo / A_HEAD_DIM + EPS), lax.rsqrt(s_hi / A_HEAD_DIM + EPS))
        return (t * r * gain).astype(BF16)

    qa = proj(C_QA, C_KA)
    for c in range(A_WIDTH // LANES):
        sl = slice(c * LANES, (c + 1) * LANES)
        qa_ref[:, sl] = pair_norm(qa[:, sl], gqa_ref[...])
    kvp = proj(C_KA, C_CQ)
    ka_ref[...] = pair_norm(kvp[:, :LANES], gka_ref[...])
    va_ref[...] = kvp[:, LANES:].astype(BF16)

    lat = proj(C_CQ, C_GATE)
    cq = lat[:, :Q_LORA]
    cqn = (cq * lax.rsqrt(jnp.mean(cq * cq, axis=-1, keepdims=True) + EPS) * gcq_ref[...]).astype(BF16)
    ckv = lat[:, Q_LORA:Q_LORA + KV_LORA]
    ckvn = (ckv * lax.rsqrt(jnp.mean(ckv * ckv, axis=-1, keepdims=True) + EPS) * gckv_ref[...]).astype(BF16)
    kpe = lat[:, C_KPE - C_CQ:C_KPR - C_CQ]
    kpr = lat[:, C_KPR - C_CQ:C_GATE - C_CQ]

    aq, bq, ak, bk = aq_ref[...], bq_ref[...], ak_ref[...], bk_ref[...]
    q2 = jnp.dot(cqn, wq2_ref[...], preferred_element_type=F32)
    k2 = jnp.dot(ckvn, wk2_ref[...], preferred_element_type=F32)
    vb_ref[...] = (jnp.dot(ckvn, wv2_ref[...], preferred_element_type=F32) + vone_ref[...]).astype(BF16)
    kr = kpr * bk
    for hd in range(B_HEADS):
        sl = slice(hd * LANES, (hd + 1) * LANES)
        qp = q2[:, sl]
        qr = q2[:, B_HEADS * LANES + hd * LANES:B_HEADS * LANES + (hd + 1) * LANES]
        rq = lax.rsqrt(jnp.sum(qp * qp, axis=-1, keepdims=True) / B_QK_DIM + EPS)
        qb_ref[:, sl] = (rq * (qp * aq + qr * bq)).astype(BF16)
        kp = k2[:, sl] + kpe
        rk = lax.rsqrt(jnp.sum(kp * kp, axis=-1, keepdims=True) / B_QK_DIM + EPS)
        kb_ref[:, sl] = (rk * (kp * ak + kr)).astype(BF16)

    gate_ref[...] = (0.5 * jnp.tanh(proj(C_GATE, C_END) + bg_ref[...]) + 0.5).astype(BF16)


def _proj_call(x2d, seq, p):
    t = x2d.shape[0]
    tm = PROJ_TM
    nt = t // tm
    npos = seq // tm
    row = lambda w: pl.BlockSpec((tm, w), lambda i: (i, 0))
    full = lambda a: pl.BlockSpec(a.shape, lambda i: (0,) * a.ndim)
    pos = pl.BlockSpec((tm, LANES), lambda i: (i % npos, 0))
    consts = [p["gn"], p["w1"], p["wq2"], p["wk2"], p["wv2"], p["vone"], p["gcq"], p["gckv"], p["bg"], p["gqa"],
              p["gka"]]
    tabs = [p["aq"], p["bq"], p["ak"], p["bk"]]
    widths = [A_WIDTH, LANES, LANES, B_HEADS * LANES, B_HEADS * LANES, B_HEADS * LANES, 2 * D_MODEL]
    return pl.pallas_call(
        _proj_kernel,
        grid=(nt,),
        in_specs=[row(D_MODEL)] + [full(a) for a in consts] + [pos] * 4,
        out_specs=[row(w) for w in widths],
        out_shape=[jax.ShapeDtypeStruct((t, w), BF16) for w in widths],
        compiler_params=_cparams(("parallel",)),
        name="proj",
    )(x2d, *consts, *tabs)


def _swa_kernel(q_ref, kc_ref, kp_ref, vc_ref, vp_ref, bias_ref, sink_ref, o_ref, *, shift):
    tq = SWA_TQ
    groups = A_WIDTH // LANES
    first = pl.program_id(1) == 0
    one_col = (lax.broadcasted_iota(jnp.int32, (2 * tq, LANES), 1) == 0).astype(BF16)
    lo = lax.broadcasted_iota(jnp.int32, (tq, LANES), 1) < A_HEAD_DIM
    no_prev = jnp.logical_and(first, lax.broadcasted_iota(jnp.int32, (groups * tq, 2 * tq), 1) < tq)
    for sub in range(SWA_SUB):
        if sub == 0:
            k = jnp.concatenate([kp_ref[...], kc_ref[0:tq, :]], axis=0)
            v = jnp.concatenate([vp_ref[...], vc_ref[0:tq, :]], axis=0)
        else:
            k = kc_ref[(sub - 1) * tq:(sub + 1) * tq, :]
            v = vc_ref[(sub - 1) * tq:(sub + 1) * tq, :]
        v_ext = jnp.concatenate([v, one_col], axis=1)
        qrows = slice(sub * tq, (sub + 1) * tq)
        qs = [q_ref[qrows, c * LANES:(c + 1) * LANES] for c in range(groups)]
        res = []
        for g in range(A_KV_HEADS):
            keep = lo if g == 0 else jnp.logical_not(lo)
            qg = jnp.concatenate([jnp.where(keep, qc, jnp.zeros_like(qc)) for qc in qs], axis=0)
            s = lax.dot_general(qg, k, (((1,), (1,)), ((), ())), preferred_element_type=F32) + bias_ref[g]
            sink = sink_ref[g][:, :1]
            if shift:
                if sub == 0:
                    s = jnp.where(no_prev, NEG, s)
                m = jnp.maximum(jnp.max(s, axis=-1, keepdims=True), sink)
                e = jnp.exp2(s - m)
                sink_term = jnp.exp2(sink - m)
            else:
                e = jnp.exp2(s)
                if sub == 0:
                    e = jnp.where(no_prev, 0.0, e)
                sink_term = jnp.exp2(sink)
            pv = jnp.dot(e.astype(BF16), v_ext, preferred_element_type=F32)
            res.append(pv[:, :LANES] * (1.0 / (pv[:, LANES:LANES + 1] + sink_term)))
        for c in range(groups):
            rows = slice(c * tq, (c + 1) * tq)
            o_ref[qrows, c * LANES:(c + 1) * LANES] = jnp.where(lo, res[0][rows], res[1][rows]).astype(BF16)


def _swa_call(qa, ka, va, sink_tab, bias, bsz, seq, score_bound):
    t = qa.shape[0]
    tq = SWA_TQ
    step = SWA_SUB * tq
    ns = seq // step
    cur = lambda b, i: (b * ns + i, 0)
    prev = lambda b, i: (b * (seq // tq) + jnp.maximum(SWA_SUB * i - 1, 0), 0)
    full = lambda a: pl.BlockSpec(a.shape, lambda b, i: (0,) * a.ndim)

    def call(shift):
        return pl.pallas_call(
            functools.partial(_swa_kernel, shift=shift),
            grid=(bsz, ns),
            in_specs=[pl.BlockSpec((step, A_WIDTH), cur),
                      pl.BlockSpec((step, LANES), cur), pl.BlockSpec((tq, LANES), prev),
                      pl.BlockSpec((step, LANES), cur), pl.BlockSpec((tq, LANES), prev),
                      full(bias), full(sink_tab)],
            out_specs=pl.BlockSpec((step, A_WIDTH), cur),
            out_shape=jax.ShapeDtypeStruct((t, A_WIDTH), BF16),
            compiler_params=_cparams(("parallel", "parallel")),
            name="swa_shifted" if shift else "swa",
        )(qa, ka, ka, va, va, bias, sink_tab)

    return lax.cond(score_bound <= EXP2_SAFE_BOUND, lambda: call(False), lambda: call(True))


def _mla_kernel(q_ref, k_ref, v_ref, o_ref, acc_sc, m_sc, *, shift):
    tq = q_ref.shape[0]
    tk = MLA_TK
    per_tile = tq // tk
    qi = pl.program_id(2)
    acc_sc[...] = jnp.zeros(acc_sc.shape, F32)
    if shift:
        m_sc[...] = jnp.full(m_sc.shape, NEG, F32)

    def tile(j, r0, r1, masked):
        keys = pl.ds(pl.multiple_of(j * tk, tk), tk)
        if masked:
            causal = (lax.broadcasted_iota(jnp.int32, (r1 - r0, tk), 1)
                      <= lax.broadcasted_iota(jnp.int32, (r1 - r0, tk), 0))
        for half in range(2):
            hs = slice(half * LANES, (half + 1) * LANES)
            s = lax.dot_general(q_ref[r0:r1, hs], k_ref[keys, hs], (((1,), (1,)), ((), ())),
                                preferred_element_type=F32)
            if shift:
                if masked:
                    s = jnp.where(causal, s, NEG)
                m_prev = m_sc[half, r0:r1]
                m_new = jnp.maximum(m_prev, jnp.max(s, axis=-1, keepdims=True))
                m_sc[half, r0:r1] = m_new
                e = jnp.exp2(s - m_new)
                acc_sc[half, r0:r1] = jnp.exp2(m_prev - m_new) * acc_sc[half, r0:r1]
            else:
                e = jnp.exp2(s)
                if masked:
                    e = jnp.where(causal, e, 0.0)
            acc_sc[half, r0:r1] += jnp.dot(e.astype(BF16), v_ref[keys, hs], preferred_element_type=F32)

    def body(j, carry):
        tile(j, 0, tq, False)
        return carry

    lax.fori_loop(0, qi * per_tile, body, 0)
    for d in range(per_tile):
        tile(qi * per_tile + d, d * tk, (d + 1) * tk, True)
        if (d + 1) * tk < tq:
            tile(qi * per_tile + d, (d + 1) * tk, tq, False)
    a0 = acc_sc[0]
    a1 = acc_sc[1]
    lo = lax.broadcasted_iota(jnp.int32, (tq, LANES), 1) < B_V_DIM
    o_ref[...] = jnp.where(lo, a0 * (1.0 / a0[:, V_ONE_EVEN:V_ONE_EVEN + 1]),
                           a1 * (1.0 / a1[:, V_ONE_ODD:V_ONE_ODD + 1])).astype(BF16)


def _mla_call(qb, kb, vb, bsz, seq, score_bound):
    t = qb.shape[0]
    tq = MLA_TQ
    nq = seq // tq
    pairs = B_HEADS // 2

    def call(shift):
        return pl.pallas_call(
            functools.partial(_mla_kernel, shift=shift),
            grid=(bsz, pairs, nq),
            in_specs=[pl.BlockSpec((tq, 2 * LANES), lambda b, p, i: (b * nq + i, p)),
                      pl.BlockSpec((seq, 2 * LANES), lambda b, p, i: (b, p)),
                      pl.BlockSpec((seq, 2 * LANES), lambda b, p, i: (b, p))],
            out_specs=pl.BlockSpec((tq, LANES), lambda b, p, i: (b * nq + i, p)),
            out_shape=jax.ShapeDtypeStruct((t, B_WIDTH), BF16),
            scratch_shapes=[pltpu.VMEM((2, tq, LANES), F32), pltpu.VMEM((2, tq, 1), F32)],
            compiler_params=_cparams(("parallel", "parallel", "arbitrary")),
            name="mla_shifted" if shift else "mla",
        )(qb, kb, vb)

    return lax.cond(score_bound <= EXP2_SAFE_BOUND, lambda: call(False), lambda: call(True))


def _merge_kernel(x_ref, oa_ref, ob_ref, gate_ref, woa_ref, wob_ref, wo_ref, gf_ref, wr_ref, br_ref,
                  tri_ref, x1_ref, w0_ref, w1_ref, eid_ref, cnt_ref):
    tm = x_ref.shape[0]
    ya = jnp.dot(oa_ref[...], woa_ref[...], preferred_element_type=F32)
    yb = jnp.dot(ob_ref[...], wob_ref[...], preferred_element_type=F32)
    g = gate_ref[...].astype(F32)
    mix = (g[:, :D_MODEL] * ya + g[:, D_MODEL:] * yb).astype(BF16)
    x1 = x_ref[...] + jnp.dot(mix, wo_ref[...], preferred_element_type=F32)
    x1_ref[...] = x1

    hn = x1 * lax.rsqrt(jnp.mean(x1 * x1, axis=-1, keepdims=True) + EPS) * gf_ref[...]
    hh = hn.astype(BF16)
    hl = (hn - hh.astype(F32)).astype(BF16)
    ph = jnp.dot(hh, wr_ref[...], preferred_element_type=F32)
    pl_ = jnp.dot(hl, wr_ref[...], preferred_element_type=F32)
    logits = (ph[:, :LANES] + ph[:, LANES:]) + (pl_[:, :LANES] + pl_[:, LANES:]) + br_ref[...]

    lane = lax.broadcasted_iota(jnp.int32, (tm, LANES), 1).astype(F32)
    big = float(LANES)
    gmask = lane < N_GROUPS
    gl = jnp.where(gmask, logits, NEG)
    gmax = jnp.max(gl, axis=-1, keepdims=True)
    gsum = jnp.sum(jnp.where(gmask, jnp.exp(gl - gmax), 0.0), axis=-1, keepdims=True)
    g_p = 1.0 / gsum
    g_idx = jnp.min(jnp.where(gl == gmax, lane, big), axis=-1, keepdims=True)
    e_lo = N_GROUPS + EXPERTS_PER_GROUP * g_idx
    emask = jnp.logical_and(lane >= e_lo, lane < e_lo + EXPERTS_PER_GROUP)
    el = jnp.where(emask, logits, NEG)
    t1 = jnp.max(el, axis=-1, keepdims=True)
    i1 = jnp.min(jnp.where(el == t1, lane, big), axis=-1, keepdims=True)
    el2 = jnp.where(lane == i1, NEG, el)
    t2 = jnp.max(el2, axis=-1, keepdims=True)
    i2 = jnp.min(jnp.where(el2 == t2, lane, big), axis=-1, keepdims=True)
    e2 = jnp.exp(t2 - t1)
    w_first = g_p / (1.0 + e2)
    w0_ref[...] = jnp.broadcast_to(w_first, (tm, LANES))
    w1_ref[...] = jnp.broadcast_to(w_first * e2, (tm, LANES))
    ex1 = i1 - N_GROUPS
    ex2 = i2 - N_GROUPS

    oh1 = lane == ex1
    oh2 = lane == ex2
    cnt = (oh1.astype(F32) + oh2.astype(F32)).astype(BF16)
    before = jnp.dot(tri_ref[...], cnt, preferred_element_type=F32)
    r1 = jnp.sum(jnp.where(oh1, before, 0.0), axis=-1, keepdims=True)
    r2 = jnp.sum(jnp.where(oh2, before, 0.0), axis=-1, keepdims=True)
    ids = jnp.where(lane == 0.0, ex1, jnp.where(lane == 1.0, ex2, jnp.where(lane == 2.0, r1,
                                                                             jnp.where(lane == 3.0, r2, 0.0))))
    eid_ref[...] = ids.astype(jnp.int32)
    tile_cnt = jnp.sum(cnt.astype(F32), axis=0, keepdims=True)
    cnt_ref[...] = jnp.broadcast_to(tile_cnt, cnt_ref.shape[1:]).astype(jnp.int32)[None]


def _merge_call(x2d, oa, ob, gates, p):
    t = x2d.shape[0]
    tm = MERGE_TM
    nt = t // tm
    row = lambda w: pl.BlockSpec((tm, w), lambda i: (i, 0))
    full = lambda a: pl.BlockSpec(a.shape, lambda i: (0,) * a.ndim)
    tri = (jnp.arange(tm)[:, None] > jnp.arange(tm)[None, :]).astype(BF16)
    consts = [p["woa"], p["wob"], p["wo"], p["gf"], p["wr"], p["br"], tri]
    return pl.pallas_call(
        _merge_kernel,
        grid=(nt,),
        in_specs=[row(D_MODEL), row(A_WIDTH), row(B_WIDTH), row(2 * D_MODEL)] + [full(a) for a in consts],
        out_specs=[row(D_MODEL), row(LANES), row(LANES), row(LANES),
                   pl.BlockSpec((1, 8, LANES), lambda i: (i, 0, 0))],
        out_shape=[jax.ShapeDtypeStruct((t, D_MODEL), F32), jax.ShapeDtypeStruct((t, LANES), F32),
                   jax.ShapeDtypeStruct((t, LANES), F32), jax.ShapeDtypeStruct((t, LANES), jnp.int32),
                   jax.ShapeDtypeStruct((nt, 8, LANES), jnp.int32)],
        compiler_params=_cparams(("parallel",)),
        name="merge",
    )(x2d, oa, ob, gates, *consts)


def _rows_copy(src_hbm, row, dst, j, sem):
    return pltpu.make_async_copy(src_hbm.at[row], dst.at[j], sem)


def _start_rows(src_hbm, idx, base, dst, sem, n):
    for j in range(n):
        _rows_copy(src_hbm, idx[base + j], dst, j, sem).start(priority=j % 2)


WAIT_ROWS = 128


def _wait_rows(src_hbm, dst, sem, n):
    for c in range(n // WAIT_ROWS):
        rows = pl.ds(c * WAIT_ROWS, WAIT_ROWS)
        pltpu.make_async_copy(src_hbm.at[rows], dst.at[rows], sem).wait()


def _gather_step(i, n, src_hbm, idx_hbm, buf, idx, sem_rows, sem_idx):
    rows = buf.shape[1]
    slot = i % 2
    nslot = 1 - slot

    def idx_copy(b, s):
        return pltpu.make_async_copy(idx_hbm.at[pl.ds(b * rows, rows)], idx.at[pl.ds(s * rows, rows)], sem_idx.at[s])

    @pl.when(jnp.logical_and(i == 0, n > 0))
    def _():
        idx_copy(0, 0).start()
        idx_copy(0, 0).wait()
        _start_rows(src_hbm, idx, 0, buf.at[0], sem_rows.at[0], rows)

        @pl.when(n > 1)
        def _():
            idx_copy(1, 1).start()

    @pl.when(i + 1 < n)
    def _():
        idx_copy(i + 1, nslot).wait()
        _start_rows(src_hbm, idx, nslot * rows, buf.at[nslot], sem_rows.at[nslot], rows)

    @pl.when(i + 2 < n)
    def _():
        idx_copy(i + 2, slot).start()

    @pl.when(i < n)
    def _():
        _wait_rows(src_hbm, buf.at[slot], sem_rows.at[slot], rows)
    return slot


SCATTER_BUFS = 3


def _pack_bf16_pair(a, b):
    abits = lax.bitcast_convert_type(a.astype(BF16).astype(F32), jnp.uint32)
    bbits = lax.bitcast_convert_type(b.astype(BF16).astype(F32), jnp.uint32)
    return lax.bitcast_convert_type((abits >> 16) | (bbits & jnp.uint32(0xFFFF0000)), F32)


def _unpack_bf16_pair(w):
    bits = lax.bitcast_convert_type(w, jnp.uint32)
    lo = lax.bitcast_convert_type(bits << 16, F32)
    hi = lax.bitcast_convert_type(bits & jnp.uint32(0xFFFF0000), F32)
    return lo, hi


def _scatter_kernel(zero_blk_ref, x_hbm, dest_hbm, gf_ref, xr_hbm, buf, rbuf, idx, sem_x, sem_rows, sem_idx,
                    sem_zero, *, blk_rows):
    i = pl.program_id(0)
    n = pl.num_programs(0)
    tm = buf.shape[1]
    rows = TOP_K * tm
    islot = i % 2
    xslot = i % SCATTER_BUFS

    def idx_copy(b, s):
        return pltpu.make_async_copy(dest_hbm.at[pl.ds(b * rows, rows)], idx.at[pl.ds(s * rows, rows)], sem_idx.at[s])

    def x_copy(b, s):
        return pltpu.make_async_copy(x_hbm.at[pl.ds(b * tm, tm), :], buf.at[s], sem_x.at[s])

    def wait_rows(s):
        for k in range(TOP_K):
            pltpu.make_async_copy(rbuf.at[s], xr_hbm.at[pl.ds(0, tm)], sem_rows.at[s]).wait()

    @pl.when(i == 0)
    def _():
        idx_copy(0, 0).start()
        x_copy(0, 0).start()
        zsrc = rbuf.at[SCATTER_BUFS - 1]
        zsrc[...] = jnp.zeros(zsrc.shape, zsrc.dtype)

        def zero_copies(e):
            b = zero_blk_ref[e]
            return b >= 0, [pltpu.make_async_copy(zsrc, xr_hbm.at[pl.ds(b * blk_rows + h * tm, tm)], sem_zero)
                            for h in range(blk_rows // tm)]

        def start(e, carry):
            live, copies = zero_copies(e)

            @pl.when(live)
            def _():
                for c in copies:
                    c.start()
            return carry

        def wait(e, carry):
            live, copies = zero_copies(e)

            @pl.when(live)
            def _():
                for c in copies:
                    c.wait()
            return carry

        lax.fori_loop(0, zero_blk_ref.shape[0], start, 0)
        lax.fori_loop(0, zero_blk_ref.shape[0], wait, 0)

    @pl.when(i >= 2)
    def _():
        wait_rows((i - 2) % SCATTER_BUFS)

    @pl.when(i + 1 < n)
    def _():
        idx_copy(i + 1, 1 - islot).start()
        x_copy(i + 1, (i + 1) % SCATTER_BUFS).start()

    idx_copy(i, islot).wait()
    x_copy(i, xslot).wait()

    x = buf[xslot]
    hn = x * lax.rsqrt(jnp.mean(x * x, axis=-1, keepdims=True) + EPS) * gf_ref[...]
    rbuf[xslot, :, 0, :] = _pack_bf16_pair(hn[:, :D_MODEL // 2], hn[:, D_MODEL // 2:])
    for j in range(tm):
        src = rbuf.at[xslot, j]
        for k in range(TOP_K):
            dst = xr_hbm.at[idx[islot * rows + k * tm + j]]
            pltpu.make_async_copy(src, dst, sem_rows.at[xslot]).start(priority=k % 2)

    @pl.when(i == n - 1)
    def _():
        @pl.when(n > 1)
        def _():
            wait_rows((i - 1) % SCATTER_BUFS)
        wait_rows(xslot)


def _scatter_call(x1, pos_tiles, zero_blk, gf, n_rows, tm, blk_rows):
    t = x1.shape[0]
    rows = TOP_K * tm
    assert blk_rows % tm == 0
    any_spec = pl.BlockSpec(memory_space=pl.ANY)
    return pl.pallas_call(
        functools.partial(_scatter_kernel, blk_rows=blk_rows),
        grid_spec=pltpu.PrefetchScalarGridSpec(
            num_scalar_prefetch=1,
            grid=(t // tm,),
            in_specs=[any_spec, any_spec, pl.BlockSpec(gf.shape, lambda i, zb: (0, 0))],
            out_specs=any_spec,
            scratch_shapes=[pltpu.VMEM((SCATTER_BUFS, tm, D_MODEL), F32),
                            pltpu.VMEM((SCATTER_BUFS, tm, 1, D_MODEL // 2), F32),
                            pltpu.SMEM((2 * rows,), jnp.int32),
                            pltpu.SemaphoreType.DMA((SCATTER_BUFS,)), pltpu.SemaphoreType.DMA((SCATTER_BUFS,)),
                            pltpu.SemaphoreType.DMA((2,)), pltpu.SemaphoreType.DMA]),
        out_shape=jax.ShapeDtypeStruct((n_rows, 1, D_MODEL // 2), F32),
        compiler_params=_cparams(("arbitrary",)),
        name="scatter",
    )(zero_blk, x1, pos_tiles, gf)


def _expert_kernel(blk_exp_ref, nblk_ref, nvalid_ref, x_ref, wg_ref, wu_ref, wd_ref, y_ref, wg_bf, wu_bf, wd_bf):
    i = pl.program_id(0)

    @pl.when(jnp.logical_or(i == 0, blk_exp_ref[i] != blk_exp_ref[jnp.maximum(i - 1, 0)]))
    def _():
        wg_bf[...] = wg_ref[0].astype(BF16)
        wu_bf[...] = wu_ref[0].astype(BF16)
        wd_bf[...] = wd_ref[0].astype(BF16)

    @pl.when(i < nblk_ref[0])
    def _():
        w = x_ref[:, 0, :]
        real = lax.broadcasted_iota(jnp.int32, w.shape, 0) < nvalid_ref[i]
        lo, hi = [v.astype(BF16) for v in _unpack_bf16_pair(jnp.where(real, w, 0.0))]
        half = D_MODEL // 2

        def up_proj(w_bf):
            return (jnp.dot(lo, w_bf[:half, :], preferred_element_type=F32)
                    + jnp.dot(hi, w_bf[half:, :], preferred_element_type=F32))

        gt, up = up_proj(wg_bf), up_proj(wu_bf)
        act = (gt * (0.5 * jnp.tanh(0.5 * gt) + 0.5) * up).astype(BF16)
        y_ref[:, 0, :] = jnp.dot(act, wd_bf[...], preferred_element_type=F32)

    @pl.when(i >= nblk_ref[0])
    def _():
        y_ref[...] = jnp.zeros(y_ref.shape, y_ref.dtype)


def _expert_call(xr, rows, blk_exp, nblk, nvalid, wg, wu, wd):
    nb = xr.shape[0] // rows
    return pl.pallas_call(
        _expert_kernel,
        grid_spec=pltpu.PrefetchScalarGridSpec(
            num_scalar_prefetch=3,
            grid=(nb,),
            in_specs=[pl.BlockSpec((rows, 1, D_MODEL // 2), lambda i, be, n, nv: (jnp.minimum(i, n[0] - 1), 0, 0)),
                      pl.BlockSpec((1, D_MODEL, D_EXPERT), lambda i, be, n, nv: (be[i], 0, 0)),
                      pl.BlockSpec((1, D_MODEL, D_EXPERT), lambda i, be, n, nv: (be[i], 0, 0)),
                      pl.BlockSpec((1, D_EXPERT, D_MODEL), lambda i, be, n, nv: (be[i], 0, 0))],
            out_specs=pl.BlockSpec((rows, 1, D_MODEL), lambda i, be, n, nv: (i, 0, 0)),
            scratch_shapes=[pltpu.VMEM((D_MODEL, D_EXPERT), BF16), pltpu.VMEM((D_MODEL, D_EXPERT), BF16),
                            pltpu.VMEM((D_EXPERT, D_MODEL), BF16)]),
        out_shape=jax.ShapeDtypeStruct((nb * rows, 1, D_MODEL), F32),
        compiler_params=_cparams(("arbitrary",)),
        name="experts",
    )(blk_exp, nblk, nvalid, xr, wg, wu, wd)


def _combine_kernel(x1_ref, w0_ref, w1_ref, y_hbm, pos_hbm, o_ref, buf, idx, sem_rows, sem_idx):
    rows = buf.shape[1]
    tm = rows // 2
    slot = _gather_step(pl.program_id(0), pl.num_programs(0), y_hbm, pos_hbm, buf, idx, sem_rows, sem_idx)
    w0 = w0_ref[...]
    w1 = w1_ref[...]
    for c in range(D_MODEL // LANES):
        sl = slice(c * LANES, (c + 1) * LANES)
        o_ref[:, sl] = x1_ref[:, sl] + w0 * buf[slot, 0:tm, 0, sl] + w1 * buf[slot, tm:rows, 0, sl]


def _combine_call(x1, w0, w1, yr, pos_tiles, tm):
    t = x1.shape[0]
    rows = TOP_K * tm
    row = lambda w: pl.BlockSpec((tm, w), lambda i: (i, 0))
    return pl.pallas_call(
        _combine_kernel,
        grid=(t // tm,),
        in_specs=[row(D_MODEL), row(LANES), row(LANES),
                  pl.BlockSpec(memory_space=pl.ANY), pl.BlockSpec(memory_space=pl.ANY)],
        out_specs=row(D_MODEL),
        out_shape=jax.ShapeDtypeStruct((t, D_MODEL), F32),
        scratch_shapes=[pltpu.VMEM((2, rows, 1, D_MODEL), F32), pltpu.SMEM((2 * rows,), jnp.int32),
                        pltpu.SemaphoreType.DMA((2,)), pltpu.SemaphoreType.DMA((2,))],
        compiler_params=_cparams(("arbitrary",)),
        name="combine",
    )(x1, w0, w1, yr, pos_tiles)


def _rot_cols(w):
    half = ROPE_DIM // 2
    return jnp.concatenate([-w[..., half:], w[..., :half]], axis=-1)


def _head_pad(nope, rope):
    z = jnp.zeros(rope.shape[:-1] + (LANES - B_QK_DIM,), rope.dtype)
    out = jnp.concatenate([nope, rope, z], axis=-1)
    return out.reshape(out.shape[:-2] + (out.shape[-2] * LANES,))


def _t5_bucket(dist):
    max_exact = REL_BUCKETS // 2
    d = jnp.maximum(dist, 0)
    ratio = jnp.maximum(d, 1).astype(F32) / max_exact
    large = max_exact + (jnp.log(ratio) / math.log(REL_MAX_DIST / max_exact)
                         * (REL_BUCKETS - max_exact)).astype(jnp.int32)
    large = jnp.minimum(large, REL_BUCKETS - 1)
    return jnp.where(d < max_exact, d, large)


def _swa_bias(rel_bias, tq):
    qi = jnp.arange(tq)[:, None]
    kj = jnp.arange(2 * tq)[None, :]
    dist = qi + tq - kj
    onehot = (_t5_bucket(dist)[:, :, None] == jnp.arange(REL_BUCKETS)[None, None, :]).astype(F32)
    bias = jnp.einsum("qkb,bh->hqk", onehot, rel_bias.astype(F32), precision=lax.Precision.HIGHEST) * LOG2E
    mask = (dist >= 0) & (dist < WINDOW)
    bias = jnp.where(mask[None], bias, NEG)
    return bias.reshape(A_KV_HEADS, (A_HEADS // A_KV_HEADS) * tq, 2 * tq)


def _rope_tables(seq):
    half = ROPE_DIM // 2
    inv_freq = ROPE_THETA ** (-jnp.arange(half, dtype=F32) / half)
    ang = jnp.arange(seq, dtype=F32)[:, None] * inv_freq[None, :]
    cos = jnp.concatenate([jnp.cos(ang), jnp.cos(ang)], axis=-1)
    sin = jnp.concatenate([jnp.sin(ang), jnp.sin(ang)], axis=-1)
    return cos, sin


def _lane_table(nope_part, rope_part):
    seq = rope_part.shape[0]
    nope_part = jnp.broadcast_to(nope_part, (seq, NOPE_DIM))
    return jnp.concatenate([nope_part, rope_part, jnp.zeros((seq, LANES - B_QK_DIM), F32)], axis=-1)


def _prep_layer(l, seq, cos, sin, attn_norm, w_in, b_gate, a_q_norm, a_k_norm, w_oa, q_a_norm, w_qb, kv_a_norm,
                w_kvb, b_q_norm, b_k_norm, w_ob, w_o, ffn_norm, w_rg, b_rg, w_re, b_re, w_eg, w_eu, w_ed):
    w = w_in[l]
    o = 0
    parts = []
    for sz in (A_WIDTH, A_KV_HEADS * A_HEAD_DIM, A_KV_HEADS * A_HEAD_DIM, Q_LORA, KV_LORA, ROPE_DIM, 2 * D_MODEL):
        parts.append(w[:, o:o + sz])
        o += sz
    wqa, wka, wva, wcq, wckv, wkpe, wg = parts
    perm = jnp.array([c + (A_HEADS // 2) * h for c in range(A_HEADS // 2) for h in range(2)])
    wqa = wqa.reshape(D_MODEL, A_HEADS, A_HEAD_DIM)[:, perm].reshape(D_MODEL, A_WIDTH)
    zl = jnp.zeros((D_MODEL, NOPE_DIM), F32)
    zr = jnp.zeros((D_MODEL, LANES - B_QK_DIM), F32)
    w1 = jnp.concatenate([wqa, wka, wva, wcq, wckv, zl, wkpe, zr, zl, _rot_cols(wkpe), zr, 0.5 * wg],
                         axis=1).astype(BF16)

    wq = w_qb[l].reshape(Q_LORA, B_HEADS, B_QK_DIM)
    zq = jnp.zeros((Q_LORA, B_HEADS, NOPE_DIM), F32)
    wq2 = jnp.concatenate([_head_pad(wq[..., :NOPE_DIM], wq[..., NOPE_DIM:]),
                           _head_pad(zq, _rot_cols(wq[..., NOPE_DIM:]))], axis=1).astype(BF16)
    wkv = w_kvb[l].reshape(KV_LORA, B_HEADS, NOPE_DIM + B_V_DIM)
    wk2 = _head_pad(wkv[..., :NOPE_DIM], jnp.zeros((KV_LORA, B_HEADS, ROPE_DIM), F32)).astype(BF16)
    wv = wkv[..., NOPE_DIM:]
    zv = jnp.zeros_like(wv)
    wv2 = jnp.where((jnp.arange(B_HEADS) % 2 == 0)[None, :, None], jnp.concatenate([wv, zv], -1),
                    jnp.concatenate([zv, wv], -1)).reshape(KV_LORA, B_HEADS * LANES).astype(BF16)
    one_lane = jnp.where(jnp.arange(B_HEADS) % 2 == 0, V_ONE_EVEN, V_ONE_ODD)
    vone = (jnp.arange(LANES)[None, :] == one_lane[:, None]).astype(F32).reshape(1, B_HEADS * LANES)

    def rope_tabs(gain, scale):
        gn, gr = gain[:NOPE_DIM], gain[NOPE_DIM:]
        gr_rot = jnp.concatenate([gr[ROPE_DIM // 2:], gr[:ROPE_DIM // 2]])
        a = _lane_table(gn[None, :] * scale, gr[None, :] * cos * scale)
        b = _lane_table(jnp.zeros((1, NOPE_DIM), F32), gr_rot[None, :] * sin * scale)
        return a, b

    aq, bq = rope_tabs(b_q_norm[l], B_QK_DIM ** -0.5 * LOG2E)
    score_bound = math.sqrt(B_QK_DIM) * LOG2E * jnp.max(jnp.abs(b_q_norm[l])) * jnp.max(jnp.abs(b_k_norm[l]))
    ak, bk = rope_tabs(b_k_norm[l], 1.0)

    gqa = (jnp.tile(a_q_norm[l], 2) * (A_HEAD_DIM ** -0.5 * LOG2E))[None, :]
    gka = jnp.tile(a_k_norm[l], 2)[None, :]
    woa = w_oa[l].reshape(A_HEADS, A_HEAD_DIM, D_MODEL)[perm].reshape(A_WIDTH, D_MODEL).astype(BF16)

    wr = jnp.concatenate([w_rg[l], w_re[l], jnp.zeros((D_MODEL, LANES - N_GROUPS - N_EXPERTS), F32)], axis=1)
    wrh = wr.astype(BF16)
    wrl = (wr - wrh.astype(F32)).astype(BF16)
    br = jnp.concatenate([b_rg[l], b_re[l], jnp.zeros((LANES - N_GROUPS - N_EXPERTS,), F32)])[None, :]

    return dict(
        gn=attn_norm[l][None, :], w1=w1, wq2=wq2, wk2=wk2, wv2=wv2, vone=vone, gcq=q_a_norm[l][None, :],
        gckv=kv_a_norm[l][None, :], score_bound=score_bound, bg=0.5 * b_gate[l][None, :], gqa=gqa, gka=gka,
        aq=aq, bq=bq, ak=ak, bk=bk,
        woa=woa, wob=w_ob[l].astype(BF16), wo=w_o[l].astype(BF16), gf=ffn_norm[l][None, :], wr=jnp.concatenate([wrh, wrl], axis=1), br=br,
        wg=w_eg[l], wu=w_eu[l], wd=w_ed[l])


def _tril(n):
    return (jnp.arange(n)[:, None] >= jnp.arange(n)[None, :]).astype(jnp.int32)


def _dispatch(route, tile_cnt, rows):
    t = route.shape[0]
    nt = tile_cnt.shape[0]
    tile_cnt = tile_cnt[:, 0, :N_EXPERTS]
    counts = jnp.sum(tile_cnt, axis=0)
    padded = ((counts + rows - 1) // rows) * rows
    pends = jnp.dot(_tril(N_EXPERTS), padded)
    pstarts = pends - padded
    tile_base = pstarts[None, :] + jnp.dot(_tril(nt), tile_cnt) - tile_cnt
    base = jnp.broadcast_to(tile_base[:, None, None, :], (nt, t // nt, 1, N_EXPERTS)).reshape(t, 1, N_EXPERTS)
    onehot = route[:, :TOP_K, None] == jnp.arange(N_EXPERTS, dtype=jnp.int32)[None, None, :]
    dest = jnp.sum(jnp.where(onehot, base, 0), axis=-1) + route[:, TOP_K:2 * TOP_K]
    n_rows = t * TOP_K + N_EXPERTS * rows
    n_blocks = n_rows // rows
    blk_start = jnp.arange(n_blocks, dtype=jnp.int32)[:, None] * rows
    blk_exp = jnp.minimum(jnp.sum((pends[None, :] <= blk_start).astype(jnp.int32), axis=1), N_EXPERTS - 1)
    nblk = (pends[-1] // rows).astype(jnp.int32).reshape(1)
    nvalid = jnp.clip((pstarts + counts)[blk_exp] - blk_start[:, 0], 0, rows).astype(jnp.int32)
    last_blk = jnp.where(padded > 0, pends // rows - 1, -1)
    trailing = nblk[0] + jnp.arange(N_EXPERTS, dtype=jnp.int32)
    zero_blk = jnp.concatenate([last_blk, jnp.where(trailing < n_blocks, trailing, -1)]).astype(jnp.int32)
    return dest, blk_exp, nblk, nvalid, zero_blk, n_rows


def _swa_tables(sinks, rel_bias, a_q_norm, a_k_norm):
    per_group = A_HEADS // A_KV_HEADS
    tab = jnp.broadcast_to((sinks * LOG2E).reshape(A_KV_HEADS, per_group, 1, 1),
                           (A_KV_HEADS, per_group, SWA_TQ, LANES)).reshape(A_KV_HEADS, per_group * SWA_TQ, LANES)
    qk = math.sqrt(A_HEAD_DIM) * jnp.max(jnp.abs(a_q_norm)) * jnp.max(jnp.abs(a_k_norm))
    bound = LOG2E * jnp.maximum(qk + jnp.max(jnp.abs(rel_bias)), jnp.max(jnp.abs(sinks)))
    return tab, bound


def _pos_tiles(dest, tm):
    t = dest.shape[0]
    return dest.reshape(t // tm, tm, TOP_K).transpose(0, 2, 1).reshape(t * TOP_K)


def _layer(x2d, bsz, seq, p, sink_tab, swa_bound, bias):
    qa, ka, va, qb, kb, vb, gates = _proj_call(x2d, seq, p)
    oa = _swa_call(qa, ka, va, sink_tab, bias, bsz, seq, swa_bound)
    ob = _mla_call(qb, kb, vb, bsz, seq, p["score_bound"])
    x1, w0, w1, route, tile_cnt = _merge_call(x2d, oa, ob, gates, p)
    dest, blk_exp, nblk, nvalid, zero_blk, n_rows = _dispatch(route, tile_cnt, MOE_ROWS)
    pos_tiles = _pos_tiles(dest, COMB_TM)
    xr = _scatter_call(x1, pos_tiles, zero_blk, p["gf"], n_rows, COMB_TM, MOE_ROWS)
    yr = _expert_call(xr, MOE_ROWS, blk_exp, nblk, nvalid, p["wg"], p["wu"], p["wd"])
    return _combine_call(x1, w0, w1, yr, pos_tiles, COMB_TM)


def kernel(x, attn_norm, w_in, b_gate, a_q_norm, a_k_norm, a_sinks, rel_bias, w_oa, q_a_norm, w_qb, kv_a_norm, w_kvb, b_q_norm, b_k_norm, w_ob, w_o, ffn_norm, w_router_group, b_router_group, w_router_expert, b_router_expert, w_exp_gate, w_exp_up, w_exp_down):
    bsz, seq, d = x.shape
    assert d == D_MODEL and seq % MLA_TQ == 0 and seq % PROJ_TM == 0 and (bsz * seq) % COMB_TM == 0
    depth = w_in.shape[0]
    cos, sin = _rope_tables(seq)
    bias = _swa_bias(rel_bias, SWA_TQ)
    x2d = x.reshape(bsz * seq, d)
    for l in range(depth):
        p = _prep_layer(l, seq, cos, sin, attn_norm, w_in, b_gate, a_q_norm, a_k_norm, w_oa, q_a_norm, w_qb,
                        kv_a_norm, w_kvb, b_q_norm, b_k_norm, w_ob, w_o, ffn_norm, w_router_group,
                        b_router_group, w_router_expert, b_router_expert, w_exp_gate, w_exp_up, w_exp_down)
        sink_tab, swa_bound = _swa_tables(a_sinks[l], rel_bias, a_q_norm[l], a_k_norm[l])
        x2d = _layer(x2d, bsz, seq, p, sink_tab, swa_bound, bias)
    return x2d.reshape(bsz, seq, d)
```

```python
import functools
import math

import jax
import jax.numpy as jnp
from jax import lax
from jax.experimental import pallas as pl
from jax.experimental.pallas import tpu as pltpu

F32 = jnp.float32
BF16 = jnp.bfloat16

D_MODEL = 1024
A_HEADS = 8
A_KV_HEADS = 2
A_HEAD_DIM = 64
A_WIDTH = A_HEADS * A_HEAD_DIM
WINDOW = 128
REL_BUCKETS = 32
REL_MAX_DIST = 128
B_HEADS = 8
Q_LORA = 256
KV_LORA = 128
NOPE_DIM = 64
ROPE_DIM = 32
B_QK_DIM = NOPE_DIM + ROPE_DIM
B_V_DIM = 64
B_WIDTH = B_HEADS * B_V_DIM
ROPE_THETA = 10000.0
N_GROUPS = 4
EXPERTS_PER_GROUP = 8
N_EXPERTS = N_GROUPS * EXPERTS_PER_GROUP
TOP_K = 2
D_EXPERT = 256
EPS = 1e-6

LANES = 128
NEG = -1e30
LOG2E = math.log2(math.e)
V_ONE_EVEN = B_V_DIM
V_ONE_ODD = 0
EXP2_SAFE_BOUND = 100.0
VMEM_LIMIT = 56 * 1024 * 1024

C_QA = 0
C_KA = C_QA + A_WIDTH
C_VA = C_KA + LANES
C_CQ = C_VA + LANES
C_CKV = C_CQ + Q_LORA
C_KPE = C_CKV + KV_LORA
C_KPR = C_KPE + LANES
C_GATE = C_KPR + LANES
C_END = C_GATE + 2 * D_MODEL

PROJ_TM = 512
SWA_TQ = 128
SWA_SUB = 8
MLA_TQ = 1024
MLA_TK = 512
MERGE_TM = 1024
MOE_ROWS = 512
COMB_TM = 256


def _cparams(sem):
    return pltpu.CompilerParams(dimension_semantics=sem, vmem_limit_bytes=VMEM_LIMIT)


def _proj_kernel(x_ref, gn_ref, w1_ref, wq2_ref, wk2_ref, wv2_ref, vone_ref, gcq_ref, gckv_ref, bg_ref,
                 gqa_ref, gka_ref, aq_ref, bq_ref, ak_ref, bk_ref,
                 qa_ref, ka_ref, va_ref, qb_ref, kb_ref, vb_ref, gate_ref):
    tm = x_ref.shape[0]
    x = x_ref[...]
    h = (x * lax.rsqrt(jnp.mean(x * x, axis=-1, keepdims=True) + EPS) * gn_ref[...]).astype(BF16)

    def proj(a, b):
        return jnp.dot(h, w1_ref[:, a:b], preferred_element_type=F32)

    lo = lax.broadcasted_iota(jnp.int32, (tm, LANES), 1) < A_HEAD_DIM

    def pair_norm(t, gain):
        t2 = t * t
        s_lo = jnp.sum(jnp.where(lo, t2, 0.0), axis=-1, keepdims=True)
        s_hi = jnp.sum(jnp.where(lo, 0.0, t2), axis=-1, keepdims=True)
        r = jnp.where(lo, lax.rsqrt(s_lo / A_HEAD_DIM + EPS), lax.rsqrt(s_hi / A_HEAD_DIM + EPS))
        return (t * r * gain).astype(BF16)

    qa = proj(C_QA, C_KA)
    for c in range(A_WIDTH // LANES):
        sl = slice(c * LANES, (c + 1) * LANES)
        qa_ref[:, sl] = pair_norm(qa[:, sl], gqa_ref[...])
    kvp = proj(C_KA, C_CQ)
    ka_ref[...] = pair_norm(kvp[:, :LANES], gka_ref[...])
    va_ref[...] = kvp[:, LANES:].astype(BF16)

    lat = proj(C_CQ, C_GATE)
    cq = lat[:, :Q_LORA]
    cqn = (cq * lax.rsqrt(jnp.mean(cq * cq, axis=-1, keepdims=True) + EPS) * gcq_ref[...]).astype(BF16)
    ckv = lat[:, Q_LORA:Q_LORA + KV_LORA]
    ckvn = (ckv * lax.rsqrt(jnp.mean(ckv * ckv, axis=-1, keepdims=True) + EPS) * gckv_ref[...]).astype(BF16)
    kpe = lat[:, C_KPE - C_CQ:C_KPR - C_CQ]
    kpr = lat[:, C_KPR - C_CQ:C_GATE - C_CQ]

    aq, bq, ak, bk = aq_ref[...], bq_ref[...], ak_ref[...], bk_ref[...]
    q2 = jnp.dot(cqn, wq2_ref[...], preferred_element_type=F32)
    k2 = jnp.dot(ckvn, wk2_ref[...], preferred_element_type=F32)
    vb_ref[...] = (jnp.dot(ckvn, wv2_ref[...], preferred_element_type=F32) + vone_ref[...]).astype(BF16)
    kr = kpr * bk
    for hd in range(B_HEADS):
        sl = slice(hd * LANES, (hd + 1) * LANES)
        qp = q2[:, sl]
        qr = q2[:, B_HEADS * LANES + hd * LANES:B_HEADS * LANES + (hd + 1) * LANES]
        rq = lax.rsqrt(jnp.sum(qp * qp, axis=-1, keepdims=True) / B_QK_DIM + EPS)
        qb_ref[:, sl] = (rq * (qp * aq + qr * bq)).astype(BF16)
        kp = k2[:, sl] + kpe
        rk = lax.rsqrt(jnp.sum(kp * kp, axis=-1, keepdims=True) / B_QK_DIM + EPS)
        kb_ref[:, sl] = (rk * (kp * ak + kr)).astype(BF16)

    gate_ref[...] = (0.5 * jnp.tanh(proj(C_GATE, C_END) + bg_ref[...]) + 0.5).astype(BF16)


def _proj_call(x2d, seq, p):
    t = x2d.shape[0]
    tm = PROJ_TM
    nt = t // tm
    npos = seq // tm
    row = lambda w: pl.BlockSpec((tm, w), lambda i: (i, 0))
    full = lambda a: pl.BlockSpec(a.shape, lambda i: (0,) * a.ndim)
    pos = pl.BlockSpec((tm, LANES), lambda i: (i % npos, 0))
    consts = [p["gn"], p["w1"], p["wq2"], p["wk2"], p["wv2"], p["vone"], p["gcq"], p["gckv"], p["bg"], p["gqa"],
              p["gka"]]
    tabs = [p["aq"], p["bq"], p["ak"], p["bk"]]
    widths = [A_WIDTH, LANES, LANES, B_HEADS * LANES, B_HEADS * LANES, B_HEADS * LANES, 2 * D_MODEL]
    return pl.pallas_call(
        _proj_kernel,
        grid=(nt,),
        in_specs=[row(D_MODEL)] + [full(a) for a in consts] + [pos] * 4,
        out_specs=[row(w) for w in widths],
        out_shape=[jax.ShapeDtypeStruct((t, w), BF16) for w in widths],
        compiler_params=_cparams(("parallel",)),
        name="proj",
    )(x2d, *consts, *tabs)


def _swa_kernel(q_ref, kc_ref, kp_ref, vc_ref, vp_ref, bias_ref, sink_ref, o_ref, *, shift):
    tq = SWA_TQ
    groups = A_WIDTH // LANES
    first = pl.program_id(1) == 0
    one_col = (lax.broadcasted_iota(jnp.int32, (2 * tq, LANES), 1) == 0).astype(BF16)
    lo = lax.broadcasted_iota(jnp.int32, (tq, LANES), 1) < A_HEAD_DIM
    no_prev = jnp.logical_and(first, lax.broadcasted_iota(jnp.int32, (groups * tq, 2 * tq), 1) < tq)
    for sub in range(SWA_SUB):
        if sub == 0:
            k = jnp.concatenate([kp_ref[...], kc_ref[0:tq, :]], axis=0)
            v = jnp.concatenate([vp_ref[...], vc_ref[0:tq, :]], axis=0)
        else:
            k = kc_ref[(sub - 1) * tq:(sub + 1) * tq, :]
            v = vc_ref[(sub - 1) * tq:(sub + 1) * tq, :]
        v_ext = jnp.concatenate([v, one_col], axis=1)
        qrows = slice(sub * tq, (sub + 1) * tq)
        qs = [q_ref[qrows, c * LANES:(c + 1) * LANES] for c in range(groups)]
        res = []
        for g in range(A_KV_HEADS):
            keep = lo if g == 0 else jnp.logical_not(lo)
            qg = jnp.concatenate([jnp.where(keep, qc, jnp.zeros_like(qc)) for qc in qs], axis=0)
            s = lax.dot_general(qg, k, (((1,), (1,)), ((), ())), preferred_element_type=F32) + bias_ref[g]
            sink = sink_ref[g][:, :1]
            if shift:
                if sub == 0:
                    s = jnp.where(no_prev, NEG, s)
                m = jnp.maximum(jnp.max(s, axis=-1, keepdims=True), sink)
                e = jnp.exp2(s - m)
                sink_term = jnp.exp2(sink - m)
            else:
                e = jnp.exp2(s)
                if sub == 0:
                    e = jnp.where(no_prev, 0.0, e)
                sink_term = jnp.exp2(sink)
            pv = jnp.dot(e.astype(BF16), v_ext, preferred_element_type=F32)
            res.append(pv[:, :LANES] * (1.0 / (pv[:, LANES:LANES + 1] + sink_term)))
        for c in range(groups):
            rows = slice(c * tq, (c + 1) * tq)
            o_ref[qrows, c * LANES:(c + 1) * LANES] = jnp.where(lo, res[0][rows], res[1][rows]).astype(BF16)


def _swa_call(qa, ka, va, sink_tab, bias, bsz, seq, score_bound):
    t = qa.shape[0]
    tq = SWA_TQ
    step = SWA_SUB * tq
    ns = seq // step
    cur = lambda b, i: (b * ns + i, 0)
    prev = lambda b, i: (b * (seq // tq) + jnp.maximum(SWA_SUB * i - 1, 0), 0)
    full = lambda a: pl.BlockSpec(a.shape, lambda b, i: (0,) * a.ndim)

    def call(shift):
        return pl.pallas_call(
            functools.partial(_swa_kernel, shift=shift),
            grid=(bsz, ns),
            in_specs=[pl.BlockSpec((step, A_WIDTH), cur),
                      pl.BlockSpec((step, LANES), cur), pl.BlockSpec((tq, LANES), prev),
                      pl.BlockSpec((step, LANES), cur), pl.BlockSpec((tq, LANES), prev),
                      full(bias), full(sink_tab)],
            out_specs=pl.BlockSpec((step, A_WIDTH), cur),
            out_shape=jax.ShapeDtypeStruct((t, A_WIDTH), BF16),
            compiler_params=_cparams(("parallel", "parallel")),
            name="swa_shifted" if shift else "swa",
        )(qa, ka, ka, va, va, bias, sink_tab)

    return lax.cond(score_bound <= EXP2_SAFE_BOUND, lambda: call(False), lambda: call(True))


def _mla_kernel(q_ref, k_ref, v_ref, o_ref, acc_sc, m_sc, *, shift):
    tq = q_ref.shape[0]
    tk = MLA_TK
    per_tile = tq // tk
    qi = pl.program_id(2)
    acc_sc[...] = jnp.zeros(acc_sc.shape, F32)
    if shift:
        m_sc[...] = jnp.full(m_sc.shape, NEG, F32)

    def tile(j, r0, r1, masked):
        keys = pl.ds(pl.multiple_of(j * tk, tk), tk)
        if masked:
            causal = (lax.broadcasted_iota(jnp.int32, (r1 - r0, tk), 1)
                      <= lax.broadcasted_iota(jnp.int32, (r1 - r0, tk), 0))
        for half in range(2):
            hs = slice(half * LANES, (half + 1) * LANES)
            s = lax.dot_general(q_ref[r0:r1, hs], k_ref[keys, hs], (((1,), (1,)), ((), ())),
                                preferred_element_type=F32)
            if shift:
                if masked:
                    s = jnp.where(causal, s, NEG)
                m_prev = m_sc[half, r0:r1]
                m_new = jnp.maximum(m_prev, jnp.max(s, axis=-1, keepdims=True))
                m_sc[half, r0:r1] = m_new
                e = jnp.exp2(s - m_new)
                acc_sc[half, r0:r1] = jnp.exp2(m_prev - m_new) * acc_sc[half, r0:r1]
            else:
                e = jnp.exp2(s)
                if masked:
                    e = jnp.where(causal, e, 0.0)
            acc_sc[half, r0:r1] += jnp.dot(e.astype(BF16), v_ref[keys, hs], preferred_element_type=F32)

    def body(j, carry):
        tile(j, 0, tq, False)
        return carry

    lax.fori_loop(0, qi * per_tile, body, 0)
    for d in range(per_tile):
        tile(qi * per_tile + d, d * tk, (d + 1) * tk, True)
        if (d + 1) * tk < tq:
            tile(qi * per_tile + d, (d + 1) * tk, tq, False)
    a0 = acc_sc[0]
    a1 = acc_sc[1]
    lo = lax.broadcasted_iota(jnp.int32, (tq, LANES), 1) < B_V_DIM
    o_ref[...] = jnp.where(lo, a0 * (1.0 / a0[:, V_ONE_EVEN:V_ONE_EVEN + 1]),
                           a1 * (1.0 / a1[:, V_ONE_ODD:V_ONE_ODD + 1])).astype(BF16)


def _mla_call(qb, kb, vb, bsz, seq, score_bound):
    t = qb.shape[0]
    tq = MLA_TQ
    nq = seq // tq
    pairs = B_HEADS // 2

    def call(shift):
        return pl.pallas_call(
            functools.partial(_mla_kernel, shift=shift),
            grid=(bsz, pairs, nq),
            in_specs=[pl.BlockSpec((tq, 2 * LANES), lambda b, p, i: (b * nq + i, p)),
                      pl.BlockSpec((seq, 2 * LANES), lambda b, p, i: (b, p)),
                      pl.BlockSpec((seq, 2 * LANES), lambda b, p, i: (b, p))],
            out_specs=pl.BlockSpec((tq, LANES), lambda b, p, i: (b * nq + i, p)),
            out_shape=jax.ShapeDtypeStruct((t, B_WIDTH), BF16),
            scratch_shapes=[pltpu.VMEM((2, tq, LANES), F32), pltpu.VMEM((2, tq, 1), F32)],
            compiler_params=_cparams(("parallel", "parallel", "arbitrary")),
            name="mla_shifted" if shift else "mla",
        )(qb, kb, vb)

    return lax.cond(score_bound <= EXP2_SAFE_BOUND, lambda: call(False), lambda: call(True))


def _merge_kernel(x_ref, oa_ref, ob_ref, gate_ref, woa_ref, wob_ref, wo_ref, gf_ref, wr_ref, br_ref,
                  tri_ref, x1_ref, w0_ref, w1_ref, eid_ref, cnt_ref):
    tm = x_ref.shape[0]
    ya = jnp.dot(oa_ref[...], woa_ref[...], preferred_element_type=F32)
    yb = jnp.dot(ob_ref[...], wob_ref[...], preferred_element_type=F32)
    g = gate_ref[...].astype(F32)
    mix = (g[:, :D_MODEL] * ya + g[:, D_MODEL:] * yb).astype(BF16)
    x1 = x_ref[...] + jnp.dot(mix, wo_ref[...], preferred_element_type=F32)
    x1_ref[...] = x1

    hn = x1 * lax.rsqrt(jnp.mean(x1 * x1, axis=-1, keepdims=True) + EPS) * gf_ref[...]
    hh = hn.astype(BF16)
    hl = (hn - hh.astype(F32)).astype(BF16)
    ph = jnp.dot(hh, wr_ref[...], preferred_element_type=F32)
    pl_ = jnp.dot(hl, wr_ref[...], preferred_element_type=F32)
    logits = (ph[:, :LANES] + ph[:, LANES:]) + (pl_[:, :LANES] + pl_[:, LANES:]) + br_ref[...]

    lane = lax.broadcasted_iota(jnp.int32, (tm, LANES), 1).astype(F32)
    big = float(LANES)
    gmask = lane < N_GROUPS
    gl = jnp.where(gmask, logits, NEG)
    gmax = jnp.max(gl, axis=-1, keepdims=True)
    gsum = jnp.sum(jnp.where(gmask, jnp.exp(gl - gmax), 0.0), axis=-1, keepdims=True)
    g_p = 1.0 / gsum
    g_idx = jnp.min(jnp.where(gl == gmax, lane, big), axis=-1, keepdims=True)
    e_lo = N_GROUPS + EXPERTS_PER_GROUP * g_idx
    emask = jnp.logical_and(lane >= e_lo, lane < e_lo + EXPERTS_PER_GROUP)
    el = jnp.where(emask, logits, NEG)
    t1 = jnp.max(el, axis=-1, keepdims=True)
    i1 = jnp.min(jnp.where(el == t1, lane, big), axis=-1, keepdims=True)
    el2 = jnp.where(lane == i1, NEG, el)
    t2 = jnp.max(el2, axis=-1, keepdims=True)
    i2 = jnp.min(jnp.where(el2 == t2, lane, big), axis=-1, keepdims=True)
    e2 = jnp.exp(t2 - t1)
    w_first = g_p / (1.0 + e2)
    w0_ref[...] = jnp.broadcast_to(w_first, (tm, LANES))
    w1_ref[...] = jnp.broadcast_to(w_first * e2, (tm, LANES))
    ex1 = i1 - N_GROUPS
    ex2 = i2 - N_GROUPS

    oh1 = lane == ex1
    oh2 = lane == ex2
    cnt = (oh1.astype(F32) + oh2.astype(F32)).astype(BF16)
    before = jnp.dot(tri_ref[...], cnt, preferred_element_type=F32)
    r1 = jnp.sum(jnp.where(oh1, before, 0.0), axis=-1, keepdims=True)
    r2 = jnp.sum(jnp.where(oh2, before, 0.0), axis=-1, keepdims=True)
    ids = jnp.where(lane == 0.0, ex1, jnp.where(lane == 1.0, ex2, jnp.where(lane == 2.0, r1,
                                                                             jnp.where(lane == 3.0, r2, 0.0))))
    eid_ref[...] = ids.astype(jnp.int32)
    tile_cnt = jnp.sum(cnt.astype(F32), axis=0, keepdims=True)
    cnt_ref[...] = jnp.broadcast_to(tile_cnt, cnt_ref.shape[1:]).astype(jnp.int32)[None]


def _merge_call(x2d, oa, ob, gates, p):
    t = x2d.shape[0]
    tm = MERGE_TM
    nt = t // tm
    row = lambda w: pl.BlockSpec((tm, w), lambda i: (i, 0))
    full = lambda a: pl.BlockSpec(a.shape, lambda i: (0,) * a.ndim)
    tri = (jnp.arange(tm)[:, None] > jnp.arange(tm)[None, :]).astype(BF16)
    consts = [p["woa"], p["wob"], p["wo"], p["gf"], p["wr"], p["br"], tri]
    return pl.pallas_call(
        _merge_kernel,
        grid=(nt,),
        in_specs=[row(D_MODEL), row(A_WIDTH), row(B_WIDTH), row(2 * D_MODEL)] + [full(a) for a in consts],
        out_specs=[row(D_MODEL), row(LANES), row(LANES), row(LANES),
                   pl.BlockSpec((1, 8, LANES), lambda i: (i, 0, 0))],
        out_shape=[jax.ShapeDtypeStruct((t, D_MODEL), F32), jax.ShapeDtypeStruct((t, LANES), F32),
                   jax.ShapeDtypeStruct((t, LANES), F32), jax.ShapeDtypeStruct((t, LANES), jnp.int32),
                   jax.ShapeDtypeStruct((nt, 8, LANES), jnp.int32)],
        compiler_params=_cparams(("parallel",)),
        name="merge",
    )(x2d, oa, ob, gates, *consts)


def _rows_copy(src_hbm, row, dst, j, sem):
    return pltpu.make_async_copy(src_hbm.at[row], dst.at[j], sem)


def _start_rows(src_hbm, idx, base, dst, sem, n):
    for j in range(n):
        _rows_copy(src_hbm, idx[base + j], dst, j, sem).start(priority=j % 2)


WAIT_ROWS = 128


def _wait_rows(src_hbm, dst, sem, n):
    for c in range(n // WAIT_ROWS):
        rows = pl.ds(c * WAIT_ROWS, WAIT_ROWS)
        pltpu.make_async_copy(src_hbm.at[rows], dst.at[rows], sem).wait()


def _gather_step(i, n, src_hbm, idx_hbm, buf, idx, sem_rows, sem_idx):
    rows = buf.shape[1]
    slot = i % 2
    nslot = 1 - slot

    def idx_copy(b, s):
        return pltpu.make_async_copy(idx_hbm.at[pl.ds(b * rows, rows)], idx.at[pl.ds(s * rows, rows)], sem_idx.at[s])

    @pl.when(jnp.logical_and(i == 0, n > 0))
    def _():
        idx_copy(0, 0).start()
        idx_copy(0, 0).wait()
        _start_rows(src_hbm, idx, 0, buf.at[0], sem_rows.at[0], rows)

        @pl.when(n > 1)
        def _():
            idx_copy(1, 1).start()

    @pl.when(i + 1 < n)
    def _():
        idx_copy(i + 1, nslot).wait()
        _start_rows(src_hbm, idx, nslot * rows, buf.at[nslot], sem_rows.at[nslot], rows)

    @pl.when(i + 2 < n)
    def _():
        idx_copy(i + 2, slot).start()

    @pl.when(i < n)
    def _():
        _wait_rows(src_hbm, buf.at[slot], sem_rows.at[slot], rows)
    return slot


SCATTER_BUFS = 3


def _pack_bf16_pair(a, b):
    abits = lax.bitcast_convert_type(a.astype(BF16).astype(F32), jnp.uint32)
    bbits = lax.bitcast_convert_type(b.astype(BF16).astype(F32), jnp.uint32)
    return lax.bitcast_convert_type((abits >> 16) | (bbits & jnp.uint32(0xFFFF0000)), F32)


def _unpack_bf16_pair(w):
    bits = lax.bitcast_convert_type(w, jnp.uint32)
    lo = lax.bitcast_convert_type(bits << 16, F32)
    hi = lax.bitcast_convert_type(bits & jnp.uint32(0xFFFF0000), F32)
    return lo, hi


def _scatter_kernel(zero_blk_ref, x_hbm, dest_hbm, gf_ref, xr_hbm, buf, rbuf, idx, sem_x, sem_rows, sem_idx,
                    sem_zero, *, blk_rows):
    i = pl.program_id(0)
    n = pl.num_programs(0)
    tm = buf.shape[1]
    rows = TOP_K * tm
    islot = i % 2
    xslot = i % SCATTER_BUFS

    def idx_copy(b, s):
        return pltpu.make_async_copy(dest_hbm.at[pl.ds(b * rows, rows)], idx.at[pl.ds(s * rows, rows)], sem_idx.at[s])

    def x_copy(b, s):
        return pltpu.make_async_copy(x_hbm.at[pl.ds(b * tm, tm), :], buf.at[s], sem_x.at[s])

    def wait_rows(s):
        for k in range(TOP_K):
            pltpu.make_async_copy(rbuf.at[s], xr_hbm.at[pl.ds(0, tm)], sem_rows.at[s]).wait()

    @pl.when(i == 0)
    def _():
        idx_copy(0, 0).start()
        x_copy(0, 0).start()
        zsrc = rbuf.at[SCATTER_BUFS - 1]
        zsrc[...] = jnp.zeros(zsrc.shape, zsrc.dtype)

        def zero_copies(e):
            b = zero_blk_ref[e]
            return b >= 0, [pltpu.make_async_copy(zsrc, xr_hbm.at[pl.ds(b * blk_rows + h * tm, tm)], sem_zero)
                            for h in range(blk_rows // tm)]

        def start(e, carry):
            live, copies = zero_copies(e)

            @pl.when(live)
            def _():
                for c in copies:
                    c.start()
            return carry

        def wait(e, carry):
            live, copies = zero_copies(e)

            @pl.when(live)
            def _():
                for c in copies:
                    c.wait()
            return carry

        lax.fori_loop(0, zero_blk_ref.shape[0], start, 0)
        lax.fori_loop(0, zero_blk_ref.shape[0], wait, 0)

    @pl.when(i >= 2)
    def _():
        wait_rows((i - 2) % SCATTER_BUFS)

    @pl.when(i + 1 < n)
    def _():
        idx_copy(i + 1, 1 - islot).start()
        x_copy(i + 1, (i + 1) % SCATTER_BUFS).start()

    idx_copy(i, islot).wait()
    x_copy(i, xslot).wait()

    x = buf[xslot]
    hn = x * lax.rsqrt(jnp.mean(x * x, axis=-1, keepdims=True) + EPS) * gf_ref[...]
    rbuf[xslot, :, 0, :] = _pack_bf16_pair(hn[:, :D_MODEL // 2], hn[:, D_MODEL // 2:])
    for j in range(tm):
        src = rbuf.at[xslot, j]
        for k in range(TOP_K):
            dst = xr_hbm.at[idx[islot * rows + k * tm + j]]
            pltpu.make_async_copy(src, dst, sem_rows.at[xslot]).start(priority=k % 2)

    @pl.when(i == n - 1)
    def _():
        @pl.when(n > 1)
        def _():
            wait_rows((i - 1) % SCATTER_BUFS)
        wait_rows(xslot)


def _scatter_call(x1, pos_tiles, zero_blk, gf, n_rows, tm, blk_rows):
    t = x1.shape[0]
    rows = TOP_K * tm
    assert blk_rows % tm == 0
    any_spec = pl.BlockSpec(memory_space=pl.ANY)
    return pl.pallas_call(
        functools.partial(_scatter_kernel, blk_rows=blk_rows),
        grid_spec=pltpu.PrefetchScalarGridSpec(
            num_scalar_prefetch=1,
            grid=(t // tm,),
            in_specs=[any_spec, any_spec, pl.BlockSpec(gf.shape, lambda i, zb: (0, 0))],
            out_specs=any_spec,
            scratch_shapes=[pltpu.VMEM((SCATTER_BUFS, tm, D_MODEL), F32),
                            pltpu.VMEM((SCATTER_BUFS, tm, 1, D_MODEL // 2), F32),
                            pltpu.SMEM((2 * rows,), jnp.int32),
                            pltpu.SemaphoreType.DMA((SCATTER_BUFS,)), pltpu.SemaphoreType.DMA((SCATTER_BUFS,)),
                            pltpu.SemaphoreType.DMA((2,)), pltpu.SemaphoreType.DMA]),
        out_shape=jax.ShapeDtypeStruct((n_rows, 1, D_MODEL // 2), F32),
        compiler_params=_cparams(("arbitrary",)),
        name="scatter",
    )(zero_blk, x1, pos_tiles, gf)


def _expert_kernel(blk_exp_ref, nblk_ref, nvalid_ref, x_ref, wg_ref, wu_ref, wd_ref, y_ref, wg_bf, wu_bf, wd_bf):
    i = pl.program_id(0)

    @pl.when(jnp.logical_or(i == 0, blk_exp_ref[i] != blk_exp_ref[jnp.maximum(i - 1, 0)]))
    def _():
        wg_bf[...] = wg_ref[0].astype(BF16)
        wu_bf[...] = wu_ref[0].astype(BF16)
        wd_bf[...] = wd_ref[0].astype(BF16)

    @pl.when(i < nblk_ref[0])
    def _():
        w = x_ref[:, 0, :]
        real = lax.broadcasted_iota(jnp.int32, w.shape, 0) < nvalid_ref[i]
        lo, hi = [v.astype(BF16) for v in _unpack_bf16_pair(jnp.where(real, w, 0.0))]
        half = D_MODEL // 2

        def up_proj(w_bf):
            return (jnp.dot(lo, w_bf[:half, :], preferred_element_type=F32)
                    + jnp.dot(hi, w_bf[half:, :], preferred_element_type=F32))

        gt, up = up_proj(wg_bf), up_proj(wu_bf)
        act = (gt * (0.5 * jnp.tanh(0.5 * gt) + 0.5) * up).astype(BF16)
        y_ref[:, 0, :] = jnp.dot(act, wd_bf[...], preferred_element_type=F32)

    @pl.when(i >= nblk_ref[0])
    def _():
        y_ref[...] = jnp.zeros(y_ref.shape, y_ref.dtype)


def _expert_call(xr, rows, blk_exp, nblk, nvalid, wg, wu, wd):
    nb = xr.shape[0] // rows
    return pl.pallas_call(
        _expert_kernel,
        grid_spec=pltpu.PrefetchScalarGridSpec(
            num_scalar_prefetch=3,
            grid=(nb,),
            in_specs=[pl.BlockSpec((rows, 1, D_MODEL // 2), lambda i, be, n, nv: (jnp.minimum(i, n[0] - 1), 0, 0)),
                      pl.BlockSpec((1, D_MODEL, D_EXPERT), lambda i, be, n, nv: (be[i], 0, 0)),
                      pl.BlockSpec((1, D_MODEL, D_EXPERT), lambda i, be, n, nv: (be[i], 0, 0)),
                      pl.BlockSpec((1, D_EXPERT, D_MODEL), lambda i, be, n, nv: (be[i], 0, 0))],
            out_specs=pl.BlockSpec((rows, 1, D_MODEL), lambda i, be, n, nv: (i, 0, 0)),
            scratch_shapes=[pltpu.VMEM((D_MODEL, D_EXPERT), BF16), pltpu.VMEM((D_MODEL, D_EXPERT), BF16),
                            pltpu.VMEM((D_EXPERT, D_MODEL), BF16)]),
        out_shape=jax.ShapeDtypeStruct((nb * rows, 1, D_MODEL), F32),
        compiler_params=_cparams(("arbitrary",)),
        name="experts",
    )(blk_exp, nblk, nvalid, xr, wg, wu, wd)


def _combine_kernel(x1_ref, w0_ref, w1_ref, y_hbm, pos_hbm, o_ref, buf, idx, sem_rows, sem_idx):
    rows = buf.shape[1]
    tm = rows // 2
    slot = _gather_step(pl.program_id(0), pl.num_programs(0), y_hbm, pos_hbm, buf, idx, sem_rows, sem_idx)
    w0 = w0_ref[...]
    w1 = w1_ref[...]
    for c in range(D_MODEL // LANES):
        sl = slice(c * LANES, (c + 1) * LANES)
        o_ref[:, sl] = x1_ref[:, sl] + w0 * buf[slot, 0:tm, 0, sl] + w1 * buf[slot, tm:rows, 0, sl]


def _combine_call(x1, w0, w1, yr, pos_tiles, tm):
    t = x1.shape[0]
    rows = TOP_K * tm
    row = lambda w: pl.BlockSpec((tm, w), lambda i: (i, 0))
    return pl.pallas_call(
        _combine_kernel,
        grid=(t // tm,),
        in_specs=[row(D_MODEL), row(LANES), row(LANES),
                  pl.BlockSpec(memory_space=pl.ANY), pl.BlockSpec(memory_space=pl.ANY)],
        out_specs=row(D_MODEL),
        out_shape=jax.ShapeDtypeStruct((t, D_MODEL), F32),
        scratch_shapes=[pltpu.VMEM((2, rows, 1, D_MODEL), F32), pltpu.SMEM((2 * rows,), jnp.int32),
                        pltpu.SemaphoreType.DMA((2,)), pltpu.SemaphoreType.DMA((2,))],
        compiler_params=_cparams(("arbitrary",)),
        name="combine",
    )(x1, w0, w1, yr, pos_tiles)


def _rot_cols(w):
    half = ROPE_DIM // 2
    return jnp.concatenate([-w[..., half:], w[..., :half]], axis=-1)


def _head_pad(nope, rope):
    z = jnp.zeros(rope.shape[:-1] + (LANES - B_QK_DIM,), rope.dtype)
    out = jnp.concatenate([nope, rope, z], axis=-1)
    return out.reshape(out.shape[:-2] + (out.shape[-2] * LANES,))


def _t5_bucket(dist):
    max_exact = REL_BUCKETS // 2
    d = jnp.maximum(dist, 0)
    ratio = jnp.maximum(d, 1).astype(F32) / max_exact
    large = max_exact + (jnp.log(ratio) / math.log(REL_MAX_DIST / max_exact)
                         * (REL_BUCKETS - max_exact)).astype(jnp.int32)
    large = jnp.minimum(large, REL_BUCKETS - 1)
    return jnp.where(d < max_exact, d, large)


def _swa_bias(rel_bias, tq):
    qi = jnp.arange(tq)[:, None]
    kj = jnp.arange(2 * tq)[None, :]
    dist = qi + tq - kj
    onehot = (_t5_bucket(dist)[:, :, None] == jnp.arange(REL_BUCKETS)[None, None, :]).astype(F32)
    bias = jnp.einsum("qkb,bh->hqk", onehot, rel_bias.astype(F32), precision=lax.Precision.HIGHEST) * LOG2E
    mask = (dist >= 0) & (dist < WINDOW)
    bias = jnp.where(mask[None], bias, NEG)
    return bias.reshape(A_KV_HEADS, (A_HEADS // A_KV_HEADS) * tq, 2 * tq)


def _rope_tables(seq):
    half = ROPE_DIM // 2
    inv_freq = ROPE_THETA ** (-jnp.arange(half, dtype=F32) / half)
    ang = jnp.arange(seq, dtype=F32)[:, None] * inv_freq[None, :]
    cos = jnp.concatenate([jnp.cos(ang), jnp.cos(ang)], axis=-1)
    sin = jnp.concatenate([jnp.sin(ang), jnp.sin(ang)], axis=-1)
    return cos, sin


def _lane_table(nope_part, rope_part):
    seq = rope_part.shape[0]
    nope_part = jnp.broadcast_to(nope_part, (seq, NOPE_DIM))
    return jnp.concatenate([nope_part, rope_part, jnp.zeros((seq, LANES - B_QK_DIM), F32)], axis=-1)


def _prep_layer(l, seq, cos, sin, attn_norm, w_in, b_gate, a_q_norm, a_k_norm, w_oa, q_a_norm, w_qb, kv_a_norm,
                w_kvb, b_q_norm, b_k_norm, w_ob, w_o, ffn_norm, w_rg, b_rg, w_re, b_re, w_eg, w_eu, w_ed):
    w = w_in[l]
    o = 0
    parts = []
    for sz in (A_WIDTH, A_KV_HEADS * A_HEAD_DIM, A_KV_HEADS * A_HEAD_DIM, Q_LORA, KV_LORA, ROPE_DIM, 2 * D_MODEL):
        parts.append(w[:, o:o + sz])
        o += sz
    wqa, wka, wva, wcq, wckv, wkpe, wg = parts
    perm = jnp.array([c + (A_HEADS // 2) * h for c in range(A_HEADS // 2) for h in range(2)])
    wqa = wqa.reshape(D_MODEL, A_HEADS, A_HEAD_DIM)[:, perm].reshape(D_MODEL, A_WIDTH)
    zl = jnp.zeros((D_MODEL, NOPE_DIM), F32)
    zr = jnp.zeros((D_MODEL, LANES - B_QK_DIM), F32)
    w1 = jnp.concatenate([wqa, wka, wva, wcq, wckv, zl, wkpe, zr, zl, _rot_cols(wkpe), zr, 0.5 * wg],
                         axis=1).astype(BF16)

    wq = w_qb[l].reshape(Q_LORA, B_HEADS, B_QK_DIM)
    zq = jnp.zeros((Q_LORA, B_HEADS, NOPE_DIM), F32)
    wq2 = jnp.concatenate([_head_pad(wq[..., :NOPE_DIM], wq[..., NOPE_DIM:]),
                           _head_pad(zq, _rot_cols(wq[..., NOPE_DIM:]))], axis=1).astype(BF16)
    wkv = w_kvb[l].reshape(KV_LORA, B_HEADS, NOPE_DIM + B_V_DIM)
    wk2 = _head_pad(wkv[..., :NOPE_DIM], jnp.zeros((KV_LORA, B_HEADS, ROPE_DIM), F32)).astype(BF16)
    wv = wkv[..., NOPE_DIM:]
    zv = jnp.zeros_like(wv)
    wv2 = jnp.where((jnp.arange(B_HEADS) % 2 == 0)[None, :, None], jnp.concatenate([wv, zv], -1),
                    jnp.concatenate([zv, wv], -1)).reshape(KV_LORA, B_HEADS * LANES).astype(BF16)
    one_lane = jnp.where(jnp.arange(B_HEADS) % 2 == 0, V_ONE_EVEN, V_ONE_ODD)
    vone = (jnp.arange(LANES)[None, :] == one_lane[:, None]).astype(F32).reshape(1, B_HEADS * LANES)

    def rope_tabs(gain, scale):
        gn, gr = gain[:NOPE_DIM], gain[NOPE_DIM:]
        gr_rot = jnp.concatenate([gr[ROPE_DIM // 2:], gr[:ROPE_DIM // 2]])
        a = _lane_table(gn[None, :] * scale, gr[None, :] * cos * scale)
        b = _lane_table(jnp.zeros((1, NOPE_DIM), F32), gr_rot[None, :] * sin * scale)
        return a, b

    aq, bq = rope_tabs(b_q_norm[l], B_QK_DIM ** -0.5 * LOG2E)
    score_bound = math.sqrt(B_QK_DIM) * LOG2E * jnp.max(jnp.abs(b_q_norm[l])) * jnp.max(jnp.abs(b_k_norm[l]))
    ak, bk = rope_tabs(b_k_norm[l], 1.0)

    gqa = (jnp.tile(a_q_norm[l], 2) * (A_HEAD_DIM ** -0.5 * LOG2E))[None, :]
    gka = jnp.tile(a_k_norm[l], 2)[None, :]
    woa = w_oa[l].reshape(A_HEADS, A_HEAD_DIM, D_MODEL)[perm].reshape(A_WIDTH, D_MODEL).astype(BF16)

    wr = jnp.concatenate([w_rg[l], w_re[l], jnp.zeros((D_MODEL, LANES - N_GROUPS - N_EXPERTS), F32)], axis=1)
    wrh = wr.astype(BF16)
    wrl = (wr - wrh.astype(F32)).astype(BF16)
    br = jnp.concatenate([b_rg[l], b_re[l], jnp.zeros((LANES - N_GROUPS - N_EXPERTS,), F32)])[None, :]

    return dict(
        gn=attn_norm[l][None, :], w1=w1, wq2=wq2, wk2=wk2, wv2=wv2, vone=vone, gcq=q_a_norm[l][None, :],
        gckv=kv_a_norm[l][None, :], score_bound=score_bound, bg=0.5 * b_gate[l][None, :], gqa=gqa, gka=gka,
        aq=aq, bq=bq, ak=ak, bk=bk,
        woa=woa, wob=w_ob[l].astype(BF16), wo=w_o[l].astype(BF16), gf=ffn_norm[l][None, :], wr=jnp.concatenate([wrh, wrl], axis=1), br=br,
        wg=w_eg.reshape((-1,) + w_eg.shape[2:]), wu=w_eu.reshape((-1,) + w_eu.shape[2:]),
        wd=w_ed.reshape((-1,) + w_ed.shape[2:]), expert_base=l * N_EXPERTS)


def _tril(n):
    return (jnp.arange(n)[:, None] >= jnp.arange(n)[None, :]).astype(jnp.int32)


def _dispatch(route, tile_cnt, rows):
    t = route.shape[0]
    nt = tile_cnt.shape[0]
    tile_cnt = tile_cnt[:, 0, :N_EXPERTS]
    counts = jnp.sum(tile_cnt, axis=0)
    padded = ((counts + rows - 1) // rows) * rows
    pends = jnp.dot(_tril(N_EXPERTS), padded)
    pstarts = pends - padded
    tile_base = pstarts[None, :] + jnp.dot(_tril(nt), tile_cnt) - tile_cnt
    base = jnp.broadcast_to(tile_base[:, None, None, :], (nt, t // nt, 1, N_EXPERTS)).reshape(t, 1, N_EXPERTS)
    onehot = route[:, :TOP_K, None] == jnp.arange(N_EXPERTS, dtype=jnp.int32)[None, None, :]
    dest = jnp.sum(jnp.where(onehot, base, 0), axis=-1) + route[:, TOP_K:2 * TOP_K]
    n_rows = t * TOP_K + N_EXPERTS * rows
    n_blocks = n_rows // rows
    blk_start = jnp.arange(n_blocks, dtype=jnp.int32)[:, None] * rows
    blk_exp = jnp.minimum(jnp.sum((pends[None, :] <= blk_start).astype(jnp.int32), axis=1), N_EXPERTS - 1)
    nblk = (pends[-1] // rows).astype(jnp.int32).reshape(1)
    nvalid = jnp.clip((pstarts + counts)[blk_exp] - blk_start[:, 0], 0, rows).astype(jnp.int32)
    last_blk = jnp.where(padded > 0, pends // rows - 1, -1)
    trailing = nblk[0] + jnp.arange(N_EXPERTS, dtype=jnp.int32)
    zero_blk = jnp.concatenate([last_blk, jnp.where(trailing < n_blocks, trailing, -1)]).astype(jnp.int32)
    return dest, blk_exp, nblk, nvalid, zero_blk, n_rows


def _swa_tables(sinks, rel_bias, a_q_norm, a_k_norm):
    per_group = A_HEADS // A_KV_HEADS
    tab = jnp.broadcast_to((sinks * LOG2E).reshape(A_KV_HEADS, per_group, 1, 1),
                           (A_KV_HEADS, per_group, SWA_TQ, LANES)).reshape(A_KV_HEADS, per_group * SWA_TQ, LANES)
    qk = math.sqrt(A_HEAD_DIM) * jnp.max(jnp.abs(a_q_norm)) * jnp.max(jnp.abs(a_k_norm))
    bound = LOG2E * jnp.maximum(qk + jnp.max(jnp.abs(rel_bias)), jnp.max(jnp.abs(sinks)))
    return tab, bound


def _pos_tiles(dest, tm):
    t = dest.shape[0]
    return dest.reshape(t // tm, tm, TOP_K).transpose(0, 2, 1).reshape(t * TOP_K)


def _layer(x2d, bsz, seq, p, sink_tab, swa_bound, bias):
    qa, ka, va, qb, kb, vb, gates = _proj_call(x2d, seq, p)
    oa = _swa_call(qa, ka, va, sink_tab, bias, bsz, seq, swa_bound)
    ob = _mla_call(qb, kb, vb, bsz, seq, p["score_bound"])
    x1, w0, w1, route, tile_cnt = _merge_call(x2d, oa, ob, gates, p)
    dest, blk_exp, nblk, nvalid, zero_blk, n_rows = _dispatch(route, tile_cnt, MOE_ROWS)
    pos_tiles = _pos_tiles(dest, COMB_TM)
    xr = _scatter_call(x1, pos_tiles, zero_blk, p["gf"], n_rows, COMB_TM, MOE_ROWS)
    yr = _expert_call(xr, MOE_ROWS, blk_exp + p["expert_base"], nblk, nvalid, p["wg"], p["wu"], p["wd"])
    return _combine_call(x1, w0, w1, yr, pos_tiles, COMB_TM)


def kernel(x, attn_norm, w_in, b_gate, a_q_norm, a_k_norm, a_sinks, rel_bias, w_oa, q_a_norm, w_qb, kv_a_norm, w_kvb, b_q_norm, b_k_norm, w_ob, w_o, ffn_norm, w_router_group, b_router_group, w_router_expert, b_router_expert, w_exp_gate, w_exp_up, w_exp_down):
    bsz, seq, d = x.shape
    assert d == D_MODEL and seq % MLA_TQ == 0 and seq % PROJ_TM == 0 and (bsz * seq) % COMB_TM == 0
    depth = w_in.shape[0]
    cos, sin = _rope_tables(seq)
    bias = _swa_bias(rel_bias, SWA_TQ)
    x2d = x.reshape(bsz * seq, d)
    for l in range(depth):
        p = _prep_layer(l, seq, cos, sin, attn_norm, w_in, b_gate, a_q_norm, a_k_norm, w_oa, q_a_norm, w_qb,
                        kv_a_norm, w_kvb, b_q_norm, b_k_norm, w_ob, w_o, ffn_norm, w_router_group,
                        b_router_group, w_router_expert, b_router_expert, w_exp_gate, w_exp_up, w_exp_down)
        sink_tab, swa_bound = _swa_tables(a_sinks[l], rel_bias, a_q_norm[l], a_k_norm[l])
        x2d = _layer(x2d, bsz, seq, p, sink_tab, swa_bound, bias)
    return x2d.reshape(bsz, seq, d)
```

```python
import functools
import math

import jax
import jax.numpy as jnp
from jax import lax
from jax.experimental import pallas as pl
from jax.experimental.pallas import tpu as pltpu

F32 = jnp.float32
BF16 = jnp.bfloat16

D_MODEL = 1024
A_HEADS = 8
A_KV_HEADS = 2
A_HEAD_DIM = 64
A_WIDTH = A_HEADS * A_HEAD_DIM
WINDOW = 128
REL_BUCKETS = 32
REL_MAX_DIST = 128
B_HEADS = 8
Q_LORA = 256
KV_LORA = 128
NOPE_DIM = 64
ROPE_DIM = 32
B_QK_DIM = NOPE_DIM + ROPE_DIM
B_V_DIM = 64
B_WIDTH = B_HEADS * B_V_DIM
ROPE_THETA = 10000.0
N_GROUPS = 4
EXPERTS_PER_GROUP = 8
N_EXPERTS = N_GROUPS * EXPERTS_PER_GROUP
TOP_K = 2
D_EXPERT = 256
EPS = 1e-6

LANES = 128
NEG = -1e30
LOG2E = math.log2(math.e)
V_ONE_EVEN = B_V_DIM
V_ONE_ODD = 0
EXP2_SAFE_BOUND = 100.0
VMEM_LIMIT = 56 * 1024 * 1024

C_QA = 0
C_KA = C_QA + A_WIDTH
C_VA = C_KA + LANES
C_CQ = C_VA + LANES
C_CKV = C_CQ + Q_LORA
C_KPE = C_CKV + KV_LORA
C_KPR = C_KPE + LANES
C_GATE = C_KPR + LANES
C_END = C_GATE + 2 * D_MODEL

PROJ_TM = 512
SWA_TQ = 128
SWA_SUB = 8
MLA_TQ = 1024
MLA_TK = 512
MERGE_TM = 1024
MOE_ROWS = 512
COMB_TM = 256


def _cparams(sem):
    return pltpu.CompilerParams(dimension_semantics=sem, vmem_limit_bytes=VMEM_LIMIT)


def _proj_kernel(x_ref, gn_ref, w1_ref, wq2_ref, wk2_ref, wv2_ref, vone_ref, gcq_ref, gckv_ref, bg_ref,
                 gqa_ref, gka_ref, aq_ref, bq_ref, ak_ref, bk_ref,
                 qa_ref, ka_ref, va_ref, qb_ref, kb_ref, vb_ref, gate_ref):
    tm = x_ref.shape[0]
    x = x_ref[...]
    h = (x * lax.rsqrt(jnp.mean(x * x, axis=-1, keepdims=True) + EPS) * gn_ref[...]).astype(BF16)

    def proj(a, b):
        return jnp.dot(h, w1_ref[:, a:b], preferred_element_type=F32)

    lo = lax.broadcasted_iota(jnp.int32, (tm, LANES), 1) < A_HEAD_DIM

    def pair_norm(t, gain):
        t2 = t * t
        s_lo = jnp.sum(jnp.where(lo, t2, 0.0), axis=-1, keepdims=True)
        s_hi = jnp.sum(jnp.where(lo, 0.0, t2), axis=-1, keepdims=True)
        r = jnp.where(lo, lax.rsqrt(s_lo / A_HEAD_DIM + EPS), lax.rsqrt(s_hi / A_HEAD_DIM + EPS))
        return (t * r * gain).astype(BF16)

    qa = proj(C_QA, C_KA)
    for c in range(A_WIDTH // LANES):
        sl = slice(c * LANES, (c + 1) * LANES)
        qa_ref[:, sl] = pair_norm(qa[:, sl], gqa_ref[...])
    kvp = proj(C_KA, C_CQ)
    ka_ref[...] = pair_norm(kvp[:, :LANES], gka_ref[...])
    va_ref[...] = kvp[:, LANES:].astype(BF16)

    lat = proj(C_CQ, C_GATE)
    cq = lat[:, :Q_LORA]
    cqn = (cq * lax.rsqrt(jnp.mean(cq * cq, axis=-1, keepdims=True) + EPS) * gcq_ref[...]).astype(BF16)
    ckv = lat[:, Q_LORA:Q_LORA + KV_LORA]
    ckvn = (ckv * lax.rsqrt(jnp.mean(ckv * ckv, axis=-1, keepdims=True) + EPS) * gckv_ref[...]).astype(BF16)
    kpe = lat[:, C_KPE - C_CQ:C_KPR - C_CQ]
    kpr = lat[:, C_KPR - C_CQ:C_GATE - C_CQ]

    aq, bq, ak, bk = aq_ref[...], bq_ref[...], ak_ref[...], bk_ref[...]
    q2 = jnp.dot(cqn, wq2_ref[...], preferred_element_type=F32)
    k2 = jnp.dot(ckvn, wk2_ref[...], preferred_element_type=F32)
    vb_ref[...] = (jnp.dot(ckvn, wv2_ref[...], preferred_element_type=F32) + vone_ref[...]).astype(BF16)
    kr = kpr * bk
    for hd in range(B_HEADS):
        sl = slice(hd * LANES, (hd + 1) * LANES)
        qp = q2[:, sl]
        qr = q2[:, B_HEADS * LANES + hd * LANES:B_HEADS * LANES + (hd + 1) * LANES]
        rq = lax.rsqrt(jnp.sum(qp * qp, axis=-1, keepdims=True) / B_QK_DIM + EPS)
        qb_ref[:, sl] = (rq * (qp * aq + qr * bq)).astype(BF16)
        kp = k2[:, sl] + kpe
        rk = lax.rsqrt(jnp.sum(kp * kp, axis=-1, keepdims=True) / B_QK_DIM + EPS)
        kb_ref[:, sl] = (rk * (kp * ak + kr)).astype(BF16)

    gate_ref[...] = (0.5 * jnp.tanh(proj(C_GATE, C_END) + bg_ref[...]) + 0.5).astype(BF16)


def _proj_call(x2d, seq, p):
    t = x2d.shape[0]
    tm = PROJ_TM
    nt = t // tm
    npos = seq // tm
    row = lambda w: pl.BlockSpec((tm, w), lambda i: (i, 0))
    full = lambda a: pl.BlockSpec(a.shape, lambda i: (0,) * a.ndim)
    pos = pl.BlockSpec((tm, LANES), lambda i: (i % npos, 0))
    consts = [p["gn"], p["w1"], p["wq2"], p["wk2"], p["wv2"], p["vone"], p["gcq"], p["gckv"], p["bg"], p["gqa"],
              p["gka"]]
    tabs = [p["aq"], p["bq"], p["ak"], p["bk"]]
    widths = [A_WIDTH, LANES, LANES, B_HEADS * LANES, B_HEADS * LANES, B_HEADS * LANES, 2 * D_MODEL]
    return pl.pallas_call(
        _proj_kernel,
        grid=(nt,),
        in_specs=[row(D_MODEL)] + [full(a) for a in consts] + [pos] * 4,
        out_specs=[row(w) for w in widths],
        out_shape=[jax.ShapeDtypeStruct((t, w), BF16) for w in widths],
        compiler_params=_cparams(("parallel",)),
        name="proj",
    )(x2d, *consts, *tabs)


def _swa_kernel(q_ref, kc_ref, kp_ref, vc_ref, vp_ref, bias_ref, sink_ref, o_ref, *, shift):
    tq = SWA_TQ
    groups = A_WIDTH // LANES
    first = pl.program_id(1) == 0
    one_col = (lax.broadcasted_iota(jnp.int32, (2 * tq, LANES), 1) == 0).astype(BF16)
    lo = lax.broadcasted_iota(jnp.int32, (tq, LANES), 1) < A_HEAD_DIM
    no_prev = jnp.logical_and(first, lax.broadcasted_iota(jnp.int32, (groups * tq, 2 * tq), 1) < tq)
    for sub in range(SWA_SUB):
        if sub == 0:
            k = jnp.concatenate([kp_ref[...], kc_ref[0:tq, :]], axis=0)
            v = jnp.concatenate([vp_ref[...], vc_ref[0:tq, :]], axis=0)
        else:
            k = kc_ref[(sub - 1) * tq:(sub + 1) * tq, :]
            v = vc_ref[(sub - 1) * tq:(sub + 1) * tq, :]
        v_ext = jnp.concatenate([v, one_col], axis=1)
        qrows = slice(sub * tq, (sub + 1) * tq)
        qs = [q_ref[qrows, c * LANES:(c + 1) * LANES] for c in range(groups)]
        res = []
        for g in range(A_KV_HEADS):
            keep = lo if g == 0 else jnp.logical_not(lo)
            qg = jnp.concatenate([jnp.where(keep, qc, jnp.zeros_like(qc)) for qc in qs], axis=0)
            s = lax.dot_general(qg, k, (((1,), (1,)), ((), ())), preferred_element_type=F32) + bias_ref[g]
            sink = sink_ref[g][:, :1]
            if shift:
                if sub == 0:
                    s = jnp.where(no_prev, NEG, s)
                m = jnp.maximum(jnp.max(s, axis=-1, keepdims=True), sink)
                e = jnp.exp2(s - m)
                sink_term = jnp.exp2(sink - m)
            else:
                e = jnp.exp2(s)
                if sub == 0:
                    e = jnp.where(no_prev, 0.0, e)
                sink_term = jnp.exp2(sink)
            pv = jnp.dot(e.astype(BF16), v_ext, preferred_element_type=F32)
            res.append(pv[:, :LANES] * (1.0 / (pv[:, LANES:LANES + 1] + sink_term)))
        for c in range(groups):
            rows = slice(c * tq, (c + 1) * tq)
            o_ref[qrows, c * LANES:(c + 1) * LANES] = jnp.where(lo, res[0][rows], res[1][rows]).astype(BF16)


def _swa_call(qa, ka, va, sink_tab, bias, bsz, seq, score_bound):
    t = qa.shape[0]
    tq = SWA_TQ
    step = SWA_SUB * tq
    ns = seq // step
    cur = lambda b, i: (b * ns + i, 0)
    prev = lambda b, i: (b * (seq // tq) + jnp.maximum(SWA_SUB * i - 1, 0), 0)
    full = lambda a: pl.BlockSpec(a.shape, lambda b, i: (0,) * a.ndim)

    def call(shift):
        return pl.pallas_call(
            functools.partial(_swa_kernel, shift=shift),
            grid=(bsz, ns),
            in_specs=[pl.BlockSpec((step, A_WIDTH), cur),
                      pl.BlockSpec((step, LANES), cur), pl.BlockSpec((tq, LANES), prev),
                      pl.BlockSpec((step, LANES), cur), pl.BlockSpec((tq, LANES), prev),
                      full(bias), full(sink_tab)],
            out_specs=pl.BlockSpec((step, A_WIDTH), cur),
            out_shape=jax.ShapeDtypeStruct((t, A_WIDTH), BF16),
            compiler_params=_cparams(("parallel", "parallel")),
            name="swa_shifted" if shift else "swa",
        )(qa, ka, ka, va, va, bias, sink_tab)

    return lax.cond(score_bound <= EXP2_SAFE_BOUND, lambda: call(False), lambda: call(True))


def _mla_kernel(q_ref, k_ref, v_ref, o_ref, acc_sc, m_sc, *, shift):
    tq = q_ref.shape[0]
    tk = MLA_TK
    per_tile = tq // tk
    qi = pl.program_id(2)
    acc_sc[...] = jnp.zeros(acc_sc.shape, F32)
    if shift:
        m_sc[...] = jnp.full(m_sc.shape, NEG, F32)

    def tile(j, r0, r1, masked):
        keys = pl.ds(pl.multiple_of(j * tk, tk), tk)
        if masked:
            causal = (lax.broadcasted_iota(jnp.int32, (r1 - r0, tk), 1)
                      <= lax.broadcasted_iota(jnp.int32, (r1 - r0, tk), 0))
        for half in range(2):
            hs = slice(half * LANES, (half + 1) * LANES)
            s = lax.dot_general(q_ref[r0:r1, hs], k_ref[keys, hs], (((1,), (1,)), ((), ())),
                                preferred_element_type=F32)
            if shift:
                if masked:
                    s = jnp.where(causal, s, NEG)
                m_prev = m_sc[half, r0:r1]
                m_new = jnp.maximum(m_prev, jnp.max(s, axis=-1, keepdims=True))
                m_sc[half, r0:r1] = m_new
                e = jnp.exp2(s - m_new)
                acc_sc[half, r0:r1] = jnp.exp2(m_prev - m_new) * acc_sc[half, r0:r1]
            else:
                e = jnp.exp2(s)
                if masked:
                    e = jnp.where(causal, e, 0.0)
            acc_sc[half, r0:r1] += jnp.dot(e.astype(BF16), v_ref[keys, hs], preferred_element_type=F32)

    def body(j, carry):
        tile(j, 0, tq, False)
        return carry

    lax.fori_loop(0, qi * per_tile, body, 0)
    for d in range(per_tile):
        tile(qi * per_tile + d, d * tk, (d + 1) * tk, True)
        if (d + 1) * tk < tq:
            tile(qi * per_tile + d, (d + 1) * tk, tq, False)
    a0 = acc_sc[0]
    a1 = acc_sc[1]
    lo = lax.broadcasted_iota(jnp.int32, (tq, LANES), 1) < B_V_DIM
    o_ref[...] = jnp.where(lo, a0 * (1.0 / a0[:, V_ONE_EVEN:V_ONE_EVEN + 1]),
                           a1 * (1.0 / a1[:, V_ONE_ODD:V_ONE_ODD + 1])).astype(BF16)


def _mla_call(qb, kb, vb, bsz, seq, score_bound):
    t = qb.shape[0]
    tq = MLA_TQ
    nq = seq // tq
    pairs = B_HEADS // 2

    def call(shift):
        return pl.pallas_call(
            functools.partial(_mla_kernel, shift=shift),
            grid=(bsz, pairs, nq),
            in_specs=[pl.BlockSpec((tq, 2 * LANES), lambda b, p, i: (b * nq + i, p)),
                      pl.BlockSpec((seq, 2 * LANES), lambda b, p, i: (b, p)),
                      pl.BlockSpec((seq, 2 * LANES), lambda b, p, i: (b, p))],
            out_specs=pl.BlockSpec((tq, LANES), lambda b, p, i: (b * nq + i, p)),
            out_shape=jax.ShapeDtypeStruct((t, B_WIDTH), BF16),
            scratch_shapes=[pltpu.VMEM((2, tq, LANES), F32), pltpu.VMEM((2, tq, 1), F32)],
            compiler_params=_cparams(("parallel", "parallel", "arbitrary")),
            name="mla_shifted" if shift else "mla",
        )(qb, kb, vb)

    return lax.cond(score_bound <= EXP2_SAFE_BOUND, lambda: call(False), lambda: call(True))


def _merge_kernel(x_ref, oa_ref, ob_ref, gate_ref, woa_ref, wob_ref, wo_ref, gf_ref, wr_ref, br_ref,
                  tri_ref, x1_ref, w0_ref, w1_ref, eid_ref, cnt_ref):
    tm = x_ref.shape[0]
    ya = jnp.dot(oa_ref[...], woa_ref[...], preferred_element_type=F32)
    yb = jnp.dot(ob_ref[...], wob_ref[...], preferred_element_type=F32)
    g = gate_ref[...].astype(F32)
    mix = (g[:, :D_MODEL] * ya + g[:, D_MODEL:] * yb).astype(BF16)
    x1 = x_ref[...] + jnp.dot(mix, wo_ref[...], preferred_element_type=F32)
    x1_ref[...] = x1

    hn = x1 * lax.rsqrt(jnp.mean(x1 * x1, axis=-1, keepdims=True) + EPS) * gf_ref[...]
    hh = hn.astype(BF16)
    hl = (hn - hh.astype(F32)).astype(BF16)
    ph = jnp.dot(hh, wr_ref[...], preferred_element_type=F32)
    pl_ = jnp.dot(hl, wr_ref[...], preferred_element_type=F32)
    logits = (ph[:, :LANES] + ph[:, LANES:]) + (pl_[:, :LANES] + pl_[:, LANES:]) + br_ref[...]

    lane = lax.broadcasted_iota(jnp.int32, (tm, LANES), 1).astype(F32)
    big = float(LANES)
    gmask = lane < N_GROUPS
    gl = jnp.where(gmask, logits, NEG)
    gmax = jnp.max(gl, axis=-1, keepdims=True)
    gsum = jnp.sum(jnp.where(gmask, jnp.exp(gl - gmax), 0.0), axis=-1, keepdims=True)
    g_p = 1.0 / gsum
    g_idx = jnp.min(jnp.where(gl == gmax, lane, big), axis=-1, keepdims=True)
    e_lo = N_GROUPS + EXPERTS_PER_GROUP * g_idx
    emask = jnp.logical_and(lane >= e_lo, lane < e_lo + EXPERTS_PER_GROUP)
    el = jnp.where(emask, logits, NEG)
    t1 = jnp.max(el, axis=-1, keepdims=True)
    i1 = jnp.min(jnp.where(el == t1, lane, big), axis=-1, keepdims=True)
    el2 = jnp.where(lane == i1, NEG, el)
    t2 = jnp.max(el2, axis=-1, keepdims=True)
    i2 = jnp.min(jnp.where(el2 == t2, lane, big), axis=-1, keepdims=True)
    e2 = jnp.exp(t2 - t1)
    w_first = g_p / (1.0 + e2)
    w0_ref[...] = jnp.broadcast_to(w_first, (tm, LANES))
    w1_ref[...] = jnp.broadcast_to(w_first * e2, (tm, LANES))
    ex1 = i1 - N_GROUPS
    ex2 = i2 - N_GROUPS

    oh1 = lane == ex1
    oh2 = lane == ex2
    cnt = (oh1.astype(F32) + oh2.astype(F32)).astype(BF16)
    before = jnp.dot(tri_ref[...], cnt, preferred_element_type=F32)
    r1 = jnp.sum(jnp.where(oh1, before, 0.0), axis=-1, keepdims=True)
    r2 = jnp.sum(jnp.where(oh2, before, 0.0), axis=-1, keepdims=True)
    ids = jnp.where(lane == 0.0, ex1, jnp.where(lane == 1.0, ex2, jnp.where(lane == 2.0, r1,
                                                                             jnp.where(lane == 3.0, r2, 0.0))))
    eid_ref[...] = ids.astype(jnp.int32)
    tile_cnt = jnp.sum(cnt.astype(F32), axis=0, keepdims=True)
    cnt_ref[...] = jnp.broadcast_to(tile_cnt, cnt_ref.shape[1:]).astype(jnp.int32)[None]


def _merge_call(x2d, oa, ob, gates, p):
    t = x2d.shape[0]
    tm = MERGE_TM
    nt = t // tm
    row = lambda w: pl.BlockSpec((tm, w), lambda i: (i, 0))
    full = lambda a: pl.BlockSpec(a.shape, lambda i: (0,) * a.ndim)
    tri = (jnp.arange(tm)[:, None] > jnp.arange(tm)[None, :]).astype(BF16)
    consts = [p["woa"], p["wob"], p["wo"], p["gf"], p["wr"], p["br"], tri]
    return pl.pallas_call(
        _merge_kernel,
        grid=(nt,),
        in_specs=[row(D_MODEL), row(A_WIDTH), row(B_WIDTH), row(2 * D_MODEL)] + [full(a) for a in consts],
        out_specs=[row(D_MODEL), row(LANES), row(LANES), row(LANES),
                   pl.BlockSpec((1, 8, LANES), lambda i: (i, 0, 0))],
        out_shape=[jax.ShapeDtypeStruct((t, D_MODEL), F32), jax.ShapeDtypeStruct((t, LANES), F32),
                   jax.ShapeDtypeStruct((t, LANES), F32), jax.ShapeDtypeStruct((t, LANES), jnp.int32),
                   jax.ShapeDtypeStruct((nt, 8, LANES), jnp.int32)],
        compiler_params=_cparams(("parallel",)),
        name="merge",
    )(x2d, oa, ob, gates, *consts)


def _rows_copy(src_hbm, row, dst, j, sem):
    return pltpu.make_async_copy(src_hbm.at[row], dst.at[j], sem)


def _start_rows(src_hbm, idx, base, dst, sem, n):
    for j in range(n):
        _rows_copy(src_hbm, idx[base + j], dst, j, sem).start(priority=j % 2)


WAIT_ROWS = 128


def _wait_rows(src_hbm, dst, sem, n):
    for c in range(n // WAIT_ROWS):
        rows = pl.ds(c * WAIT_ROWS, WAIT_ROWS)
        pltpu.make_async_copy(src_hbm.at[rows], dst.at[rows], sem).wait()


def _gather_step(i, n, src_hbm, idx_hbm, buf, idx, sem_rows, sem_idx):
    rows = buf.shape[1]
    slot = i % 2
    nslot = 1 - slot

    def idx_copy(b, s):
        return pltpu.make_async_copy(idx_hbm.at[pl.ds(b * rows, rows)], idx.at[pl.ds(s * rows, rows)], sem_idx.at[s])

    @pl.when(jnp.logical_and(i == 0, n > 0))
    def _():
        idx_copy(0, 0).start()
        idx_copy(0, 0).wait()
        _start_rows(src_hbm, idx, 0, buf.at[0], sem_rows.at[0], rows)

        @pl.when(n > 1)
        def _():
            idx_copy(1, 1).start()

    @pl.when(i + 1 < n)
    def _():
        idx_copy(i + 1, nslot).wait()
        _start_rows(src_hbm, idx, nslot * rows, buf.at[nslot], sem_rows.at[nslot], rows)

    @pl.when(i + 2 < n)
    def _():
        idx_copy(i + 2, slot).start()

    @pl.when(i < n)
    def _():
        _wait_rows(src_hbm, buf.at[slot], sem_rows.at[slot], rows)
    return slot


SCATTER_BUFS = 3


def _pack_bf16_pair(a, b):
    abits = lax.bitcast_convert_type(a.astype(BF16).astype(F32), jnp.uint32)
    bbits = lax.bitcast_convert_type(b.astype(BF16).astype(F32), jnp.uint32)
    return lax.bitcast_convert_type((abits >> 16) | (bbits & jnp.uint32(0xFFFF0000)), F32)


def _unpack_bf16_pair(w):
    bits = lax.bitcast_convert_type(w, jnp.uint32)
    lo = lax.bitcast_convert_type(bits << 16, F32)
    hi = lax.bitcast_convert_type(bits & jnp.uint32(0xFFFF0000), F32)
    return lo, hi


def _scatter_kernel(zero_blk_ref, x_hbm, dest_hbm, gf_ref, xr_hbm, buf, rbuf, idx, sem_x, sem_rows, sem_idx,
                    sem_zero, *, blk_rows):
    i = pl.program_id(0)
    n = pl.num_programs(0)
    tm = buf.shape[1]
    rows = TOP_K * tm
    islot = i % 2
    xslot = i % SCATTER_BUFS

    def idx_copy(b, s):
        return pltpu.make_async_copy(dest_hbm.at[pl.ds(b * rows, rows)], idx.at[pl.ds(s * rows, rows)], sem_idx.at[s])

    def x_copy(b, s):
        return pltpu.make_async_copy(x_hbm.at[pl.ds(b * tm, tm), :], buf.at[s], sem_x.at[s])

    def wait_rows(s):
        for k in range(TOP_K):
            pltpu.make_async_copy(rbuf.at[s], xr_hbm.at[pl.ds(0, tm)], sem_rows.at[s]).wait()

    @pl.when(i == 0)
    def _():
        idx_copy(0, 0).start()
        x_copy(0, 0).start()
        zsrc = rbuf.at[SCATTER_BUFS - 1]
        zsrc[...] = jnp.zeros(zsrc.shape, zsrc.dtype)

        def zero_copies(e):
            b = zero_blk_ref[e]
            return b >= 0, [pltpu.make_async_copy(zsrc, xr_hbm.at[pl.ds(b * blk_rows + h * tm, tm)], sem_zero)
                            for h in range(blk_rows // tm)]

        def start(e, carry):
            live, copies = zero_copies(e)

            @pl.when(live)
            def _():
                for c in copies:
                    c.start()
            return carry

        def wait(e, carry):
            live, copies = zero_copies(e)

            @pl.when(live)
            def _():
                for c in copies:
                    c.wait()
            return carry

        lax.fori_loop(0, zero_blk_ref.shape[0], start, 0)
        lax.fori_loop(0, zero_blk_ref.shape[0], wait, 0)

    @pl.when(i >= 2)
    def _():
        wait_rows((i - 2) % SCATTER_BUFS)

    @pl.when(i + 1 < n)
    def _():
        idx_copy(i + 1, 1 - islot).start()
        x_copy(i + 1, (i + 1) % SCATTER_BUFS).start()

    idx_copy(i, islot).wait()
    x_copy(i, xslot).wait()

    x = buf[xslot]
    hn = x * lax.rsqrt(jnp.mean(x * x, axis=-1, keepdims=True) + EPS) * gf_ref[...]
    rbuf[xslot, :, 0, :] = _pack_bf16_pair(hn[:, :D_MODEL // 2], hn[:, D_MODEL // 2:])
    for j in range(tm):
        src = rbuf.at[xslot, j]
        for k in range(TOP_K):
            dst = xr_hbm.at[idx[islot * rows + k * tm + j]]
            pltpu.make_async_copy(src, dst, sem_rows.at[xslot]).start(priority=k % 2)

    @pl.when(i == n - 1)
    def _():
        @pl.when(n > 1)
        def _():
            wait_rows((i - 1) % SCATTER_BUFS)
        wait_rows(xslot)


def _scatter_call(x1, pos_tiles, zero_blk, gf, n_rows, tm, blk_rows):
    t = x1.shape[0]
    rows = TOP_K * tm
    assert blk_rows % tm == 0
    any_spec = pl.BlockSpec(memory_space=pl.ANY)
    return pl.pallas_call(
        functools.partial(_scatter_kernel, blk_rows=blk_rows),
        grid_spec=pltpu.PrefetchScalarGridSpec(
            num_scalar_prefetch=1,
            grid=(t // tm,),
            in_specs=[any_spec, any_spec, pl.BlockSpec(gf.shape, lambda i, zb: (0, 0))],
            out_specs=any_spec,
            scratch_shapes=[pltpu.VMEM((SCATTER_BUFS, tm, D_MODEL), F32),
                            pltpu.VMEM((SCATTER_BUFS, tm, 1, D_MODEL // 2), F32),
                            pltpu.SMEM((2 * rows,), jnp.int32),
                            pltpu.SemaphoreType.DMA((SCATTER_BUFS,)), pltpu.SemaphoreType.DMA((SCATTER_BUFS,)),
                            pltpu.SemaphoreType.DMA((2,)), pltpu.SemaphoreType.DMA]),
        out_shape=jax.ShapeDtypeStruct((n_rows, 1, D_MODEL // 2), F32),
        compiler_params=_cparams(("arbitrary",)),
        name="scatter",
    )(zero_blk, x1, pos_tiles, gf)


def _expert_kernel(blk_exp_ref, nblk_ref, nvalid_ref, x_ref, wg_ref, wu_ref, wd_ref, y_ref, wg_bf, wu_bf, wd_bf):
    i = pl.program_id(0)

    @pl.when(jnp.logical_or(i == 0, blk_exp_ref[i] != blk_exp_ref[jnp.maximum(i - 1, 0)]))
    def _():
        wg_bf[...] = wg_ref[0].astype(BF16)
        wu_bf[...] = wu_ref[0].astype(BF16)
        wd_bf[...] = wd_ref[0].astype(BF16)

    @pl.when(i < nblk_ref[0])
    def _():
        w = x_ref[:, 0, :]
        real = lax.broadcasted_iota(jnp.int32, w.shape, 0) < nvalid_ref[i]
        lo, hi = [v.astype(BF16) for v in _unpack_bf16_pair(jnp.where(real, w, 0.0))]
        half = D_MODEL // 2

        def up_proj(w_bf):
            return (jnp.dot(lo, w_bf[:half, :], preferred_element_type=F32)
                    + jnp.dot(hi, w_bf[half:, :], preferred_element_type=F32))

        gt, up = up_proj(wg_bf), up_proj(wu_bf)
        act = (gt * (0.5 * jnp.tanh(0.5 * gt) + 0.5) * up).astype(BF16)
        y_ref[:, 0, :] = jnp.dot(act, wd_bf[...], preferred_element_type=F32)

    @pl.when(i >= nblk_ref[0])
    def _():
        y_ref[...] = jnp.zeros(y_ref.shape, y_ref.dtype)


def _expert_call(xr, rows, blk_exp, nblk, nvalid, wg, wu, wd):
    nb = xr.shape[0] // rows
    return pl.pallas_call(
        _expert_kernel,
        grid_spec=pltpu.PrefetchScalarGridSpec(
            num_scalar_prefetch=3,
            grid=(nb,),
            in_specs=[pl.BlockSpec((rows, 1, D_MODEL // 2), lambda i, be, n, nv: (jnp.minimum(i, n[0] - 1), 0, 0)),
                      pl.BlockSpec((1, D_MODEL, D_EXPERT), lambda i, be, n, nv: (be[i], 0, 0)),
                      pl.BlockSpec((1, D_MODEL, D_EXPERT), lambda i, be, n, nv: (be[i], 0, 0)),
                      pl.BlockSpec((1, D_EXPERT, D_MODEL), lambda i, be, n, nv: (be[i], 0, 0))],
            out_specs=pl.BlockSpec((rows, 1, D_MODEL), lambda i, be, n, nv: (i, 0, 0)),
            scratch_shapes=[pltpu.VMEM((D_MODEL, D_EXPERT), BF16), pltpu.VMEM((D_MODEL, D_EXPERT), BF16),
                            pltpu.VMEM((D_EXPERT, D_MODEL), BF16)]),
        out_shape=jax.ShapeDtypeStruct((nb * rows, 1, D_MODEL), F32),
        compiler_params=_cparams(("arbitrary",)),
        name="experts",
    )(blk_exp, nblk, nvalid, xr, wg, wu, wd)


def _combine_kernel(x1_ref, w0_ref, w1_ref, y_hbm, pos_hbm, o_ref, buf, idx, sem_rows, sem_idx):
    rows = buf.shape[1]
    tm = rows // 2
    slot = _gather_step(pl.program_id(0), pl.num_programs(0), y_hbm, pos_hbm, buf, idx, sem_rows, sem_idx)
    w0 = w0_ref[...]
    w1 = w1_ref[...]
    for c in range(D_MODEL // LANES):
        sl = slice(c * LANES, (c + 1) * LANES)
        o_ref[:, sl] = x1_ref[:, sl] + w0 * buf[slot, 0:tm, 0, sl] + w1 * buf[slot, tm:rows, 0, sl]


def _combine_call(x1, w0, w1, yr, pos_tiles, tm):
    t = x1.shape[0]
    rows = TOP_K * tm
    row = lambda w: pl.BlockSpec((tm, w), lambda i: (i, 0))
    return pl.pallas_call(
        _combine_kernel,
        grid=(t // tm,),
        in_specs=[row(D_MODEL), row(LANES), row(LANES),
                  pl.BlockSpec(memory_space=pl.ANY), pl.BlockSpec(memory_space=pl.ANY)],
        out_specs=row(D_MODEL),
        out_shape=jax.ShapeDtypeStruct((t, D_MODEL), F32),
        scratch_shapes=[pltpu.VMEM((2, rows, 1, D_MODEL), F32), pltpu.SMEM((2 * rows,), jnp.int32),
                        pltpu.SemaphoreType.DMA((2,)), pltpu.SemaphoreType.DMA((2,))],
        compiler_params=_cparams(("arbitrary",)),
        name="combine",
    )(x1, w0, w1, yr, pos_tiles)


def _rot_cols(w):
    half = ROPE_DIM // 2
    return jnp.concatenate([-w[..., half:], w[..., :half]], axis=-1)


def _head_pad(nope, rope):
    z = jnp.zeros(rope.shape[:-1] + (LANES - B_QK_DIM,), rope.dtype)
    out = jnp.concatenate([nope, rope, z], axis=-1)
    return out.reshape(out.shape[:-2] + (out.shape[-2] * LANES,))


def _t5_bucket(dist):
    max_exact = REL_BUCKETS // 2
    d = jnp.maximum(dist, 0)
    ratio = jnp.maximum(d, 1).astype(F32) / max_exact
    large = max_exact + (jnp.log(ratio) / math.log(REL_MAX_DIST / max_exact)
                         * (REL_BUCKETS - max_exact)).astype(jnp.int32)
    large = jnp.minimum(large, REL_BUCKETS - 1)
    return jnp.where(d < max_exact, d, large)


def _swa_bias(rel_bias, tq):
    qi = jnp.arange(tq)[:, None]
    kj = jnp.arange(2 * tq)[None, :]
    dist = qi + tq - kj
    onehot = (_t5_bucket(dist)[:, :, None] == jnp.arange(REL_BUCKETS)[None, None, :]).astype(F32)
    bias = jnp.einsum("qkb,bh->hqk", onehot, rel_bias.astype(F32), precision=lax.Precision.HIGHEST) * LOG2E
    mask = (dist >= 0) & (dist < WINDOW)
    bias = jnp.where(mask[None], bias, NEG)
    return bias.reshape(A_KV_HEADS, (A_HEADS // A_KV_HEADS) * tq, 2 * tq)


def _rope_tables(seq):
    half = ROPE_DIM // 2
    inv_freq = ROPE_THETA ** (-jnp.arange(half, dtype=F32) / half)
    ang = jnp.arange(seq, dtype=F32)[:, None] * inv_freq[None, :]
    cos = jnp.concatenate([jnp.cos(ang), jnp.cos(ang)], axis=-1)
    sin = jnp.concatenate([jnp.sin(ang), jnp.sin(ang)], axis=-1)
    return cos, sin


def _lane_table(nope_part, rope_part):
    seq = rope_part.shape[0]
    nope_part = jnp.broadcast_to(nope_part, (seq, NOPE_DIM))
    return jnp.concatenate([nope_part, rope_part, jnp.zeros((seq, LANES - B_QK_DIM), F32)], axis=-1)


def _prep_layer(l, seq, cos, sin, attn_norm, w_in, b_gate, a_q_norm, a_k_norm, w_oa, q_a_norm, w_qb, kv_a_norm,
                w_kvb, b_q_norm, b_k_norm, w_ob, w_o, ffn_norm, w_rg, b_rg, w_re, b_re, w_eg, w_eu, w_ed):
    w = w_in[l]
    o = 0
    parts = []
    for sz in (A_WIDTH, A_KV_HEADS * A_HEAD_DIM, A_KV_HEADS * A_HEAD_DIM, Q_LORA, KV_LORA, ROPE_DIM, 2 * D_MODEL):
        parts.append(w[:, o:o + sz])
        o += sz
    wqa, wka, wva, wcq, wckv, wkpe, wg = parts
    perm = jnp.array([c + (A_HEADS // 2) * h for c in range(A_HEADS // 2) for h in range(2)])
    wqa = wqa.reshape(D_MODEL, A_HEADS, A_HEAD_DIM)[:, perm].reshape(D_MODEL, A_WIDTH)
    zl = jnp.zeros((D_MODEL, NOPE_DIM), F32)
    zr = jnp.zeros((D_MODEL, LANES - B_QK_DIM), F32)
    w1 = jnp.concatenate([wqa, wka, wva, wcq, wckv, zl, wkpe, zr, zl, _rot_cols(wkpe), zr, 0.5 * wg],
                         axis=1).astype(BF16)

    wq = w_qb[l].reshape(Q_LORA, B_HEADS, B_QK_DIM)
    zq = jnp.zeros((Q_LORA, B_HEADS, NOPE_DIM), F32)
    wq2 = jnp.concatenate([_head_pad(wq[..., :NOPE_DIM], wq[..., NOPE_DIM:]),
                           _head_pad(zq, _rot_cols(wq[..., NOPE_DIM:]))], axis=1).astype(BF16)
    wkv = w_kvb[l].reshape(KV_LORA, B_HEADS, NOPE_DIM + B_V_DIM)
    wk2 = _head_pad(wkv[..., :NOPE_DIM], jnp.zeros((KV_LORA, B_HEADS, ROPE_DIM), F32)).astype(BF16)
    wv = wkv[..., NOPE_DIM:]
    zv = jnp.zeros_like(wv)
    wv2 = jnp.where((jnp.arange(B_HEADS) % 2 == 0)[None, :, None], jnp.concatenate([wv, zv], -1),
                    jnp.concatenate([zv, wv], -1)).reshape(KV_LORA, B_HEADS * LANES).astype(BF16)
    one_lane = jnp.where(jnp.arange(B_HEADS) % 2 == 0, V_ONE_EVEN, V_ONE_ODD)
    vone = (jnp.arange(LANES)[None, :] == one_lane[:, None]).astype(F32).reshape(1, B_HEADS * LANES)

    def rope_tabs(gain, scale):
        gn, gr = gain[:NOPE_DIM], gain[NOPE_DIM:]
        gr_rot = jnp.concatenate([gr[ROPE_DIM // 2:], gr[:ROPE_DIM // 2]])
        a = _lane_table(gn[None, :] * scale, gr[None, :] * cos * scale)
        b = _lane_table(jnp.zeros((1, NOPE_DIM), F32), gr_rot[None, :] * sin * scale)
        return a, b

    aq, bq = rope_tabs(b_q_norm[l], B_QK_DIM ** -0.5 * LOG2E)
    score_bound = math.sqrt(B_QK_DIM) * LOG2E * jnp.max(jnp.abs(b_q_norm[l])) * jnp.max(jnp.abs(b_k_norm[l]))
    ak, bk = rope_tabs(b_k_norm[l], 1.0)

    gqa = (jnp.tile(a_q_norm[l], 2) * (A_HEAD_DIM ** -0.5 * LOG2E))[None, :]
    gka = jnp.tile(a_k_norm[l], 2)[None, :]
    woa = w_oa[l].reshape(A_HEADS, A_HEAD_DIM, D_MODEL)[perm].reshape(A_WIDTH, D_MODEL).astype(BF16)

    wr = jnp.concatenate([w_rg[l], w_re[l], jnp.zeros((D_MODEL, LANES - N_GROUPS - N_EXPERTS), F32)], axis=1)
    wrh = wr.astype(BF16)
    wrl = (wr - wrh.astype(F32)).astype(BF16)
    br = jnp.concatenate([b_rg[l], b_re[l], jnp.zeros((LANES - N_GROUPS - N_EXPERTS,), F32)])[None, :]

    return dict(
        gn=attn_norm[l][None, :], w1=w1, wq2=wq2, wk2=wk2, wv2=wv2, vone=vone, gcq=q_a_norm[l][None, :],
        gckv=kv_a_norm[l][None, :], score_bound=score_bound, bg=0.5 * b_gate[l][None, :], gqa=gqa, gka=gka,
        aq=aq, bq=bq, ak=ak, bk=bk,
        woa=woa, wob=w_ob[l].astype(BF16), wo=w_o[l].astype(BF16), gf=ffn_norm[l][None, :], wr=jnp.concatenate([wrh, wrl], axis=1), br=br,
        wg=w_eg.reshape((-1,) + w_eg.shape[2:]), wu=w_eu.reshape((-1,) + w_eu.shape[2:]),
        wd=w_ed.reshape((-1,) + w_ed.shape[2:]), expert_base=l * N_EXPERTS)


def _tril(n):
    return (jnp.arange(n)[:, None] >= jnp.arange(n)[None, :]).astype(jnp.int32)


def _dispatch(route, tile_cnt, rows):
    t = route.shape[0]
    nt = tile_cnt.shape[0]
    tile_cnt = tile_cnt[:, 0, :N_EXPERTS]
    counts = jnp.sum(tile_cnt, axis=0)
    padded = ((counts + rows - 1) // rows) * rows
    pends = jnp.dot(_tril(N_EXPERTS), padded)
    pstarts = pends - padded
    tile_base = pstarts[None, :] + jnp.dot(_tril(nt), tile_cnt) - tile_cnt
    base = jnp.broadcast_to(tile_base[:, None, None, :], (nt, t // nt, 1, N_EXPERTS)).reshape(t, 1, N_EXPERTS)
    onehot = route[:, :TOP_K, None] == jnp.arange(N_EXPERTS, dtype=jnp.int32)[None, None, :]
    dest = jnp.sum(jnp.where(onehot, base, 0), axis=-1) + route[:, TOP_K:2 * TOP_K]
    n_rows = t * TOP_K + N_EXPERTS * rows
    n_blocks = n_rows // rows
    blk_start = jnp.arange(n_blocks, dtype=jnp.int32)[:, None] * rows
    blk_exp = jnp.minimum(jnp.sum((pends[None, :] <= blk_start).astype(jnp.int32), axis=1), N_EXPERTS - 1)
    nblk = (pends[-1] // rows).astype(jnp.int32).reshape(1)
    nvalid = jnp.clip((pstarts + counts)[blk_exp] - blk_start[:, 0], 0, rows).astype(jnp.int32)
    last_blk = jnp.where(padded > 0, pends // rows - 1, -1)
    trailing = nblk[0] + jnp.arange(N_EXPERTS, dtype=jnp.int32)
    zero_blk = jnp.concatenate([last_blk, jnp.where(trailing < n_blocks, trailing, -1)]).astype(jnp.int32)
    return dest, blk_exp, nblk, nvalid, zero_blk, n_rows


def _swa_tables(sinks, rel_bias, a_q_norm, a_k_norm):
    per_group = A_HEADS // A_KV_HEADS
    tab = jnp.broadcast_to((sinks * LOG2E).reshape(A_KV_HEADS, per_group, 1, 1),
                           (A_KV_HEADS, per_group, SWA_TQ, LANES)).reshape(A_KV_HEADS, per_group * SWA_TQ, LANES)
    qk = math.sqrt(A_HEAD_DIM) * jnp.max(jnp.abs(a_q_norm)) * jnp.max(jnp.abs(a_k_norm))
    bound = LOG2E * jnp.maximum(qk + jnp.max(jnp.abs(rel_bias)), jnp.max(jnp.abs(sinks)))
    return tab, bound


def _pos_tiles(dest, tm):
    t = dest.shape[0]
    return dest.reshape(t // tm, tm, TOP_K).transpose(0, 2, 1).reshape(t * TOP_K)


def _layer(x2d, bsz, seq, p, sink_tab, swa_bound, bias):
    qa, ka, va, qb, kb, vb, gates = _proj_call(x2d, seq, p)
    oa = _swa_call(qa, ka, va, sink_tab, bias, bsz, seq, swa_bound)
    ob = _mla_call(qb, kb, vb, bsz, seq, p["score_bound"])
    x1, w0, w1, route, tile_cnt = _merge_call(x2d, oa, ob, gates, p)
    dest, blk_exp, nblk, nvalid, zero_blk, n_rows = _dispatch(route, tile_cnt, MOE_ROWS)
    pos_tiles = _pos_tiles(dest, COMB_TM)
    xr = _scatter_call(x1, pos_tiles, zero_blk, p["gf"], n_rows, COMB_TM, MOE_ROWS)
    yr = _expert_call(xr, MOE_ROWS, blk_exp + p["expert_base"], nblk, nvalid, p["wg"], p["wu"], p["wd"])
    return _combine_call(x1, w0, w1, yr, pos_tiles, COMB_TM)


def kernel(x, attn_norm, w_in, b_gate, a_q_norm, a_k_norm, a_sinks, rel_bias, w_oa, q_a_norm, w_qb, kv_a_norm, w_kvb, b_q_norm, b_k_norm, w_ob, w_o, ffn_norm, w_router_group, b_router_group, w_router_expert, b_router_expert, w_exp_gate, w_exp_up, w_exp_down):
    bsz, seq, d = x.shape
    assert d == D_MODEL and WINDOW == SWA_TQ and MLA_TQ % MLA_TK == 0 and MOE_ROWS % COMB_TM == 0
    assert all(seq % tile == 0 for tile in (MLA_TQ, PROJ_TM, SWA_SUB * SWA_TQ))
    assert all((bsz * seq) % tile == 0 for tile in (MERGE_TM, COMB_TM))
    depth = w_in.shape[0]
    cos, sin = _rope_tables(seq)
    bias = _swa_bias(rel_bias, SWA_TQ)
    x2d = x.reshape(bsz * seq, d)
    for l in range(depth):
        p = _prep_layer(l, seq, cos, sin, attn_norm, w_in, b_gate, a_q_norm, a_k_norm, w_oa, q_a_norm, w_qb,
                        kv_a_norm, w_kvb, b_q_norm, b_k_norm, w_ob, w_o, ffn_norm, w_router_group,
                        b_router_group, w_router_expert, b_router_expert, w_exp_gate, w_exp_up, w_exp_down)
        sink_tab, swa_bound = _swa_tables(a_sinks[l], rel_bias, a_q_norm[l], a_k_norm[l])
        x2d = _layer(x2d, bsz, seq, p, sink_tab, swa_bound, bias)
    return x2d.reshape(bsz, seq, d)
```

```python
import functools
import math

import jax
import jax.numpy as jnp
from jax import lax
from jax.experimental import pallas as pl
from jax.experimental.pallas import tpu as pltpu

F32 = jnp.float32
BF16 = jnp.bfloat16

D_MODEL = 1024
A_HEADS = 8
A_KV_HEADS = 2
A_HEAD_DIM = 64
A_WIDTH = A_HEADS * A_HEAD_DIM
WINDOW = 128
REL_BUCKETS = 32
REL_MAX_DIST = 128
B_HEADS = 8
Q_LORA = 256
KV_LORA = 128
NOPE_DIM = 64
ROPE_DIM = 32
B_QK_DIM = NOPE_DIM + ROPE_DIM
B_V_DIM = 64
B_WIDTH = B_HEADS * B_V_DIM
ROPE_THETA = 10000.0
N_GROUPS = 4
EXPERTS_PER_GROUP = 8
N_EXPERTS = N_GROUPS * EXPERTS_PER_GROUP
TOP_K = 2
D_EXPERT = 256
EPS = 1e-6

LANES = 128
NEG = -1e30
LOG2E = math.log2(math.e)
V_ONE_EVEN = B_V_DIM
V_ONE_ODD = 0
EXP2_SAFE_BOUND = 100.0
VMEM_LIMIT = 56 * 1024 * 1024

C_QA = 0
C_KA = C_QA + A_WIDTH
C_VA = C_KA + LANES
C_CQ = C_VA + LANES
C_CKV = C_CQ + Q_LORA
C_KPE = C_CKV + KV_LORA
C_KPR = C_KPE + LANES
C_GATE = C_KPR + LANES
C_END = C_GATE + 2 * D_MODEL

PROJ_TM = 512
SWA_TQ = 128
SWA_SUB = 8
MLA_TQ = 1024
MLA_TK = 512
MERGE_TM = 1024
MOE_ROWS = 512
COMB_TM = 256


def _cparams(sem):
    return pltpu.CompilerParams(dimension_semantics=sem, vmem_limit_bytes=VMEM_LIMIT)


def _proj_kernel(x_ref, gn_ref, w1_ref, wq2_ref, wk2_ref, wv2_ref, vone_ref, gcq_ref, gckv_ref, bg_ref,
                 gqa_ref, gka_ref, aq_ref, bq_ref, ak_ref, bk_ref,
                 qa_ref, ka_ref, va_ref, qb_ref, kb_ref, vb_ref, gate_ref):
    tm = x_ref.shape[0]
    x = x_ref[...]
    h = (x * lax.rsqrt(jnp.mean(x * x, axis=-1, keepdims=True) + EPS) * gn_ref[...]).astype(BF16)

    def proj(a, b):
        return jnp.dot(h, w1_ref[:, a:b], preferred_element_type=F32)

    lo = lax.broadcasted_iota(jnp.int32, (tm, LANES), 1) < A_HEAD_DIM

    def pair_norm(t, gain):
        t2 = t * t
        s_lo = jnp.sum(jnp.where(lo, t2, 0.0), axis=-1, keepdims=True)
        s_hi = jnp.sum(jnp.where(lo, 0.0, t2), axis=-1, keepdims=True)
        r = jnp.where(lo, lax.rsqrt(s_lo / A_HEAD_DIM + EPS), lax.rsqrt(s_hi / A_HEAD_DIM + EPS))
        return (t * r * gain).astype(BF16)

    qa = proj(C_QA, C_KA)
    for c in range(A_WIDTH // LANES):
        sl = slice(c * LANES, (c + 1) * LANES)
        qa_ref[:, sl] = pair_norm(qa[:, sl], gqa_ref[...])
    kvp = proj(C_KA, C_CQ)
    ka_ref[...] = pair_norm(kvp[:, :LANES], gka_ref[...])
    va_ref[...] = kvp[:, LANES:].astype(BF16)

    lat = proj(C_CQ, C_GATE)
    cq = lat[:, :Q_LORA]
    cqn = (cq * lax.rsqrt(jnp.mean(cq * cq, axis=-1, keepdims=True) + EPS) * gcq_ref[...]).astype(BF16)
    ckv = lat[:, Q_LORA:Q_LORA + KV_LORA]
    ckvn = (ckv * lax.rsqrt(jnp.mean(ckv * ckv, axis=-1, keepdims=True) + EPS) * gckv_ref[...]).astype(BF16)
    kpe = lat[:, C_KPE - C_CQ:C_KPR - C_CQ]
    kpr = lat[:, C_KPR - C_CQ:C_GATE - C_CQ]

    aq, bq, ak, bk = aq_ref[...], bq_ref[...], ak_ref[...], bk_ref[...]
    q2 = jnp.dot(cqn, wq2_ref[...], preferred_element_type=F32)
    k2 = jnp.dot(ckvn, wk2_ref[...], preferred_element_type=F32)
    vb_ref[...] = (jnp.dot(ckvn, wv2_ref[...], preferred_element_type=F32) + vone_ref[...]).astype(BF16)
    kr = kpr * bk
    for hd in range(B_HEADS):
        sl = slice(hd * LANES, (hd + 1) * LANES)
        qp = q2[:, sl]
        qr = q2[:, B_HEADS * LANES + hd * LANES:B_HEADS * LANES + (hd + 1) * LANES]
        rq = lax.rsqrt(jnp.sum(qp * qp, axis=-1, keepdims=True) / B_QK_DIM + EPS)
        qb_ref[:, sl] = (rq * (qp * aq + qr * bq)).astype(BF16)
        kp = k2[:, sl] + kpe
        rk = lax.rsqrt(jnp.sum(kp * kp, axis=-1, keepdims=True) / B_QK_DIM + EPS)
        kb_ref[:, sl] = (rk * (kp * ak + kr)).astype(BF16)

    gate_ref[...] = (0.5 * jnp.tanh(proj(C_GATE, C_END) + bg_ref[...]) + 0.5).astype(BF16)


def _proj_call(x2d, seq, p):
    t = x2d.shape[0]
    tm = PROJ_TM
    nt = t // tm
    npos = seq // tm
    row = lambda w: pl.BlockSpec((tm, w), lambda i: (i, 0))
    full = lambda a: pl.BlockSpec(a.shape, lambda i: (0,) * a.ndim)
    pos = pl.BlockSpec((tm, LANES), lambda i: (i % npos, 0))
    consts = [p["gn"], p["w1"], p["wq2"], p["wk2"], p["wv2"], p["vone"], p["gcq"], p["gckv"], p["bg"], p["gqa"],
              p["gka"]]
    tabs = [p["aq"], p["bq"], p["ak"], p["bk"]]
    widths = [A_WIDTH, LANES, LANES, B_HEADS * LANES, B_HEADS * LANES, B_HEADS * LANES, 2 * D_MODEL]
    return pl.pallas_call(
        _proj_kernel,
        grid=(nt,),
        in_specs=[row(D_MODEL)] + [full(a) for a in consts] + [pos] * 4,
        out_specs=[row(w) for w in widths],
        out_shape=[jax.ShapeDtypeStruct((t, w), BF16) for w in widths],
        compiler_params=_cparams(("parallel",)),
        name="proj",
    )(x2d, *consts, *tabs)


def _swa_kernel(q_ref, kc_ref, kp_ref, vc_ref, vp_ref, bias_ref, sink_ref, o_ref, *, shift):
    tq = SWA_TQ
    groups = A_WIDTH // LANES
    first = pl.program_id(1) == 0
    one_col = (lax.broadcasted_iota(jnp.int32, (2 * tq, LANES), 1) == 0).astype(BF16)
    lo = lax.broadcasted_iota(jnp.int32, (tq, LANES), 1) < A_HEAD_DIM
    no_prev = jnp.logical_and(first, lax.broadcasted_iota(jnp.int32, (groups * tq, 2 * tq), 1) < tq)
    for sub in range(SWA_SUB):
        if sub == 0:
            k = jnp.concatenate([kp_ref[...], kc_ref[0:tq, :]], axis=0)
            v = jnp.concatenate([vp_ref[...], vc_ref[0:tq, :]], axis=0)
        else:
            k = kc_ref[(sub - 1) * tq:(sub + 1) * tq, :]
            v = vc_ref[(sub - 1) * tq:(sub + 1) * tq, :]
        v_ext = jnp.concatenate([v, one_col], axis=1)
        qrows = slice(sub * tq, (sub + 1) * tq)
        qs = [q_ref[qrows, c * LANES:(c + 1) * LANES] for c in range(groups)]
        res = []
        for g in range(A_KV_HEADS):
            keep = lo if g == 0 else jnp.logical_not(lo)
            qg = jnp.concatenate([jnp.where(keep, qc, jnp.zeros_like(qc)) for qc in qs], axis=0)
            s = lax.dot_general(qg, k, (((1,), (1,)), ((), ())), preferred_element_type=F32) + bias_ref[g]
            sink = sink_ref[g][:, :1]
            if shift:
                if sub == 0:
                    s = jnp.where(no_prev, NEG, s)
                m = jnp.maximum(jnp.max(s, axis=-1, keepdims=True), sink)
                e = jnp.exp2(s - m)
                sink_term = jnp.exp2(sink - m)
            else:
                e = jnp.exp2(s)
                if sub == 0:
                    e = jnp.where(no_prev, 0.0, e)
                sink_term = jnp.exp2(sink)
            pv = jnp.dot(e.astype(BF16), v_ext, preferred_element_type=F32)
            res.append(pv[:, :LANES] * (1.0 / (pv[:, LANES:LANES + 1] + sink_term)))
        for c in range(groups):
            rows = slice(c * tq, (c + 1) * tq)
            o_ref[qrows, c * LANES:(c + 1) * LANES] = jnp.where(lo, res[0][rows], res[1][rows]).astype(BF16)


def _swa_call(qa, ka, va, sink_tab, bias, bsz, seq, score_bound):
    t = qa.shape[0]
    tq = SWA_TQ
    step = SWA_SUB * tq
    ns = seq // step
    cur = lambda b, i: (b * ns + i, 0)
    prev = lambda b, i: (b * (seq // tq) + jnp.maximum(SWA_SUB * i - 1, 0), 0)
    full = lambda a: pl.BlockSpec(a.shape, lambda b, i: (0,) * a.ndim)

    def call(shift):
        return pl.pallas_call(
            functools.partial(_swa_kernel, shift=shift),
            grid=(bsz, ns),
            in_specs=[pl.BlockSpec((step, A_WIDTH), cur),
                      pl.BlockSpec((step, LANES), cur), pl.BlockSpec((tq, LANES), prev),
                      pl.BlockSpec((step, LANES), cur), pl.BlockSpec((tq, LANES), prev),
                      full(bias), full(sink_tab)],
            out_specs=pl.BlockSpec((step, A_WIDTH), cur),
            out_shape=jax.ShapeDtypeStruct((t, A_WIDTH), BF16),
            compiler_params=_cparams(("parallel", "parallel")),
            name="swa_shifted" if shift else "swa",
        )(qa, ka, ka, va, va, bias, sink_tab)

    return lax.cond(score_bound <= EXP2_SAFE_BOUND, lambda: call(False), lambda: call(True))


def _mla_kernel(q_ref, k_ref, v_ref, o_ref, acc_sc, m_sc, *, shift):
    tq = q_ref.shape[0]
    tk = MLA_TK
    per_tile = tq // tk
    qi = pl.program_id(2)
    acc_sc[...] = jnp.zeros(acc_sc.shape, F32)
    if shift:
        m_sc[...] = jnp.full(m_sc.shape, NEG, F32)

    def tile(j, r0, r1, masked):
        keys = pl.ds(pl.multiple_of(j * tk, tk), tk)
        if masked:
            causal = (lax.broadcasted_iota(jnp.int32, (r1 - r0, tk), 1)
                      <= lax.broadcasted_iota(jnp.int32, (r1 - r0, tk), 0))
        for half in range(2):
            hs = slice(half * LANES, (half + 1) * LANES)
            s = lax.dot_general(q_ref[r0:r1, hs], k_ref[keys, hs], (((1,), (1,)), ((), ())),
                                preferred_element_type=F32)
            if shift:
                if masked:
                    s = jnp.where(causal, s, NEG)
                m_prev = m_sc[half, r0:r1]
                m_new = jnp.maximum(m_prev, jnp.max(s, axis=-1, keepdims=True))
                m_sc[half, r0:r1] = m_new
                e = jnp.exp2(s - m_new)
                acc_sc[half, r0:r1] = jnp.exp2(m_prev - m_new) * acc_sc[half, r0:r1]
            else:
                e = jnp.exp2(s)
                if masked:
                    e = jnp.where(causal, e, 0.0)
            acc_sc[half, r0:r1] += jnp.dot(e.astype(BF16), v_ref[keys, hs], preferred_element_type=F32)

    def body(j, carry):
        tile(j, 0, tq, False)
        return carry

    lax.fori_loop(0, qi * per_tile, body, 0)
    for d in range(per_tile):
        tile(qi * per_tile + d, d * tk, (d + 1) * tk, True)
        if (d + 1) * tk < tq:
            tile(qi * per_tile + d, (d + 1) * tk, tq, False)
    a0 = acc_sc[0]
    a1 = acc_sc[1]
    lo = lax.broadcasted_iota(jnp.int32, (tq, LANES), 1) < B_V_DIM
    o_ref[...] = jnp.where(lo, a0 * (1.0 / a0[:, V_ONE_EVEN:V_ONE_EVEN + 1]),
                           a1 * (1.0 / a1[:, V_ONE_ODD:V_ONE_ODD + 1])).astype(BF16)


def _mla_call(qb, kb, vb, bsz, seq, score_bound):
    t = qb.shape[0]
    tq = MLA_TQ
    nq = seq // tq
    pairs = B_HEADS // 2

    def call(shift):
        return pl.pallas_call(
            functools.partial(_mla_kernel, shift=shift),
            grid=(bsz, pairs, nq),
            in_specs=[pl.BlockSpec((tq, 2 * LANES), lambda b, p, i: (b * nq + i, p)),
                      pl.BlockSpec((seq, 2 * LANES), lambda b, p, i: (b, p)),
                      pl.BlockSpec((seq, 2 * LANES), lambda b, p, i: (b, p))],
            out_specs=pl.BlockSpec((tq, LANES), lambda b, p, i: (b * nq + i, p)),
            out_shape=jax.ShapeDtypeStruct((t, B_WIDTH), BF16),
            scratch_shapes=[pltpu.VMEM((2, tq, LANES), F32), pltpu.VMEM((2, tq, 1), F32)],
            compiler_params=_cparams(("parallel", "parallel", "arbitrary")),
            name="mla_shifted" if shift else "mla",
        )(qb, kb, vb)

    return lax.cond(score_bound <= EXP2_SAFE_BOUND, lambda: call(False), lambda: call(True))


def _merge_kernel(x_ref, oa_ref, ob_ref, gate_ref, woa_ref, wob_ref, wo_ref, gf_ref, wr_ref, br_ref,
                  tri_ref, x1_ref, w0_ref, w1_ref, eid_ref, cnt_ref):
    tm = x_ref.shape[0]
    ya = jnp.dot(oa_ref[...], woa_ref[...], preferred_element_type=F32)
    yb = jnp.dot(ob_ref[...], wob_ref[...], preferred_element_type=F32)
    g = gate_ref[...].astype(F32)
    mix = (g[:, :D_MODEL] * ya + g[:, D_MODEL:] * yb).astype(BF16)
    x1 = x_ref[...] + jnp.dot(mix, wo_ref[...], preferred_element_type=F32)
    x1_ref[...] = x1

    hn = x1 * lax.rsqrt(jnp.mean(x1 * x1, axis=-1, keepdims=True) + EPS) * gf_ref[...]
    hh = hn.astype(BF16)
    hl = (hn - hh.astype(F32)).astype(BF16)
    ph = jnp.dot(hh, wr_ref[...], preferred_element_type=F32)
    pl_ = jnp.dot(hl, wr_ref[...], preferred_element_type=F32)
    logits = (ph[:, :LANES] + ph[:, LANES:]) + (pl_[:, :LANES] + pl_[:, LANES:]) + br_ref[...]

    lane = lax.broadcasted_iota(jnp.int32, (tm, LANES), 1).astype(F32)
    big = float(LANES)
    gmask = lane < N_GROUPS
    gl = jnp.where(gmask, logits, NEG)
    gmax = jnp.max(gl, axis=-1, keepdims=True)
    gsum = jnp.sum(jnp.where(gmask, jnp.exp(gl - gmax), 0.0), axis=-1, keepdims=True)
    g_p = 1.0 / gsum
    g_idx = jnp.min(jnp.where(gl == gmax, lane, big), axis=-1, keepdims=True)
    e_lo = N_GROUPS + EXPERTS_PER_GROUP * g_idx
    emask = jnp.logical_and(lane >= e_lo, lane < e_lo + EXPERTS_PER_GROUP)
    el = jnp.where(emask, logits, NEG)
    t1 = jnp.max(el, axis=-1, keepdims=True)
    i1 = jnp.min(jnp.where(el == t1, lane, big), axis=-1, keepdims=True)
    el2 = jnp.where(lane == i1, NEG, el)
    t2 = jnp.max(el2, axis=-1, keepdims=True)
    i2 = jnp.min(jnp.where(el2 == t2, lane, big), axis=-1, keepdims=True)
    e2 = jnp.exp(t2 - t1)
    w_first = g_p / (1.0 + e2)
    w0_ref[...] = jnp.broadcast_to(w_first, (tm, LANES))
    w1_ref[...] = jnp.broadcast_to(w_first * e2, (tm, LANES))
    ex1 = i1 - N_GROUPS
    ex2 = i2 - N_GROUPS

    oh1 = lane == ex1
    oh2 = lane == ex2
    cnt = (oh1.astype(F32) + oh2.astype(F32)).astype(BF16)
    before = jnp.dot(tri_ref[...], cnt, preferred_element_type=F32)
    r1 = jnp.sum(jnp.where(oh1, before, 0.0), axis=-1, keepdims=True)
    r2 = jnp.sum(jnp.where(oh2, before, 0.0), axis=-1, keepdims=True)
    ids = jnp.where(lane == 0.0, ex1, jnp.where(lane == 1.0, ex2, jnp.where(lane == 2.0, r1,
                                                                             jnp.where(lane == 3.0, r2, 0.0))))
    eid_ref[...] = ids.T[:eid_ref.shape[0], :].astype(jnp.int32)
    tile_cnt = jnp.sum(cnt.astype(F32), axis=0, keepdims=True)
    cnt_ref[...] = jnp.broadcast_to(tile_cnt, cnt_ref.shape[1:]).astype(jnp.int32)[None]


def _merge_call(x2d, oa, ob, gates, p):
    t = x2d.shape[0]
    tm = MERGE_TM
    nt = t // tm
    row = lambda w: pl.BlockSpec((tm, w), lambda i: (i, 0))
    full = lambda a: pl.BlockSpec(a.shape, lambda i: (0,) * a.ndim)
    tri = (jnp.arange(tm)[:, None] > jnp.arange(tm)[None, :]).astype(BF16)
    consts = [p["woa"], p["wob"], p["wo"], p["gf"], p["wr"], p["br"], tri]
    return pl.pallas_call(
        _merge_kernel,
        grid=(nt,),
        in_specs=[row(D_MODEL), row(A_WIDTH), row(B_WIDTH), row(2 * D_MODEL)] + [full(a) for a in consts],
        out_specs=[row(D_MODEL), row(LANES), row(LANES), pl.BlockSpec((8, tm), lambda i: (0, i)),
                   pl.BlockSpec((1, 8, LANES), lambda i: (i, 0, 0))],
        out_shape=[jax.ShapeDtypeStruct((t, D_MODEL), F32), jax.ShapeDtypeStruct((t, LANES), F32),
                   jax.ShapeDtypeStruct((t, LANES), F32), jax.ShapeDtypeStruct((8, t), jnp.int32),
                   jax.ShapeDtypeStruct((nt, 8, LANES), jnp.int32)],
        compiler_params=_cparams(("parallel",)),
        name="merge",
    )(x2d, oa, ob, gates, *consts)


def _rows_copy(src_hbm, row, dst, j, sem):
    return pltpu.make_async_copy(src_hbm.at[row], dst.at[j], sem)


def _start_rows(src_hbm, idx, base, dst, sem, n):
    for j in range(n):
        _rows_copy(src_hbm, idx[base + j], dst, j, sem).start(priority=j % 2)


WAIT_ROWS = 128


def _wait_rows(src_hbm, dst, sem, n):
    for c in range(n // WAIT_ROWS):
        rows = pl.ds(c * WAIT_ROWS, WAIT_ROWS)
        pltpu.make_async_copy(src_hbm.at[rows], dst.at[rows], sem).wait()


def _gather_step(i, n, src_hbm, idx_hbm, buf, idx, sem_rows, sem_idx):
    rows = buf.shape[1]
    slot = i % 2
    nslot = 1 - slot

    def idx_copy(b, s):
        return pltpu.make_async_copy(idx_hbm.at[pl.ds(b * rows, rows)], idx.at[pl.ds(s * rows, rows)], sem_idx.at[s])

    @pl.when(jnp.logical_and(i == 0, n > 0))
    def _():
        idx_copy(0, 0).start()
        idx_copy(0, 0).wait()
        _start_rows(src_hbm, idx, 0, buf.at[0], sem_rows.at[0], rows)

        @pl.when(n > 1)
        def _():
            idx_copy(1, 1).start()

    @pl.when(i + 1 < n)
    def _():
        idx_copy(i + 1, nslot).wait()
        _start_rows(src_hbm, idx, nslot * rows, buf.at[nslot], sem_rows.at[nslot], rows)

    @pl.when(i + 2 < n)
    def _():
        idx_copy(i + 2, slot).start()

    @pl.when(i < n)
    def _():
        _wait_rows(src_hbm, buf.at[slot], sem_rows.at[slot], rows)
    return slot


SCATTER_BUFS = 3


def _pack_bf16_pair(a, b):
    abits = lax.bitcast_convert_type(a.astype(BF16).astype(F32), jnp.uint32)
    bbits = lax.bitcast_convert_type(b.astype(BF16).astype(F32), jnp.uint32)
    return lax.bitcast_convert_type((abits >> 16) | (bbits & jnp.uint32(0xFFFF0000)), F32)


def _unpack_bf16_pair(w):
    bits = lax.bitcast_convert_type(w, jnp.uint32)
    lo = lax.bitcast_convert_type(bits << 16, F32)
    hi = lax.bitcast_convert_type(bits & jnp.uint32(0xFFFF0000), F32)
    return lo, hi


def _scatter_kernel(zero_blk_ref, x_hbm, dest_hbm, gf_ref, xr_hbm, buf, rbuf, idx, sem_x, sem_rows, sem_idx,
                    sem_zero, *, blk_rows):
    i = pl.program_id(0)
    n = pl.num_programs(0)
    tm = buf.shape[1]
    rows = TOP_K * tm
    islot = i % 2
    xslot = i % SCATTER_BUFS

    def idx_copy(b, s):
        return pltpu.make_async_copy(dest_hbm.at[pl.ds(b * rows, rows)], idx.at[pl.ds(s * rows, rows)], sem_idx.at[s])

    def x_copy(b, s):
        return pltpu.make_async_copy(x_hbm.at[pl.ds(b * tm, tm), :], buf.at[s], sem_x.at[s])

    def wait_rows(s):
        for k in range(TOP_K):
            pltpu.make_async_copy(rbuf.at[s], xr_hbm.at[pl.ds(0, tm)], sem_rows.at[s]).wait()

    @pl.when(i == 0)
    def _():
        idx_copy(0, 0).start()
        x_copy(0, 0).start()
        zsrc = rbuf.at[SCATTER_BUFS - 1]
        zsrc[...] = jnp.zeros(zsrc.shape, zsrc.dtype)

        def zero_copies(e):
            b = zero_blk_ref[e]
            return b >= 0, [pltpu.make_async_copy(zsrc, xr_hbm.at[pl.ds(b * blk_rows + h * tm, tm)], sem_zero)
                            for h in range(blk_rows // tm)]

        def start(e, carry):
            live, copies = zero_copies(e)

            @pl.when(live)
            def _():
                for c in copies:
                    c.start()
            return carry

        def wait(e, carry):
            live, copies = zero_copies(e)

            @pl.when(live)
            def _():
                for c in copies:
                    c.wait()
            return carry

        lax.fori_loop(0, zero_blk_ref.shape[0], start, 0)
        lax.fori_loop(0, zero_blk_ref.shape[0], wait, 0)

    @pl.when(i >= 2)
    def _():
        wait_rows((i - 2) % SCATTER_BUFS)

    @pl.when(i + 1 < n)
    def _():
        idx_copy(i + 1, 1 - islot).start()
        x_copy(i + 1, (i + 1) % SCATTER_BUFS).start()

    idx_copy(i, islot).wait()
    x_copy(i, xslot).wait()

    x = buf[xslot]
    hn = x * lax.rsqrt(jnp.mean(x * x, axis=-1, keepdims=True) + EPS) * gf_ref[...]
    rbuf[xslot, :, 0, :] = _pack_bf16_pair(hn[:, :D_MODEL // 2], hn[:, D_MODEL // 2:])
    for j in range(tm):
        src = rbuf.at[xslot, j]
        for k in range(TOP_K):
            dst = xr_hbm.at[idx[islot * rows + k * tm + j]]
            pltpu.make_async_copy(src, dst, sem_rows.at[xslot]).start(priority=k % 2)

    @pl.when(i == n - 1)
    def _():
        @pl.when(n > 1)
        def _():
            wait_rows((i - 1) % SCATTER_BUFS)
        wait_rows(xslot)


def _scatter_call(x1, pos_tiles, zero_blk, gf, n_rows, tm, blk_rows):
    t = x1.shape[0]
    rows = TOP_K * tm
    assert blk_rows % tm == 0
    any_spec = pl.BlockSpec(memory_space=pl.ANY)
    return pl.pallas_call(
        functools.partial(_scatter_kernel, blk_rows=blk_rows),
        grid_spec=pltpu.PrefetchScalarGridSpec(
            num_scalar_prefetch=1,
            grid=(t // tm,),
            in_specs=[any_spec, any_spec, pl.BlockSpec(gf.shape, lambda i, zb: (0, 0))],
            out_specs=any_spec,
            scratch_shapes=[pltpu.VMEM((SCATTER_BUFS, tm, D_MODEL), F32),
                            pltpu.VMEM((SCATTER_BUFS, tm, 1, D_MODEL // 2), F32),
                            pltpu.SMEM((2 * rows,), jnp.int32),
                            pltpu.SemaphoreType.DMA((SCATTER_BUFS,)), pltpu.SemaphoreType.DMA((SCATTER_BUFS,)),
                            pltpu.SemaphoreType.DMA((2,)), pltpu.SemaphoreType.DMA]),
        out_shape=jax.ShapeDtypeStruct((n_rows, 1, D_MODEL // 2), F32),
        compiler_params=_cparams(("arbitrary",)),
        name="scatter",
    )(zero_blk, x1, pos_tiles, gf)


def _expert_kernel(blk_exp_ref, nblk_ref, nvalid_ref, x_ref, wg_ref, wu_ref, wd_ref, y_ref, wg_bf, wu_bf, wd_bf):
    i = pl.program_id(0)

    @pl.when(jnp.logical_or(i == 0, blk_exp_ref[i] != blk_exp_ref[jnp.maximum(i - 1, 0)]))
    def _():
        wg_bf[...] = wg_ref[0].astype(BF16)
        wu_bf[...] = wu_ref[0].astype(BF16)
        wd_bf[...] = wd_ref[0].astype(BF16)

    @pl.when(i < nblk_ref[0])
    def _():
        w = x_ref[:, 0, :]
        real = lax.broadcasted_iota(jnp.int32, w.shape, 0) < nvalid_ref[i]
        lo, hi = [v.astype(BF16) for v in _unpack_bf16_pair(jnp.where(real, w, 0.0))]
        half = D_MODEL // 2

        def up_proj(w_bf):
            return (jnp.dot(lo, w_bf[:half, :], preferred_element_type=F32)
                    + jnp.dot(hi, w_bf[half:, :], preferred_element_type=F32))

        gt, up = up_proj(wg_bf), up_proj(wu_bf)
        act = (gt * (0.5 * jnp.tanh(0.5 * gt) + 0.5) * up).astype(BF16)
        y_ref[:, 0, :] = jnp.dot(act, wd_bf[...], preferred_element_type=F32)

    @pl.when(i >= nblk_ref[0])
    def _():
        y_ref[...] = jnp.zeros(y_ref.shape, y_ref.dtype)


def _expert_call(xr, rows, blk_exp, nblk, nvalid, wg, wu, wd):
    nb = xr.shape[0] // rows
    return pl.pallas_call(
        _expert_kernel,
        grid_spec=pltpu.PrefetchScalarGridSpec(
            num_scalar_prefetch=3,
            grid=(nb,),
            in_specs=[pl.BlockSpec((rows, 1, D_MODEL // 2), lambda i, be, n, nv: (jnp.minimum(i, n[0] - 1), 0, 0)),
                      pl.BlockSpec((1, D_MODEL, D_EXPERT), lambda i, be, n, nv: (be[i], 0, 0)),
                      pl.BlockSpec((1, D_MODEL, D_EXPERT), lambda i, be, n, nv: (be[i], 0, 0)),
                      pl.BlockSpec((1, D_EXPERT, D_MODEL), lambda i, be, n, nv: (be[i], 0, 0))],
            out_specs=pl.BlockSpec((rows, 1, D_MODEL), lambda i, be, n, nv: (i, 0, 0)),
            scratch_shapes=[pltpu.VMEM((D_MODEL, D_EXPERT), BF16), pltpu.VMEM((D_MODEL, D_EXPERT), BF16),
                            pltpu.VMEM((D_EXPERT, D_MODEL), BF16)]),
        out_shape=jax.ShapeDtypeStruct((nb * rows, 1, D_MODEL), F32),
        compiler_params=_cparams(("arbitrary",)),
        name="experts",
    )(blk_exp, nblk, nvalid, xr, wg, wu, wd)


def _combine_kernel(x1_ref, w0_ref, w1_ref, y_hbm, pos_hbm, o_ref, buf, idx, sem_rows, sem_idx):
    rows = buf.shape[1]
    tm = rows // 2
    slot = _gather_step(pl.program_id(0), pl.num_programs(0), y_hbm, pos_hbm, buf, idx, sem_rows, sem_idx)
    w0 = w0_ref[...]
    w1 = w1_ref[...]
    for c in range(D_MODEL // LANES):
        sl = slice(c * LANES, (c + 1) * LANES)
        o_ref[:, sl] = x1_ref[:, sl] + w0 * buf[slot, 0:tm, 0, sl] + w1 * buf[slot, tm:rows, 0, sl]


def _combine_call(x1, w0, w1, yr, pos_tiles, tm):
    t = x1.shape[0]
    rows = TOP_K * tm
    row = lambda w: pl.BlockSpec((tm, w), lambda i: (i, 0))
    return pl.pallas_call(
        _combine_kernel,
        grid=(t // tm,),
        in_specs=[row(D_MODEL), row(LANES), row(LANES),
                  pl.BlockSpec(memory_space=pl.ANY), pl.BlockSpec(memory_space=pl.ANY)],
        out_specs=row(D_MODEL),
        out_shape=jax.ShapeDtypeStruct((t, D_MODEL), F32),
        scratch_shapes=[pltpu.VMEM((2, rows, 1, D_MODEL), F32), pltpu.SMEM((2 * rows,), jnp.int32),
                        pltpu.SemaphoreType.DMA((2,)), pltpu.SemaphoreType.DMA((2,))],
        compiler_params=_cparams(("arbitrary",)),
        name="combine",
    )(x1, w0, w1, yr, pos_tiles)


def _rot_cols(w):
    half = ROPE_DIM // 2
    return jnp.concatenate([-w[..., half:], w[..., :half]], axis=-1)


def _head_pad(nope, rope):
    z = jnp.zeros(rope.shape[:-1] + (LANES - B_QK_DIM,), rope.dtype)
    out = jnp.concatenate([nope, rope, z], axis=-1)
    return out.reshape(out.shape[:-2] + (out.shape[-2] * LANES,))


def _t5_bucket(dist):
    max_exact = REL_BUCKETS // 2
    d = jnp.maximum(dist, 0)
    ratio = jnp.maximum(d, 1).astype(F32) / max_exact
    large = max_exact + (jnp.log(ratio) / math.log(REL_MAX_DIST / max_exact)
                         * (REL_BUCKETS - max_exact)).astype(jnp.int32)
    large = jnp.minimum(large, REL_BUCKETS - 1)
    return jnp.where(d < max_exact, d, large)


def _swa_bias(rel_bias, tq):
    qi = jnp.arange(tq)[:, None]
    kj = jnp.arange(2 * tq)[None, :]
    dist = qi + tq - kj
    onehot = (_t5_bucket(dist)[:, :, None] == jnp.arange(REL_BUCKETS)[None, None, :]).astype(F32)
    bias = jnp.einsum("qkb,bh->hqk", onehot, rel_bias.astype(F32), precision=lax.Precision.HIGHEST) * LOG2E
    mask = (dist >= 0) & (dist < WINDOW)
    bias = jnp.where(mask[None], bias, NEG)
    return bias.reshape(A_KV_HEADS, (A_HEADS // A_KV_HEADS) * tq, 2 * tq)


def _rope_tables(seq):
    half = ROPE_DIM // 2
    inv_freq = ROPE_THETA ** (-jnp.arange(half, dtype=F32) / half)
    ang = jnp.arange(seq, dtype=F32)[:, None] * inv_freq[None, :]
    cos = jnp.concatenate([jnp.cos(ang), jnp.cos(ang)], axis=-1)
    sin = jnp.concatenate([jnp.sin(ang), jnp.sin(ang)], axis=-1)
    return cos, sin


def _lane_table(nope_part, rope_part):
    seq = rope_part.shape[0]
    nope_part = jnp.broadcast_to(nope_part, (seq, NOPE_DIM))
    return jnp.concatenate([nope_part, rope_part, jnp.zeros((seq, LANES - B_QK_DIM), F32)], axis=-1)


def _prep_layer(l, seq, cos, sin, attn_norm, w_in, b_gate, a_q_norm, a_k_norm, w_oa, q_a_norm, w_qb, kv_a_norm,
                w_kvb, b_q_norm, b_k_norm, w_ob, w_o, ffn_norm, w_rg, b_rg, w_re, b_re, w_eg, w_eu, w_ed):
    w = w_in[l]
    o = 0
    parts = []
    for sz in (A_WIDTH, A_KV_HEADS * A_HEAD_DIM, A_KV_HEADS * A_HEAD_DIM, Q_LORA, KV_LORA, ROPE_DIM, 2 * D_MODEL):
        parts.append(w[:, o:o + sz])
        o += sz
    wqa, wka, wva, wcq, wckv, wkpe, wg = parts
    perm = jnp.array([c + (A_HEADS // 2) * h for c in range(A_HEADS // 2) for h in range(2)])
    wqa = wqa.reshape(D_MODEL, A_HEADS, A_HEAD_DIM)[:, perm].reshape(D_MODEL, A_WIDTH)
    zl = jnp.zeros((D_MODEL, NOPE_DIM), F32)
    zr = jnp.zeros((D_MODEL, LANES - B_QK_DIM), F32)
    w1 = jnp.concatenate([wqa, wka, wva, wcq, wckv, zl, wkpe, zr, zl, _rot_cols(wkpe), zr, 0.5 * wg],
                         axis=1).astype(BF16)

    wq = w_qb[l].reshape(Q_LORA, B_HEADS, B_QK_DIM)
    zq = jnp.zeros((Q_LORA, B_HEADS, NOPE_DIM), F32)
    wq2 = jnp.concatenate([_head_pad(wq[..., :NOPE_DIM], wq[..., NOPE_DIM:]),
                           _head_pad(zq, _rot_cols(wq[..., NOPE_DIM:]))], axis=1).astype(BF16)
    wkv = w_kvb[l].reshape(KV_LORA, B_HEADS, NOPE_DIM + B_V_DIM)
    wk2 = _head_pad(wkv[..., :NOPE_DIM], jnp.zeros((KV_LORA, B_HEADS, ROPE_DIM), F32)).astype(BF16)
    wv = wkv[..., NOPE_DIM:]
    zv = jnp.zeros_like(wv)
    wv2 = jnp.where((jnp.arange(B_HEADS) % 2 == 0)[None, :, None], jnp.concatenate([wv, zv], -1),
                    jnp.concatenate([zv, wv], -1)).reshape(KV_LORA, B_HEADS * LANES).astype(BF16)
    one_lane = jnp.where(jnp.arange(B_HEADS) % 2 == 0, V_ONE_EVEN, V_ONE_ODD)
    vone = (jnp.arange(LANES)[None, :] == one_lane[:, None]).astype(F32).reshape(1, B_HEADS * LANES)

    def rope_tabs(gain, scale):
        gn, gr = gain[:NOPE_DIM], gain[NOPE_DIM:]
        gr_rot = jnp.concatenate([gr[ROPE_DIM // 2:], gr[:ROPE_DIM // 2]])
        a = _lane_table(gn[None, :] * scale, gr[None, :] * cos * scale)
        b = _lane_table(jnp.zeros((1, NOPE_DIM), F32), gr_rot[None, :] * sin * scale)
        return a, b

    aq, bq = rope_tabs(b_q_norm[l], B_QK_DIM ** -0.5 * LOG2E)
    score_bound = math.sqrt(B_QK_DIM) * LOG2E * jnp.max(jnp.abs(b_q_norm[l])) * jnp.max(jnp.abs(b_k_norm[l]))
    ak, bk = rope_tabs(b_k_norm[l], 1.0)

    gqa = (jnp.tile(a_q_norm[l], 2) * (A_HEAD_DIM ** -0.5 * LOG2E))[None, :]
    gka = jnp.tile(a_k_norm[l], 2)[None, :]
    woa = w_oa[l].reshape(A_HEADS, A_HEAD_DIM, D_MODEL)[perm].reshape(A_WIDTH, D_MODEL).astype(BF16)

    wr = jnp.concatenate([w_rg[l], w_re[l], jnp.zeros((D_MODEL, LANES - N_GROUPS - N_EXPERTS), F32)], axis=1)
    wrh = wr.astype(BF16)
    wrl = (wr - wrh.astype(F32)).astype(BF16)
    br = jnp.concatenate([b_rg[l], b_re[l], jnp.zeros((LANES - N_GROUPS - N_EXPERTS,), F32)])[None, :]

    return dict(
        gn=attn_norm[l][None, :], w1=w1, wq2=wq2, wk2=wk2, wv2=wv2, vone=vone, gcq=q_a_norm[l][None, :],
        gckv=kv_a_norm[l][None, :], score_bound=score_bound, bg=0.5 * b_gate[l][None, :], gqa=gqa, gka=gka,
        aq=aq, bq=bq, ak=ak, bk=bk,
        woa=woa, wob=w_ob[l].astype(BF16), wo=w_o[l].astype(BF16), gf=ffn_norm[l][None, :], wr=jnp.concatenate([wrh, wrl], axis=1), br=br,
        wg=w_eg.reshape((-1,) + w_eg.shape[2:]), wu=w_eu.reshape((-1,) + w_eu.shape[2:]),
        wd=w_ed.reshape((-1,) + w_ed.shape[2:]), expert_base=l * N_EXPERTS)


def _tril(n):
    return (jnp.arange(n)[:, None] >= jnp.arange(n)[None, :]).astype(jnp.int32)


def _dispatch(route, tile_cnt, rows):
    t = route.shape[1]
    nt = tile_cnt.shape[0]
    tile_cnt = tile_cnt[:, 0, :N_EXPERTS]
    counts = jnp.sum(tile_cnt, axis=0)
    padded = ((counts + rows - 1) // rows) * rows
    pends = jnp.dot(_tril(N_EXPERTS), padded)
    pstarts = pends - padded
    tile_base = pstarts[None, :] + jnp.dot(_tril(nt), tile_cnt) - tile_cnt
    base = jnp.repeat(tile_base.T, t // nt, axis=1)
    onehot = route[:TOP_K, None, :] == jnp.arange(N_EXPERTS, dtype=jnp.int32)[None, :, None]
    dest = jnp.sum(jnp.where(onehot, base[None], 0), axis=1) + route[TOP_K:2 * TOP_K]
    n_rows = t * TOP_K + N_EXPERTS * rows
    n_blocks = n_rows // rows
    blk_start = jnp.arange(n_blocks, dtype=jnp.int32)[:, None] * rows
    blk_exp = jnp.minimum(jnp.sum((pends[None, :] <= blk_start).astype(jnp.int32), axis=1), N_EXPERTS - 1)
    nblk = (pends[-1] // rows).astype(jnp.int32).reshape(1)
    nvalid = jnp.clip((pstarts + counts)[blk_exp] - blk_start[:, 0], 0, rows).astype(jnp.int32)
    last_blk = jnp.where(padded > 0, pends // rows - 1, -1)
    trailing = nblk[0] + jnp.arange(N_EXPERTS, dtype=jnp.int32)
    zero_blk = jnp.concatenate([last_blk, jnp.where(trailing < n_blocks, trailing, -1)]).astype(jnp.int32)
    return dest, blk_exp, nblk, nvalid, zero_blk, n_rows


def _swa_tables(sinks, rel_bias, a_q_norm, a_k_norm):
    per_group = A_HEADS // A_KV_HEADS
    tab = jnp.broadcast_to((sinks * LOG2E).reshape(A_KV_HEADS, per_group, 1, 1),
                           (A_KV_HEADS, per_group, SWA_TQ, LANES)).reshape(A_KV_HEADS, per_group * SWA_TQ, LANES)
    qk = math.sqrt(A_HEAD_DIM) * jnp.max(jnp.abs(a_q_norm)) * jnp.max(jnp.abs(a_k_norm))
    bound = LOG2E * jnp.maximum(qk + jnp.max(jnp.abs(rel_bias)), jnp.max(jnp.abs(sinks)))
    return tab, bound


def _pos_tiles(dest, tm):
    t = dest.shape[1]
    return dest.reshape(TOP_K, t // tm, tm).transpose(1, 0, 2).reshape(t * TOP_K)


def _layer(x2d, bsz, seq, p, sink_tab, swa_bound, bias):
    qa, ka, va, qb, kb, vb, gates = _proj_call(x2d, seq, p)
    oa = _swa_call(qa, ka, va, sink_tab, bias, bsz, seq, swa_bound)
    ob = _mla_call(qb, kb, vb, bsz, seq, p["score_bound"])
    x1, w0, w1, route, tile_cnt = _merge_call(x2d, oa, ob, gates, p)
    dest, blk_exp, nblk, nvalid, zero_blk, n_rows = _dispatch(route, tile_cnt, MOE_ROWS)
    pos_tiles = _pos_tiles(dest, COMB_TM)
    xr = _scatter_call(x1, pos_tiles, zero_blk, p["gf"], n_rows, COMB_TM, MOE_ROWS)
    yr = _expert_call(xr, MOE_ROWS, blk_exp + p["expert_base"], nblk, nvalid, p["wg"], p["wu"], p["wd"])
    return _combine_call(x1, w0, w1, yr, pos_tiles, COMB_TM)


def kernel(x, attn_norm, w_in, b_gate, a_q_norm, a_k_norm, a_sinks, rel_bias, w_oa, q_a_norm, w_qb, kv_a_norm, w_kvb, b_q_norm, b_k_norm, w_ob, w_o, ffn_norm, w_router_group, b_router_group, w_router_expert, b_router_expert, w_exp_gate, w_exp_up, w_exp_down):
    bsz, seq, d = x.shape
    assert d == D_MODEL and WINDOW == SWA_TQ and MLA_TQ % MLA_TK == 0 and MOE_ROWS % COMB_TM == 0
    assert all(seq % tile == 0 for tile in (MLA_TQ, PROJ_TM, SWA_SUB * SWA_TQ))
    assert all((bsz * seq) % tile == 0 for tile in (MERGE_TM, COMB_TM))
    depth = w_in.shape[0]
    cos, sin = _rope_tables(seq)
    bias = _swa_bias(rel_bias, SWA_TQ)
    x2d = x.reshape(bsz * seq, d)
    for l in range(depth):
        p = _prep_layer(l, seq, cos, sin, attn_norm, w_in, b_gate, a_q_norm, a_k_norm, w_oa, q_a_norm, w_qb,
                        kv_a_norm, w_kvb, b_q_norm, b_k_norm, w_ob, w_o, ffn_norm, w_router_group,
                        b_router_group, w_router_expert, b_router_expert, w_exp_gate, w_exp_up, w_exp_down)
        sink_tab, swa_bound = _swa_tables(a_sinks[l], rel_bias, a_q_norm[l], a_k_norm[l])
        x2d = _layer(x2d, bsz, seq, p, sink_tab, swa_bound, bias)
    return x2d.reshape(bsz, seq, d)
```

```python
import functools
import math

import jax
import jax.numpy as jnp
from jax import lax
from jax.experimental import pallas as pl
from jax.experimental.pallas import tpu as pltpu

F32 = jnp.float32
BF16 = jnp.bfloat16

D_MODEL = 1024
A_HEADS = 8
A_KV_HEADS = 2
A_HEAD_DIM = 64
A_WIDTH = A_HEADS * A_HEAD_DIM
WINDOW = 128
REL_BUCKETS = 32
REL_MAX_DIST = 128
B_HEADS = 8
Q_LORA = 256
KV_LORA = 128
NOPE_DIM = 64
ROPE_DIM = 32
B_QK_DIM = NOPE_DIM + ROPE_DIM
B_V_DIM = 64
B_WIDTH = B_HEADS * B_V_DIM
ROPE_THETA = 10000.0
N_GROUPS = 4
EXPERTS_PER_GROUP = 8
N_EXPERTS = N_GROUPS * EXPERTS_PER_GROUP
TOP_K = 2
D_EXPERT = 256
EPS = 1e-6

LANES = 128
NEG = -1e30
LOG2E = math.log2(math.e)
V_ONE_EVEN = B_V_DIM
V_ONE_ODD = 0
EXP2_SAFE_BOUND = 100.0
VMEM_LIMIT = 56 * 1024 * 1024

C_QA = 0
C_KA = C_QA + A_WIDTH
C_VA = C_KA + LANES
C_CQ = C_VA + LANES
C_CKV = C_CQ + Q_LORA
C_KPE = C_CKV + KV_LORA
C_KPR = C_KPE + LANES
C_GATE = C_KPR + LANES
C_END = C_GATE + 2 * D_MODEL

PROJ_TM = 512
SWA_TQ = 128
SWA_SUB = 8
MLA_TQ = 1024
MLA_TK = 512
MERGE_TM = 1024
MOE_ROWS = 512
COMB_TM = 256


def _cparams(sem):
    return pltpu.CompilerParams(dimension_semantics=sem, vmem_limit_bytes=VMEM_LIMIT)


def _proj_kernel(x_ref, gn_ref, w1_ref, wq2_ref, wk2_ref, wv2_ref, vone_ref, gcq_ref, gckv_ref, bg_ref,
                 gqa_ref, gka_ref, aq_ref, bq_ref, ak_ref, bk_ref,
                 qa_ref, ka_ref, va_ref, qb_ref, kb_ref, vb_ref, gate_ref):
    tm = x_ref.shape[0]
    x = x_ref[...]
    h = (x * lax.rsqrt(jnp.mean(x * x, axis=-1, keepdims=True) + EPS) * gn_ref[...]).astype(BF16)

    def proj(a, b):
        return jnp.dot(h, w1_ref[:, a:b], preferred_element_type=F32)

    lo = lax.broadcasted_iota(jnp.int32, (tm, LANES), 1) < A_HEAD_DIM

    def pair_norm(t, gain):
        t2 = t * t
        s_lo = jnp.sum(jnp.where(lo, t2, 0.0), axis=-1, keepdims=True)
        s_hi = jnp.sum(jnp.where(lo, 0.0, t2), axis=-1, keepdims=True)
        r = jnp.where(lo, lax.rsqrt(s_lo / A_HEAD_DIM + EPS), lax.rsqrt(s_hi / A_HEAD_DIM + EPS))
        return (t * r * gain).astype(BF16)

    qa = proj(C_QA, C_KA)
    for c in range(A_WIDTH // LANES):
        sl = slice(c * LANES, (c + 1) * LANES)
        qa_ref[:, sl] = pair_norm(qa[:, sl], gqa_ref[...])
    kvp = proj(C_KA, C_CQ)
    ka_ref[...] = pair_norm(kvp[:, :LANES], gka_ref[...])
    va_ref[...] = kvp[:, LANES:].astype(BF16)

    lat = proj(C_CQ, C_GATE)
    cq = lat[:, :Q_LORA]
    cqn = (cq * lax.rsqrt(jnp.mean(cq * cq, axis=-1, keepdims=True) + EPS) * gcq_ref[...]).astype(BF16)
    ckv = lat[:, Q_LORA:Q_LORA + KV_LORA]
    ckvn = (ckv * lax.rsqrt(jnp.mean(ckv * ckv, axis=-1, keepdims=True) + EPS) * gckv_ref[...]).astype(BF16)
    kpe = lat[:, C_KPE - C_CQ:C_KPR - C_CQ]
    kpr = lat[:, C_KPR - C_CQ:C_GATE - C_CQ]

    aq, bq, ak, bk = aq_ref[...], bq_ref[...], ak_ref[...], bk_ref[...]
    q2 = jnp.dot(cqn, wq2_ref[...], preferred_element_type=F32)
    k2 = jnp.dot(ckvn, wk2_ref[...], preferred_element_type=F32)
    vb_ref[...] = (jnp.dot(ckvn, wv2_ref[...], preferred_element_type=F32) + vone_ref[...]).astype(BF16)
    kr = kpr * bk
    for hd in range(B_HEADS):
        sl = slice(hd * LANES, (hd + 1) * LANES)
        qp = q2[:, sl]
        qr = q2[:, B_HEADS * LANES + hd * LANES:B_HEADS * LANES + (hd + 1) * LANES]
        rq = lax.rsqrt(jnp.sum(qp * qp, axis=-1, keepdims=True) / B_QK_DIM + EPS)
        qb_ref[:, sl] = (rq * (qp * aq + qr * bq)).astype(BF16)
        kp = k2[:, sl] + kpe
        rk = lax.rsqrt(jnp.sum(kp * kp, axis=-1, keepdims=True) / B_QK_DIM + EPS)
        kb_ref[:, sl] = (rk * (kp * ak + kr)).astype(BF16)

    gate_ref[...] = (0.5 * jnp.tanh(proj(C_GATE, C_END) + bg_ref[...]) + 0.5).astype(BF16)


def _proj_call(x2d, seq, p):
    t = x2d.shape[0]
    tm = PROJ_TM
    nt = t // tm
    npos = seq // tm
    row = lambda w: pl.BlockSpec((tm, w), lambda i: (i, 0))
    full = lambda a: pl.BlockSpec(a.shape, lambda i: (0,) * a.ndim)
    pos = pl.BlockSpec((tm, LANES), lambda i: (i % npos, 0))
    consts = [p["gn"], p["w1"], p["wq2"], p["wk2"], p["wv2"], p["vone"], p["gcq"], p["gckv"], p["bg"], p["gqa"],
              p["gka"]]
    tabs = [p["aq"], p["bq"], p["ak"], p["bk"]]
    widths = [A_WIDTH, LANES, LANES, B_HEADS * LANES, B_HEADS * LANES, B_HEADS * LANES, 2 * D_MODEL]
    return pl.pallas_call(
        _proj_kernel,
        grid=(nt,),
        in_specs=[row(D_MODEL)] + [full(a) for a in consts] + [pos] * 4,
        out_specs=[row(w) for w in widths],
        out_shape=[jax.ShapeDtypeStruct((t, w), BF16) for w in widths],
        compiler_params=_cparams(("parallel",)),
        name="proj",
    )(x2d, *consts, *tabs)


def _swa_kernel(q_ref, kc_ref, kp_ref, vc_ref, vp_ref, bias_ref, sink_ref, o_ref, *, shift):
    tq = SWA_TQ
    groups = A_WIDTH // LANES
    first = pl.program_id(1) == 0
    one_col = (lax.broadcasted_iota(jnp.int32, (2 * tq, LANES), 1) == 0).astype(BF16)
    lo = lax.broadcasted_iota(jnp.int32, (tq, LANES), 1) < A_HEAD_DIM
    no_prev = jnp.logical_and(first, lax.broadcasted_iota(jnp.int32, (groups * tq, 2 * tq), 1) < tq)
    for sub in range(SWA_SUB):
        if sub == 0:
            k = jnp.concatenate([kp_ref[...], kc_ref[0:tq, :]], axis=0)
            v = jnp.concatenate([vp_ref[...], vc_ref[0:tq, :]], axis=0)
        else:
            k = kc_ref[(sub - 1) * tq:(sub + 1) * tq, :]
            v = vc_ref[(sub - 1) * tq:(sub + 1) * tq, :]
        v_ext = jnp.concatenate([v, one_col], axis=1)
        qrows = slice(sub * tq, (sub + 1) * tq)
        qs = [q_ref[qrows, c * LANES:(c + 1) * LANES] for c in range(groups)]
        res = []
        for g in range(A_KV_HEADS):
            keep = lo if g == 0 else jnp.logical_not(lo)
            qg = jnp.concatenate([jnp.where(keep, qc, jnp.zeros_like(qc)) for qc in qs], axis=0)
            s = lax.dot_general(qg, k, (((1,), (1,)), ((), ())), preferred_element_type=F32) + bias_ref[g]
            sink = sink_ref[g][:, :1]
            if shift:
                if sub == 0:
                    s = jnp.where(no_prev, NEG, s)
                m = jnp.maximum(jnp.max(s, axis=-1, keepdims=True), sink)
                e = jnp.exp2(s - m)
                sink_term = jnp.exp2(sink - m)
            else:
                e = jnp.exp2(s)
                if sub == 0:
                    e = jnp.where(no_prev, 0.0, e)
                sink_term = jnp.exp2(sink)
            pv = jnp.dot(e.astype(BF16), v_ext, preferred_element_type=F32)
            res.append(pv[:, :LANES] * (1.0 / (pv[:, LANES:LANES + 1] + sink_term)))
        for c in range(groups):
            rows = slice(c * tq, (c + 1) * tq)
            o_ref[qrows, c * LANES:(c + 1) * LANES] = jnp.where(lo, res[0][rows], res[1][rows]).astype(BF16)


def _swa_call(qa, ka, va, sink_tab, bias, bsz, seq, score_bound):
    t = qa.shape[0]
    tq = SWA_TQ
    step = SWA_SUB * tq
    ns = seq // step
    cur = lambda b, i: (b * ns + i, 0)
    prev = lambda b, i: (b * (seq // tq) + jnp.maximum(SWA_SUB * i - 1, 0), 0)
    full = lambda a: pl.BlockSpec(a.shape, lambda b, i: (0,) * a.ndim)

    def call(shift):
        return pl.pallas_call(
            functools.partial(_swa_kernel, shift=shift),
            grid=(bsz, ns),
            in_specs=[pl.BlockSpec((step, A_WIDTH), cur),
                      pl.BlockSpec((step, LANES), cur), pl.BlockSpec((tq, LANES), prev),
                      pl.BlockSpec((step, LANES), cur), pl.BlockSpec((tq, LANES), prev),
                      full(bias), full(sink_tab)],
            out_specs=pl.BlockSpec((step, A_WIDTH), cur),
            out_shape=jax.ShapeDtypeStruct((t, A_WIDTH), BF16),
            compiler_params=_cparams(("parallel", "parallel")),
            name="swa_shifted" if shift else "swa",
        )(qa, ka, ka, va, va, bias, sink_tab)

    return lax.cond(score_bound <= EXP2_SAFE_BOUND, lambda: call(False), lambda: call(True))


def _mla_kernel(q_ref, k_ref, v_ref, o_ref, acc_sc, m_sc, *, shift):
    tq = q_ref.shape[0]
    tk = MLA_TK
    per_tile = tq // tk
    qi = pl.program_id(2)
    acc_sc[...] = jnp.zeros(acc_sc.shape, F32)
    if shift:
        m_sc[...] = jnp.full(m_sc.shape, NEG, F32)

    def tile(j, r0, r1, masked):
        keys = pl.ds(pl.multiple_of(j * tk, tk), tk)
        if masked:
            causal = (lax.broadcasted_iota(jnp.int32, (r1 - r0, tk), 1)
                      <= lax.broadcasted_iota(jnp.int32, (r1 - r0, tk), 0))
        for half in range(2):
            hs = slice(half * LANES, (half + 1) * LANES)
            s = lax.dot_general(q_ref[r0:r1, hs], k_ref[keys, hs], (((1,), (1,)), ((), ())),
                                preferred_element_type=F32)
            if shift:
                if masked:
                    s = jnp.where(causal, s, NEG)
                m_prev = m_sc[half, r0:r1]
                m_new = jnp.maximum(m_prev, jnp.max(s, axis=-1, keepdims=True))
                m_sc[half, r0:r1] = m_new
                e = jnp.exp2(s - m_new)
                acc_sc[half, r0:r1] = jnp.exp2(m_prev - m_new) * acc_sc[half, r0:r1]
            else:
                e = jnp.exp2(s)
                if masked:
                    e = jnp.where(causal, e, 0.0)
            acc_sc[half, r0:r1] += jnp.dot(e.astype(BF16), v_ref[keys, hs], preferred_element_type=F32)

    def body(j, carry):
        tile(j, 0, tq, False)
        return carry

    lax.fori_loop(0, qi * per_tile, body, 0)
    for d in range(per_tile):
        tile(qi * per_tile + d, d * tk, (d + 1) * tk, True)
        if (d + 1) * tk < tq:
            tile(qi * per_tile + d, (d + 1) * tk, tq, False)
    a0 = acc_sc[0]
    a1 = acc_sc[1]
    lo = lax.broadcasted_iota(jnp.int32, (tq, LANES), 1) < B_V_DIM
    o_ref[...] = jnp.where(lo, a0 * (1.0 / a0[:, V_ONE_EVEN:V_ONE_EVEN + 1]),
                           a1 * (1.0 / a1[:, V_ONE_ODD:V_ONE_ODD + 1])).astype(BF16)


def _mla_call(qb, kb, vb, bsz, seq, score_bound):
    t = qb.shape[0]
    tq = MLA_TQ
    nq = seq // tq
    pairs = B_HEADS // 2

    def call(shift):
        return pl.pallas_call(
            functools.partial(_mla_kernel, shift=shift),
            grid=(bsz, pairs, nq),
            in_specs=[pl.BlockSpec((tq, 2 * LANES), lambda b, p, i: (b * nq + i, p)),
                      pl.BlockSpec((seq, 2 * LANES), lambda b, p, i: (b, p)),
                      pl.BlockSpec((seq, 2 * LANES), lambda b, p, i: (b, p))],
            out_specs=pl.BlockSpec((tq, LANES), lambda b, p, i: (b * nq + i, p)),
            out_shape=jax.ShapeDtypeStruct((t, B_WIDTH), BF16),
            scratch_shapes=[pltpu.VMEM((2, tq, LANES), F32), pltpu.VMEM((2, tq, 1), F32)],
            compiler_params=_cparams(("parallel", "parallel", "arbitrary")),
            name="mla_shifted" if shift else "mla",
        )(qb, kb, vb)

    return lax.cond(score_bound <= EXP2_SAFE_BOUND, lambda: call(False), lambda: call(True))


def _merge_kernel(x_ref, oa_ref, ob_ref, gate_ref, woa_ref, wob_ref, wo_ref, gf_ref, wr_ref, br_ref,
                  tri_ref, x1_ref, w0_ref, w1_ref, eid_ref, cnt_ref):
    tm = x_ref.shape[0]
    ya = jnp.dot(oa_ref[...], woa_ref[...], preferred_element_type=F32)
    yb = jnp.dot(ob_ref[...], wob_ref[...], preferred_element_type=F32)
    g = gate_ref[...].astype(F32)
    mix = (g[:, :D_MODEL] * ya + g[:, D_MODEL:] * yb).astype(BF16)
    x1 = x_ref[...] + jnp.dot(mix, wo_ref[...], preferred_element_type=F32)
    x1_ref[...] = x1

    hn = x1 * lax.rsqrt(jnp.mean(x1 * x1, axis=-1, keepdims=True) + EPS) * gf_ref[...]
    hh = hn.astype(BF16)
    hl = (hn - hh.astype(F32)).astype(BF16)
    ph = jnp.dot(hh, wr_ref[...], preferred_element_type=F32)
    pl_ = jnp.dot(hl, wr_ref[...], preferred_element_type=F32)
    logits = (ph[:, :LANES] + ph[:, LANES:]) + (pl_[:, :LANES] + pl_[:, LANES:]) + br_ref[...]

    lane = lax.broadcasted_iota(jnp.int32, (tm, LANES), 1).astype(F32)
    big = float(LANES)
    gmask = lane < N_GROUPS
    gl = jnp.where(gmask, logits, NEG)
    gmax = jnp.max(gl, axis=-1, keepdims=True)
    gsum = jnp.sum(jnp.where(gmask, jnp.exp(gl - gmax), 0.0), axis=-1, keepdims=True)
    g_p = 1.0 / gsum
    g_idx = jnp.min(jnp.where(gl == gmax, lane, big), axis=-1, keepdims=True)
    e_lo = N_GROUPS + EXPERTS_PER_GROUP * g_idx
    emask = jnp.logical_and(lane >= e_lo, lane < e_lo + EXPERTS_PER_GROUP)
    el = jnp.where(emask, logits, NEG)
    t1 = jnp.max(el, axis=-1, keepdims=True)
    i1 = jnp.min(jnp.where(el == t1, lane, big), axis=-1, keepdims=True)
    el2 = jnp.where(lane == i1, NEG, el)
    t2 = jnp.max(el2, axis=-1, keepdims=True)
    i2 = jnp.min(jnp.where(el2 == t2, lane, big), axis=-1, keepdims=True)
    e2 = jnp.exp(t2 - t1)
    w_first = g_p / (1.0 + e2)
    w0_ref[...] = jnp.broadcast_to(w_first, (tm, LANES))
    w1_ref[...] = jnp.broadcast_to(w_first * e2, (tm, LANES))
    ex1 = i1 - N_GROUPS
    ex2 = i2 - N_GROUPS

    oh1 = lane == ex1
    oh2 = lane == ex2
    cnt = (oh1.astype(F32) + oh2.astype(F32)).astype(BF16)
    before = jnp.dot(tri_ref[...], cnt, preferred_element_type=F32)
    r1 = jnp.sum(jnp.where(oh1, before, 0.0), axis=-1, keepdims=True)
    r2 = jnp.sum(jnp.where(oh2, before, 0.0), axis=-1, keepdims=True)
    ids = jnp.where(lane == 0.0, ex1, jnp.where(lane == 1.0, ex2, jnp.where(lane == 2.0, r1,
                                                                             jnp.where(lane == 3.0, r2, 0.0))))
    eid_ref[...] = ids.T[:eid_ref.shape[0], :].astype(jnp.int32)
    tile_cnt = jnp.sum(cnt.astype(F32), axis=0, keepdims=True)
    cnt_ref[...] = jnp.broadcast_to(tile_cnt, cnt_ref.shape[1:]).astype(jnp.int32)[None]


def _merge_call(x2d, oa, ob, gates, p):
    t = x2d.shape[0]
    tm = MERGE_TM
    nt = t // tm
    row = lambda w: pl.BlockSpec((tm, w), lambda i: (i, 0))
    full = lambda a: pl.BlockSpec(a.shape, lambda i: (0,) * a.ndim)
    tri = (jnp.arange(tm)[:, None] > jnp.arange(tm)[None, :]).astype(BF16)
    consts = [p["woa"], p["wob"], p["wo"], p["gf"], p["wr"], p["br"], tri]
    return pl.pallas_call(
        _merge_kernel,
        grid=(nt,),
        in_specs=[row(D_MODEL), row(A_WIDTH), row(B_WIDTH), row(2 * D_MODEL)] + [full(a) for a in consts],
        out_specs=[row(D_MODEL), row(LANES), row(LANES), pl.BlockSpec((8, tm), lambda i: (0, i)),
                   pl.BlockSpec((1, 8, LANES), lambda i: (i, 0, 0))],
        out_shape=[jax.ShapeDtypeStruct((t, D_MODEL), F32), jax.ShapeDtypeStruct((t, LANES), F32),
                   jax.ShapeDtypeStruct((t, LANES), F32), jax.ShapeDtypeStruct((8, t), jnp.int32),
                   jax.ShapeDtypeStruct((nt, 8, LANES), jnp.int32)],
        compiler_params=_cparams(("parallel",)),
        name="merge",
    )(x2d, oa, ob, gates, *consts)


def _rows_copy(src_hbm, row, dst, j, sem):
    return pltpu.make_async_copy(src_hbm.at[row], dst.at[j], sem)


def _start_rows(src_hbm, idx, base, dst, sem, n):
    for j in range(n):
        _rows_copy(src_hbm, idx[base + j], dst, j, sem).start(priority=j % 2)


WAIT_ROWS = 128


def _wait_rows(src_hbm, dst, sem, n):
    for c in range(n // WAIT_ROWS):
        rows = pl.ds(c * WAIT_ROWS, WAIT_ROWS)
        pltpu.make_async_copy(src_hbm.at[rows], dst.at[rows], sem).wait()


def _gather_step(i, n, src_hbm, idx_hbm, buf, idx, sem_rows, sem_idx):
    rows = buf.shape[1]
    slot = i % 2
    nslot = 1 - slot

    def idx_copy(b, s):
        return pltpu.make_async_copy(idx_hbm.at[pl.ds(b * rows, rows)], idx.at[pl.ds(s * rows, rows)], sem_idx.at[s])

    @pl.when(jnp.logical_and(i == 0, n > 0))
    def _():
        idx_copy(0, 0).start()
        idx_copy(0, 0).wait()
        _start_rows(src_hbm, idx, 0, buf.at[0], sem_rows.at[0], rows)

        @pl.when(n > 1)
        def _():
            idx_copy(1, 1).start()

    @pl.when(i + 1 < n)
    def _():
        idx_copy(i + 1, nslot).wait()
        _start_rows(src_hbm, idx, nslot * rows, buf.at[nslot], sem_rows.at[nslot], rows)

    @pl.when(i + 2 < n)
    def _():
        idx_copy(i + 2, slot).start()

    @pl.when(i < n)
    def _():
        _wait_rows(src_hbm, buf.at[slot], sem_rows.at[slot], rows)
    return slot


SCATTER_BUFS = 3


def _pack_bf16_pair(a, b):
    abits = lax.bitcast_convert_type(a.astype(BF16).astype(F32), jnp.uint32)
    bbits = lax.bitcast_convert_type(b.astype(BF16).astype(F32), jnp.uint32)
    return lax.bitcast_convert_type((abits >> 16) | (bbits & jnp.uint32(0xFFFF0000)), F32)


def _unpack_bf16_pair(w):
    bits = lax.bitcast_convert_type(w, jnp.uint32)
    lo = lax.bitcast_convert_type(bits << 16, F32)
    hi = lax.bitcast_convert_type(bits & jnp.uint32(0xFFFF0000), F32)
    return lo, hi


def _scatter_kernel(zero_blk_ref, x_hbm, dest_hbm, gf_ref, xr_hbm, buf, rbuf, idx, sem_x, sem_rows, sem_idx,
                    sem_zero, *, blk_rows):
    i = pl.program_id(0)
    n = pl.num_programs(0)
    tm = buf.shape[1]
    rows = TOP_K * tm
    islot = i % 2
    xslot = i % SCATTER_BUFS

    def idx_copy(b, s):
        return pltpu.make_async_copy(dest_hbm.at[pl.ds(b * rows, rows)], idx.at[pl.ds(s * rows, rows)], sem_idx.at[s])

    def x_copy(b, s):
        return pltpu.make_async_copy(x_hbm.at[pl.ds(b * tm, tm), :], buf.at[s], sem_x.at[s])

    def wait_rows(s):
        for k in range(TOP_K):
            pltpu.make_async_copy(rbuf.at[s], xr_hbm.at[pl.ds(0, tm)], sem_rows.at[s]).wait()

    @pl.when(i == 0)
    def _():
        idx_copy(0, 0).start()
        x_copy(0, 0).start()
        zsrc = rbuf.at[SCATTER_BUFS - 1]
        zsrc[...] = jnp.zeros(zsrc.shape, zsrc.dtype)

        def zero_copies(e):
            b = zero_blk_ref[e]
            return b >= 0, [pltpu.make_async_copy(zsrc, xr_hbm.at[pl.ds(b * blk_rows + h * tm, tm)], sem_zero)
                            for h in range(blk_rows // tm)]

        def start(e, carry):
            live, copies = zero_copies(e)

            @pl.when(live)
            def _():
                for c in copies:
                    c.start()
            return carry

        def wait(e, carry):
            live, copies = zero_copies(e)

            @pl.when(live)
            def _():
                for c in copies:
                    c.wait()
            return carry

        lax.fori_loop(0, zero_blk_ref.shape[0], start, 0)
        lax.fori_loop(0, zero_blk_ref.shape[0], wait, 0)

    @pl.when(i >= 2)
    def _():
        wait_rows((i - 2) % SCATTER_BUFS)

    @pl.when(i + 1 < n)
    def _():
        idx_copy(i + 1, 1 - islot).start()
        x_copy(i + 1, (i + 1) % SCATTER_BUFS).start()

    idx_copy(i, islot).wait()
    x_copy(i, xslot).wait()

    x = buf[xslot]
    hn = x * lax.rsqrt(jnp.mean(x * x, axis=-1, keepdims=True) + EPS) * gf_ref[...]
    rbuf[xslot, :, 0, :] = _pack_bf16_pair(hn[:, :D_MODEL // 2], hn[:, D_MODEL // 2:])
    for j in range(tm):
        src = rbuf.at[xslot, j]
        for k in range(TOP_K):
            dst = xr_hbm.at[idx[islot * rows + k * tm + j]]
            pltpu.make_async_copy(src, dst, sem_rows.at[xslot]).start(priority=k % 2)

    @pl.when(i == n - 1)
    def _():
        @pl.when(n > 1)
        def _():
            wait_rows((i - 1) % SCATTER_BUFS)
        wait_rows(xslot)


def _scatter_call(x1, pos_tiles, zero_blk, gf, n_rows, tm, blk_rows):
    t = x1.shape[0]
    rows = TOP_K * tm
    assert blk_rows % tm == 0
    any_spec = pl.BlockSpec(memory_space=pl.ANY)
    return pl.pallas_call(
        functools.partial(_scatter_kernel, blk_rows=blk_rows),
        grid_spec=pltpu.PrefetchScalarGridSpec(
            num_scalar_prefetch=1,
            grid=(t // tm,),
            in_specs=[any_spec, any_spec, pl.BlockSpec(gf.shape, lambda i, zb: (0, 0))],
            out_specs=any_spec,
            scratch_shapes=[pltpu.VMEM((SCATTER_BUFS, tm, D_MODEL), F32),
                            pltpu.VMEM((SCATTER_BUFS, tm, 1, D_MODEL // 2), F32),
                            pltpu.SMEM((2 * rows,), jnp.int32),
                            pltpu.SemaphoreType.DMA((SCATTER_BUFS,)), pltpu.SemaphoreType.DMA((SCATTER_BUFS,)),
                            pltpu.SemaphoreType.DMA((2,)), pltpu.SemaphoreType.DMA]),
        out_shape=jax.ShapeDtypeStruct((n_rows, 1, D_MODEL // 2), F32),
        compiler_params=_cparams(("arbitrary",)),
        name="scatter",
    )(zero_blk, x1, pos_tiles, gf)


EXPERT_BUFS = 3


def _expert_kernel(blk_exp_ref, nblk_ref, nvalid_ref, x_hbm, wg_ref, wu_ref, wd_ref, y_ref, wg_bf, wu_bf, wd_bf,
                   xbuf, sem_x):
    i = pl.program_id(0)
    n = nblk_ref[0]
    rows = xbuf.shape[1]

    def x_copy(b):
        s = b % EXPERT_BUFS
        return pltpu.make_async_copy(x_hbm.at[pl.ds(b * rows, rows)], xbuf.at[s], sem_x.at[s])

    @pl.when(i == 0)
    def _():
        for b in range(EXPERT_BUFS - 1):
            pl.when(b < n)(lambda b=b: x_copy(b).start())

    @pl.when(i + EXPERT_BUFS - 1 < n)
    def _():
        x_copy(i + EXPERT_BUFS - 1).start()

    @pl.when(jnp.logical_or(i == 0, blk_exp_ref[i] != blk_exp_ref[jnp.maximum(i - 1, 0)]))
    def _():
        wg_bf[...] = wg_ref[0].astype(BF16)
        wu_bf[...] = wu_ref[0].astype(BF16)
        wd_bf[...] = wd_ref[0].astype(BF16)

    @pl.when(i < nblk_ref[0])
    def _():
        x_copy(i).wait()
        w = xbuf[i % EXPERT_BUFS, :, 0, :]
        real = lax.broadcasted_iota(jnp.int32, w.shape, 0) < nvalid_ref[i]
        lo, hi = [v.astype(BF16) for v in _unpack_bf16_pair(jnp.where(real, w, 0.0))]
        half = D_MODEL // 2

        def up_proj(w_bf):
            return (jnp.dot(lo, w_bf[:half, :], preferred_element_type=F32)
                    + jnp.dot(hi, w_bf[half:, :], preferred_element_type=F32))

        gt, up = up_proj(wg_bf), up_proj(wu_bf)
        act = (gt * (0.5 * jnp.tanh(0.5 * gt) + 0.5) * up).astype(BF16)
        y_ref[:, 0, :] = jnp.dot(act, wd_bf[...], preferred_element_type=F32)

    @pl.when(i >= nblk_ref[0])
    def _():
        y_ref[...] = jnp.zeros(y_ref.shape, y_ref.dtype)


def _expert_call(xr, rows, blk_exp, nblk, nvalid, wg, wu, wd):
    nb = xr.shape[0] // rows
    return pl.pallas_call(
        _expert_kernel,
        grid_spec=pltpu.PrefetchScalarGridSpec(
            num_scalar_prefetch=3,
            grid=(nb,),
            in_specs=[pl.BlockSpec(memory_space=pl.ANY),
                      pl.BlockSpec((1, D_MODEL, D_EXPERT), lambda i, be, n, nv: (be[i], 0, 0)),
                      pl.BlockSpec((1, D_MODEL, D_EXPERT), lambda i, be, n, nv: (be[i], 0, 0)),
                      pl.BlockSpec((1, D_EXPERT, D_MODEL), lambda i, be, n, nv: (be[i], 0, 0))],
            out_specs=pl.BlockSpec((rows, 1, D_MODEL), lambda i, be, n, nv: (i, 0, 0)),
            scratch_shapes=[pltpu.VMEM((D_MODEL, D_EXPERT), BF16), pltpu.VMEM((D_MODEL, D_EXPERT), BF16),
                            pltpu.VMEM((D_EXPERT, D_MODEL), BF16),
                            pltpu.VMEM((EXPERT_BUFS, rows, 1, D_MODEL // 2), F32),
                            pltpu.SemaphoreType.DMA((EXPERT_BUFS,))]),
        out_shape=jax.ShapeDtypeStruct((nb * rows, 1, D_MODEL), F32),
        compiler_params=_cparams(("arbitrary",)),
        name="experts",
    )(blk_exp, nblk, nvalid, xr, wg, wu, wd)


def _combine_kernel(x1_ref, w0_ref, w1_ref, y_hbm, pos_hbm, o_ref, buf, idx, sem_rows, sem_idx):
    rows = buf.shape[1]
    tm = rows // 2
    slot = _gather_step(pl.program_id(0), pl.num_programs(0), y_hbm, pos_hbm, buf, idx, sem_rows, sem_idx)
    w0 = w0_ref[...]
    w1 = w1_ref[...]
    for c in range(D_MODEL // LANES):
        sl = slice(c * LANES, (c + 1) * LANES)
        o_ref[:, sl] = x1_ref[:, sl] + w0 * buf[slot, 0:tm, 0, sl] + w1 * buf[slot, tm:rows, 0, sl]


def _combine_call(x1, w0, w1, yr, pos_tiles, tm):
    t = x1.shape[0]
    rows = TOP_K * tm
    row = lambda w: pl.BlockSpec((tm, w), lambda i: (i, 0))
    return pl.pallas_call(
        _combine_kernel,
        grid=(t // tm,),
        in_specs=[row(D_MODEL), row(LANES), row(LANES),
                  pl.BlockSpec(memory_space=pl.ANY), pl.BlockSpec(memory_space=pl.ANY)],
        out_specs=row(D_MODEL),
        out_shape=jax.ShapeDtypeStruct((t, D_MODEL), F32),
        scratch_shapes=[pltpu.VMEM((2, rows, 1, D_MODEL), F32), pltpu.SMEM((2 * rows,), jnp.int32),
                        pltpu.SemaphoreType.DMA((2,)), pltpu.SemaphoreType.DMA((2,))],
        compiler_params=_cparams(("arbitrary",)),
        name="combine",
    )(x1, w0, w1, yr, pos_tiles)


def _rot_cols(w):
    half = ROPE_DIM // 2
    return jnp.concatenate([-w[..., half:], w[..., :half]], axis=-1)


def _head_pad(nope, rope):
    z = jnp.zeros(rope.shape[:-1] + (LANES - B_QK_DIM,), rope.dtype)
    out = jnp.concatenate([nope, rope, z], axis=-1)
    return out.reshape(out.shape[:-2] + (out.shape[-2] * LANES,))


def _t5_bucket(dist):
    max_exact = REL_BUCKETS // 2
    d = jnp.maximum(dist, 0)
    ratio = jnp.maximum(d, 1).astype(F32) / max_exact
    large = max_exact + (jnp.log(ratio) / math.log(REL_MAX_DIST / max_exact)
                         * (REL_BUCKETS - max_exact)).astype(jnp.int32)
    large = jnp.minimum(large, REL_BUCKETS - 1)
    return jnp.where(d < max_exact, d, large)


def _swa_bias(rel_bias, tq):
    qi = jnp.arange(tq)[:, None]
    kj = jnp.arange(2 * tq)[None, :]
    dist = qi + tq - kj
    onehot = (_t5_bucket(dist)[:, :, None] == jnp.arange(REL_BUCKETS)[None, None, :]).astype(F32)
    bias = jnp.einsum("qkb,bh->hqk", onehot, rel_bias.astype(F32), precision=lax.Precision.HIGHEST) * LOG2E
    mask = (dist >= 0) & (dist < WINDOW)
    bias = jnp.where(mask[None], bias, NEG)
    return bias.reshape(A_KV_HEADS, (A_HEADS // A_KV_HEADS) * tq, 2 * tq)


def _rope_tables(seq):
    half = ROPE_DIM // 2
    inv_freq = ROPE_THETA ** (-jnp.arange(half, dtype=F32) / half)
    ang = jnp.arange(seq, dtype=F32)[:, None] * inv_freq[None, :]
    cos = jnp.concatenate([jnp.cos(ang), jnp.cos(ang)], axis=-1)
    sin = jnp.concatenate([jnp.sin(ang), jnp.sin(ang)], axis=-1)
    return cos, sin


def _lane_table(nope_part, rope_part):
    seq = rope_part.shape[0]
    nope_part = jnp.broadcast_to(nope_part, (seq, NOPE_DIM))
    return jnp.concatenate([nope_part, rope_part, jnp.zeros((seq, LANES - B_QK_DIM), F32)], axis=-1)


def _prep_layer(l, seq, cos, sin, attn_norm, w_in, b_gate, a_q_norm, a_k_norm, w_oa, q_a_norm, w_qb, kv_a_norm,
                w_kvb, b_q_norm, b_k_norm, w_ob, w_o, ffn_norm, w_rg, b_rg, w_re, b_re, w_eg, w_eu, w_ed):
    w = w_in[l]
    o = 0
    parts = []
    for sz in (A_WIDTH, A_KV_HEADS * A_HEAD_DIM, A_KV_HEADS * A_HEAD_DIM, Q_LORA, KV_LORA, ROPE_DIM, 2 * D_MODEL):
        parts.append(w[:, o:o + sz])
        o += sz
    wqa, wka, wva, wcq, wckv, wkpe, wg = parts
    perm = jnp.array([c + (A_HEADS // 2) * h for c in range(A_HEADS // 2) for h in range(2)])
    wqa = wqa.reshape(D_MODEL, A_HEADS, A_HEAD_DIM)[:, perm].reshape(D_MODEL, A_WIDTH)
    zl = jnp.zeros((D_MODEL, NOPE_DIM), F32)
    zr = jnp.zeros((D_MODEL, LANES - B_QK_DIM), F32)
    w1 = jnp.concatenate([wqa, wka, wva, wcq, wckv, zl, wkpe, zr, zl, _rot_cols(wkpe), zr, 0.5 * wg],
                         axis=1).astype(BF16)

    wq = w_qb[l].reshape(Q_LORA, B_HEADS, B_QK_DIM)
    zq = jnp.zeros((Q_LORA, B_HEADS, NOPE_DIM), F32)
    wq2 = jnp.concatenate([_head_pad(wq[..., :NOPE_DIM], wq[..., NOPE_DIM:]),
                           _head_pad(zq, _rot_cols(wq[..., NOPE_DIM:]))], axis=1).astype(BF16)
    wkv = w_kvb[l].reshape(KV_LORA, B_HEADS, NOPE_DIM + B_V_DIM)
    wk2 = _head_pad(wkv[..., :NOPE_DIM], jnp.zeros((KV_LORA, B_HEADS, ROPE_DIM), F32)).astype(BF16)
    wv = wkv[..., NOPE_DIM:]
    zv = jnp.zeros_like(wv)
    wv2 = jnp.where((jnp.arange(B_HEADS) % 2 == 0)[None, :, None], jnp.concatenate([wv, zv], -1),
                    jnp.concatenate([zv, wv], -1)).reshape(KV_LORA, B_HEADS * LANES).astype(BF16)
    one_lane = jnp.where(jnp.arange(B_HEADS) % 2 == 0, V_ONE_EVEN, V_ONE_ODD)
    vone = (jnp.arange(LANES)[None, :] == one_lane[:, None]).astype(F32).reshape(1, B_HEADS * LANES)

    def rope_tabs(gain, scale):
        gn, gr = gain[:NOPE_DIM], gain[NOPE_DIM:]
        gr_rot = jnp.concatenate([gr[ROPE_DIM // 2:], gr[:ROPE_DIM // 2]])
        a = _lane_table(gn[None, :] * scale, gr[None, :] * cos * scale)
        b = _lane_table(jnp.zeros((1, NOPE_DIM), F32), gr_rot[None, :] * sin * scale)
        return a, b

    aq, bq = rope_tabs(b_q_norm[l], B_QK_DIM ** -0.5 * LOG2E)
    score_bound = math.sqrt(B_QK_DIM) * LOG2E * jnp.max(jnp.abs(b_q_norm[l])) * jnp.max(jnp.abs(b_k_norm[l]))
    ak, bk = rope_tabs(b_k_norm[l], 1.0)

    gqa = (jnp.tile(a_q_norm[l], 2) * (A_HEAD_DIM ** -0.5 * LOG2E))[None, :]
    gka = jnp.tile(a_k_norm[l], 2)[None, :]
    woa = w_oa[l].reshape(A_HEADS, A_HEAD_DIM, D_MODEL)[perm].reshape(A_WIDTH, D_MODEL).astype(BF16)

    wr = jnp.concatenate([w_rg[l], w_re[l], jnp.zeros((D_MODEL, LANES - N_GROUPS - N_EXPERTS), F32)], axis=1)
    wrh = wr.astype(BF16)
    wrl = (wr - wrh.astype(F32)).astype(BF16)
    br = jnp.concatenate([b_rg[l], b_re[l], jnp.zeros((LANES - N_GROUPS - N_EXPERTS,), F32)])[None, :]

    return dict(
        gn=attn_norm[l][None, :], w1=w1, wq2=wq2, wk2=wk2, wv2=wv2, vone=vone, gcq=q_a_norm[l][None, :],
        gckv=kv_a_norm[l][None, :], score_bound=score_bound, bg=0.5 * b_gate[l][None, :], gqa=gqa, gka=gka,
        aq=aq, bq=bq, ak=ak, bk=bk,
        woa=woa, wob=w_ob[l].astype(BF16), wo=w_o[l].astype(BF16), gf=ffn_norm[l][None, :], wr=jnp.concatenate([wrh, wrl], axis=1), br=br,
        wg=w_eg.reshape((-1,) + w_eg.shape[2:]), wu=w_eu.reshape((-1,) + w_eu.shape[2:]),
        wd=w_ed.reshape((-1,) + w_ed.shape[2:]), expert_base=l * N_EXPERTS)


def _tril(n):
    return (jnp.arange(n)[:, None] >= jnp.arange(n)[None, :]).astype(jnp.int32)


def _dispatch(route, tile_cnt, rows):
    t = route.shape[1]
    nt = tile_cnt.shape[0]
    tile_cnt = tile_cnt[:, 0, :N_EXPERTS]
    counts = jnp.sum(tile_cnt, axis=0)
    padded = ((counts + rows - 1) // rows) * rows
    pends = jnp.dot(_tril(N_EXPERTS), padded)
    pstarts = pends - padded
    tile_base = pstarts[None, :] + jnp.dot(_tril(nt), tile_cnt) - tile_cnt
    base = jnp.repeat(tile_base.T, t // nt, axis=1)
    onehot = route[:TOP_K, None, :] == jnp.arange(N_EXPERTS, dtype=jnp.int32)[None, :, None]
    dest = jnp.sum(jnp.where(onehot, base[None], 0), axis=1) + route[TOP_K:2 * TOP_K]
    n_rows = t * TOP_K + N_EXPERTS * rows
    n_blocks = n_rows // rows
    blk_start = jnp.arange(n_blocks, dtype=jnp.int32)[:, None] * rows
    blk_exp = jnp.minimum(jnp.sum((pends[None, :] <= blk_start).astype(jnp.int32), axis=1), N_EXPERTS - 1)
    nblk = (pends[-1] // rows).astype(jnp.int32).reshape(1)
    nvalid = jnp.clip((pstarts + counts)[blk_exp] - blk_start[:, 0], 0, rows).astype(jnp.int32)
    last_blk = jnp.where(padded > 0, pends // rows - 1, -1)
    trailing = nblk[0] + jnp.arange(N_EXPERTS, dtype=jnp.int32)
    zero_blk = jnp.concatenate([last_blk, jnp.where(trailing < n_blocks, trailing, -1)]).astype(jnp.int32)
    return dest, blk_exp, nblk, nvalid, zero_blk, n_rows


def _swa_tables(sinks, rel_bias, a_q_norm, a_k_norm):
    per_group = A_HEADS // A_KV_HEADS
    tab = jnp.broadcast_to((sinks * LOG2E).reshape(A_KV_HEADS, per_group, 1, 1),
                           (A_KV_HEADS, per_group, SWA_TQ, LANES)).reshape(A_KV_HEADS, per_group * SWA_TQ, LANES)
    qk = math.sqrt(A_HEAD_DIM) * jnp.max(jnp.abs(a_q_norm)) * jnp.max(jnp.abs(a_k_norm))
    bound = LOG2E * jnp.maximum(qk + jnp.max(jnp.abs(rel_bias)), jnp.max(jnp.abs(sinks)))
    return tab, bound


def _pos_tiles(dest, tm):
    t = dest.shape[1]
    return dest.reshape(TOP_K, t // tm, tm).transpose(1, 0, 2).reshape(t * TOP_K)


def _layer(x2d, bsz, seq, p, sink_tab, swa_bound, bias):
    qa, ka, va, qb, kb, vb, gates = _proj_call(x2d, seq, p)
    oa = _swa_call(qa, ka, va, sink_tab, bias, bsz, seq, swa_bound)
    ob = _mla_call(qb, kb, vb, bsz, seq, p["score_bound"])
    x1, w0, w1, route, tile_cnt = _merge_call(x2d, oa, ob, gates, p)
    dest, blk_exp, nblk, nvalid, zero_blk, n_rows = _dispatch(route, tile_cnt, MOE_ROWS)
    pos_tiles = _pos_tiles(dest, COMB_TM)
    xr = _scatter_call(x1, pos_tiles, zero_blk, p["gf"], n_rows, COMB_TM, MOE_ROWS)
    yr = _expert_call(xr, MOE_ROWS, blk_exp + p["expert_base"], nblk, nvalid, p["wg"], p["wu"], p["wd"])
    return _combine_call(x1, w0, w1, yr, pos_tiles, COMB_TM)


def kernel(x, attn_norm, w_in, b_gate, a_q_norm, a_k_norm, a_sinks, rel_bias, w_oa, q_a_norm, w_qb, kv_a_norm, w_kvb, b_q_norm, b_k_norm, w_ob, w_o, ffn_norm, w_router_group, b_router_group, w_router_expert, b_router_expert, w_exp_gate, w_exp_up, w_exp_down):
    bsz, seq, d = x.shape
    assert d == D_MODEL and WINDOW == SWA_TQ and MLA_TQ % MLA_TK == 0 and MOE_ROWS % COMB_TM == 0
    assert all(seq % tile == 0 for tile in (MLA_TQ, PROJ_TM, SWA_SUB * SWA_TQ))
    assert all((bsz * seq) % tile == 0 for tile in (MERGE_TM, COMB_TM))
    depth = w_in.shape[0]
    cos, sin = _rope_tables(seq)
    bias = _swa_bias(rel_bias, SWA_TQ)
    x2d = x.reshape(bsz * seq, d)
    for l in range(depth):
        p = _prep_layer(l, seq, cos, sin, attn_norm, w_in, b_gate, a_q_norm, a_k_norm, w_oa, q_a_norm, w_qb,
                        kv_a_norm, w_kvb, b_q_norm, b_k_norm, w_ob, w_o, ffn_norm, w_router_group,
                        b_router_group, w_router_expert, b_router_expert, w_exp_gate, w_exp_up, w_exp_down)
        sink_tab, swa_bound = _swa_tables(a_sinks[l], rel_bias, a_q_norm[l], a_k_norm[l])
        x2d = _layer(x2d, bsz, seq, p, sink_tab, swa_bound, bias)
    return x2d.reshape(bsz, seq, d)
```
